```python
import jax
import jax.numpy as jnp
from jax import lax
import numpy as np

D_MODEL = 1024
BATCH = 16
SEQ = 2048
DEPTH = 2

D_MIX = D_MODEL
MLSTM_HEADS = 4
MLSTM_DQK = D_MODEL // 8
MLSTM_DV = D_MODEL // 8
MLSTM_WIDTH = MLSTM_HEADS * MLSTM_DV
CONV_WIDTH = D_MIX - MLSTM_WIDTH
CONV_K = 3
CHUNK = 64
N_GROUPS = 4
EXPERTS_PER_GROUP = 4
TOP_K = 2
D_EXPERT = D_MODEL // 2
ALPHA = (2 * DEPTH) ** 0.25
BETA = (8 * DEPTH) ** -0.25
LN_EPS = 1e-5
IN_SIZES = (MLSTM_HEADS * MLSTM_DQK, MLSTM_HEADS * MLSTM_DQK, MLSTM_WIDTH, MLSTM_WIDTH,
            MLSTM_HEADS, MLSTM_HEADS, CONV_WIDTH, CONV_WIDTH, CONV_WIDTH)
D_IN = sum(IN_SIZES)

kernel_name = 'hybrid_mlstm_shortconv_hiermoe_deepnorm_adaln'


def layer_norm(x, g, b):
    xf = x.astype(jnp.float32)
    mu = xf.mean(-1, keepdims=True)
    var = jnp.square(xf - mu).mean(-1, keepdims=True)
    y = (xf - mu) * lax.rsqrt(var + LN_EPS) * g.astype(jnp.float32) + b.astype(jnp.float32)
    return y.astype(x.dtype)


def head_norm(h, w):
    bsz, seq, nh, dv = h.shape
    mu = h.mean(-1, keepdims=True)
    var = jnp.square(h - mu).mean(-1, keepdims=True)
    hn = (h - mu) * lax.rsqrt(var + LN_EPS)
    return hn.reshape(bsz, seq, nh * dv) * w.astype(jnp.float32)


def mlstm_chunkwise(q, k, v, i_pre, f_pre):
    f32 = jnp.float32
    bsz, seq, nh, dk = q.shape
    dv = v.shape[-1]
    nc = seq // CHUNK

    def to_chunks(t):
        t = t.astype(f32).reshape((bsz, nc, CHUNK, nh) + t.shape[3:])
        return jnp.moveaxis(t, 3, 1)

    qc = to_chunks(q) * (dk ** -0.5)
    kc = to_chunks(k)
    vc = to_chunks(v)
    ig = to_chunks(i_pre)
    logf = jax.nn.log_sigmoid(to_chunks(f_pre))
    bcum = jnp.cumsum(logf, axis=-1)
    gtot = bcum[..., -1]

    causal = jnp.tril(jnp.ones((CHUNK, CHUNK), dtype=bool))
    dmat = jnp.where(causal, bcum[..., :, None] - bcum[..., None, :] + ig[..., None, :], -jnp.inf)

    a = gtot[..., None] - bcum + ig
    m_loc = a.max(-1)
    w_loc = jnp.exp(a - m_loc[..., None])
    c_loc = jnp.einsum('bhcsv,bhcsk->bhcvk', w_loc[..., None] * vc, kc)
    n_loc = jnp.einsum('bhcs,bhcsk->bhck', w_loc, kc)

    def step(carry, xs):
        c_st, n_st, m_st = carry
        g_c, m_c, cl, nl = xs
        m_new = jnp.maximum(g_c + m_st, m_c)
        decay = jnp.exp(g_c + m_st - m_new)
        inj = jnp.exp(m_c - m_new)
        c_new = decay[..., None, None] * c_st + inj[..., None, None] * cl
        n_new = decay[..., None] * n_st + inj[..., None] * nl
        return (c_new, n_new, m_new), (c_st, n_st, m_st)

    init = (jnp.zeros((bsz, nh, dv, dk), f32), jnp.zeros((bsz, nh, dk), f32),
            jnp.zeros((bsz, nh), f32))
    xs = (jnp.moveaxis(gtot, 2, 0), jnp.moveaxis(m_loc, 2, 0),
          jnp.moveaxis(c_loc, 2, 0), jnp.moveaxis(n_loc, 2, 0))
    _, (c_prev, n_prev, m_prev) = lax.scan(step, init, xs)
    c_prev = jnp.moveaxis(c_prev, 0, 2)
    n_prev = jnp.moveaxis(n_prev, 0, 2)
    m_prev = jnp.moveaxis(m_prev, 0, 2)

    e_inter = bcum + m_prev[..., None]
    m_t = jnp.maximum(e_inter, dmat.max(-1))
    s_mat = jnp.einsum('bhctk,bhcsk->bhcts', qc, kc) * jnp.exp(dmat - m_t[..., None])
    w_inter = jnp.exp(e_inter - m_t)
    num = (jnp.einsum('bhcts,bhcsv->bhctv', s_mat, vc)
           + w_inter[..., None] * jnp.einsum('bhcvk,bhctk->bhctv', c_prev, qc))
    den = s_mat.sum(-1) + w_inter * jnp.einsum('bhck,bhctk->bhct', n_prev, qc)
    h = num / jnp.maximum(jnp.abs(den), jnp.exp(-m_t))[..., None]
    return jnp.moveaxis(h, 1, 3).reshape(bsz, seq, nh, dv)


def short_gated_conv(b_gate, c_gate, h_in, w_conv):
    z = c_gate * h_in
    zc = lax.conv_general_dilated(z, w_conv[:, None, :], window_strides=(1,),
                                  padding=[(CONV_K - 1, 0)],
                                  dimension_numbers=('NWC', 'WIO', 'NWC'),
                                  feature_group_count=z.shape[-1])
    return b_gate * zc


def hier_moe(u, w_grp, b_grp, w_router, b_router, w_gate, w_up, w_down):
    f32 = jnp.float32
    bsz, seq, dm = u.shape
    t = u.reshape(bsz * seq, dm)
    grp_logits = (t @ w_grp + b_grp).astype(f32)
    grp_prob = jax.nn.softmax(grp_logits, axis=-1)
    grp_idx = jnp.argmax(grp_logits, axis=-1)
    grp_w = jnp.take_along_axis(grp_prob, grp_idx[:, None], axis=-1)
    exp_logits = (t @ w_router + b_router).astype(f32).reshape(-1, N_GROUPS, EXPERTS_PER_GROUP)
    sel_logits = jnp.take_along_axis(exp_logits, grp_idx[:, None, None], axis=1)[:, 0]
    top_vals, top_idx = lax.top_k(sel_logits, TOP_K)
    top_w = jax.nn.softmax(top_vals, axis=-1) * grp_w
    exp_w = jnp.einsum('tk,tke->te', top_w, jax.nn.one_hot(top_idx, EXPERTS_PER_GROUP, dtype=f32))
    comb = (jax.nn.one_hot(grp_idx, N_GROUPS, dtype=f32)[:, :, None] * exp_w[:, None, :]).astype(u.dtype)
    y = jnp.zeros_like(t)
    for gi in range(N_GROUPS):
        hg = (jax.nn.silu(jnp.einsum('td,edf->tef', t, w_gate[gi]))
              * jnp.einsum('td,edf->tef', t, w_up[gi]))
        y = y + jnp.einsum('tef,efd->td', hg * comb[:, gi, :, None], w_down[gi])
    return y.reshape(bsz, seq, dm)


def setup_inputs(seed: int = 0) -> dict:
    key = jax.random.key(seed)
    ks = jax.random.split(key, 24)
    nrm = jax.random.normal
    H = MLSTM_HEADS
    G, E, F = N_GROUPS, EXPERTS_PER_GROUP, D_EXPERT
    x = nrm(ks[0], (BATCH, SEQ, D_MODEL), jnp.float32)
    c = nrm(ks[1], (BATCH, D_MODEL), jnp.float32)
    w_ada = nrm(ks[2], (DEPTH, D_MODEL, 6 * D_MODEL)) * (0.2 * D_MODEL ** -0.5)
    b_ada = 0.02 * nrm(ks[3], (DEPTH, 6 * D_MODEL))
    w_in = nrm(ks[4], (DEPTH, D_MODEL, D_IN)) * D_MODEL ** -0.5
    b_i = 0.1 * nrm(ks[5], (DEPTH, H))
    b_f = jnp.linspace(3.0, 6.0, H)[None, :] + 0.1 * nrm(ks[6], (DEPTH, H))
    b_gates = jnp.concatenate([b_i, b_f], axis=-1)
    mh_norm_w = 1.0 + 0.02 * nrm(ks[7], (DEPTH, MLSTM_WIDTH))
    w_conv = nrm(ks[8], (DEPTH, CONV_K, CONV_WIDTH)) * CONV_K ** -0.5
    w_out = nrm(ks[9], (DEPTH, D_MIX, D_MODEL)) * (BETA * D_MIX ** -0.5)
    ln1_g = 1.0 + 0.02 * nrm(ks[10], (DEPTH, D_MODEL))
    ln1_b = 0.02 * nrm(ks[11], (DEPTH, D_MODEL))
    w_grp = nrm(ks[12], (DEPTH, D_MODEL, G)) * D_MODEL ** -0.5
    b_grp = 0.01 * nrm(ks[13], (DEPTH, G))
    w_router = nrm(ks[14], (DEPTH, D_MODEL, G * E)) * D_MODEL ** -0.5
    b_router = 0.01 * nrm(ks[15], (DEPTH, G * E))
    w_gate = nrm(ks[16], (DEPTH, G, E, D_MODEL, F)) * D_MODEL ** -0.5
    w_up = nrm(ks[17], (DEPTH, G, E, D_MODEL, F)) * D_MODEL ** -0.5
    w_down = nrm(ks[18], (DEPTH, G, E, F, D_MODEL)) * (BETA * F ** -0.5)
    ln2_g = 1.0 + 0.02 * nrm(ks[19], (DEPTH, D_MODEL))
    ln2_b = 0.02 * nrm(ks[20], (DEPTH, D_MODEL))
    return {'x': x, 'c': c, 'w_ada': w_ada, 'b_ada': b_ada, 'w_in': w_in, 'b_gates': b_gates,
            'mh_norm_w': mh_norm_w, 'w_conv': w_conv, 'w_out': w_out, 'ln1_g': ln1_g,
            'ln1_b': ln1_b, 'w_grp': w_grp, 'b_grp': b_grp, 'w_router': w_router,
            'b_router': b_router, 'w_gate': w_gate, 'w_up': w_up, 'w_down': w_down,
            'ln2_g': ln2_g, 'ln2_b': ln2_b}


def reference(x, c, w_ada, b_ada, w_in, b_gates, mh_norm_w, w_conv, w_out, ln1_g, ln1_b,
              w_grp, b_grp, w_router, b_router, w_gate, w_up, w_down, ln2_g, ln2_b):
    bsz, seq, _ = x.shape
    H = MLSTM_HEADS
    split_at = np.cumsum(IN_SIZES)[:-1].tolist()
    cond = jax.nn.silu(c)
    for l in range(DEPTH):
        mod = cond @ w_ada[l] + b_ada[l]
        sh1, sc1, gt1, sh2, sc2, gt2 = [m[:, None, :] for m in jnp.split(mod, 6, axis=-1)]
        u = x * (1 + sc1) + sh1
        proj = u @ w_in[l]
        q, k, v, o, i_pre, f_pre, cb, cc, ch = jnp.split(proj, split_at, axis=-1)
        i_pre = i_pre + b_gates[l, :H]
        f_pre = f_pre + b_gates[l, H:]
        h = mlstm_chunkwise(q.reshape(bsz, seq, H, MLSTM_DQK), k.reshape(bsz, seq, H, MLSTM_DQK),
                            v.reshape(bsz, seq, H, MLSTM_DV), i_pre, f_pre)
        h = head_norm(h, mh_norm_w[l]) * jax.nn.sigmoid(o.astype(jnp.float32))
        y_conv = short_gated_conv(cb, cc, ch, w_conv[l])
        mix = jnp.concatenate([h.astype(x.dtype), y_conv], axis=-1) @ w_out[l]
        x = layer_norm(ALPHA * x + (1 + gt1) * mix, ln1_g[l], ln1_b[l])
        u = x * (1 + sc2) + sh2
        y = hier_moe(u, w_grp[l], b_grp[l], w_router[l], b_router[l], w_gate[l], w_up[l], w_down[l])
        x = layer_norm(ALPHA * x + (1 + gt2) * y, ln2_g[l], ln2_b[l])
    return x
```

```python
import functools

import jax
import jax.numpy as jnp
from jax import lax
from jax.experimental import pallas as pl
from jax.experimental.pallas import tpu as pltpu

F32 = jnp.float32
BF16 = jnp.bfloat16
I32 = jnp.int32

D_MODEL = 1024
DEPTH = 2
HEADS = 4
D_HEAD = 128
M_WIDTH = HEADS * D_HEAD
C_WIDTH = D_MODEL - M_WIDTH
N_GROUPS = 4
N_EXP = 4
D_EXPERT = 512
N_PAIRS = 6
N_BUCKETS = N_GROUPS * N_PAIRS
ALPHA = (2 * DEPTH) ** 0.25
LN_EPS = 1e-5
QK_SCALE = D_HEAD ** -0.5

LANES = 128
ROUTE_ROWS = 32
X_EXT = D_MODEL + LANES

TOK_TILE = 512
CHUNK = 128
MOE_TILE = 256
VMEM_LIMIT = 56 * 1024 * 1024

_PAIR_A = (0, 0, 0, 1, 1, 2)
_PAIR_B = (1, 2, 3, 2, 3, 3)


def _cparams(sem):
    return pltpu.CompilerParams(dimension_semantics=sem, vmem_limit_bytes=VMEM_LIMIT)


def _ada_kernel(c_ref, w_ref, b_ref, o_ref):
    c = c_ref[...]
    cond = c * jax.nn.sigmoid(c)
    o_ref[0] = jnp.dot(cond, w_ref[0], preferred_element_type=F32,
                       precision=lax.Precision.HIGHEST) + b_ref[0]


def _ada_mod(c, w_ada, b_ada):
    bsz = c.shape[0]
    nblk = w_ada.shape[2] // D_MODEL
    return pl.pallas_call(
        _ada_kernel,
        grid=(DEPTH, nblk),
        in_specs=[
            pl.BlockSpec((bsz, D_MODEL), lambda l, j: (0, 0)),
            pl.BlockSpec((1, D_MODEL, D_MODEL), lambda l, j: (l, 0, j)),
            pl.BlockSpec((1, 1, D_MODEL), lambda l, j: (l, 0, j)),
        ],
        out_specs=pl.BlockSpec((1, bsz, D_MODEL), lambda l, j: (l, 0, j)),
        out_shape=jax.ShapeDtypeStruct((DEPTH, bsz, nblk * D_MODEL), F32),
        compiler_params=_cparams(("arbitrary", "arbitrary")),
        name="ada_mod",
    )(c, w_ada, b_ada.reshape(DEPTH, 1, -1))


def _log_sigmoid(x):
    return jnp.minimum(x, 0.0) - jnp.log1p(jnp.exp(-jnp.abs(x)))


def _inproj_kernel(x_ref, sc_ref, sh_ref, wq_ref, wg_ref, wc_ref, bg_ref, wconv_ref,
                   qkvo_ref, gcol_ref, grow_ref, yc_ref, zbuf):
    tm = x_ref.shape[0]
    u = (x_ref[...] * (1.0 + sc_ref[0]) + sh_ref[0]).astype(BF16)
    qkvo_ref[...] = jnp.dot(u, wq_ref[...], preferred_element_type=F32).astype(BF16)
    g = jnp.dot(u, wg_ref[...], preferred_element_type=F32) + bg_ref[...]
    lane = lax.broadcasted_iota(I32, g.shape, 1)
    gp = jnp.where(lane < HEADS, g, _log_sigmoid(g))
    gcol_ref[...] = gp
    grow_ref[...] = gp.T[:8]
    pc = jnp.dot(u, wc_ref[...], preferred_element_type=F32)
    z = pc[:, C_WIDTH:2 * C_WIDTH] * pc[:, 2 * C_WIDTH:]

    @pl.when(pl.program_id(1) == 0)
    def _():
        zbuf[0:8, :] = jnp.zeros((8, C_WIDTH), F32)

    zbuf[8:8 + tm, :] = z
    zc = (wconv_ref[0:1, :] * zbuf[6:6 + tm, :] + wconv_ref[1:2, :] * zbuf[7:7 + tm, :]
          + wconv_ref[2:3, :] * z)
    yc_ref[...] = (pc[:, :C_WIDTH] * zc).astype(BF16)
    zbuf[0:8, :] = zbuf[tm:tm + 8, :]


def _inproj(x2, sc, sh, wq, wg, wc, bg, wconv, bsz, seq):
    tm = min(TOK_TILE, seq)
    nt = seq // tm
    tok = bsz * seq
    row = lambda b, s: (b * nt + s, 0)
    const = lambda b, s: (0, 0)
    mod = lambda b, s: (b, 0, 0)
    return pl.pallas_call(
        _inproj_kernel,
        grid=(bsz, nt),
        in_specs=[
            pl.BlockSpec((tm, D_MODEL), row),
            pl.BlockSpec((1, 1, D_MODEL), mod),
            pl.BlockSpec((1, 1, D_MODEL), mod),
            pl.BlockSpec(wq.shape, const),
            pl.BlockSpec(wg.shape, const),
            pl.BlockSpec(wc.shape, const),
            pl.BlockSpec(bg.shape, const),
            pl.BlockSpec(wconv.shape, const),
        ],
        out_specs=[
            pl.BlockSpec((tm, 4 * M_WIDTH), row),
            pl.BlockSpec((tm, LANES), row),
            pl.BlockSpec((8, tm), lambda b, s: (0, b * nt + s)),
            pl.BlockSpec((tm, C_WIDTH), row),
        ],
        out_shape=[
            jax.ShapeDtypeStruct((tok, 4 * M_WIDTH), BF16),
            jax.ShapeDtypeStruct((tok, LANES), F32),
            jax.ShapeDtypeStruct((8, tok), F32),
            jax.ShapeDtypeStruct((tok, C_WIDTH), BF16),
        ],
        scratch_shapes=[pltpu.VMEM((tm + 8, C_WIDTH), F32)],
        compiler_params=_cparams(("arbitrary", "arbitrary")),
        name="inproj",
    )(x2, sc, sh, wq, wg, wc, bg, wconv)


def _mlstm_kernel(qkvo_ref, gcol_ref, grow_ref, nw_ref, out_ref, cst, mst):
    sb = qkvo_ref.shape[0]
    L = CHUNK

    @pl.when(pl.program_id(1) == 0)
    def _():
        cst[...] = jnp.zeros(cst.shape, F32)
        mst[...] = jnp.zeros(mst.shape, F32)

    rid = lax.broadcasted_iota(I32, (L, L), 0)
    cid = lax.broadcasted_iota(I32, (L, L), 1)
    causal = rid >= cid
    tril_f = causal.astype(F32)
    triu_f = (rid <= cid).astype(F32)
    e0 = (lax.broadcasted_iota(I32, (L, D_HEAD), 1) == 0).astype(BF16)

    def chunk(ci, carry):
        r0 = pl.multiple_of(ci * L, L)
        gc = gcol_ref[pl.ds(r0, L), :]
        gr = grow_ref[:, pl.ds(r0, L)]
        bcol = jnp.dot(tril_f, gc, preferred_element_type=F32, precision=lax.Precision.HIGHEST)
        brow = jnp.dot(gr, triu_f, preferred_element_type=F32, precision=lax.Precision.HIGHEST)
        for h in range(HEADS):
            c0 = h * D_HEAD
            q = qkvo_ref[pl.ds(r0, L), c0:c0 + D_HEAD]
            k = qkvo_ref[pl.ds(r0, L), M_WIDTH + c0:M_WIDTH + c0 + D_HEAD]
            v = qkvo_ref[pl.ds(r0, L), 2 * M_WIDTH + c0:2 * M_WIDTH + c0 + D_HEAD]
            o = qkvo_ref[pl.ds(r0, L), 3 * M_WIDTH + c0:3 * M_WIDTH + c0 + D_HEAD]
            bc = bcol[:, HEADS + h:HEADS + h + 1]
            br = brow[HEADS + h:HEADS + h + 1, :]
            igr = gr[h:h + 1, :]
            igc = gc[:, h:h + 1]
            m_prev = mst[h, 0:1, 0:1]

            dm = jnp.where(causal, bc - br + igr, -jnp.inf)
            e_int = bc + m_prev
            m_t = jnp.maximum(e_int, jnp.max(dm, axis=1, keepdims=True))
            dexp = jnp.exp(dm - m_t)
            s = lax.dot_general(q, k, (((1,), (1,)), ((), ())), preferred_element_type=F32)
            p = (s * (dexp * QK_SCALE)).astype(BF16)
            vext = jnp.concatenate([v, e0], axis=1)
            intra = jnp.dot(p, vext, preferred_element_type=F32)
            inter = jnp.dot(q, cst[h].astype(BF16), preferred_element_type=F32)
            w_int = jnp.exp(e_int - m_t) * QK_SCALE
            tot = intra + w_int * inter
            den = tot[:, D_HEAD:D_HEAD + 1]
            hh = tot[:, :D_HEAD] / jnp.maximum(jnp.abs(den), jnp.exp(-m_t))
            mu = jnp.mean(hh, axis=1, keepdims=True)
            var = jnp.mean(jnp.square(hh - mu), axis=1, keepdims=True)
            hn = (hh - mu) * lax.rsqrt(var + LN_EPS) * nw_ref[:, c0:c0 + D_HEAD]
            out_ref[pl.ds(r0, L), c0:c0 + D_HEAD] = (hn * jax.nn.sigmoid(o.astype(F32))).astype(BF16)

            gtot = bc[L - 1:L, :]
            a = gtot - bc + igc
            m_new = jnp.maximum(gtot + m_prev, jnp.max(a, axis=0, keepdims=True))
            kw = (k.astype(F32) * jnp.exp(a - m_new)).astype(BF16)
            upd = lax.dot_general(kw, vext, (((0,), (0,)), ((), ())), preferred_element_type=F32)
            cst[h] = jnp.exp(gtot + m_prev - m_new) * cst[h] + upd
            mst[h] = jnp.broadcast_to(m_new, mst.shape[1:])
        return carry

    lax.fori_loop(0, sb // L, chunk, 0)


def _mlstm(qkvo, gcol, grow, nw, bsz, seq):
    sb = min(TOK_TILE, seq)
    nt = seq // sb
    tok = bsz * seq
    row = lambda b, s: (b * nt + s, 0)
    return pl.pallas_call(
        _mlstm_kernel,
        grid=(bsz, nt),
        in_specs=[
            pl.BlockSpec((sb, 4 * M_WIDTH), row),
            pl.BlockSpec((sb, LANES), row),
            pl.BlockSpec((8, sb), lambda b, s: (0, b * nt + s)),
            pl.BlockSpec((1, M_WIDTH), lambda b, s: (0, 0)),
        ],
        out_specs=pl.BlockSpec((sb, M_WIDTH), row),
        out_shape=jax.ShapeDtypeStruct((tok, M_WIDTH), BF16),
        scratch_shapes=[pltpu.VMEM((HEADS, D_HEAD, 2 * D_HEAD), F32),
                        pltpu.VMEM((HEADS, 8, LANES), F32)],
        compiler_params=_cparams(("arbitrary", "arbitrary")),
        name="mlstm",
    )(qkvo, gcol, grow, nw)


def _layer_norm(r, g, b):
    mu = jnp.mean(r, axis=-1, keepdims=True)
    var = jnp.mean(jnp.square(r - mu), axis=-1, keepdims=True)
    return (r - mu) * lax.rsqrt(var + LN_EPS) * g + b


def _outproj_kernel(hn_ref, yc_ref, x_ref, gt_ref, sc_ref, sh_ref, wo_ref, lg_ref, lb_ref,
                    wr_ref, br_ref, x1_ref, u2_ref, route_ref, cnt_ref, base, triu):
    tm = x_ref.shape[0]

    @pl.when(pl.program_id(0) == 0)
    def _():
        base[...] = jnp.zeros(base.shape, F32)
        rid = lax.broadcasted_iota(I32, (tm, tm), 0)
        cid = lax.broadcasted_iota(I32, (tm, tm), 1)
        triu[...] = (rid <= cid).astype(BF16)

    mix = (jnp.dot(hn_ref[...], wo_ref[0:M_WIDTH, :], preferred_element_type=F32)
           + jnp.dot(yc_ref[...], wo_ref[M_WIDTH:, :], preferred_element_type=F32))
    x1 = _layer_norm(ALPHA * x_ref[...] + (1.0 + gt_ref[0]) * mix, lg_ref[...], lb_ref[...])
    x1_ref[...] = x1
    u2 = x1 * (1.0 + sc_ref[0]) + sh_ref[0]
    u2_ref[:, 0:D_MODEL] = u2

    lt = lax.dot_general(wr_ref[...], u2.astype(BF16), (((1,), (1,)), ((), ())),
                         preferred_element_type=F32) + br_ref[:, 0:1]
    lg = [lt[j:j + 1, :] for j in range(N_GROUPS + N_GROUPS * N_EXP)]
    best = lg[0]
    grp = jnp.zeros((1, tm), I32)
    for j in range(1, N_GROUPS):
        c = lg[j] > best
        grp = jnp.where(c, j, grp)
        best = jnp.where(c, lg[j], best)
    denom = jnp.exp(lg[0] - best)
    for j in range(1, N_GROUPS):
        denom = denom + jnp.exp(lg[j] - best)
    grp_w = 1.0 / denom
    sel = []
    for e in range(N_EXP):
        val = lg[N_GROUPS + e]
        for gg in range(1, N_GROUPS):
            val = jnp.where(grp == gg, lg[N_GROUPS + gg * N_EXP + e], val)
        sel.append(val)
    v1 = sel[0]
    i1 = jnp.zeros((1, tm), I32)
    for e in range(1, N_EXP):
        c = sel[e] > v1
        i1 = jnp.where(c, e, i1)
        v1 = jnp.where(c, sel[e], v1)
    v2 = jnp.full((1, tm), -jnp.inf, F32)
    i2 = jnp.zeros((1, tm), I32)
    for e in range(N_EXP):
        cand = jnp.where(i1 == e, -jnp.inf, sel[e])
        c = cand > v2
        i2 = jnp.where(c, e, i2)
        v2 = jnp.where(c, cand, v2)
    t = jnp.exp(v2 - v1)
    w1 = grp_w / (1.0 + t)
    w2 = w1 * t
    first_low = i1 < i2
    ea = jnp.minimum(i1, i2)
    eb = jnp.maximum(i1, i2)
    wa = jnp.where(first_low, w1, w2)
    wb = jnp.where(first_low, w2, w1)
    pair = jnp.where(ea == 0, 0, jnp.where(ea == 1, 3, 5)) + eb - ea - 1
    bucket = grp * N_PAIRS + pair

    onehot = lax.broadcasted_iota(I32, (ROUTE_ROWS, tm), 0) == bucket
    cum = jnp.dot(onehot.astype(BF16), triu[...], preferred_element_type=F32)
    prev = base[:, 0:1]
    rank = jnp.sum(jnp.where(onehot, cum - 1.0 + prev, 0.0), axis=0, keepdims=True)
    new_base = prev + cum[:, tm - 1:tm]
    base[...] = jnp.broadcast_to(new_base, base.shape)
    cnt_ref[...] = jnp.broadcast_to(new_base, cnt_ref.shape)

    zrow = jnp.zeros((1, tm), F32)
    route_ref[...] = jnp.concatenate(
        [bucket.astype(F32), rank, wa, wb, zrow, zrow, zrow, zrow], axis=0)
    wpad = jnp.concatenate([wa, wb, jnp.zeros((LANES - 2, tm), F32)], axis=0)
    u2_ref[:, D_MODEL:] = wpad.T


def _outproj(hn, yc, x2, gt, sc, sh, wo, lg, lb, wr, br, bsz, seq):
    tm = min(TOK_TILE, seq)
    nt = seq // tm
    tok = bsz * seq
    row = lambda i: (i, 0)
    const = lambda i: (0, 0)
    mod = lambda i: (i // nt, 0, 0)
    return pl.pallas_call(
        _outproj_kernel,
        grid=(bsz * nt,),
        in_specs=[
            pl.BlockSpec((tm, M_WIDTH), row),
            pl.BlockSpec((tm, C_WIDTH), row),
            pl.BlockSpec((tm, D_MODEL), row),
            pl.BlockSpec((1, 1, D_MODEL), mod),
            pl.BlockSpec((1, 1, D_MODEL), mod),
            pl.BlockSpec((1, 1, D_MODEL), mod),
            pl.BlockSpec(wo.shape, const),
            pl.BlockSpec(lg.shape, const),
            pl.BlockSpec(lb.shape, const),
            pl.BlockSpec(wr.shape, const),
            pl.BlockSpec(br.shape, const),
        ],
        out_specs=[
            pl.BlockSpec((tm, D_MODEL), row),
            pl.BlockSpec((tm, X_EXT), row),
            pl.BlockSpec((8, tm), lambda i: (0, i)),
            pl.BlockSpec((ROUTE_ROWS, LANES), const),
        ],
        out_shape=[
            jax.ShapeDtypeStruct((tok, D_MODEL), F32),
            jax.ShapeDtypeStruct((tok, X_EXT), F32),
            jax.ShapeDtypeStruct((8, tok), F32),
            jax.ShapeDtypeStruct((ROUTE_ROWS, LANES), F32),
        ],
        scratch_shapes=[pltpu.VMEM((ROUTE_ROWS, LANES), F32), pltpu.VMEM((tm, tm), BF16)],
        compiler_params=_cparams(("arbitrary",)),
        name="outproj",
    )(hn, yc, x2, gt, sc, sh, wo, lg, lb, wr, br)


def _dispatch_kernel(pos_ref, u_ref, init_ref, xs_ref, sem):
    del init_ref
    ts = u_ref.shape[0]
    t0 = pl.program_id(0) * ts

    def issue(r, carry):
        pltpu.make_async_copy(u_ref.at[pl.ds(r, 1), :],
                              xs_ref.at[pl.ds(pos_ref[t0 + r], 1), :], sem).start()
        return carry

    lax.fori_loop(0, ts, issue, 0)
    pltpu.make_async_copy(u_ref, xs_ref.at[pl.ds(0, ts), :], sem).wait()


def _dispatch(pos, u2, n_rows, seq):
    ts = min(TOK_TILE, seq)
    tok = u2.shape[0]
    init = jnp.zeros((n_rows, X_EXT), F32)
    return pl.pallas_call(
        _dispatch_kernel,
        grid_spec=pltpu.PrefetchScalarGridSpec(
            num_scalar_prefetch=1,
            grid=(tok // ts,),
            in_specs=[pl.BlockSpec((ts, X_EXT), lambda i, pos: (i, 0)),
                      pl.BlockSpec(memory_space=pl.ANY)],
            out_specs=pl.BlockSpec(memory_space=pl.ANY),
            scratch_shapes=[pltpu.SemaphoreType.DMA(())],
        ),
        out_shape=jax.ShapeDtypeStruct((n_rows, X_EXT), F32),
        input_output_aliases={2: 0},
        compiler_params=_cparams(("arbitrary",)),
        name="dispatch",
    )(pos, u2, init)


def _moe_kernel(e1_ref, e2_ref, valid_ref, fresh_ref, xs_ref, wg1, wu1, wd1, wg2, wu2, wd2,
                y_ref, wgs, wus, wds):
    j = pl.program_id(0)

    @pl.when(fresh_ref[j] == 1)
    def _():
        wgs[0] = wg1[0].astype(BF16)
        wus[0] = wu1[0].astype(BF16)
        wds[0] = wd1[0].astype(BF16)
        wgs[1] = wg2[0].astype(BF16)
        wus[1] = wu2[0].astype(BF16)
        wds[1] = wd2[0].astype(BF16)

    @pl.when(valid_ref[j] == 1)
    def _():
        x = xs_ref[:, 0:D_MODEL].astype(BF16)
        acc = None
        for slot in range(2):
            wcol = xs_ref[:, D_MODEL + slot:D_MODEL + slot + 1]
            g = jnp.dot(x, wgs[slot], preferred_element_type=F32)
            u = jnp.dot(x, wus[slot], preferred_element_type=F32)
            hid = (g * jax.nn.sigmoid(g) * u * wcol).astype(BF16)
            y = jnp.dot(hid, wds[slot], preferred_element_type=F32)
            acc = y if acc is None else acc + y
        y_ref[...] = acc

    @pl.when(valid_ref[j] == 0)
    def _():
        y_ref[...] = jnp.zeros(y_ref.shape, F32)


def _moe(e1, e2, valid, fresh, xs, w_gate, w_up, w_down):
    n_rows = xs.shape[0]
    n_tiles = n_rows // MOE_TILE
    wsel1 = lambda j, e1, e2, v, f: (e1[j], 0, 0)
    wsel2 = lambda j, e1, e2, v, f: (e2[j], 0, 0)
    up_spec = lambda sel: pl.BlockSpec((1, D_MODEL, D_EXPERT), sel)
    dn_spec = lambda sel: pl.BlockSpec((1, D_EXPERT, D_MODEL), sel)
    return pl.pallas_call(
        _moe_kernel,
        grid_spec=pltpu.PrefetchScalarGridSpec(
            num_scalar_prefetch=4,
            grid=(n_tiles,),
            in_specs=[pl.BlockSpec((MOE_TILE, X_EXT), lambda j, *_: (j, 0)),
                      up_spec(wsel1), up_spec(wsel1), dn_spec(wsel1),
                      up_spec(wsel2), up_spec(wsel2), dn_spec(wsel2)],
            out_specs=pl.BlockSpec((MOE_TILE, D_MODEL), lambda j, *_: (j, 0)),
            scratch_shapes=[pltpu.VMEM((2, D_MODEL, D_EXPERT), BF16),
                            pltpu.VMEM((2, D_MODEL, D_EXPERT), BF16),
                            pltpu.VMEM((2, D_EXPERT, D_MODEL), BF16)],
        ),
        out_shape=jax.ShapeDtypeStruct((n_rows, D_MODEL), F32),
        compiler_params=_cparams(("arbitrary",)),
        name="moe",
    )(e1, e2, valid, fresh, xs, w_gate, w_up, w_down, w_gate, w_up, w_down)


def _combine_kernel(pos_ref, ys_ref, x1_ref, gt_ref, lg_ref, lb_ref, out_ref, ybuf, sem):
    ts = x1_ref.shape[0]
    t0 = pl.program_id(0) * ts

    def issue(r, carry):
        pltpu.make_async_copy(ys_ref.at[pl.ds(pos_ref[t0 + r], 1), :],
                              ybuf.at[pl.ds(r, 1), :], sem).start()
        return carry

    lax.fori_loop(0, ts, issue, 0)
    pltpu.make_async_copy(ys_ref.at[pl.ds(0, ts), :], ybuf, sem).wait()
    out_ref[...] = _layer_norm(ALPHA * x1_ref[...] + (1.0 + gt_ref[0]) * ybuf[...],
                               lg_ref[...], lb_ref[...])


def _combine(pos, ys, x1, gt, lg, lb, bsz, seq):
    ts = min(TOK_TILE, seq)
    nt = seq // ts
    tok = bsz * seq
    return pl.pallas_call(
        _combine_kernel,
        grid_spec=pltpu.PrefetchScalarGridSpec(
            num_scalar_prefetch=1,
            grid=(tok // ts,),
            in_specs=[pl.BlockSpec(memory_space=pl.ANY),
                      pl.BlockSpec((ts, D_MODEL), lambda i, pos: (i, 0)),
                      pl.BlockSpec((1, 1, D_MODEL), lambda i, pos: (i // nt, 0, 0)),
                      pl.BlockSpec((1, D_MODEL), lambda i, pos: (0, 0)),
                      pl.BlockSpec((1, D_MODEL), lambda i, pos: (0, 0))],
            out_specs=pl.BlockSpec((ts, D_MODEL), lambda i, pos: (i, 0)),
            scratch_shapes=[pltpu.VMEM((ts, D_MODEL), F32), pltpu.SemaphoreType.DMA(())],
        ),
        out_shape=jax.ShapeDtypeStruct((tok, D_MODEL), F32),
        compiler_params=_cparams(("arbitrary",)),
        name="combine",
    )(pos, ys, x1, gt, lg, lb)


def _route_plan(route, cnt, n_tiles):
    bucket = route[0].astype(I32)
    rank = route[1].astype(I32)
    counts = cnt[:N_BUCKETS, 0].astype(I32)
    tiles_b = (counts + MOE_TILE - 1) // MOE_TILE
    tile_end = jnp.cumsum(tiles_b)
    row_off = (tile_end - tiles_b) * MOE_TILE
    pos = row_off[bucket] + rank
    total = tile_end[-1]
    j = jnp.arange(n_tiles, dtype=I32)
    valid = j < total
    tb = jnp.searchsorted(tile_end, jnp.minimum(j, total - 1), side="right").astype(I32)
    tb = jnp.clip(tb, 0, N_BUCKETS - 1)
    grp = tb // N_PAIRS
    pair = tb % N_PAIRS
    e1 = grp * N_EXP + jnp.asarray(_PAIR_A, I32)[pair]
    e2 = grp * N_EXP + jnp.asarray(_PAIR_B, I32)[pair]
    fresh = jnp.concatenate([jnp.ones((1,), I32), (tb[1:] != tb[:-1]).astype(I32)])
    return pos, e1, e2, valid.astype(I32), fresh


def kernel(x, c, w_ada, b_ada, w_in, b_gates, mh_norm_w, w_conv, w_out, ln1_g, ln1_b,
           w_grp, b_grp, w_router, b_router, w_gate, w_up, w_down, ln2_g, ln2_b):
    bsz, seq, _ = x.shape
    tok = bsz * seq
    n_rows = tok + N_BUCKETS * MOE_TILE
    n_tiles = n_rows // MOE_TILE
    n_exp_total = N_GROUPS * N_EXP

    mod = _ada_mod(c, w_ada, b_ada).reshape(DEPTH, bsz, 6, 1, D_MODEL)
    x2 = x.reshape(tok, D_MODEL)
    for l in range(DEPTH):
        sh1, sc1, gt1, sh2, sc2, gt2 = [mod[l, :, i] for i in range(6)]
        wl = w_in[l]
        wq = wl[:, :4 * M_WIDTH].astype(BF16)
        wg = jnp.pad(wl[:, 4 * M_WIDTH:4 * M_WIDTH + 2 * HEADS],
                     ((0, 0), (0, LANES - 2 * HEADS))).astype(BF16)
        wc = wl[:, 4 * M_WIDTH + 2 * HEADS:].astype(BF16)
        bg = jnp.pad(b_gates[l], (0, LANES - 2 * HEADS)).reshape(1, LANES)
        qkvo, gcol, grow, yc = _inproj(x2, sc1, sh1, wq, wg, wc, bg, w_conv[l], bsz, seq)
        hn = _mlstm(qkvo, gcol, grow, mh_norm_w[l].reshape(1, M_WIDTH), bsz, seq)

        n_logit = N_GROUPS + n_exp_total
        wr = jnp.pad(jnp.concatenate([w_grp[l], w_router[l]], axis=1).T,
                     ((0, ROUTE_ROWS - n_logit), (0, 0))).astype(BF16)
        br = jnp.pad(jnp.concatenate([b_grp[l], b_router[l]]), (0, ROUTE_ROWS - n_logit))
        br = jnp.broadcast_to(br[:, None], (ROUTE_ROWS, LANES))
        x1, u2, route, cnt = _outproj(hn, yc, x2, gt1, sc2, sh2, w_out[l].astype(BF16),
                                      ln1_g[l].reshape(1, D_MODEL), ln1_b[l].reshape(1, D_MODEL),
                                      wr, br, bsz, seq)
        pos, e1, e2, valid, fresh = _route_plan(route, cnt, n_tiles)
        xs = _dispatch(pos, u2, n_rows, seq)
        ys = _moe(e1, e2, valid, fresh, xs,
                  w_gate[l].reshape(n_exp_total, D_MODEL, D_EXPERT),
                  w_up[l].reshape(n_exp_total, D_MODEL, D_EXPERT),
                  w_down[l].reshape(n_exp_total, D_EXPERT, D_MODEL))
        x2 = _combine(pos, ys, x1, gt2, ln2_g[l].reshape(1, D_MODEL),
                      ln2_b[l].reshape(1, D_MODEL), bsz, seq)
    return x2.reshape(bsz, seq, D_MODEL)
```

```python
import math

import jax
import jax.numpy as jnp
from jax import lax
from jax.experimental import pallas as pl
from jax.experimental.pallas import tpu as pltpu

F32 = jnp.float32
BF16 = jnp.bfloat16
I32 = jnp.int32

D_MODEL = 1024
DEPTH = 2
HEADS = 4
D_HEAD = 128
M_WIDTH = HEADS * D_HEAD
C_WIDTH = D_MODEL - M_WIDTH
N_GROUPS = 4
N_EXP = 4
D_EXPERT = 512
N_PAIRS = 6
N_BUCKETS = N_GROUPS * N_PAIRS
ALPHA = (2 * DEPTH) ** 0.25
LN_EPS = 1e-5
QK_SCALE = D_HEAD ** -0.5
LOG_QK_SCALE = math.log(QK_SCALE)
GATE_ROWS = 24

LANES = 128
ROUTE_ROWS = 32
X_EXT = D_MODEL + LANES

TOK_TILE = 512
CHUNK = 128
MOE_TILE = 256
VMEM_LIMIT = 56 * 1024 * 1024

_PAIR_A = (0, 0, 0, 1, 1, 2)
_PAIR_B = (1, 2, 3, 2, 3, 3)


def _cparams(sem):
    return pltpu.CompilerParams(dimension_semantics=sem, vmem_limit_bytes=VMEM_LIMIT)


def _ada_kernel(c_ref, w_ref, b_ref, o_ref):
    c = c_ref[...]
    cond = c * jax.nn.sigmoid(c)
    o_ref[0] = jnp.dot(cond, w_ref[0], preferred_element_type=F32,
                       precision=lax.Precision.HIGHEST) + b_ref[0]


def _ada_mod(c, w_ada, b_ada):
    bsz = c.shape[0]
    nblk = w_ada.shape[2] // D_MODEL
    return pl.pallas_call(
        _ada_kernel,
        grid=(DEPTH, nblk),
        in_specs=[
            pl.BlockSpec((bsz, D_MODEL), lambda l, j: (0, 0)),
            pl.BlockSpec((1, D_MODEL, D_MODEL), lambda l, j: (l, 0, j)),
            pl.BlockSpec((1, 1, D_MODEL), lambda l, j: (l, 0, j)),
        ],
        out_specs=pl.BlockSpec((1, bsz, D_MODEL), lambda l, j: (l, 0, j)),
        out_shape=jax.ShapeDtypeStruct((DEPTH, bsz, nblk * D_MODEL), F32),
        compiler_params=_cparams(("arbitrary", "arbitrary")),
        name="ada_mod",
    )(c, w_ada, b_ada.reshape(DEPTH, 1, -1))


def _log_sigmoid(x):
    return jnp.minimum(x, 0.0) - jnp.log1p(jnp.exp(-jnp.abs(x)))


def _chunk_scan(x, op, identity):
    lane = lax.broadcasted_iota(I32, x.shape, 1) & (CHUNK - 1)
    sh = 1
    while sh < CHUNK:
        x = op(x, jnp.where(lane >= sh, pltpu.roll(x, sh, 1), identity))
        sh *= 2
    return x


def _inproj_kernel(x_ref, sc_ref, sh_ref, wq_ref, wg_ref, wc_ref, bg_ref, wconv_ref,
                   qkvo_ref, grow_ref, yc_ref, zbuf):
    tm = x_ref.shape[0]
    u = (x_ref[...] * (1.0 + sc_ref[0]) + sh_ref[0]).astype(BF16)
    qkvo_ref[...] = jnp.dot(u, wq_ref[...], preferred_element_type=F32).astype(BF16)
    g = jnp.dot(u, wg_ref[...], preferred_element_type=F32) + bg_ref[...]
    gt = g.T[:8]
    head_row = lax.broadcasted_iota(I32, gt.shape, 0) < HEADS
    ig = jnp.where(head_row, gt, 0.0)
    logf = jnp.where(head_row, _log_sigmoid(pltpu.roll(gt, HEADS, 0)), 0.0)
    bcum = _chunk_scan(logf, jnp.add, 0.0)
    ug = ig - bcum
    grow_ref[0:8, :] = ug
    grow_ref[8:16, :] = bcum
    grow_ref[16:24, :] = _chunk_scan(ug, jnp.maximum, -jnp.inf)
    pc = jnp.dot(u, wc_ref[...], preferred_element_type=F32)
    z = pc[:, C_WIDTH:2 * C_WIDTH] * pc[:, 2 * C_WIDTH:]

    @pl.when(pl.program_id(1) == 0)
    def _():
        zbuf[0:8, :] = jnp.zeros((8, C_WIDTH), F32)

    zbuf[8:8 + tm, :] = z
    zc = (wconv_ref[0:1, :] * zbuf[6:6 + tm, :] + wconv_ref[1:2, :] * zbuf[7:7 + tm, :]
          + wconv_ref[2:3, :] * z)
    yc_ref[...] = (pc[:, :C_WIDTH] * zc).astype(BF16)
    zbuf[0:8, :] = zbuf[tm:tm + 8, :]


def _inproj(x2, sc, sh, wq, wg, wc, bg, wconv, bsz, seq):
    tm = min(TOK_TILE, seq)
    nt = seq // tm
    tok = bsz * seq
    row = lambda b, s: (b * nt + s, 0)
    const = lambda b, s: (0, 0)
    mod = lambda b, s: (b, 0, 0)
    return pl.pallas_call(
        _inproj_kernel,
        grid=(bsz, nt),
        in_specs=[
            pl.BlockSpec((tm, D_MODEL), row),
            pl.BlockSpec((1, 1, D_MODEL), mod),
            pl.BlockSpec((1, 1, D_MODEL), mod),
            pl.BlockSpec(wq.shape, const),
            pl.BlockSpec(wg.shape, const),
            pl.BlockSpec(wc.shape, const),
            pl.BlockSpec(bg.shape, const),
            pl.BlockSpec(wconv.shape, const),
        ],
        out_specs=[
            pl.BlockSpec((tm, 4 * M_WIDTH), row),
            pl.BlockSpec((GATE_ROWS, tm), lambda b, s: (0, b * nt + s)),
            pl.BlockSpec((tm, C_WIDTH), row),
        ],
        out_shape=[
            jax.ShapeDtypeStruct((tok, 4 * M_WIDTH), BF16),
            jax.ShapeDtypeStruct((GATE_ROWS, tok), F32),
            jax.ShapeDtypeStruct((tok, C_WIDTH), BF16),
        ],
        scratch_shapes=[pltpu.VMEM((tm + 8, C_WIDTH), F32)],
        compiler_params=_cparams(("arbitrary", "arbitrary")),
        name="inproj",
    )(x2, sc, sh, wq, wg, wc, bg, wconv)


N_ROWQ = 4
SPLIT = 3


def _split3(x):
    hi = x.astype(BF16).astype(F32)
    r1 = x - hi
    mid = r1.astype(BF16).astype(F32)
    return [hi, mid, r1 - mid]


def _mlstm_kernel(qkvo_ref, grow_ref, nw_ref, out_ref, cst, mst, sel):
    sb = qkvo_ref.shape[0]
    L = CHUNK

    @pl.when(pl.program_id(1) == 0)
    def _():
        cst[...] = jnp.zeros(cst.shape, F32)
        mst[...] = jnp.zeros(mst.shape, F32)

    @pl.when((pl.program_id(0) == 0) & (pl.program_id(1) == 0))
    def _():
        r = lax.broadcasted_iota(I32, sel.shape, 0)
        c = lax.broadcasted_iota(I32, sel.shape, 1) // LANES
        rq = r // 8
        quantity = ((rq >= SPLIT).astype(I32) + (rq >= 2 * SPLIT).astype(I32)
                    + (rq >= 3 * SPLIT).astype(I32))
        hit = ((r % 8) == c // N_ROWQ) & (quantity == c % N_ROWQ) & (rq < N_ROWQ * SPLIT)
        sel[...] = hit.astype(BF16)

    causal = lax.broadcasted_iota(I32, (L, L), 0) >= lax.broadcasted_iota(I32, (L, L), 1)
    ones_ext = jnp.ones((L, D_HEAD), BF16)
    pad_rows = jnp.zeros((LANES - 8 * N_ROWQ * SPLIT, L), F32)
    eye = (lax.broadcasted_iota(I32, (L, L), 0) == lax.broadcasted_iota(I32, (L, L), 1)).astype(BF16)
    n_chunks = sb // L
    pairs = [(c, h) for c in range(n_chunks) for h in range(HEADS)]

    def cols(part, c, h):
        c0 = part * M_WIDTH + h * D_HEAD
        return qkvo_ref[c * L:(c + 1) * L, c0:c0 + D_HEAD]

    m_prev = mst[...]
    stack_t, us, decay = [], [], []
    for c in range(n_chunks):
        ug = grow_ref[0:8, c * L:(c + 1) * L]
        bcum = grow_ref[8:16, c * L:(c + 1) * L]
        cmax = grow_ref[16:24, c * L:(c + 1) * L]
        big_m = jnp.maximum(m_prev, cmax)
        m_last = jnp.broadcast_to(big_m[:, L - 1:L], (8, L))
        g_tot = jnp.broadcast_to(bcum[:, L - 1:L], (8, L))
        wq = jnp.exp(m_prev - big_m) * QK_SCALE
        log_em = -(bcum + big_m)
        wk = jnp.exp(ug - m_last)
        decay.append(jnp.exp(m_prev - m_last))
        us.append(ug + LOG_QK_SCALE)
        m_prev = g_tot + m_last
        stack = jnp.concatenate(_split3(big_m) + _split3(wq) + _split3(log_em) + _split3(wk)
                                + [pad_rows], axis=0)
        stack_t.append(stack.T.astype(BF16))
    mst[...] = m_prev

    def rep(c, h, j):
        c0 = (h * N_ROWQ + j) * LANES
        return jnp.dot(stack_t[c], sel[:, c0:c0 + LANES], preferred_element_type=F32)

    s_mat = {ch: lax.dot_general(cols(0, *ch), cols(1, *ch), (((1,), (1,)), ((), ())),
                                 preferred_element_type=F32) for ch in pairs}
    kw = {ch: (cols(1, *ch).astype(F32) * rep(*ch, 3)).astype(BF16) for ch in pairs}
    kw_t = {ch: lax.dot_general(eye, kw[ch], (((1,), (1,)), ((), ())),
                                preferred_element_type=F32).astype(BF16) for ch in pairs}
    lhs = {}
    for ch in pairs:
        c, h = ch
        p = jnp.where(causal, s_mat[ch] * jnp.exp(us[c][h:h + 1, :] - rep(c, h, 0)), 0.0)
        qw = cols(0, c, h).astype(F32) * rep(c, h, 1)
        lhs[ch] = jnp.concatenate([p.astype(BF16), qw.astype(BF16)], axis=1)
    vext = {ch: jnp.concatenate([cols(2, *ch), ones_ext], axis=1) for ch in pairs}
    upd = {ch: jnp.dot(kw_t[ch], vext[ch], preferred_element_type=F32) for ch in pairs}
    tot = {}
    for c in range(n_chunks):
        for h in range(HEADS):
            state = cst[h]
            rhs = jnp.concatenate([vext[(c, h)], state.astype(BF16)], axis=0)
            tot[(c, h)] = jnp.dot(lhs[(c, h)], rhs, preferred_element_type=F32)
            dec = jnp.concatenate([decay[c][h:h + 1, :], decay[c][h:h + 1, :]], axis=1)
            cst[h] = dec * state + upd[(c, h)]
    for ch in pairs:
        c, h = ch
        c0 = h * D_HEAD
        num, den = tot[ch][:, :D_HEAD], tot[ch][:, D_HEAD:]
        hh = num / jnp.maximum(jnp.abs(den), jnp.exp(rep(c, h, 2)))
        mu = jnp.mean(hh, axis=1, keepdims=True)
        var = jnp.mean(jnp.square(hh - mu), axis=1, keepdims=True)
        hn = (hh - mu) * lax.rsqrt(var + LN_EPS) * nw_ref[:, c0:c0 + D_HEAD]
        og = jax.nn.sigmoid(cols(3, c, h).astype(F32))
        out_ref[c * L:(c + 1) * L, c0:c0 + D_HEAD] = (hn * og).astype(BF16)


def _mlstm(qkvo, grow, nw, bsz, seq):
    sb = min(TOK_TILE, seq)
    nt = seq // sb
    tok = bsz * seq
    row = lambda b, s: (b * nt + s, 0)
    return pl.pallas_call(
        _mlstm_kernel,
        grid=(bsz, nt),
        in_specs=[
            pl.BlockSpec((sb, 4 * M_WIDTH), row),
            pl.BlockSpec((GATE_ROWS, sb), lambda b, s: (0, b * nt + s)),
            pl.BlockSpec((1, M_WIDTH), lambda b, s: (0, 0)),
        ],
        out_specs=pl.BlockSpec((sb, M_WIDTH), row),
        out_shape=jax.ShapeDtypeStruct((tok, M_WIDTH), BF16),
        scratch_shapes=[pltpu.VMEM((HEADS, D_HEAD, 2 * D_HEAD), F32),
                        pltpu.VMEM((8, CHUNK), F32),
                        pltpu.VMEM((LANES, HEADS * N_ROWQ * LANES), BF16)],
        compiler_params=_cparams(("arbitrary", "arbitrary")),
        name="mlstm",
    )(qkvo, grow, nw)


def _layer_norm(r, g, b):
    mu = jnp.mean(r, axis=-1, keepdims=True)
    var = jnp.mean(jnp.square(r - mu), axis=-1, keepdims=True)
    return (r - mu) * lax.rsqrt(var + LN_EPS) * g + b


def _outproj_kernel(hn_ref, yc_ref, x_ref, gt_ref, sc_ref, sh_ref, wo_ref, lg_ref, lb_ref,
                    wr_ref, br_ref, x1_ref, u2_ref, route_ref, cnt_ref, base, triu):
    tm = x_ref.shape[0]

    @pl.when(pl.program_id(0) == 0)
    def _():
        base[...] = jnp.zeros(base.shape, F32)
        rid = lax.broadcasted_iota(I32, (tm, tm), 0)
        cid = lax.broadcasted_iota(I32, (tm, tm), 1)
        triu[...] = (rid <= cid).astype(BF16)

    mix = (jnp.dot(hn_ref[...], wo_ref[0:M_WIDTH, :], preferred_element_type=F32)
           + jnp.dot(yc_ref[...], wo_ref[M_WIDTH:, :], preferred_element_type=F32))
    x1 = _layer_norm(ALPHA * x_ref[...] + (1.0 + gt_ref[0]) * mix, lg_ref[...], lb_ref[...])
    x1_ref[...] = x1
    u2 = x1 * (1.0 + sc_ref[0]) + sh_ref[0]
    u2_ref[:, 0:D_MODEL] = u2

    lt = lax.dot_general(wr_ref[...], u2.astype(BF16), (((1,), (1,)), ((), ())),
                         preferred_element_type=F32) + br_ref[:, 0:1]
    lg = [lt[j:j + 1, :] for j in range(N_GROUPS + N_GROUPS * N_EXP)]
    best = lg[0]
    grp = jnp.zeros((1, tm), I32)
    for j in range(1, N_GROUPS):
        c = lg[j] > best
        grp = jnp.where(c, j, grp)
        best = jnp.where(c, lg[j], best)
    denom = jnp.exp(lg[0] - best)
    for j in range(1, N_GROUPS):
        denom = denom + jnp.exp(lg[j] - best)
    grp_w = 1.0 / denom
    sel = []
    for e in range(N_EXP):
        val = lg[N_GROUPS + e]
        for gg in range(1, N_GROUPS):
            val = jnp.where(grp == gg, lg[N_GROUPS + gg * N_EXP + e], val)
        sel.append(val)
    v1 = sel[0]
    i1 = jnp.zeros((1, tm), I32)
    for e in range(1, N_EXP):
        c = sel[e] > v1
        i1 = jnp.where(c, e, i1)
        v1 = jnp.where(c, sel[e], v1)
    v2 = jnp.full((1, tm), -jnp.inf, F32)
    i2 = jnp.zeros((1, tm), I32)
    for e in range(N_EXP):
        cand = jnp.where(i1 == e, -jnp.inf, sel[e])
        c = cand > v2
        i2 = jnp.where(c, e, i2)
        v2 = jnp.where(c, cand, v2)
    t = jnp.exp(v2 - v1)
    w1 = grp_w / (1.0 + t)
    w2 = w1 * t
    first_low = i1 < i2
    ea = jnp.minimum(i1, i2)
    eb = jnp.maximum(i1, i2)
    wa = jnp.where(first_low, w1, w2)
    wb = jnp.where(first_low, w2, w1)
    pair = jnp.where(ea == 0, 0, jnp.where(ea == 1, 3, 5)) + eb - ea - 1
    bucket = grp * N_PAIRS + pair

    onehot = lax.broadcasted_iota(I32, (ROUTE_ROWS, tm), 0) == bucket
    cum = jnp.dot(onehot.astype(BF16), triu[...], preferred_element_type=F32)
    prev = base[:, 0:1]
    rank = jnp.sum(jnp.where(onehot, cum - 1.0 + prev, 0.0), axis=0, keepdims=True)
    new_base = prev + cum[:, tm - 1:tm]
    base[...] = jnp.broadcast_to(new_base, base.shape)
    cnt_ref[...] = jnp.broadcast_to(new_base, cnt_ref.shape)

    zrow = jnp.zeros((1, tm), F32)
    route_ref[...] = jnp.concatenate(
        [bucket.astype(F32), rank, wa, wb, zrow, zrow, zrow, zrow], axis=0)
    wpad = jnp.concatenate([wa, wb, jnp.zeros((LANES - 2, tm), F32)], axis=0)
    u2_ref[:, D_MODEL:] = wpad.T


def _outproj(hn, yc, x2, gt, sc, sh, wo, lg, lb, wr, br, bsz, seq):
    tm = min(TOK_TILE, seq)
    nt = seq // tm
    tok = bsz * seq
    row = lambda i: (i, 0)
    const = lambda i: (0, 0)
    mod = lambda i: (i // nt, 0, 0)
    return pl.pallas_call(
        _outproj_kernel,
        grid=(bsz * nt,),
        in_specs=[
            pl.BlockSpec((tm, M_WIDTH), row),
            pl.BlockSpec((tm, C_WIDTH), row),
            pl.BlockSpec((tm, D_MODEL), row),
            pl.BlockSpec((1, 1, D_MODEL), mod),
            pl.BlockSpec((1, 1, D_MODEL), mod),
            pl.BlockSpec((1, 1, D_MODEL), mod),
            pl.BlockSpec(wo.shape, const),
            pl.BlockSpec(lg.shape, const),
            pl.BlockSpec(lb.shape, const),
            pl.BlockSpec(wr.shape, const),
            pl.BlockSpec(br.shape, const),
        ],
        out_specs=[
            pl.BlockSpec((tm, D_MODEL), row),
            pl.BlockSpec((tm, X_EXT), row),
            pl.BlockSpec((8, tm), lambda i: (0, i)),
            pl.BlockSpec((ROUTE_ROWS, LANES), const),
        ],
        out_shape=[
            jax.ShapeDtypeStruct((tok, D_MODEL), F32),
            jax.ShapeDtypeStruct((tok, X_EXT), F32),
            jax.ShapeDtypeStruct((8, tok), F32),
            jax.ShapeDtypeStruct((ROUTE_ROWS, LANES), F32),
        ],
        scratch_shapes=[pltpu.VMEM((ROUTE_ROWS, LANES), F32), pltpu.VMEM((tm, tm), BF16)],
        compiler_params=_cparams(("arbitrary",)),
        name="outproj",
    )(hn, yc, x2, gt, sc, sh, wo, lg, lb, wr, br)


def _dispatch_kernel(pos_ref, u_ref, init_ref, xs_ref, sem):
    del init_ref
    ts = u_ref.shape[0]
    t0 = pl.program_id(0) * ts

    def issue(r, carry):
        pltpu.make_async_copy(u_ref.at[pl.ds(r, 1), :],
                              xs_ref.at[pl.ds(pos_ref[t0 + r], 1), :], sem).start()
        return carry

    lax.fori_loop(0, ts, issue, 0)
    pltpu.make_async_copy(u_ref, xs_ref.at[pl.ds(0, ts), :], sem).wait()


def _dispatch(pos, u2, n_rows, seq):
    ts = min(TOK_TILE, seq)
    tok = u2.shape[0]
    init = jnp.zeros((n_rows, X_EXT), F32)
    return pl.pallas_call(
        _dispatch_kernel,
        grid_spec=pltpu.PrefetchScalarGridSpec(
            num_scalar_prefetch=1,
            grid=(tok // ts,),
            in_specs=[pl.BlockSpec((ts, X_EXT), lambda i, pos: (i, 0)),
                      pl.BlockSpec(memory_space=pl.ANY)],
            out_specs=pl.BlockSpec(memory_space=pl.ANY),
            scratch_shapes=[pltpu.SemaphoreType.DMA(())],
        ),
        out_shape=jax.ShapeDtypeStruct((n_rows, X_EXT), F32),
        input_output_aliases={2: 0},
        compiler_params=_cparams(("arbitrary",)),
        name="dispatch",
    )(pos, u2, init)


def _moe_kernel(e1_ref, e2_ref, valid_ref, fresh_ref, xs_ref, wg1, wu1, wd1, wg2, wu2, wd2,
                y_ref, wgs, wus, wds):
    j = pl.program_id(0)

    @pl.when(fresh_ref[j] == 1)
    def _():
        wgs[0] = wg1[0].astype(BF16)
        wus[0] = wu1[0].astype(BF16)
        wds[0] = wd1[0].astype(BF16)
        wgs[1] = wg2[0].astype(BF16)
        wus[1] = wu2[0].astype(BF16)
        wds[1] = wd2[0].astype(BF16)

    @pl.when(valid_ref[j] == 1)
    def _():
        x = xs_ref[:, 0:D_MODEL].astype(BF16)
        acc = None
        for slot in range(2):
            wcol = xs_ref[:, D_MODEL + slot:D_MODEL + slot + 1]
            g = jnp.dot(x, wgs[slot], preferred_element_type=F32)
            u = jnp.dot(x, wus[slot], preferred_element_type=F32)
            hid = (g * jax.nn.sigmoid(g) * u * wcol).astype(BF16)
            y = jnp.dot(hid, wds[slot], preferred_element_type=F32)
            acc = y if acc is None else acc + y
        y_ref[...] = acc

    @pl.when(valid_ref[j] == 0)
    def _():
        y_ref[...] = jnp.zeros(y_ref.shape, F32)


def _moe(e1, e2, valid, fresh, xs, w_gate, w_up, w_down):
    n_rows = xs.shape[0]
    n_tiles = n_rows // MOE_TILE
    wsel1 = lambda j, e1, e2, v, f: (e1[j], 0, 0)
    wsel2 = lambda j, e1, e2, v, f: (e2[j], 0, 0)
    up_spec = lambda sel: pl.BlockSpec((1, D_MODEL, D_EXPERT), sel)
    dn_spec = lambda sel: pl.BlockSpec((1, D_EXPERT, D_MODEL), sel)
    return pl.pallas_call(
        _moe_kernel,
        grid_spec=pltpu.PrefetchScalarGridSpec(
            num_scalar_prefetch=4,
            grid=(n_tiles,),
            in_specs=[pl.BlockSpec((MOE_TILE, X_EXT), lambda j, *_: (j, 0)),
                      up_spec(wsel1), up_spec(wsel1), dn_spec(wsel1),
                      up_spec(wsel2), up_spec(wsel2), dn_spec(wsel2)],
            out_specs=pl.BlockSpec((MOE_TILE, D_MODEL), lambda j, *_: (j, 0)),
            scratch_shapes=[pltpu.VMEM((2, D_MODEL, D_EXPERT), BF16),
                            pltpu.VMEM((2, D_MODEL, D_EXPERT), BF16),
                            pltpu.VMEM((2, D_EXPERT, D_MODEL), BF16)],
        ),
        out_shape=jax.ShapeDtypeStruct((n_rows, D_MODEL), F32),
        compiler_params=_cparams(("arbitrary",)),
        name="moe",
    )(e1, e2, valid, fresh, xs, w_gate, w_up, w_down, w_gate, w_up, w_down)


def _combine_kernel(pos_ref, ys_ref, x1_ref, gt_ref, lg_ref, lb_ref, out_ref, ybuf, sem):
    ts = x1_ref.shape[0]
    t0 = pl.program_id(0) * ts

    def issue(r, carry):
        pltpu.make_async_copy(ys_ref.at[pl.ds(pos_ref[t0 + r], 1), :],
                              ybuf.at[pl.ds(r, 1), :], sem).start()
        return carry

    lax.fori_loop(0, ts, issue, 0)
    pltpu.make_async_copy(ys_ref.at[pl.ds(0, ts), :], ybuf, sem).wait()
    out_ref[...] = _layer_norm(ALPHA * x1_ref[...] + (1.0 + gt_ref[0]) * ybuf[...],
                               lg_ref[...], lb_ref[...])


def _combine(pos, ys, x1, gt, lg, lb, bsz, seq):
    ts = min(TOK_TILE, seq)
    nt = seq // ts
    tok = bsz * seq
    return pl.pallas_call(
        _combine_kernel,
        grid_spec=pltpu.PrefetchScalarGridSpec(
            num_scalar_prefetch=1,
            grid=(tok // ts,),
            in_specs=[pl.BlockSpec(memory_space=pl.ANY),
                      pl.BlockSpec((ts, D_MODEL), lambda i, pos: (i, 0)),
                      pl.BlockSpec((1, 1, D_MODEL), lambda i, pos: (i // nt, 0, 0)),
                      pl.BlockSpec((1, D_MODEL), lambda i, pos: (0, 0)),
                      pl.BlockSpec((1, D_MODEL), lambda i, pos: (0, 0))],
            out_specs=pl.BlockSpec((ts, D_MODEL), lambda i, pos: (i, 0)),
            scratch_shapes=[pltpu.VMEM((ts, D_MODEL), F32), pltpu.SemaphoreType.DMA(())],
        ),
        out_shape=jax.ShapeDtypeStruct((tok, D_MODEL), F32),
        compiler_params=_cparams(("arbitrary",)),
        name="combine",
    )(pos, ys, x1, gt, lg, lb)


def _route_plan(route, cnt, n_tiles):
    bucket = route[0].astype(I32)
    rank = route[1].astype(I32)
    counts = cnt[:N_BUCKETS, 0].astype(I32)
    tiles_b = (counts + MOE_TILE - 1) // MOE_TILE
    tile_end = jnp.cumsum(tiles_b)
    row_off = (tile_end - tiles_b) * MOE_TILE
    bucket_ids = jnp.arange(N_BUCKETS, dtype=I32)
    pos = jnp.sum(jnp.where(bucket[None, :] == bucket_ids[:, None], row_off[:, None], 0), axis=0) + rank
    total = tile_end[-1]
    j = jnp.arange(n_tiles, dtype=I32)
    valid = j < total
    tb = jnp.sum((tile_end[None, :] <= jnp.minimum(j, total - 1)[:, None]).astype(I32), axis=1)
    tb = jnp.clip(tb, 0, N_BUCKETS - 1)
    grp = tb // N_PAIRS
    pair = tb % N_PAIRS
    pair_a = sum((pair == i).astype(I32) * _PAIR_A[i] for i in range(N_PAIRS))
    pair_b = sum((pair == i).astype(I32) * _PAIR_B[i] for i in range(N_PAIRS))
    e1 = grp * N_EXP + pair_a
    e2 = grp * N_EXP + pair_b
    fresh = jnp.concatenate([jnp.ones((1,), I32), (tb[1:] != tb[:-1]).astype(I32)])
    return pos, e1, e2, valid.astype(I32), fresh


def kernel(x, c, w_ada, b_ada, w_in, b_gates, mh_norm_w, w_conv, w_out, ln1_g, ln1_b,
           w_grp, b_grp, w_router, b_router, w_gate, w_up, w_down, ln2_g, ln2_b):
    bsz, seq, _ = x.shape
    tok = bsz * seq
    n_rows = tok + N_BUCKETS * MOE_TILE
    n_tiles = n_rows // MOE_TILE
    n_exp_total = N_GROUPS * N_EXP

    mod = _ada_mod(c, w_ada, b_ada).reshape(DEPTH, bsz, 6, 1, D_MODEL)
    x2 = x.reshape(tok, D_MODEL)
    for l in range(DEPTH):
        sh1, sc1, gt1, sh2, sc2, gt2 = [mod[l, :, i] for i in range(6)]
        wl = w_in[l]
        wq = wl[:, :4 * M_WIDTH].astype(BF16)
        wg = jnp.pad(wl[:, 4 * M_WIDTH:4 * M_WIDTH + 2 * HEADS],
                     ((0, 0), (0, LANES - 2 * HEADS))).astype(BF16)
        wc = wl[:, 4 * M_WIDTH + 2 * HEADS:].astype(BF16)
        bg = jnp.pad(b_gates[l], (0, LANES - 2 * HEADS)).reshape(1, LANES)
        qkvo, grow, yc = _inproj(x2, sc1, sh1, wq, wg, wc, bg, w_conv[l], bsz, seq)
        hn = _mlstm(qkvo, grow, mh_norm_w[l].reshape(1, M_WIDTH), bsz, seq)

        n_logit = N_GROUPS + n_exp_total
        wr = jnp.pad(jnp.concatenate([w_grp[l], w_router[l]], axis=1).T,
                     ((0, ROUTE_ROWS - n_logit), (0, 0))).astype(BF16)
        br = jnp.pad(jnp.concatenate([b_grp[l], b_router[l]]), (0, ROUTE_ROWS - n_logit))
        br = jnp.broadcast_to(br[:, None], (ROUTE_ROWS, LANES))
        x1, u2, route, cnt = _outproj(hn, yc, x2, gt1, sc2, sh2, w_out[l].astype(BF16),
                                      ln1_g[l].reshape(1, D_MODEL), ln1_b[l].reshape(1, D_MODEL),
                                      wr, br, bsz, seq)
        pos, e1, e2, valid, fresh = _route_plan(route, cnt, n_tiles)
        xs = _dispatch(pos, u2, n_rows, seq)
        ys = _moe(e1, e2, valid, fresh, xs,
                  w_gate[l].reshape(n_exp_total, D_MODEL, D_EXPERT),
                  w_up[l].reshape(n_exp_total, D_MODEL, D_EXPERT),
                  w_down[l].reshape(n_exp_total, D_EXPERT, D_MODEL))
        x2 = _combine(pos, ys, x1, gt2, ln2_g[l].reshape(1, D_MODEL),
                      ln2_b[l].reshape(1, D_MODEL), bsz, seq)
    return x2.reshape(bsz, seq, D_MODEL)
```

```python
import math

import jax
import jax.numpy as jnp
import numpy as np
from jax import lax
from jax.experimental import pallas as pl
from jax.experimental.pallas import tpu as pltpu

F32 = jnp.float32
BF16 = jnp.bfloat16
I32 = jnp.int32

D_MODEL = 1024
DEPTH = 2
HEADS = 4
D_HEAD = 128
M_WIDTH = HEADS * D_HEAD
C_WIDTH = D_MODEL - M_WIDTH
N_GROUPS = 4
N_EXP = 4
D_EXPERT = 512
N_PAIRS = 6
N_BUCKETS = N_GROUPS * N_PAIRS
ALPHA = (2 * DEPTH) ** 0.25
LN_EPS = 1e-5
QK_SCALE = D_HEAD ** -0.5
LOG_QK_SCALE = math.log(QK_SCALE)
GATE_ROWS = 24

LANES = 128
ROUTE_ROWS = 32

TOK_TILE = 512
CHUNK = 128
MOE_TILE = 256
VMEM_LIMIT = 56 * 1024 * 1024

_PAIR_A = (0, 0, 0, 1, 1, 2)
_PAIR_B = (1, 2, 3, 2, 3, 3)
_BUCKET_EA = np.array([g * N_EXP + _PAIR_A[p] for g in range(N_GROUPS) for p in range(N_PAIRS)])
_BUCKET_EB = np.array([g * N_EXP + _PAIR_B[p] for g in range(N_GROUPS) for p in range(N_PAIRS)])


def _cparams(sem):
    return pltpu.CompilerParams(dimension_semantics=sem, vmem_limit_bytes=VMEM_LIMIT)


def _ada_kernel(c_ref, w_ref, b_ref, o_ref):
    c = c_ref[...]
    cond = c * jax.nn.sigmoid(c)
    o_ref[0] = jnp.dot(cond, w_ref[0], preferred_element_type=F32,
                       precision=lax.Precision.HIGHEST) + b_ref[0]


def _ada_mod(c, w_ada, b_ada):
    bsz = c.shape[0]
    nblk = w_ada.shape[2] // D_MODEL
    return pl.pallas_call(
        _ada_kernel,
        grid=(DEPTH, nblk),
        in_specs=[
            pl.BlockSpec((bsz, D_MODEL), lambda l, j: (0, 0)),
            pl.BlockSpec((1, D_MODEL, D_MODEL), lambda l, j: (l, 0, j)),
            pl.BlockSpec((1, 1, D_MODEL), lambda l, j: (l, 0, j)),
        ],
        out_specs=pl.BlockSpec((1, bsz, D_MODEL), lambda l, j: (l, 0, j)),
        out_shape=jax.ShapeDtypeStruct((DEPTH, bsz, nblk * D_MODEL), F32),
        compiler_params=_cparams(("arbitrary", "arbitrary")),
        name="ada_mod",
    )(c, w_ada, b_ada.reshape(DEPTH, 1, -1))


def _log_sigmoid(x):
    return jnp.minimum(x, 0.0) - jnp.log1p(jnp.exp(-jnp.abs(x)))


def _chunk_scan(x, op, identity):
    lane = lax.broadcasted_iota(I32, x.shape, 1) & (CHUNK - 1)
    sh = 1
    while sh < CHUNK:
        x = op(x, jnp.where(lane >= sh, pltpu.roll(x, sh, 1), identity))
        sh *= 2
    return x


def _inproj_kernel(x_ref, sc_ref, sh_ref, wq_ref, wg_ref, wc_ref, bg_ref, wconv_ref,
                   qkvo_ref, grow_ref, yc_ref, zbuf):
    tm = x_ref.shape[0]
    u = (x_ref[...] * (1.0 + sc_ref[0]) + sh_ref[0]).astype(BF16)
    qkvo_ref[...] = jnp.dot(u, wq_ref[...], preferred_element_type=F32).astype(BF16)
    g = jnp.dot(u, wg_ref[...], preferred_element_type=F32) + bg_ref[...]
    gt = g.T[:8]
    head_row = lax.broadcasted_iota(I32, gt.shape, 0) < HEADS
    ig = jnp.where(head_row, gt, 0.0)
    logf = jnp.where(head_row, _log_sigmoid(pltpu.roll(gt, HEADS, 0)), 0.0)
    bcum = _chunk_scan(logf, jnp.add, 0.0)
    ug = ig - bcum
    grow_ref[0:8, :] = ug
    grow_ref[8:16, :] = bcum
    grow_ref[16:24, :] = _chunk_scan(ug, jnp.maximum, -jnp.inf)
    pc = jnp.dot(u, wc_ref[...], preferred_element_type=F32)
    z = pc[:, C_WIDTH:2 * C_WIDTH] * pc[:, 2 * C_WIDTH:]

    @pl.when(pl.program_id(1) == 0)
    def _():
        zbuf[0:8, :] = jnp.zeros((8, C_WIDTH), F32)

    zbuf[8:8 + tm, :] = z
    zc = (wconv_ref[0:1, :] * zbuf[6:6 + tm, :] + wconv_ref[1:2, :] * zbuf[7:7 + tm, :]
          + wconv_ref[2:3, :] * z)
    yc_ref[...] = (pc[:, :C_WIDTH] * zc).astype(BF16)
    zbuf[0:8, :] = zbuf[tm:tm + 8, :]


def _inproj(x2, sc, sh, wq, wg, wc, bg, wconv, bsz, seq):
    tm = min(TOK_TILE, seq)
    nt = seq // tm
    tok = bsz * seq
    row = lambda b, s: (b * nt + s, 0)
    const = lambda b, s: (0, 0)
    mod = lambda b, s: (b, 0, 0)
    return pl.pallas_call(
        _inproj_kernel,
        grid=(bsz, nt),
        in_specs=[
            pl.BlockSpec((tm, D_MODEL), row),
            pl.BlockSpec((1, 1, D_MODEL), mod),
            pl.BlockSpec((1, 1, D_MODEL), mod),
            pl.BlockSpec(wq.shape, const),
            pl.BlockSpec(wg.shape, const),
            pl.BlockSpec(wc.shape, const),
            pl.BlockSpec(bg.shape, const),
            pl.BlockSpec(wconv.shape, const),
        ],
        out_specs=[
            pl.BlockSpec((tm, 4 * M_WIDTH), row),
            pl.BlockSpec((GATE_ROWS, tm), lambda b, s: (0, b * nt + s)),
            pl.BlockSpec((tm, C_WIDTH), row),
        ],
        out_shape=[
            jax.ShapeDtypeStruct((tok, 4 * M_WIDTH), BF16),
            jax.ShapeDtypeStruct((GATE_ROWS, tok), F32),
            jax.ShapeDtypeStruct((tok, C_WIDTH), BF16),
        ],
        scratch_shapes=[pltpu.VMEM((tm + 8, C_WIDTH), F32)],
        compiler_params=_cparams(("arbitrary", "arbitrary")),
        name="inproj",
    )(x2, sc, sh, wq, wg, wc, bg, wconv)


N_ROWQ = 4
SPLIT = 3


def _split3(x):
    hi = x.astype(BF16).astype(F32)
    r1 = x - hi
    mid = r1.astype(BF16).astype(F32)
    return [hi, mid, r1 - mid]


def _mlstm_kernel(qkvo_ref, grow_ref, nw_ref, out_ref, cst, mst, sel):
    sb = qkvo_ref.shape[0]
    L = CHUNK

    @pl.when(pl.program_id(1) == 0)
    def _():
        cst[...] = jnp.zeros(cst.shape, F32)
        mst[...] = jnp.zeros(mst.shape, F32)

    @pl.when((pl.program_id(0) == 0) & (pl.program_id(1) == 0))
    def _():
        r = lax.broadcasted_iota(I32, sel.shape, 0)
        c = lax.broadcasted_iota(I32, sel.shape, 1) // LANES
        rq = r // 8
        quantity = ((rq >= SPLIT).astype(I32) + (rq >= 2 * SPLIT).astype(I32)
                    + (rq >= 3 * SPLIT).astype(I32))
        hit = ((r % 8) == c // N_ROWQ) & (quantity == c % N_ROWQ) & (rq < N_ROWQ * SPLIT)
        sel[...] = hit.astype(BF16)

    causal = lax.broadcasted_iota(I32, (L, L), 0) >= lax.broadcasted_iota(I32, (L, L), 1)
    ones_ext = jnp.ones((L, D_HEAD), BF16)
    pad_rows = jnp.zeros((LANES - 8 * N_ROWQ * SPLIT, L), F32)
    eye = (lax.broadcasted_iota(I32, (L, L), 0) == lax.broadcasted_iota(I32, (L, L), 1)).astype(BF16)
    n_chunks = sb // L
    pairs = [(c, h) for c in range(n_chunks) for h in range(HEADS)]

    def cols(part, c, h):
        c0 = part * M_WIDTH + h * D_HEAD
        return qkvo_ref[c * L:(c + 1) * L, c0:c0 + D_HEAD]

    m_prev = mst[...]
    stack_t, us, decay = [], [], []
    for c in range(n_chunks):
        ug = grow_ref[0:8, c * L:(c + 1) * L]
        bcum = grow_ref[8:16, c * L:(c + 1) * L]
        cmax = grow_ref[16:24, c * L:(c + 1) * L]
        big_m = jnp.maximum(m_prev, cmax)
        m_last = jnp.broadcast_to(big_m[:, L - 1:L], (8, L))
        g_tot = jnp.broadcast_to(bcum[:, L - 1:L], (8, L))
        wq = jnp.exp(m_prev - big_m) * QK_SCALE
        log_em = -(bcum + big_m)
        wk = jnp.exp(ug - m_last)
        decay.append(jnp.exp(m_prev - m_last))
        us.append(ug + LOG_QK_SCALE)
        m_prev = g_tot + m_last
        stack = jnp.concatenate(_split3(big_m) + _split3(wq) + _split3(log_em) + _split3(wk)
                                + [pad_rows], axis=0)
        stack_t.append(stack.T.astype(BF16))
    mst[...] = m_prev

    def rep(c, h, j):
        c0 = (h * N_ROWQ + j) * LANES
        return jnp.dot(stack_t[c], sel[:, c0:c0 + LANES], preferred_element_type=F32)

    s_mat = {ch: lax.dot_general(cols(0, *ch), cols(1, *ch), (((1,), (1,)), ((), ())),
                                 preferred_element_type=F32) for ch in pairs}
    kw = {ch: (cols(1, *ch).astype(F32) * rep(*ch, 3)).astype(BF16) for ch in pairs}
    kw_t = {ch: lax.dot_general(eye, kw[ch], (((1,), (1,)), ((), ())),
                                preferred_element_type=F32).astype(BF16) for ch in pairs}
    lhs = {}
    for ch in pairs:
        c, h = ch
        p = jnp.where(causal, s_mat[ch] * jnp.exp(us[c][h:h + 1, :] - rep(c, h, 0)), 0.0)
        qw = cols(0, c, h).astype(F32) * rep(c, h, 1)
        lhs[ch] = jnp.concatenate([p.astype(BF16), qw.astype(BF16)], axis=1)
    vext = {ch: jnp.concatenate([cols(2, *ch), ones_ext], axis=1) for ch in pairs}
    upd = {ch: jnp.dot(kw_t[ch], vext[ch], preferred_element_type=F32) for ch in pairs}
    tot = {}
    for c in range(n_chunks):
        for h in range(HEADS):
            state = cst[h]
            rhs = jnp.concatenate([vext[(c, h)], state.astype(BF16)], axis=0)
            tot[(c, h)] = jnp.dot(lhs[(c, h)], rhs, preferred_element_type=F32)
            dec = jnp.concatenate([decay[c][h:h + 1, :], decay[c][h:h + 1, :]], axis=1)
            cst[h] = dec * state + upd[(c, h)]
    for ch in pairs:
        c, h = ch
        c0 = h * D_HEAD
        num, den = tot[ch][:, :D_HEAD], tot[ch][:, D_HEAD:]
        hh = num / jnp.maximum(jnp.abs(den), jnp.exp(rep(c, h, 2)))
        mu = jnp.mean(hh, axis=1, keepdims=True)
        var = jnp.mean(jnp.square(hh - mu), axis=1, keepdims=True)
        hn = (hh - mu) * lax.rsqrt(var + LN_EPS) * nw_ref[:, c0:c0 + D_HEAD]
        og = jax.nn.sigmoid(cols(3, c, h).astype(F32))
        out_ref[c * L:(c + 1) * L, c0:c0 + D_HEAD] = (hn * og).astype(BF16)


def _mlstm(qkvo, grow, nw, bsz, seq):
    sb = min(TOK_TILE, seq)
    nt = seq // sb
    tok = bsz * seq
    row = lambda b, s: (b * nt + s, 0)
    return pl.pallas_call(
        _mlstm_kernel,
        grid=(bsz, nt),
        in_specs=[
            pl.BlockSpec((sb, 4 * M_WIDTH), row),
            pl.BlockSpec((GATE_ROWS, sb), lambda b, s: (0, b * nt + s)),
            pl.BlockSpec((1, M_WIDTH), lambda b, s: (0, 0)),
        ],
        out_specs=pl.BlockSpec((sb, M_WIDTH), row),
        out_shape=jax.ShapeDtypeStruct((tok, M_WIDTH), BF16),
        scratch_shapes=[pltpu.VMEM((HEADS, D_HEAD, 2 * D_HEAD), F32),
                        pltpu.VMEM((8, CHUNK), F32),
                        pltpu.VMEM((LANES, HEADS * N_ROWQ * LANES), BF16)],
        compiler_params=_cparams(("arbitrary", "arbitrary")),
        name="mlstm",
    )(qkvo, grow, nw)


def _layer_norm(r, g, b):
    mu = jnp.mean(r, axis=-1, keepdims=True)
    var = jnp.mean(jnp.square(r - mu), axis=-1, keepdims=True)
    return (r - mu) * lax.rsqrt(var + LN_EPS) * g + b


TOK_SUB = D_MODEL // LANES


def _store_token_major(ref, val):
    rows = val.shape[0]
    for k in range(TOK_SUB):
        ref[pl.ds(k, rows, stride=TOK_SUB), :] = val[:, k * LANES:(k + 1) * LANES]


def _load_token_major(ref, rows):
    return jnp.concatenate([ref[pl.ds(k, rows, stride=TOK_SUB), :] for k in range(TOK_SUB)], axis=1)


def _outproj_kernel(hn_ref, yc_ref, x_ref, gt_ref, sc_ref, sh_ref, wo_ref, lg_ref, lb_ref,
                    wr_ref, br_ref, x1_ref, u2_ref, route_ref, cnt_ref, base, triu):
    tm = x_ref.shape[0]

    @pl.when(pl.program_id(0) == 0)
    def _():
        base[...] = jnp.zeros(base.shape, F32)
        rid = lax.broadcasted_iota(I32, (tm, tm), 0)
        cid = lax.broadcasted_iota(I32, (tm, tm), 1)
        triu[...] = (rid <= cid).astype(BF16)

    mix = (jnp.dot(hn_ref[...], wo_ref[0:M_WIDTH, :], preferred_element_type=F32)
           + jnp.dot(yc_ref[...], wo_ref[M_WIDTH:, :], preferred_element_type=F32))
    x1 = _layer_norm(ALPHA * x_ref[...] + (1.0 + gt_ref[0]) * mix, lg_ref[...], lb_ref[...])
    x1_ref[...] = x1
    u2 = x1 * (1.0 + sc_ref[0]) + sh_ref[0]
    _store_token_major(u2_ref, u2)

    lt = lax.dot_general(wr_ref[...], u2.astype(BF16), (((1,), (1,)), ((), ())),
                         preferred_element_type=F32) + br_ref[:, 0:1]
    lg = [lt[j:j + 1, :] for j in range(N_GROUPS + N_GROUPS * N_EXP)]
    best = lg[0]
    grp = jnp.zeros((1, tm), I32)
    for j in range(1, N_GROUPS):
        c = lg[j] > best
        grp = jnp.where(c, j, grp)
        best = jnp.where(c, lg[j], best)
    sel = []
    for e in range(N_EXP):
        val = lg[N_GROUPS + e]
        for gg in range(1, N_GROUPS):
            val = jnp.where(grp == gg, lg[N_GROUPS + gg * N_EXP + e], val)
        sel.append(val)
    v1 = sel[0]
    i1 = jnp.zeros((1, tm), I32)
    for e in range(1, N_EXP):
        c = sel[e] > v1
        i1 = jnp.where(c, e, i1)
        v1 = jnp.where(c, sel[e], v1)
    v2 = jnp.full((1, tm), -jnp.inf, F32)
    i2 = jnp.zeros((1, tm), I32)
    for e in range(N_EXP):
        cand = jnp.where(i1 == e, -jnp.inf, sel[e])
        c = cand > v2
        i2 = jnp.where(c, e, i2)
        v2 = jnp.where(c, cand, v2)
    ea = jnp.minimum(i1, i2)
    eb = jnp.maximum(i1, i2)
    pair = jnp.where(ea == 0, 0, jnp.where(ea == 1, 3, 5)) + eb - ea - 1
    bucket = grp * N_PAIRS + pair

    onehot = lax.broadcasted_iota(I32, (ROUTE_ROWS, tm), 0) == bucket
    cum = jnp.dot(onehot.astype(BF16), triu[...], preferred_element_type=F32)
    prev = base[:, 0:1]
    rank = jnp.sum(jnp.where(onehot, cum - 1.0 + prev, 0.0), axis=0, keepdims=True)
    new_base = prev + cum[:, tm - 1:tm]
    base[...] = jnp.broadcast_to(new_base, base.shape)
    cnt_ref[...] = jnp.broadcast_to(new_base, cnt_ref.shape)

    zrow = jnp.zeros((1, tm), F32)
    route_ref[...] = jnp.concatenate(
        [bucket.astype(F32), rank, zrow, zrow, zrow, zrow, zrow, zrow], axis=0)


def _outproj(hn, yc, x2, gt, sc, sh, wo, lg, lb, wr, br, bsz, seq):
    tm = min(TOK_TILE, seq)
    nt = seq // tm
    tok = bsz * seq
    row = lambda i: (i, 0)
    const = lambda i: (0, 0)
    mod = lambda i: (i // nt, 0, 0)
    return pl.pallas_call(
        _outproj_kernel,
        grid=(bsz * nt,),
        in_specs=[
            pl.BlockSpec((tm, M_WIDTH), row),
            pl.BlockSpec((tm, C_WIDTH), row),
            pl.BlockSpec((tm, D_MODEL), row),
            pl.BlockSpec((1, 1, D_MODEL), mod),
            pl.BlockSpec((1, 1, D_MODEL), mod),
            pl.BlockSpec((1, 1, D_MODEL), mod),
            pl.BlockSpec(wo.shape, const),
            pl.BlockSpec(lg.shape, const),
            pl.BlockSpec(lb.shape, const),
            pl.BlockSpec(wr.shape, const),
            pl.BlockSpec(br.shape, const),
        ],
        out_specs=[
            pl.BlockSpec((tm, D_MODEL), row),
            pl.BlockSpec((tm * TOK_SUB, LANES), row),
            pl.BlockSpec((8, tm), lambda i: (0, i)),
            pl.BlockSpec((ROUTE_ROWS, LANES), const),
        ],
        out_shape=[
            jax.ShapeDtypeStruct((tok, D_MODEL), F32),
            jax.ShapeDtypeStruct((tok * TOK_SUB, LANES), F32),
            jax.ShapeDtypeStruct((8, tok), F32),
            jax.ShapeDtypeStruct((ROUTE_ROWS, LANES), F32),
        ],
        scratch_shapes=[pltpu.VMEM((ROUTE_ROWS, LANES), F32), pltpu.VMEM((tm, tm), BF16)],
        compiler_params=_cparams(("arbitrary",)),
        name="outproj",
    )(hn, yc, x2, gt, sc, sh, wo, lg, lb, wr, br)


DISPATCH_TILE = 1024
DMA_UNROLL = 8


def _dispatch_kernel(pos_ref, pad_ref, u_ref, xs_ref, zeros, sem, zsem):
    ts = u_ref.shape[0] // TOK_SUB
    t0 = pl.program_id(0) * ts

    def zero_rows(first_row, n_rows_static):
        dst = pl.multiple_of(first_row * TOK_SUB, TOK_SUB)
        return pltpu.make_async_copy(zeros.at[pl.ds(0, n_rows_static * TOK_SUB), :],
                                     xs_ref.at[pl.ds(dst, n_rows_static * TOK_SUB), :], zsem)

    def fill_pads(wait):
        def bucket(b, carry):
            row = pad_ref[b]
            n = pad_ref[N_BUCKETS + b]
            size = MOE_TILE // 2
            while size >= 1:
                @pl.when((n & size) != 0)
                def _(row=row, size=size):
                    cp = zero_rows(row, size)
                    cp.wait() if wait else cp.start()
                row = row + (n & size)
                size //= 2
            return carry
        lax.fori_loop(0, N_BUCKETS, bucket, 0)

        def idle_tile(j, carry):
            cp = zero_rows(j * MOE_TILE, MOE_TILE)
            cp.wait() if wait else cp.start()
            return carry
        lax.fori_loop(pad_ref[2 * N_BUCKETS], xs_ref.shape[0] // (MOE_TILE * TOK_SUB), idle_tile, 0)

    @pl.when(pl.program_id(0) == 0)
    def _():
        zeros[...] = jnp.zeros(zeros.shape, F32)
        fill_pads(wait=False)
        fill_pads(wait=True)

    def issue(r, carry):
        dst = pl.multiple_of(pos_ref[t0 + r] * TOK_SUB, TOK_SUB)
        src = pl.multiple_of(r * TOK_SUB, TOK_SUB)
        pltpu.make_async_copy(u_ref.at[pl.ds(src, TOK_SUB), :],
                              xs_ref.at[pl.ds(dst, TOK_SUB), :], sem).start()
        return carry

    lax.fori_loop(0, ts, issue, 0, unroll=DMA_UNROLL)
    pltpu.make_async_copy(u_ref, xs_ref.at[pl.ds(0, ts * TOK_SUB), :], sem).wait()


def _dispatch(pos, pad_start, u2, n_rows):
    tok = u2.shape[0] // TOK_SUB
    ts = min(DISPATCH_TILE, tok)
    return pl.pallas_call(
        _dispatch_kernel,
        grid_spec=pltpu.PrefetchScalarGridSpec(
            num_scalar_prefetch=2,
            grid=(tok // ts,),
            in_specs=[pl.BlockSpec((ts * TOK_SUB, LANES), lambda i, pos, pad: (i, 0))],
            out_specs=pl.BlockSpec(memory_space=pl.ANY),
            scratch_shapes=[pltpu.VMEM((MOE_TILE * TOK_SUB, LANES), F32),
                            pltpu.SemaphoreType.DMA(()), pltpu.SemaphoreType.DMA(())],
        ),
        out_shape=jax.ShapeDtypeStruct((n_rows * TOK_SUB, LANES), F32),
        compiler_params=_cparams(("arbitrary",)),
        name="dispatch",
    )(pos, pad_start, u2)


def _moe_kernel(e1_ref, e2_ref, valid_ref, fresh_ref, tb_ref, xb_ref, xs_ref, wsel_ref, bsel_ref,
                wg1, wu1, wd1, wg2, wu2, wd2, y_ref, wgs, wus, wds):
    del tb_ref, xb_ref
    j = pl.program_id(0)
    rows = y_ref.shape[0] // TOK_SUB

    @pl.when(fresh_ref[j] == 1)
    def _():
        wgs[0] = wg1[0].astype(BF16)
        wus[0] = wu1[0].astype(BF16)
        wds[0] = wd1[0].astype(BF16)
        wgs[1] = wg2[0].astype(BF16)
        wus[1] = wu2[0].astype(BF16)
        wds[1] = wd2[0].astype(BF16)

    @pl.when(valid_ref[j] == 1)
    def _():
        x = _load_token_major(xs_ref, rows).astype(BF16)
        lg = jnp.dot(x, wsel_ref[0], preferred_element_type=F32) + bsel_ref[0]
        grp = [lg[:, i:i + 1] for i in range(N_GROUPS)]
        gmax = jnp.maximum(jnp.maximum(grp[0], grp[1]), jnp.maximum(grp[2], grp[3]))
        gsum = sum(jnp.exp(v - gmax) for v in grp)
        la, lb = lg[:, N_GROUPS:N_GROUPS + 1], lg[:, N_GROUPS + 1:N_GROUPS + 2]
        emax = jnp.maximum(la, lb)
        ea, eb = jnp.exp(la - emax), jnp.exp(lb - emax)
        scale = 1.0 / (gsum * (ea + eb))
        wcols = (ea * scale, eb * scale)
        acc = None
        for slot in range(2):
            g = jnp.dot(x, wgs[slot], preferred_element_type=F32)
            u = jnp.dot(x, wus[slot], preferred_element_type=F32)
            hid = (g * jax.nn.sigmoid(g) * u * wcols[slot]).astype(BF16)
            y = jnp.dot(hid, wds[slot], preferred_element_type=F32)
            acc = y if acc is None else acc + y
        _store_token_major(y_ref, acc)

    @pl.when(valid_ref[j] == 0)
    def _():
        y_ref[...] = jnp.zeros(y_ref.shape, F32)


def _moe(plan, xs, wsel, bsel, w_gate, w_up, w_down, n_tiles):
    wsel1 = lambda j, e1, e2, v, f, tb, xb: (e1[j], 0, 0)
    wsel2 = lambda j, e1, e2, v, f, tb, xb: (e2[j], 0, 0)
    by_bucket = lambda j, e1, e2, v, f, tb, xb: (tb[j], 0, 0)
    up_spec = lambda sel: pl.BlockSpec((1, D_MODEL, D_EXPERT), sel)
    dn_spec = lambda sel: pl.BlockSpec((1, D_EXPERT, D_MODEL), sel)
    tile_rows = MOE_TILE * TOK_SUB
    return pl.pallas_call(
        _moe_kernel,
        grid_spec=pltpu.PrefetchScalarGridSpec(
            num_scalar_prefetch=6,
            grid=(n_tiles,),
            in_specs=[pl.BlockSpec((tile_rows, LANES), lambda j, e1, e2, v, f, tb, xb: (xb[j], 0)),
                      pl.BlockSpec((1, D_MODEL, LANES), by_bucket),
                      pl.BlockSpec((1, 1, LANES), by_bucket),
                      up_spec(wsel1), up_spec(wsel1), dn_spec(wsel1),
                      up_spec(wsel2), up_spec(wsel2), dn_spec(wsel2)],
            out_specs=pl.BlockSpec((tile_rows, LANES), lambda j, *_: (j, 0)),
            scratch_shapes=[pltpu.VMEM((2, D_MODEL, D_EXPERT), BF16),
                            pltpu.VMEM((2, D_MODEL, D_EXPERT), BF16),
                            pltpu.VMEM((2, D_EXPERT, D_MODEL), BF16)],
        ),
        out_shape=jax.ShapeDtypeStruct((n_tiles * tile_rows, LANES), F32),
        compiler_params=_cparams(("arbitrary",)),
        name="moe",
    )(*plan, xs, wsel, bsel, w_gate, w_up, w_down, w_gate, w_up, w_down)


def _combine_kernel(pos_ref, ys_ref, x1_ref, gt_ref, lg_ref, lb_ref, out_ref, ybuf, sems):
    ts = x1_ref.shape[0]
    i = pl.program_id(0)
    n = pl.num_programs(0)
    slot = i % 2

    def gather(tile, to_slot):
        def issue(r, carry):
            src = pl.multiple_of(pos_ref[tile * ts + r] * TOK_SUB, TOK_SUB)
            dst = pl.multiple_of(r * TOK_SUB, TOK_SUB)
            pltpu.make_async_copy(ys_ref.at[pl.ds(src, TOK_SUB), :],
                                  ybuf.at[to_slot, pl.ds(dst, TOK_SUB), :], sems.at[to_slot]).start()
            return carry
        lax.fori_loop(0, ts, issue, 0, unroll=DMA_UNROLL)

    @pl.when(i == 0)
    def _():
        gather(0, 0)

    @pl.when(i + 1 < n)
    def _():
        gather(i + 1, 1 - slot)

    pltpu.make_async_copy(ys_ref.at[pl.ds(0, ts * TOK_SUB), :], ybuf.at[slot], sems.at[slot]).wait()
    y = _load_token_major(ybuf.at[slot], ts)
    out_ref[...] = _layer_norm(ALPHA * x1_ref[...] + (1.0 + gt_ref[0]) * y, lg_ref[...], lb_ref[...])


def _combine(pos, ys, x1, gt, lg, lb, bsz, seq):
    ts = min(TOK_TILE, seq)
    nt = seq // ts
    tok = bsz * seq
    return pl.pallas_call(
        _combine_kernel,
        grid_spec=pltpu.PrefetchScalarGridSpec(
            num_scalar_prefetch=1,
            grid=(tok // ts,),
            in_specs=[pl.BlockSpec(memory_space=pl.ANY),
                      pl.BlockSpec((ts, D_MODEL), lambda i, pos: (i, 0)),
                      pl.BlockSpec((1, 1, D_MODEL), lambda i, pos: (i // nt, 0, 0)),
                      pl.BlockSpec((1, D_MODEL), lambda i, pos: (0, 0)),
                      pl.BlockSpec((1, D_MODEL), lambda i, pos: (0, 0))],
            out_specs=pl.BlockSpec((ts, D_MODEL), lambda i, pos: (i, 0)),
            scratch_shapes=[pltpu.VMEM((2, ts * TOK_SUB, LANES), F32), pltpu.SemaphoreType.DMA((2,))],
        ),
        out_shape=jax.ShapeDtypeStruct((tok, D_MODEL), F32),
        compiler_params=_cparams(("arbitrary",)),
        name="combine",
    )(pos, ys, x1, gt, lg, lb)


def _route_plan(route, cnt, n_tiles, layer):
    bucket = route[0].astype(I32)
    rank = route[1].astype(I32)
    counts = cnt[:N_BUCKETS, 0].astype(I32)
    tiles_b = (counts + MOE_TILE - 1) // MOE_TILE
    tile_end = jnp.cumsum(tiles_b)
    row_off = (tile_end - tiles_b) * MOE_TILE
    bucket_ids = jnp.arange(N_BUCKETS, dtype=I32)
    pos = jnp.sum(jnp.where(bucket[None, :] == bucket_ids[:, None], row_off[:, None], 0), axis=0) + rank
    total = tile_end[-1]
    j = jnp.arange(n_tiles, dtype=I32)
    valid = j < total
    tb = jnp.sum((tile_end[None, :] <= jnp.minimum(j, total - 1)[:, None]).astype(I32), axis=1)
    tb = jnp.clip(tb, 0, N_BUCKETS - 1)
    grp = tb // N_PAIRS
    pair = tb % N_PAIRS
    pair_a = sum((pair == i).astype(I32) * _PAIR_A[i] for i in range(N_PAIRS))
    pair_b = sum((pair == i).astype(I32) * _PAIR_B[i] for i in range(N_PAIRS))
    e1 = layer * (N_GROUPS * N_EXP) + grp * N_EXP + pair_a
    e2 = layer * (N_GROUPS * N_EXP) + grp * N_EXP + pair_b
    fresh = jnp.concatenate([jnp.ones((1,), I32), (tb[1:] != tb[:-1]).astype(I32)])
    xblk = jnp.minimum(j, total - 1)
    pads = jnp.concatenate([row_off + counts, tiles_b * MOE_TILE - counts, total[None]])
    return pos, pads, (e1, e2, valid.astype(I32), fresh, tb, xblk)


def kernel(x, c, w_ada, b_ada, w_in, b_gates, mh_norm_w, w_conv, w_out, ln1_g, ln1_b,
           w_grp, b_grp, w_router, b_router, w_gate, w_up, w_down, ln2_g, ln2_b):
    bsz, seq, _ = x.shape
    tok = bsz * seq
    n_rows = tok + N_BUCKETS * MOE_TILE
    n_tiles = n_rows // MOE_TILE
    n_exp_total = N_GROUPS * N_EXP

    wg_all = w_gate.reshape(DEPTH * n_exp_total, D_MODEL, D_EXPERT)
    wu_all = w_up.reshape(DEPTH * n_exp_total, D_MODEL, D_EXPERT)
    wd_all = w_down.reshape(DEPTH * n_exp_total, D_EXPERT, D_MODEL)
    mod = _ada_mod(c, w_ada, b_ada).reshape(DEPTH, bsz, 6, 1, D_MODEL)
    x2 = x.reshape(tok, D_MODEL)
    for l in range(DEPTH):
        sh1, sc1, gt1, sh2, sc2, gt2 = [mod[l, :, i] for i in range(6)]
        wl = w_in[l]
        wq = wl[:, :4 * M_WIDTH].astype(BF16)
        wg = jnp.pad(wl[:, 4 * M_WIDTH:4 * M_WIDTH + 2 * HEADS],
                     ((0, 0), (0, LANES - 2 * HEADS))).astype(BF16)
        wc = wl[:, 4 * M_WIDTH + 2 * HEADS:].astype(BF16)
        bg = jnp.pad(b_gates[l], (0, LANES - 2 * HEADS)).reshape(1, LANES)
        qkvo, grow, yc = _inproj(x2, sc1, sh1, wq, wg, wc, bg, w_conv[l], bsz, seq)
        hn = _mlstm(qkvo, grow, mh_norm_w[l].reshape(1, M_WIDTH), bsz, seq)

        n_logit = N_GROUPS + n_exp_total
        wr = jnp.pad(jnp.concatenate([w_grp[l], w_router[l]], axis=1).T,
                     ((0, ROUTE_ROWS - n_logit), (0, 0))).astype(BF16)
        br = jnp.pad(jnp.concatenate([b_grp[l], b_router[l]]), (0, ROUTE_ROWS - n_logit))
        br = jnp.broadcast_to(br[:, None], (ROUTE_ROWS, LANES))
        x1, u2, route, cnt = _outproj(hn, yc, x2, gt1, sc2, sh2, w_out[l].astype(BF16),
                                      ln1_g[l].reshape(1, D_MODEL), ln1_b[l].reshape(1, D_MODEL),
                                      wr, br, bsz, seq)
        pos, pads, plan = _route_plan(route, cnt, n_tiles, l)
        xs = _dispatch(pos, pads, u2, n_rows)
        wsel = jnp.concatenate([jnp.broadcast_to(w_grp[l], (N_BUCKETS, D_MODEL, N_GROUPS)),
                                w_router[l][:, _BUCKET_EA].T[:, :, None],
                                w_router[l][:, _BUCKET_EB].T[:, :, None]], axis=2)
        wsel = jnp.pad(wsel, ((0, 0), (0, 0), (0, LANES - N_GROUPS - 2))).astype(BF16)
        bsel = jnp.concatenate([jnp.broadcast_to(b_grp[l], (N_BUCKETS, N_GROUPS)),
                                b_router[l][_BUCKET_EA][:, None], b_router[l][_BUCKET_EB][:, None]], axis=1)
        bsel = jnp.pad(bsel, ((0, 0), (0, LANES - N_GROUPS - 2))).reshape(N_BUCKETS, 1, LANES)
        ys = _moe(plan, xs, wsel, bsel, wg_all, wu_all, wd_all, n_tiles)
        x2 = _combine(pos, ys, x1, gt2, ln2_g[l].reshape(1, D_MODEL),
                      ln2_b[l].reshape(1, D_MODEL), bsz, seq)
    return x2.reshape(bsz, seq, D_MODEL)
```

```python
import math

import jax
import jax.numpy as jnp
import numpy as np
from jax import lax
from jax.experimental import pallas as pl
from jax.experimental.pallas import tpu as pltpu

F32 = jnp.float32
BF16 = jnp.bfloat16
I32 = jnp.int32

D_MODEL = 1024
DEPTH = 2
HEADS = 4
D_HEAD = 128
M_WIDTH = HEADS * D_HEAD
C_WIDTH = D_MODEL - M_WIDTH
N_GROUPS = 4
N_EXP = 4
D_EXPERT = 512
N_PAIRS = 6
N_BUCKETS = N_GROUPS * N_PAIRS
ALPHA = (2 * DEPTH) ** 0.25
LN_EPS = 1e-5
QK_SCALE = D_HEAD ** -0.5
LOG_QK_SCALE = math.log(QK_SCALE)
GATE_ROWS = 24

LANES = 128
ROUTE_ROWS = 32

TOK_TILE = 512
CHUNK = 128
MOE_TILE = 256
VMEM_LIMIT = 56 * 1024 * 1024

_PAIR_A = (0, 0, 0, 1, 1, 2)
_PAIR_B = (1, 2, 3, 2, 3, 3)
_BUCKET_EA = np.array([g * N_EXP + _PAIR_A[p] for g in range(N_GROUPS) for p in range(N_PAIRS)])
_BUCKET_EB = np.array([g * N_EXP + _PAIR_B[p] for g in range(N_GROUPS) for p in range(N_PAIRS)])


def _cparams(sem):
    return pltpu.CompilerParams(dimension_semantics=sem, vmem_limit_bytes=VMEM_LIMIT)


def _ada_kernel(c_ref, w_ref, b_ref, o_ref):
    c = c_ref[...]
    cond = c * jax.nn.sigmoid(c)
    o_ref[0] = jnp.dot(cond, w_ref[0], preferred_element_type=F32,
                       precision=lax.Precision.HIGHEST) + b_ref[0]


def _ada_mod(c, w_ada, b_ada):
    bsz = c.shape[0]
    nblk = w_ada.shape[2] // D_MODEL
    return pl.pallas_call(
        _ada_kernel,
        grid=(DEPTH, nblk),
        in_specs=[
            pl.BlockSpec((bsz, D_MODEL), lambda l, j: (0, 0)),
            pl.BlockSpec((1, D_MODEL, D_MODEL), lambda l, j: (l, 0, j)),
            pl.BlockSpec((1, 1, D_MODEL), lambda l, j: (l, 0, j)),
        ],
        out_specs=pl.BlockSpec((1, bsz, D_MODEL), lambda l, j: (l, 0, j)),
        out_shape=jax.ShapeDtypeStruct((DEPTH, bsz, nblk * D_MODEL), F32),
        compiler_params=_cparams(("arbitrary", "arbitrary")),
        name="ada_mod",
    )(c, w_ada, b_ada.reshape(DEPTH, 1, -1))


def _log_sigmoid(x):
    return jnp.minimum(x, 0.0) - jnp.log1p(jnp.exp(-jnp.abs(x)))


def _chunk_scan(x, op, identity):
    lane = lax.broadcasted_iota(I32, x.shape, 1) & (CHUNK - 1)
    sh = 1
    while sh < CHUNK:
        x = op(x, jnp.where(lane >= sh, pltpu.roll(x, sh, 1), identity))
        sh *= 2
    return x


def _inproj_kernel(x_ref, sc_ref, sh_ref, wq_ref, wg_ref, wc_ref, bg_ref, wconv_ref,
                   qkvo_ref, grow_ref, yc_ref, zbuf):
    tm = x_ref.shape[0]
    u = (x_ref[...] * (1.0 + sc_ref[0]) + sh_ref[0]).astype(BF16)
    qkvo_ref[...] = jnp.dot(u, wq_ref[...], preferred_element_type=F32).astype(BF16)
    g = jnp.dot(u, wg_ref[...], preferred_element_type=F32) + bg_ref[...]
    gt = g.T[:8]
    head_row = lax.broadcasted_iota(I32, gt.shape, 0) < HEADS
    ig = jnp.where(head_row, gt, 0.0)
    logf = jnp.where(head_row, _log_sigmoid(pltpu.roll(gt, HEADS, 0)), 0.0)
    bcum = _chunk_scan(logf, jnp.add, 0.0)
    ug = ig - bcum
    grow_ref[0:8, :] = ug
    grow_ref[8:16, :] = bcum
    grow_ref[16:24, :] = _chunk_scan(ug, jnp.maximum, -jnp.inf)
    pc = jnp.dot(u, wc_ref[...], preferred_element_type=F32)
    z = pc[:, C_WIDTH:2 * C_WIDTH] * pc[:, 2 * C_WIDTH:]

    @pl.when(pl.program_id(1) == 0)
    def _():
        zbuf[0:8, :] = jnp.zeros((8, C_WIDTH), F32)

    zbuf[8:8 + tm, :] = z
    zc = (wconv_ref[0:1, :] * zbuf[6:6 + tm, :] + wconv_ref[1:2, :] * zbuf[7:7 + tm, :]
          + wconv_ref[2:3, :] * z)
    yc_ref[...] = (pc[:, :C_WIDTH] * zc).astype(BF16)
    zbuf[0:8, :] = zbuf[tm:tm + 8, :]


def _inproj(x2, sc, sh, wq, wg, wc, bg, wconv, bsz, seq):
    tm = min(TOK_TILE, seq)
    nt = seq // tm
    tok = bsz * seq
    row = lambda b, s: (b * nt + s, 0)
    const = lambda b, s: (0, 0)
    mod = lambda b, s: (b, 0, 0)
    return pl.pallas_call(
        _inproj_kernel,
        grid=(bsz, nt),
        in_specs=[
            pl.BlockSpec((tm, D_MODEL), row),
            pl.BlockSpec((1, 1, D_MODEL), mod),
            pl.BlockSpec((1, 1, D_MODEL), mod),
            pl.BlockSpec(wq.shape, const),
            pl.BlockSpec(wg.shape, const),
            pl.BlockSpec(wc.shape, const),
            pl.BlockSpec(bg.shape, const),
            pl.BlockSpec(wconv.shape, const),
        ],
        out_specs=[
            pl.BlockSpec((tm, 4 * M_WIDTH), row),
            pl.BlockSpec((GATE_ROWS, tm), lambda b, s: (0, b * nt + s)),
            pl.BlockSpec((tm, C_WIDTH), row),
        ],
        out_shape=[
            jax.ShapeDtypeStruct((tok, 4 * M_WIDTH), BF16),
            jax.ShapeDtypeStruct((GATE_ROWS, tok), F32),
            jax.ShapeDtypeStruct((tok, C_WIDTH), BF16),
        ],
        scratch_shapes=[pltpu.VMEM((tm + 8, C_WIDTH), F32)],
        compiler_params=_cparams(("arbitrary", "arbitrary")),
        name="inproj",
    )(x2, sc, sh, wq, wg, wc, bg, wconv)


N_ROWQ = 4
SPLIT = 3


def _split3(x):
    hi = x.astype(BF16).astype(F32)
    r1 = x - hi
    mid = r1.astype(BF16).astype(F32)
    return [hi, mid, r1 - mid]


def _mlstm_kernel(qkvo_ref, grow_ref, nw_ref, out_ref, cst, mst, sel):
    sb = qkvo_ref.shape[0]
    L = CHUNK

    @pl.when(pl.program_id(1) == 0)
    def _():
        cst[...] = jnp.zeros(cst.shape, F32)
        mst[...] = jnp.zeros(mst.shape, F32)

    @pl.when((pl.program_id(0) == 0) & (pl.program_id(1) == 0))
    def _():
        r = lax.broadcasted_iota(I32, sel.shape, 0)
        c = lax.broadcasted_iota(I32, sel.shape, 1) // LANES
        rq = r // 8
        quantity = ((rq >= SPLIT).astype(I32) + (rq >= 2 * SPLIT).astype(I32)
                    + (rq >= 3 * SPLIT).astype(I32))
        hit = ((r % 8) == c // N_ROWQ) & (quantity == c % N_ROWQ) & (rq < N_ROWQ * SPLIT)
        sel[...] = hit.astype(BF16)

    causal = lax.broadcasted_iota(I32, (L, L), 0) >= lax.broadcasted_iota(I32, (L, L), 1)
    ones_ext = jnp.ones((L, D_HEAD), BF16)
    pad_rows = jnp.zeros((LANES - 8 * N_ROWQ * SPLIT, L), F32)
    eye = (lax.broadcasted_iota(I32, (L, L), 0) == lax.broadcasted_iota(I32, (L, L), 1)).astype(BF16)
    n_chunks = sb // L
    pairs = [(c, h) for c in range(n_chunks) for h in range(HEADS)]

    def cols(part, c, h):
        c0 = part * M_WIDTH + h * D_HEAD
        return qkvo_ref[c * L:(c + 1) * L, c0:c0 + D_HEAD]

    m_prev = mst[...]
    stack_t, us, decay = [], [], []
    for c in range(n_chunks):
        ug = grow_ref[0:8, c * L:(c + 1) * L]
        bcum = grow_ref[8:16, c * L:(c + 1) * L]
        cmax = grow_ref[16:24, c * L:(c + 1) * L]
        big_m = jnp.maximum(m_prev, cmax)
        m_last = jnp.broadcast_to(big_m[:, L - 1:L], (8, L))
        g_tot = jnp.broadcast_to(bcum[:, L - 1:L], (8, L))
        wq = jnp.exp(m_prev - big_m) * QK_SCALE
        log_em = -(bcum + big_m)
        wk = jnp.exp(ug - m_last)
        decay.append(jnp.exp(m_prev - m_last))
        us.append(ug + LOG_QK_SCALE)
        m_prev = g_tot + m_last
        stack = jnp.concatenate(_split3(big_m) + _split3(wq) + _split3(log_em) + _split3(wk)
                                + [pad_rows], axis=0)
        stack_t.append(stack.T.astype(BF16))
    mst[...] = m_prev

    def rep(c, h, j):
        c0 = (h * N_ROWQ + j) * LANES
        return jnp.dot(stack_t[c], sel[:, c0:c0 + LANES], preferred_element_type=F32)

    s_mat = {ch: lax.dot_general(cols(0, *ch), cols(1, *ch), (((1,), (1,)), ((), ())),
                                 preferred_element_type=F32) for ch in pairs}
    kw = {ch: (cols(1, *ch).astype(F32) * rep(*ch, 3)).astype(BF16) for ch in pairs}
    kw_t = {ch: lax.dot_general(eye, kw[ch], (((1,), (1,)), ((), ())),
                                preferred_element_type=F32).astype(BF16) for ch in pairs}
    lhs = {}
    for ch in pairs:
        c, h = ch
        p = jnp.where(causal, s_mat[ch] * jnp.exp(us[c][h:h + 1, :] - rep(c, h, 0)), 0.0)
        qw = cols(0, c, h).astype(F32) * rep(c, h, 1)
        lhs[ch] = jnp.concatenate([p.astype(BF16), qw.astype(BF16)], axis=1)
    vext = {ch: jnp.concatenate([cols(2, *ch), ones_ext], axis=1) for ch in pairs}
    upd = {ch: jnp.dot(kw_t[ch], vext[ch], preferred_element_type=F32) for ch in pairs}
    tot = {}
    for c in range(n_chunks):
        for h in range(HEADS):
            state = cst[h]
            rhs = jnp.concatenate([vext[(c, h)], state.astype(BF16)], axis=0)
            tot[(c, h)] = jnp.dot(lhs[(c, h)], rhs, preferred_element_type=F32)
            dec = jnp.concatenate([decay[c][h:h + 1, :], decay[c][h:h + 1, :]], axis=1)
            cst[h] = dec * state + upd[(c, h)]
    for ch in pairs:
        c, h = ch
        c0 = h * D_HEAD
        num, den = tot[ch][:, :D_HEAD], tot[ch][:, D_HEAD:]
        hh = num / jnp.maximum(jnp.abs(den), jnp.exp(rep(c, h, 2)))
        mu = jnp.mean(hh, axis=1, keepdims=True)
        var = jnp.mean(jnp.square(hh - mu), axis=1, keepdims=True)
        hn = (hh - mu) * lax.rsqrt(var + LN_EPS) * nw_ref[:, c0:c0 + D_HEAD]
        og = jax.nn.sigmoid(cols(3, c, h).astype(F32))
        out_ref[c * L:(c + 1) * L, c0:c0 + D_HEAD] = (hn * og).astype(BF16)


def _mlstm(qkvo, grow, nw, bsz, seq):
    sb = min(TOK_TILE, seq)
    nt = seq // sb
    tok = bsz * seq
    row = lambda b, s: (b * nt + s, 0)
    return pl.pallas_call(
        _mlstm_kernel,
        grid=(bsz, nt),
        in_specs=[
            pl.BlockSpec((sb, 4 * M_WIDTH), row),
            pl.BlockSpec((GATE_ROWS, sb), lambda b, s: (0, b * nt + s)),
            pl.BlockSpec((1, M_WIDTH), lambda b, s: (0, 0)),
        ],
        out_specs=pl.BlockSpec((sb, M_WIDTH), row),
        out_shape=jax.ShapeDtypeStruct((tok, M_WIDTH), BF16),
        scratch_shapes=[pltpu.VMEM((HEADS, D_HEAD, 2 * D_HEAD), F32),
                        pltpu.VMEM((8, CHUNK), F32),
                        pltpu.VMEM((LANES, HEADS * N_ROWQ * LANES), BF16)],
        compiler_params=_cparams(("arbitrary", "arbitrary")),
        name="mlstm",
    )(qkvo, grow, nw)


def _layer_norm(r, g, b):
    mu = jnp.mean(r, axis=-1, keepdims=True)
    var = jnp.mean(jnp.square(r - mu), axis=-1, keepdims=True)
    return (r - mu) * lax.rsqrt(var + LN_EPS) * g + b


TOK_SUB = D_MODEL // LANES


def _store_token_major(ref, val):
    rows = val.shape[0]
    for k in range(TOK_SUB):
        ref[pl.ds(k, rows, stride=TOK_SUB), :] = val[:, k * LANES:(k + 1) * LANES]


def _load_token_major(ref, rows):
    return jnp.concatenate([ref[pl.ds(k, rows, stride=TOK_SUB), :] for k in range(TOK_SUB)], axis=1)


def _outproj_kernel(hn_ref, yc_ref, x_ref, gt_ref, sc_ref, sh_ref, wo_ref, lg_ref, lb_ref,
                    wr_ref, br_ref, x1_ref, u2_ref, route_ref, cnt_ref, base, triu):
    tm = x_ref.shape[0]

    @pl.when(pl.program_id(0) == 0)
    def _():
        base[...] = jnp.zeros(base.shape, F32)
        rid = lax.broadcasted_iota(I32, (tm, tm), 0)
        cid = lax.broadcasted_iota(I32, (tm, tm), 1)
        triu[...] = (rid <= cid).astype(BF16)

    mix = (jnp.dot(hn_ref[...], wo_ref[0:M_WIDTH, :], preferred_element_type=F32)
           + jnp.dot(yc_ref[...], wo_ref[M_WIDTH:, :], preferred_element_type=F32))
    x1 = _layer_norm(ALPHA * x_ref[...] + (1.0 + gt_ref[0]) * mix, lg_ref[...], lb_ref[...])
    x1_ref[...] = x1
    u2 = x1 * (1.0 + sc_ref[0]) + sh_ref[0]
    _store_token_major(u2_ref, u2)

    lt = lax.dot_general(wr_ref[...], u2.astype(BF16), (((1,), (1,)), ((), ())),
                         preferred_element_type=F32) + br_ref[:, 0:1]
    lg = [lt[j:j + 1, :] for j in range(N_GROUPS + N_GROUPS * N_EXP)]
    best = lg[0]
    grp = jnp.zeros((1, tm), I32)
    for j in range(1, N_GROUPS):
        c = lg[j] > best
        grp = jnp.where(c, j, grp)
        best = jnp.where(c, lg[j], best)
    sel = []
    for e in range(N_EXP):
        val = lg[N_GROUPS + e]
        for gg in range(1, N_GROUPS):
            val = jnp.where(grp == gg, lg[N_GROUPS + gg * N_EXP + e], val)
        sel.append(val)
    v1 = sel[0]
    i1 = jnp.zeros((1, tm), I32)
    for e in range(1, N_EXP):
        c = sel[e] > v1
        i1 = jnp.where(c, e, i1)
        v1 = jnp.where(c, sel[e], v1)
    v2 = jnp.full((1, tm), -jnp.inf, F32)
    i2 = jnp.zeros((1, tm), I32)
    for e in range(N_EXP):
        cand = jnp.where(i1 == e, -jnp.inf, sel[e])
        c = cand > v2
        i2 = jnp.where(c, e, i2)
        v2 = jnp.where(c, cand, v2)
    ea = jnp.minimum(i1, i2)
    eb = jnp.maximum(i1, i2)
    pair = jnp.where(ea == 0, 0, jnp.where(ea == 1, 3, 5)) + eb - ea - 1
    bucket = grp * N_PAIRS + pair

    onehot = lax.broadcasted_iota(I32, (ROUTE_ROWS, tm), 0) == bucket
    cum = jnp.dot(onehot.astype(BF16), triu[...], preferred_element_type=F32)
    prev = base[:, 0:1]
    rank = jnp.sum(jnp.where(onehot, cum - 1.0 + prev, 0.0), axis=0, keepdims=True)
    new_base = prev + cum[:, tm - 1:tm]
    base[...] = jnp.broadcast_to(new_base, base.shape)
    cnt_ref[...] = jnp.broadcast_to(new_base, cnt_ref.shape)

    zrow = jnp.zeros((1, tm), F32)
    route_ref[...] = jnp.concatenate(
        [bucket.astype(F32), rank, zrow, zrow, zrow, zrow, zrow, zrow], axis=0)


def _outproj(hn, yc, x2, gt, sc, sh, wo, lg, lb, wr, br, bsz, seq):
    tm = min(TOK_TILE, seq)
    nt = seq // tm
    tok = bsz * seq
    row = lambda i: (i, 0)
    const = lambda i: (0, 0)
    mod = lambda i: (i // nt, 0, 0)
    return pl.pallas_call(
        _outproj_kernel,
        grid=(bsz * nt,),
        in_specs=[
            pl.BlockSpec((tm, M_WIDTH), row),
            pl.BlockSpec((tm, C_WIDTH), row),
            pl.BlockSpec((tm, D_MODEL), row),
            pl.BlockSpec((1, 1, D_MODEL), mod),
            pl.BlockSpec((1, 1, D_MODEL), mod),
            pl.BlockSpec((1, 1, D_MODEL), mod),
            pl.BlockSpec(wo.shape, const),
            pl.BlockSpec(lg.shape, const),
            pl.BlockSpec(lb.shape, const),
            pl.BlockSpec(wr.shape, const),
            pl.BlockSpec(br.shape, const),
        ],
        out_specs=[
            pl.BlockSpec((tm, D_MODEL), row),
            pl.BlockSpec((tm * TOK_SUB, LANES), row),
            pl.BlockSpec((8, tm), lambda i: (0, i)),
            pl.BlockSpec((ROUTE_ROWS, LANES), const),
        ],
        out_shape=[
            jax.ShapeDtypeStruct((tok, D_MODEL), F32),
            jax.ShapeDtypeStruct((tok * TOK_SUB, LANES), F32),
            jax.ShapeDtypeStruct((8, tok), F32),
            jax.ShapeDtypeStruct((ROUTE_ROWS, LANES), F32),
        ],
        scratch_shapes=[pltpu.VMEM((ROUTE_ROWS, LANES), F32), pltpu.VMEM((tm, tm), BF16)],
        compiler_params=_cparams(("arbitrary",)),
        name="outproj",
    )(hn, yc, x2, gt, sc, sh, wo, lg, lb, wr, br)


DISPATCH_TILE = 1024
DMA_UNROLL = 8


def _dispatch_kernel(pos_ref, pad_ref, u_ref, xs_ref, zeros, sem, zsem):
    ts = u_ref.shape[0] // TOK_SUB
    t0 = pl.program_id(0) * ts

    def zero_rows(first_row, n_rows_static):
        dst = pl.multiple_of(first_row * TOK_SUB, TOK_SUB)
        return pltpu.make_async_copy(zeros.at[pl.ds(0, n_rows_static * TOK_SUB), :],
                                     xs_ref.at[pl.ds(dst, n_rows_static * TOK_SUB), :], zsem)

    def fill_pads(wait):
        def bucket(b, carry):
            row = pad_ref[b]
            n = pad_ref[N_BUCKETS + b]
            size = MOE_TILE // 2
            while size >= 1:
                @pl.when((n & size) != 0)
                def _(row=row, size=size):
                    cp = zero_rows(row, size)
                    cp.wait() if wait else cp.start()
                row = row + (n & size)
                size //= 2
            return carry
        lax.fori_loop(0, N_BUCKETS, bucket, 0)

        def idle_tile(j, carry):
            cp = zero_rows(j * MOE_TILE, MOE_TILE)
            cp.wait() if wait else cp.start()
            return carry
        lax.fori_loop(pad_ref[2 * N_BUCKETS], xs_ref.shape[0] // (MOE_TILE * TOK_SUB), idle_tile, 0)

    @pl.when(pl.program_id(0) == 0)
    def _():
        zeros[...] = jnp.zeros(zeros.shape, F32)
        fill_pads(wait=False)
        fill_pads(wait=True)

    def issue(g, carry):
        for u in range(DMA_UNROLL):
            r = g * DMA_UNROLL + u
            dst = pl.multiple_of(pos_ref[t0 + r] * TOK_SUB, TOK_SUB)
            src = pl.multiple_of(r * TOK_SUB, TOK_SUB)
            pltpu.make_async_copy(u_ref.at[pl.ds(src, TOK_SUB), :],
                                  xs_ref.at[pl.ds(dst, TOK_SUB), :], sem).start(priority=u % 2)
        return carry

    lax.fori_loop(0, ts // DMA_UNROLL, issue, 0)
    pltpu.make_async_copy(u_ref, xs_ref.at[pl.ds(0, ts * TOK_SUB), :], sem).wait()


def _dispatch(pos, pad_start, u2, n_rows):
    tok = u2.shape[0] // TOK_SUB
    ts = min(DISPATCH_TILE, tok)
    return pl.pallas_call(
        _dispatch_kernel,
        grid_spec=pltpu.PrefetchScalarGridSpec(
            num_scalar_prefetch=2,
            grid=(tok // ts,),
            in_specs=[pl.BlockSpec((ts * TOK_SUB, LANES), lambda i, pos, pad: (i, 0))],
            out_specs=pl.BlockSpec(memory_space=pl.ANY),
            scratch_shapes=[pltpu.VMEM((MOE_TILE * TOK_SUB, LANES), F32),
                            pltpu.SemaphoreType.DMA(()), pltpu.SemaphoreType.DMA(())],
        ),
        out_shape=jax.ShapeDtypeStruct((n_rows * TOK_SUB, LANES), F32),
        compiler_params=_cparams(("arbitrary",)),
        name="dispatch",
    )(pos, pad_start, u2)


def _moe_kernel(e1_ref, e2_ref, valid_ref, fresh_ref, tb_ref, xb_ref, xs_ref, wsel_ref, bsel_ref,
                wg1, wu1, wd1, wg2, wu2, wd2, y_ref, wgs, wus, wds):
    del tb_ref, xb_ref
    j = pl.program_id(0)
    rows = y_ref.shape[0] // TOK_SUB

    @pl.when(fresh_ref[j] == 1)
    def _():
        wgs[0] = wg1[0].astype(BF16)
        wus[0] = wu1[0].astype(BF16)
        wds[0] = wd1[0].astype(BF16)
        wgs[1] = wg2[0].astype(BF16)
        wus[1] = wu2[0].astype(BF16)
        wds[1] = wd2[0].astype(BF16)

    @pl.when(valid_ref[j] == 1)
    def _():
        x = _load_token_major(xs_ref, rows).astype(BF16)
        lg = jnp.dot(x, wsel_ref[0], preferred_element_type=F32) + bsel_ref[0]
        grp = [lg[:, i:i + 1] for i in range(N_GROUPS)]
        gmax = jnp.maximum(jnp.maximum(grp[0], grp[1]), jnp.maximum(grp[2], grp[3]))
        gsum = sum(jnp.exp(v - gmax) for v in grp)
        la, lb = lg[:, N_GROUPS:N_GROUPS + 1], lg[:, N_GROUPS + 1:N_GROUPS + 2]
        emax = jnp.maximum(la, lb)
        ea, eb = jnp.exp(la - emax), jnp.exp(lb - emax)
        scale = 1.0 / (gsum * (ea + eb))
        wcols = (ea * scale, eb * scale)
        acc = None
        for slot in range(2):
            g = jnp.dot(x, wgs[slot], preferred_element_type=F32)
            u = jnp.dot(x, wus[slot], preferred_element_type=F32)
            hid = (g * jax.nn.sigmoid(g) * u * wcols[slot]).astype(BF16)
            y = jnp.dot(hid, wds[slot], preferred_element_type=F32)
            acc = y if acc is None else acc + y
        _store_token_major(y_ref, acc)

    @pl.when(valid_ref[j] == 0)
    def _():
        y_ref[...] = jnp.zeros(y_ref.shape, F32)


def _moe(plan, xs, wsel, bsel, w_gate, w_up, w_down, n_tiles):
    wsel1 = lambda j, e1, e2, v, f, tb, xb: (e1[j], 0, 0)
    wsel2 = lambda j, e1, e2, v, f, tb, xb: (e2[j], 0, 0)
    by_bucket = lambda j, e1, e2, v, f, tb, xb: (tb[j], 0, 0)
    up_spec = lambda sel: pl.BlockSpec((1, D_MODEL, D_EXPERT), sel)
    dn_spec = lambda sel: pl.BlockSpec((1, D_EXPERT, D_MODEL), sel)
    tile_rows = MOE_TILE * TOK_SUB
    return pl.pallas_call(
        _moe_kernel,
        grid_spec=pltpu.PrefetchScalarGridSpec(
            num_scalar_prefetch=6,
            grid=(n_tiles,),
            in_specs=[pl.BlockSpec((tile_rows, LANES), lambda j, e1, e2, v, f, tb, xb: (xb[j], 0)),
                      pl.BlockSpec((1, D_MODEL, LANES), by_bucket),
                      pl.BlockSpec((1, 1, LANES), by_bucket),
                      up_spec(wsel1), up_spec(wsel1), dn_spec(wsel1),
                      up_spec(wsel2), up_spec(wsel2), dn_spec(wsel2)],
            out_specs=pl.BlockSpec((tile_rows, LANES), lambda j, *_: (j, 0)),
            scratch_shapes=[pltpu.VMEM((2, D_MODEL, D_EXPERT), BF16),
                            pltpu.VMEM((2, D_MODEL, D_EXPERT), BF16),
                            pltpu.VMEM((2, D_EXPERT, D_MODEL), BF16)],
        ),
        out_shape=jax.ShapeDtypeStruct((n_tiles * tile_rows, LANES), F32),
        compiler_params=_cparams(("arbitrary",)),
        name="moe",
    )(*plan, xs, wsel, bsel, w_gate, w_up, w_down, w_gate, w_up, w_down)


def _combine_kernel(pos_ref, ys_ref, x1_ref, gt_ref, lg_ref, lb_ref, out_ref, ybuf, sems):
    ts = x1_ref.shape[0]
    i = pl.program_id(0)
    n = pl.num_programs(0)
    slot = i % 2

    def gather(tile, to_slot):
        def issue(g, carry):
            for u in range(DMA_UNROLL):
                r = g * DMA_UNROLL + u
                src = pl.multiple_of(pos_ref[tile * ts + r] * TOK_SUB, TOK_SUB)
                dst = pl.multiple_of(r * TOK_SUB, TOK_SUB)
                pltpu.make_async_copy(ys_ref.at[pl.ds(src, TOK_SUB), :],
                                      ybuf.at[to_slot, pl.ds(dst, TOK_SUB), :],
                                      sems.at[to_slot]).start(priority=u % 2)
            return carry
        lax.fori_loop(0, ts // DMA_UNROLL, issue, 0)

    @pl.when(i == 0)
    def _():
        gather(0, 0)

    @pl.when(i + 1 < n)
    def _():
        gather(i + 1, 1 - slot)

    pltpu.make_async_copy(ys_ref.at[pl.ds(0, ts * TOK_SUB), :], ybuf.at[slot], sems.at[slot]).wait()
    y = _load_token_major(ybuf.at[slot], ts)
    out_ref[...] = _layer_norm(ALPHA * x1_ref[...] + (1.0 + gt_ref[0]) * y, lg_ref[...], lb_ref[...])


def _combine(pos, ys, x1, gt, lg, lb, bsz, seq):
    ts = min(TOK_TILE, seq)
    nt = seq // ts
    tok = bsz * seq
    return pl.pallas_call(
        _combine_kernel,
        grid_spec=pltpu.PrefetchScalarGridSpec(
            num_scalar_prefetch=1,
            grid=(tok // ts,),
            in_specs=[pl.BlockSpec(memory_space=pl.ANY),
                      pl.BlockSpec((ts, D_MODEL), lambda i, pos: (i, 0)),
                      pl.BlockSpec((1, 1, D_MODEL), lambda i, pos: (i // nt, 0, 0)),
                      pl.BlockSpec((1, D_MODEL), lambda i, pos: (0, 0)),
                      pl.BlockSpec((1, D_MODEL), lambda i, pos: (0, 0))],
            out_specs=pl.BlockSpec((ts, D_MODEL), lambda i, pos: (i, 0)),
            scratch_shapes=[pltpu.VMEM((2, ts * TOK_SUB, LANES), F32), pltpu.SemaphoreType.DMA((2,))],
        ),
        out_shape=jax.ShapeDtypeStruct((tok, D_MODEL), F32),
        compiler_params=_cparams(("arbitrary",)),
        name="combine",
    )(pos, ys, x1, gt, lg, lb)


def _route_plan(route, cnt, n_tiles, layer):
    bucket = route[0].astype(I32)
    rank = route[1].astype(I32)
    counts = cnt[:N_BUCKETS, 0].astype(I32)
    tiles_b = (counts + MOE_TILE - 1) // MOE_TILE
    tile_end = jnp.cumsum(tiles_b)
    row_off = (tile_end - tiles_b) * MOE_TILE
    bucket_ids = jnp.arange(N_BUCKETS, dtype=I32)
    pos = jnp.sum(jnp.where(bucket[None, :] == bucket_ids[:, None], row_off[:, None], 0), axis=0) + rank
    total = tile_end[-1]
    j = jnp.arange(n_tiles, dtype=I32)
    valid = j < total
    tb = jnp.sum((tile_end[None, :] <= jnp.minimum(j, total - 1)[:, None]).astype(I32), axis=1)
    tb = jnp.clip(tb, 0, N_BUCKETS - 1)
    grp = tb // N_PAIRS
    pair = tb % N_PAIRS
    pair_a = sum((pair == i).astype(I32) * _PAIR_A[i] for i in range(N_PAIRS))
    pair_b = sum((pair == i).astype(I32) * _PAIR_B[i] for i in range(N_PAIRS))
    e1 = layer * (N_GROUPS * N_EXP) + grp * N_EXP + pair_a
    e2 = layer * (N_GROUPS * N_EXP) + grp * N_EXP + pair_b
    fresh = jnp.concatenate([jnp.ones((1,), I32), (tb[1:] != tb[:-1]).astype(I32)])
    xblk = jnp.minimum(j, total - 1)
    pads = jnp.concatenate([row_off + counts, tiles_b * MOE_TILE - counts, total[None]])
    return pos, pads, (e1, e2, valid.astype(I32), fresh, tb, xblk)


def kernel(x, c, w_ada, b_ada, w_in, b_gates, mh_norm_w, w_conv, w_out, ln1_g, ln1_b,
           w_grp, b_grp, w_router, b_router, w_gate, w_up, w_down, ln2_g, ln2_b):
    bsz, seq, _ = x.shape
    tok = bsz * seq
    n_rows = tok + N_BUCKETS * MOE_TILE
    n_tiles = n_rows // MOE_TILE
    n_exp_total = N_GROUPS * N_EXP

    wg_all = w_gate.reshape(DEPTH * n_exp_total, D_MODEL, D_EXPERT)
    wu_all = w_up.reshape(DEPTH * n_exp_total, D_MODEL, D_EXPERT)
    wd_all = w_down.reshape(DEPTH * n_exp_total, D_EXPERT, D_MODEL)
    mod = _ada_mod(c, w_ada, b_ada).reshape(DEPTH, bsz, 6, 1, D_MODEL)
    x2 = x.reshape(tok, D_MODEL)
    for l in range(DEPTH):
        sh1, sc1, gt1, sh2, sc2, gt2 = [mod[l, :, i] for i in range(6)]
        wl = w_in[l]
        wq = wl[:, :4 * M_WIDTH].astype(BF16)
        wg = jnp.pad(wl[:, 4 * M_WIDTH:4 * M_WIDTH + 2 * HEADS],
                     ((0, 0), (0, LANES - 2 * HEADS))).astype(BF16)
        wc = wl[:, 4 * M_WIDTH + 2 * HEADS:].astype(BF16)
        bg = jnp.pad(b_gates[l], (0, LANES - 2 * HEADS)).reshape(1, LANES)
        qkvo, grow, yc = _inproj(x2, sc1, sh1, wq, wg, wc, bg, w_conv[l], bsz, seq)
        hn = _mlstm(qkvo, grow, mh_norm_w[l].reshape(1, M_WIDTH), bsz, seq)

        n_logit = N_GROUPS + n_exp_total
        wr = jnp.pad(jnp.concatenate([w_grp[l], w_router[l]], axis=1).T,
                     ((0, ROUTE_ROWS - n_logit), (0, 0))).astype(BF16)
        br = jnp.pad(jnp.concatenate([b_grp[l], b_router[l]]), (0, ROUTE_ROWS - n_logit))
        br = jnp.broadcast_to(br[:, None], (ROUTE_ROWS, LANES))
        x1, u2, route, cnt = _outproj(hn, yc, x2, gt1, sc2, sh2, w_out[l].astype(BF16),
                                      ln1_g[l].reshape(1, D_MODEL), ln1_b[l].reshape(1, D_MODEL),
                                      wr, br, bsz, seq)
        pos, pads, plan = _route_plan(route, cnt, n_tiles, l)
        xs = _dispatch(pos, pads, u2, n_rows)
        wsel = jnp.concatenate([jnp.broadcast_to(w_grp[l], (N_BUCKETS, D_MODEL, N_GROUPS)),
                                w_router[l][:, _BUCKET_EA].T[:, :, None],
                                w_router[l][:, _BUCKET_EB].T[:, :, None]], axis=2)
        wsel = jnp.pad(wsel, ((0, 0), (0, 0), (0, LANES - N_GROUPS - 2))).astype(BF16)
        bsel = jnp.concatenate([jnp.broadcast_to(b_grp[l], (N_BUCKETS, N_GROUPS)),
                                b_router[l][_BUCKET_EA][:, None], b_router[l][_BUCKET_EB][:, None]], axis=1)
        bsel = jnp.pad(bsel, ((0, 0), (0, LANES - N_GROUPS - 2))).reshape(N_BUCKETS, 1, LANES)
        ys = _moe(plan, xs, wsel, bsel, wg_all, wu_all, wd_all, n_tiles)
        x2 = _combine(pos, ys, x1, gt2, ln2_g[l].reshape(1, D_MODEL),
                      ln2_b[l].reshape(1, D_MODEL), bsz, seq)
    return x2.reshape(bsz, seq, D_MODEL)
```

```python
import math

import jax
import jax.numpy as jnp
import numpy as np
from jax import lax
from jax.experimental import pallas as pl
from jax.experimental.pallas import tpu as pltpu

F32 = jnp.float32
BF16 = jnp.bfloat16
I32 = jnp.int32

D_MODEL = 1024
DEPTH = 2
HEADS = 4
D_HEAD = 128
M_WIDTH = HEADS * D_HEAD
C_WIDTH = D_MODEL - M_WIDTH
N_GROUPS = 4
N_EXP = 4
D_EXPERT = 512
N_PAIRS = 6
N_BUCKETS = N_GROUPS * N_PAIRS
ALPHA = (2 * DEPTH) ** 0.25
LN_EPS = 1e-5
QK_SCALE = D_HEAD ** -0.5
LOG_QK_SCALE = math.log(QK_SCALE)
GATE_ROWS = 24

LANES = 128
ROUTE_ROWS = 32

TOK_TILE = 512
CHUNK = 128
MOE_TILE = 256
VMEM_LIMIT = 56 * 1024 * 1024

_PAIR_A = (0, 0, 0, 1, 1, 2)
_PAIR_B = (1, 2, 3, 2, 3, 3)
_BUCKET_EA = np.array([g * N_EXP + _PAIR_A[p] for g in range(N_GROUPS) for p in range(N_PAIRS)])
_BUCKET_EB = np.array([g * N_EXP + _PAIR_B[p] for g in range(N_GROUPS) for p in range(N_PAIRS)])


def _cparams(sem):
    return pltpu.CompilerParams(dimension_semantics=sem, vmem_limit_bytes=VMEM_LIMIT)


def _ada_kernel(c_ref, w_ref, b_ref, o_ref):
    c = c_ref[...]
    cond = c * jax.nn.sigmoid(c)
    o_ref[0] = jnp.dot(cond, w_ref[0], preferred_element_type=F32,
                       precision=lax.Precision.HIGHEST) + b_ref[0]


def _ada_mod(c, w_ada, b_ada):
    bsz = c.shape[0]
    nblk = w_ada.shape[2] // D_MODEL
    return pl.pallas_call(
        _ada_kernel,
        grid=(DEPTH, nblk),
        in_specs=[
            pl.BlockSpec((bsz, D_MODEL), lambda l, j: (0, 0)),
            pl.BlockSpec((1, D_MODEL, D_MODEL), lambda l, j: (l, 0, j)),
            pl.BlockSpec((1, 1, D_MODEL), lambda l, j: (l, 0, j)),
        ],
        out_specs=pl.BlockSpec((1, bsz, D_MODEL), lambda l, j: (l, 0, j)),
        out_shape=jax.ShapeDtypeStruct((DEPTH, bsz, nblk * D_MODEL), F32),
        compiler_params=_cparams(("arbitrary", "arbitrary")),
        name="ada_mod",
    )(c, w_ada, b_ada.reshape(DEPTH, 1, -1))


def _log_sigmoid(x):
    return jnp.minimum(x, 0.0) - jnp.log1p(jnp.exp(-jnp.abs(x)))


def _chunk_scan(x, op, identity):
    lane = lax.broadcasted_iota(I32, x.shape, 1) & (CHUNK - 1)
    sh = 1
    while sh < CHUNK:
        x = op(x, jnp.where(lane >= sh, pltpu.roll(x, sh, 1), identity))
        sh *= 2
    return x


def _inproj_kernel(x_ref, sc_ref, sh_ref, wq_ref, wg_ref, wc_ref, bg_ref, wconv_ref,
                   qkvo_ref, grow_ref, yc_ref, zbuf):
    tm = x_ref.shape[0]
    u = (x_ref[...] * (1.0 + sc_ref[0]) + sh_ref[0]).astype(BF16)
    qkvo_ref[...] = jnp.dot(u, wq_ref[...], preferred_element_type=F32).astype(BF16)
    g = jnp.dot(u, wg_ref[...], preferred_element_type=F32) + bg_ref[...]
    gt = g.T[:8]
    head_row = lax.broadcasted_iota(I32, gt.shape, 0) < HEADS
    ig = jnp.where(head_row, gt, 0.0)
    logf = jnp.where(head_row, _log_sigmoid(pltpu.roll(gt, HEADS, 0)), 0.0)
    bcum = _chunk_scan(logf, jnp.add, 0.0)
    ug = ig - bcum
    grow_ref[0:8, :] = ug
    grow_ref[8:16, :] = bcum
    grow_ref[16:24, :] = _chunk_scan(ug, jnp.maximum, -jnp.inf)
    pc = jnp.dot(u, wc_ref[...], preferred_element_type=F32)
    z = pc[:, C_WIDTH:2 * C_WIDTH] * pc[:, 2 * C_WIDTH:]

    @pl.when(pl.program_id(1) == 0)
    def _():
        zbuf[0:8, :] = jnp.zeros((8, C_WIDTH), F32)

    zbuf[8:8 + tm, :] = z
    zc = (wconv_ref[0:1, :] * zbuf[6:6 + tm, :] + wconv_ref[1:2, :] * zbuf[7:7 + tm, :]
          + wconv_ref[2:3, :] * z)
    yc_ref[...] = (pc[:, :C_WIDTH] * zc).astype(BF16)
    zbuf[0:8, :] = zbuf[tm:tm + 8, :]


def _inproj(x2, sc, sh, wq, wg, wc, bg, wconv, bsz, seq):
    tm = min(TOK_TILE, seq)
    nt = seq // tm
    tok = bsz * seq
    row = lambda b, s: (b * nt + s, 0)
    const = lambda b, s: (0, 0)
    mod = lambda b, s: (b, 0, 0)
    return pl.pallas_call(
        _inproj_kernel,
        grid=(bsz, nt),
        in_specs=[
            pl.BlockSpec((tm, D_MODEL), row),
            pl.BlockSpec((1, 1, D_MODEL), mod),
            pl.BlockSpec((1, 1, D_MODEL), mod),
            pl.BlockSpec(wq.shape, const),
            pl.BlockSpec(wg.shape, const),
            pl.BlockSpec(wc.shape, const),
            pl.BlockSpec(bg.shape, const),
            pl.BlockSpec(wconv.shape, const),
        ],
        out_specs=[
            pl.BlockSpec((tm, 4 * M_WIDTH), row),
            pl.BlockSpec((GATE_ROWS, tm), lambda b, s: (0, b * nt + s)),
            pl.BlockSpec((tm, C_WIDTH), row),
        ],
        out_shape=[
            jax.ShapeDtypeStruct((tok, 4 * M_WIDTH), BF16),
            jax.ShapeDtypeStruct((GATE_ROWS, tok), F32),
            jax.ShapeDtypeStruct((tok, C_WIDTH), BF16),
        ],
        scratch_shapes=[pltpu.VMEM((tm + 8, C_WIDTH), F32)],
        compiler_params=_cparams(("arbitrary", "arbitrary")),
        name="inproj",
    )(x2, sc, sh, wq, wg, wc, bg, wconv)


N_ROWQ = 4
SPLIT = 3


def _split3(x):
    hi = x.astype(BF16).astype(F32)
    r1 = x - hi
    mid = r1.astype(BF16).astype(F32)
    return [hi, mid, r1 - mid]


def _mlstm_kernel(qkvo_ref, grow_ref, nw_ref, out_ref, cst, mst, sel):
    sb = qkvo_ref.shape[0]
    L = CHUNK

    @pl.when(pl.program_id(1) == 0)
    def _():
        cst[...] = jnp.zeros(cst.shape, F32)
        mst[...] = jnp.zeros(mst.shape, F32)

    @pl.when((pl.program_id(0) == 0) & (pl.program_id(1) == 0))
    def _():
        r = lax.broadcasted_iota(I32, sel.shape, 0)
        c = lax.broadcasted_iota(I32, sel.shape, 1) // LANES
        rq = r // 8
        quantity = ((rq >= SPLIT).astype(I32) + (rq >= 2 * SPLIT).astype(I32)
                    + (rq >= 3 * SPLIT).astype(I32))
        hit = ((r % 8) == c // N_ROWQ) & (quantity == c % N_ROWQ) & (rq < N_ROWQ * SPLIT)
        sel[...] = hit.astype(BF16)

    causal = lax.broadcasted_iota(I32, (L, L), 0) >= lax.broadcasted_iota(I32, (L, L), 1)
    ones_ext = jnp.ones((L, D_HEAD), BF16)
    pad_rows = jnp.zeros((LANES - 8 * N_ROWQ * SPLIT, L), F32)
    eye = (lax.broadcasted_iota(I32, (L, L), 0) == lax.broadcasted_iota(I32, (L, L), 1)).astype(BF16)
    n_chunks = sb // L
    pairs = [(c, h) for c in range(n_chunks) for h in range(HEADS)]

    def cols(part, c, h):
        c0 = part * M_WIDTH + h * D_HEAD
        return qkvo_ref[c * L:(c + 1) * L, c0:c0 + D_HEAD]

    m_prev = mst[...]
    stack_t, us, decay = [], [], []
    for c in range(n_chunks):
        ug = grow_ref[0:8, c * L:(c + 1) * L]
        bcum = grow_ref[8:16, c * L:(c + 1) * L]
        cmax = grow_ref[16:24, c * L:(c + 1) * L]
        big_m = jnp.maximum(m_prev, cmax)
        m_last = jnp.broadcast_to(big_m[:, L - 1:L], (8, L))
        g_tot = jnp.broadcast_to(bcum[:, L - 1:L], (8, L))
        wq = jnp.exp(m_prev - big_m) * QK_SCALE
        log_em = -(bcum + big_m)
        wk = jnp.exp(ug - m_last)
        decay.append(jnp.exp(m_prev - m_last))
        us.append(ug + LOG_QK_SCALE)
        m_prev = g_tot + m_last
        stack = jnp.concatenate(_split3(big_m) + _split3(wq) + _split3(log_em) + _split3(wk)
                                + [pad_rows], axis=0)
        stack_t.append(stack.T.astype(BF16))
    mst[...] = m_prev

    def rep(c, h, j):
        c0 = (h * N_ROWQ + j) * LANES
        return jnp.dot(stack_t[c], sel[:, c0:c0 + LANES], preferred_element_type=F32)

    s_mat = {ch: lax.dot_general(cols(0, *ch), cols(1, *ch), (((1,), (1,)), ((), ())),
                                 preferred_element_type=F32) for ch in pairs}
    kw = {ch: (cols(1, *ch).astype(F32) * rep(*ch, 3)).astype(BF16) for ch in pairs}
    kw_t = {ch: lax.dot_general(eye, kw[ch], (((1,), (1,)), ((), ())),
                                preferred_element_type=F32).astype(BF16) for ch in pairs}
    lhs = {}
    for ch in pairs:
        c, h = ch
        p = jnp.where(causal, s_mat[ch] * jnp.exp(us[c][h:h + 1, :] - rep(c, h, 0)), 0.0)
        qw = cols(0, c, h).astype(F32) * rep(c, h, 1)
        lhs[ch] = jnp.concatenate([p.astype(BF16), qw.astype(BF16)], axis=1)
    vext = {ch: jnp.concatenate([cols(2, *ch), ones_ext], axis=1) for ch in pairs}
    upd = {ch: jnp.dot(kw_t[ch], vext[ch], preferred_element_type=F32) for ch in pairs}
    tot = {}
    for c in range(n_chunks):
        for h in range(HEADS):
            state = cst[h]
            rhs = jnp.concatenate([vext[(c, h)], state.astype(BF16)], axis=0)
            tot[(c, h)] = jnp.dot(lhs[(c, h)], rhs, preferred_element_type=F32)
            dec = jnp.concatenate([decay[c][h:h + 1, :], decay[c][h:h + 1, :]], axis=1)
            cst[h] = dec * state + upd[(c, h)]
    for ch in pairs:
        c, h = ch
        c0 = h * D_HEAD
        num, den = tot[ch][:, :D_HEAD], tot[ch][:, D_HEAD:]
        hh = num / jnp.maximum(jnp.abs(den), jnp.exp(rep(c, h, 2)))
        mu = jnp.mean(hh, axis=1, keepdims=True)
        var = jnp.mean(jnp.square(hh - mu), axis=1, keepdims=True)
        hn = (hh - mu) * lax.rsqrt(var + LN_EPS) * nw_ref[:, c0:c0 + D_HEAD]
        og = jax.nn.sigmoid(cols(3, c, h).astype(F32))
        out_ref[c * L:(c + 1) * L, c0:c0 + D_HEAD] = (hn * og).astype(BF16)


def _mlstm(qkvo, grow, nw, bsz, seq):
    sb = min(TOK_TILE, seq)
    nt = seq // sb
    tok = bsz * seq
    row = lambda b, s: (b * nt + s, 0)
    return pl.pallas_call(
        _mlstm_kernel,
        grid=(bsz, nt),
        in_specs=[
            pl.BlockSpec((sb, 4 * M_WIDTH), row),
            pl.BlockSpec((GATE_ROWS, sb), lambda b, s: (0, b * nt + s)),
            pl.BlockSpec((1, M_WIDTH), lambda b, s: (0, 0)),
        ],
        out_specs=pl.BlockSpec((sb, M_WIDTH), row),
        out_shape=jax.ShapeDtypeStruct((tok, M_WIDTH), BF16),
        scratch_shapes=[pltpu.VMEM((HEADS, D_HEAD, 2 * D_HEAD), F32),
                        pltpu.VMEM((8, CHUNK), F32),
                        pltpu.VMEM((LANES, HEADS * N_ROWQ * LANES), BF16)],
        compiler_params=_cparams(("arbitrary", "arbitrary")),
        name="mlstm",
    )(qkvo, grow, nw)


def _layer_norm(r, g, b):
    mu = jnp.mean(r, axis=-1, keepdims=True)
    var = jnp.mean(jnp.square(r - mu), axis=-1, keepdims=True)
    return (r - mu) * lax.rsqrt(var + LN_EPS) * g + b


TOK_SUB = D_MODEL // LANES


def _store_token_major(ref, val):
    rows = val.shape[0]
    for k in range(TOK_SUB):
        ref[pl.ds(k, rows, stride=TOK_SUB), :] = val[:, k * LANES:(k + 1) * LANES]


def _load_token_major(ref, rows):
    return jnp.concatenate([ref[pl.ds(k, rows, stride=TOK_SUB), :] for k in range(TOK_SUB)], axis=1)


def _outproj_kernel(hn_ref, yc_ref, x_ref, gt_ref, sc_ref, sh_ref, wo_ref, lg_ref, lb_ref,
                    wr_ref, br_ref, x1_ref, u2_ref, route_ref, cnt_ref, base, triu):
    tm = x_ref.shape[0]

    @pl.when(pl.program_id(0) == 0)
    def _():
        base[...] = jnp.zeros(base.shape, F32)
        rid = lax.broadcasted_iota(I32, (tm, tm), 0)
        cid = lax.broadcasted_iota(I32, (tm, tm), 1)
        triu[...] = (rid <= cid).astype(BF16)

    mix = (jnp.dot(hn_ref[...], wo_ref[0:M_WIDTH, :], preferred_element_type=F32)
           + jnp.dot(yc_ref[...], wo_ref[M_WIDTH:, :], preferred_element_type=F32))
    x1 = _layer_norm(ALPHA * x_ref[...] + (1.0 + gt_ref[0]) * mix, lg_ref[...], lb_ref[...])
    x1_ref[...] = x1
    u2 = x1 * (1.0 + sc_ref[0]) + sh_ref[0]
    _store_token_major(u2_ref, u2)

    lt = lax.dot_general(wr_ref[...], u2.astype(BF16), (((1,), (1,)), ((), ())),
                         preferred_element_type=F32) + br_ref[:, 0:1]
    lg = [lt[j:j + 1, :] for j in range(N_GROUPS + N_GROUPS * N_EXP)]
    best = lg[0]
    grp = jnp.zeros((1, tm), I32)
    for j in range(1, N_GROUPS):
        c = lg[j] > best
        grp = jnp.where(c, j, grp)
        best = jnp.where(c, lg[j], best)
    sel = []
    for e in range(N_EXP):
        val = lg[N_GROUPS + e]
        for gg in range(1, N_GROUPS):
            val = jnp.where(grp == gg, lg[N_GROUPS + gg * N_EXP + e], val)
        sel.append(val)
    v1 = sel[0]
    i1 = jnp.zeros((1, tm), I32)
    for e in range(1, N_EXP):
        c = sel[e] > v1
        i1 = jnp.where(c, e, i1)
        v1 = jnp.where(c, sel[e], v1)
    v2 = jnp.full((1, tm), -jnp.inf, F32)
    i2 = jnp.zeros((1, tm), I32)
    for e in range(N_EXP):
        cand = jnp.where(i1 == e, -jnp.inf, sel[e])
        c = cand > v2
        i2 = jnp.where(c, e, i2)
        v2 = jnp.where(c, cand, v2)
    ea = jnp.minimum(i1, i2)
    eb = jnp.maximum(i1, i2)
    pair = jnp.where(ea == 0, 0, jnp.where(ea == 1, 3, 5)) + eb - ea - 1
    bucket = grp * N_PAIRS + pair

    onehot = lax.broadcasted_iota(I32, (ROUTE_ROWS, tm), 0) == bucket
    cum = jnp.dot(onehot.astype(BF16), triu[...], preferred_element_type=F32)
    prev = base[:, 0:1]
    rank = jnp.sum(jnp.where(onehot, cum - 1.0 + prev, 0.0), axis=0, keepdims=True)
    new_base = prev + cum[:, tm - 1:tm]
    base[...] = jnp.broadcast_to(new_base, base.shape)
    cnt_ref[...] = jnp.broadcast_to(new_base, cnt_ref.shape)

    zrow = jnp.zeros((1, tm), F32)
    route_ref[...] = jnp.concatenate(
        [bucket.astype(F32), rank, zrow, zrow, zrow, zrow, zrow, zrow], axis=0)


def _outproj(hn, yc, x2, gt, sc, sh, wo, lg, lb, wr, br, bsz, seq):
    tm = min(TOK_TILE, seq)
    nt = seq // tm
    tok = bsz * seq
    row = lambda i: (i, 0)
    const = lambda i: (0, 0)
    mod = lambda i: (i // nt, 0, 0)
    return pl.pallas_call(
        _outproj_kernel,
        grid=(bsz * nt,),
        in_specs=[
            pl.BlockSpec((tm, M_WIDTH), row),
            pl.BlockSpec((tm, C_WIDTH), row),
            pl.BlockSpec((tm, D_MODEL), row),
            pl.BlockSpec((1, 1, D_MODEL), mod),
            pl.BlockSpec((1, 1, D_MODEL), mod),
            pl.BlockSpec((1, 1, D_MODEL), mod),
            pl.BlockSpec(wo.shape, const),
            pl.BlockSpec(lg.shape, const),
            pl.BlockSpec(lb.shape, const),
            pl.BlockSpec(wr.shape, const),
            pl.BlockSpec(br.shape, const),
        ],
        out_specs=[
            pl.BlockSpec((tm, D_MODEL), row),
            pl.BlockSpec((tm * TOK_SUB, LANES), row),
            pl.BlockSpec((8, tm), lambda i: (0, i)),
            pl.BlockSpec((ROUTE_ROWS, LANES), const),
        ],
        out_shape=[
            jax.ShapeDtypeStruct((tok, D_MODEL), F32),
            jax.ShapeDtypeStruct((tok * TOK_SUB, LANES), F32),
            jax.ShapeDtypeStruct((8, tok), F32),
            jax.ShapeDtypeStruct((ROUTE_ROWS, LANES), F32),
        ],
        scratch_shapes=[pltpu.VMEM((ROUTE_ROWS, LANES), F32), pltpu.VMEM((tm, tm), BF16)],
        compiler_params=_cparams(("arbitrary",)),
        name="outproj",
    )(hn, yc, x2, gt, sc, sh, wo, lg, lb, wr, br)


DISPATCH_TILE = 1024
DMA_UNROLL = 8
COMBINE_HALF = 256


def _dispatch_kernel(pos_ref, pad_ref, u_ref, xs_ref, zeros, sem, zsem):
    ts = u_ref.shape[0] // TOK_SUB
    t0 = pl.program_id(0) * ts

    def zero_rows(first_row, n_rows_static):
        dst = pl.multiple_of(first_row * TOK_SUB, TOK_SUB)
        return pltpu.make_async_copy(zeros.at[pl.ds(0, n_rows_static * TOK_SUB), :],
                                     xs_ref.at[pl.ds(dst, n_rows_static * TOK_SUB), :], zsem)

    def fill_pads(wait):
        def bucket(b, carry):
            row = pad_ref[b]
            n = pad_ref[N_BUCKETS + b]
            size = MOE_TILE // 2
            while size >= 1:
                @pl.when((n & size) != 0)
                def _(row=row, size=size):
                    cp = zero_rows(row, size)
                    cp.wait() if wait else cp.start()
                row = row + (n & size)
                size //= 2
            return carry
        lax.fori_loop(0, N_BUCKETS, bucket, 0)

        def idle_tile(j, carry):
            cp = zero_rows(j * MOE_TILE, MOE_TILE)
            cp.wait() if wait else cp.start()
            return carry
        lax.fori_loop(pad_ref[2 * N_BUCKETS], xs_ref.shape[0] // (MOE_TILE * TOK_SUB), idle_tile, 0)

    @pl.when(pl.program_id(0) == 0)
    def _():
        zeros[...] = jnp.zeros(zeros.shape, F32)
        fill_pads(wait=False)
        fill_pads(wait=True)

    def issue(g, carry):
        for u in range(DMA_UNROLL):
            r = g * DMA_UNROLL + u
            dst = pl.multiple_of(pos_ref[t0 + r] * TOK_SUB, TOK_SUB)
            src = pl.multiple_of(r * TOK_SUB, TOK_SUB)
            pltpu.make_async_copy(u_ref.at[pl.ds(src, TOK_SUB), :],
                                  xs_ref.at[pl.ds(dst, TOK_SUB), :], sem).start(priority=u % 2)
        return carry

    lax.fori_loop(0, ts // DMA_UNROLL, issue, 0)
    pltpu.make_async_copy(u_ref, xs_ref.at[pl.ds(0, ts * TOK_SUB), :], sem).wait()


def _dispatch(pos, pad_start, u2, n_rows):
    tok = u2.shape[0] // TOK_SUB
    ts = min(DISPATCH_TILE, tok)
    return pl.pallas_call(
        _dispatch_kernel,
        grid_spec=pltpu.PrefetchScalarGridSpec(
            num_scalar_prefetch=2,
            grid=(tok // ts,),
            in_specs=[pl.BlockSpec((ts * TOK_SUB, LANES), lambda i, pos, pad: (i, 0))],
            out_specs=pl.BlockSpec(memory_space=pl.ANY),
            scratch_shapes=[pltpu.VMEM((MOE_TILE * TOK_SUB, LANES), F32),
                            pltpu.SemaphoreType.DMA(()), pltpu.SemaphoreType.DMA(())],
        ),
        out_shape=jax.ShapeDtypeStruct((n_rows * TOK_SUB, LANES), F32),
        compiler_params=_cparams(("arbitrary",)),
        name="dispatch",
    )(pos, pad_start, u2)


def _moe_kernel(e1_ref, e2_ref, valid_ref, fresh_ref, tb_ref, xb_ref, xs_ref, wsel_ref, bsel_ref,
                wg1, wu1, wd1, wg2, wu2, wd2, y_ref, wgs, wus, wds):
    del tb_ref, xb_ref
    j = pl.program_id(0)
    rows = y_ref.shape[0] // TOK_SUB

    @pl.when(fresh_ref[j] == 1)
    def _():
        wgs[0] = wg1[0].astype(BF16)
        wus[0] = wu1[0].astype(BF16)
        wds[0] = wd1[0].astype(BF16)
        wgs[1] = wg2[0].astype(BF16)
        wus[1] = wu2[0].astype(BF16)
        wds[1] = wd2[0].astype(BF16)

    @pl.when(valid_ref[j] == 1)
    def _():
        x = _load_token_major(xs_ref, rows).astype(BF16)
        lg = jnp.dot(x, wsel_ref[0], preferred_element_type=F32) + bsel_ref[0]
        grp = [lg[:, i:i + 1] for i in range(N_GROUPS)]
        gmax = jnp.maximum(jnp.maximum(grp[0], grp[1]), jnp.maximum(grp[2], grp[3]))
        gsum = sum(jnp.exp(v - gmax) for v in grp)
        la, lb = lg[:, N_GROUPS:N_GROUPS + 1], lg[:, N_GROUPS + 1:N_GROUPS + 2]
        emax = jnp.maximum(la, lb)
        ea, eb = jnp.exp(la - emax), jnp.exp(lb - emax)
        scale = 1.0 / (gsum * (ea + eb))
        wcols = (ea * scale, eb * scale)
        acc = None
        for slot in range(2):
            g = jnp.dot(x, wgs[slot], preferred_element_type=F32)
            u = jnp.dot(x, wus[slot], preferred_element_type=F32)
            hid = (g * jax.nn.sigmoid(g) * u * wcols[slot]).astype(BF16)
            y = jnp.dot(hid, wds[slot], preferred_element_type=F32)
            acc = y if acc is None else acc + y
        _store_token_major(y_ref, acc)

    @pl.when(valid_ref[j] == 0)
    def _():
        y_ref[...] = jnp.zeros(y_ref.shape, F32)


def _moe(plan, xs, wsel, bsel, w_gate, w_up, w_down, n_tiles):
    wsel1 = lambda j, e1, e2, v, f, tb, xb: (e1[j], 0, 0)
    wsel2 = lambda j, e1, e2, v, f, tb, xb: (e2[j], 0, 0)
    by_bucket = lambda j, e1, e2, v, f, tb, xb: (tb[j], 0, 0)
    up_spec = lambda sel: pl.BlockSpec((1, D_MODEL, D_EXPERT), sel)
    dn_spec = lambda sel: pl.BlockSpec((1, D_EXPERT, D_MODEL), sel)
    tile_rows = MOE_TILE * TOK_SUB
    return pl.pallas_call(
        _moe_kernel,
        grid_spec=pltpu.PrefetchScalarGridSpec(
            num_scalar_prefetch=6,
            grid=(n_tiles,),
            in_specs=[pl.BlockSpec((tile_rows, LANES), lambda j, e1, e2, v, f, tb, xb: (xb[j], 0)),
                      pl.BlockSpec((1, D_MODEL, LANES), by_bucket),
                      pl.BlockSpec((1, 1, LANES), by_bucket),
                      up_spec(wsel1), up_spec(wsel1), dn_spec(wsel1),
                      up_spec(wsel2), up_spec(wsel2), dn_spec(wsel2)],
            out_specs=pl.BlockSpec((tile_rows, LANES), lambda j, *_: (j, 0)),
            scratch_shapes=[pltpu.VMEM((2, D_MODEL, D_EXPERT), BF16),
                            pltpu.VMEM((2, D_MODEL, D_EXPERT), BF16),
                            pltpu.VMEM((2, D_EXPERT, D_MODEL), BF16)],
        ),
        out_shape=jax.ShapeDtypeStruct((n_tiles * tile_rows, LANES), F32),
        compiler_params=_cparams(("arbitrary",)),
        name="moe",
    )(*plan, xs, wsel, bsel, w_gate, w_up, w_down, w_gate, w_up, w_down)


def _combine_kernel(pos_ref, ys_ref, x1_ref, gt_ref, lg_ref, lb_ref, out_ref, buf_a, buf_b, sem_a, sem_b):
    ts = x1_ref.shape[0] // 2
    i = pl.program_id(0)
    last = pl.num_programs(0) - 1
    t0 = i * 2 * ts

    def request(first_tok, buf, sem):
        for r in range(ts):
            src = pl.multiple_of(pos_ref[first_tok + r] * TOK_SUB, TOK_SUB)
            pltpu.make_async_copy(ys_ref.at[pl.ds(src, TOK_SUB), :],
                                  buf.at[pl.ds(r * TOK_SUB, TOK_SUB), :], sem).start(priority=r % 2)

    def arrived(buf, sem):
        pltpu.make_async_copy(ys_ref.at[pl.ds(0, ts * TOK_SUB), :], buf, sem).wait()

    def normalise(buf, lo):
        y = _load_token_major(buf, ts)
        out_ref[lo:lo + ts, :] = _layer_norm(ALPHA * x1_ref[lo:lo + ts, :] + (1.0 + gt_ref[0]) * y,
                                             lg_ref[...], lb_ref[...])

    @pl.when(i == 0)
    def _():
        request(t0, buf_a, sem_a)

    arrived(buf_a, sem_a)
    request(t0 + ts, buf_b, sem_b)
    normalise(buf_a, 0)
    arrived(buf_b, sem_b)
    request(jnp.where(i == last, t0, t0 + 2 * ts), buf_a, sem_a)
    normalise(buf_b, ts)

    @pl.when(i == last)
    def _():
        arrived(buf_a, sem_a)


def _combine(pos, ys, x1, gt, lg, lb, bsz, seq):
    half = min(COMBINE_HALF, seq // 2)
    ts = 2 * half
    nt = seq // ts
    tok = bsz * seq
    return pl.pallas_call(
        _combine_kernel,
        grid_spec=pltpu.PrefetchScalarGridSpec(
            num_scalar_prefetch=1,
            grid=(tok // ts,),
            in_specs=[pl.BlockSpec(memory_space=pl.ANY),
                      pl.BlockSpec((ts, D_MODEL), lambda i, pos: (i, 0)),
                      pl.BlockSpec((1, 1, D_MODEL), lambda i, pos: (i // nt, 0, 0)),
                      pl.BlockSpec((1, D_MODEL), lambda i, pos: (0, 0)),
                      pl.BlockSpec((1, D_MODEL), lambda i, pos: (0, 0))],
            out_specs=pl.BlockSpec((ts, D_MODEL), lambda i, pos: (i, 0)),
            scratch_shapes=[pltpu.VMEM((half * TOK_SUB, LANES), F32),
                            pltpu.VMEM((half * TOK_SUB, LANES), F32),
                            pltpu.SemaphoreType.DMA(()), pltpu.SemaphoreType.DMA(())],
        ),
        out_shape=jax.ShapeDtypeStruct((tok, D_MODEL), F32),
        compiler_params=_cparams(("arbitrary",)),
        name="combine",
    )(pos, ys, x1, gt, lg, lb)


def _route_plan(route, cnt, n_tiles, layer):
    bucket = route[0].astype(I32)
    rank = route[1].astype(I32)
    counts = cnt[:N_BUCKETS, 0].astype(I32)
    tiles_b = (counts + MOE_TILE - 1) // MOE_TILE
    tile_end = jnp.cumsum(tiles_b)
    row_off = (tile_end - tiles_b) * MOE_TILE
    bucket_ids = jnp.arange(N_BUCKETS, dtype=I32)
    pos = jnp.sum(jnp.where(bucket[None, :] == bucket_ids[:, None], row_off[:, None], 0), axis=0) + rank
    total = tile_end[-1]
    j = jnp.arange(n_tiles, dtype=I32)
    valid = j < total
    tb = jnp.sum((tile_end[None, :] <= jnp.minimum(j, total - 1)[:, None]).astype(I32), axis=1)
    tb = jnp.clip(tb, 0, N_BUCKETS - 1)
    grp = tb // N_PAIRS
    pair = tb % N_PAIRS
    pair_a = sum((pair == i).astype(I32) * _PAIR_A[i] for i in range(N_PAIRS))
    pair_b = sum((pair == i).astype(I32) * _PAIR_B[i] for i in range(N_PAIRS))
    e1 = layer * (N_GROUPS * N_EXP) + grp * N_EXP + pair_a
    e2 = layer * (N_GROUPS * N_EXP) + grp * N_EXP + pair_b
    fresh = jnp.concatenate([jnp.ones((1,), I32), (tb[1:] != tb[:-1]).astype(I32)])
    xblk = jnp.minimum(j, total - 1)
    pads = jnp.concatenate([row_off + counts, tiles_b * MOE_TILE - counts, total[None]])
    return pos, pads, (e1, e2, valid.astype(I32), fresh, tb, xblk)


def kernel(x, c, w_ada, b_ada, w_in, b_gates, mh_norm_w, w_conv, w_out, ln1_g, ln1_b,
           w_grp, b_grp, w_router, b_router, w_gate, w_up, w_down, ln2_g, ln2_b):
    bsz, seq, _ = x.shape
    tok = bsz * seq
    n_rows = tok + N_BUCKETS * MOE_TILE
    n_tiles = n_rows // MOE_TILE
    n_exp_total = N_GROUPS * N_EXP

    wg_all = w_gate.reshape(DEPTH * n_exp_total, D_MODEL, D_EXPERT)
    wu_all = w_up.reshape(DEPTH * n_exp_total, D_MODEL, D_EXPERT)
    wd_all = w_down.reshape(DEPTH * n_exp_total, D_EXPERT, D_MODEL)
    mod = _ada_mod(c, w_ada, b_ada).reshape(DEPTH, bsz, 6, 1, D_MODEL)
    x2 = x.reshape(tok, D_MODEL)
    for l in range(DEPTH):
        sh1, sc1, gt1, sh2, sc2, gt2 = [mod[l, :, i] for i in range(6)]
        wl = w_in[l]
        wq = wl[:, :4 * M_WIDTH].astype(BF16)
        wg = jnp.pad(wl[:, 4 * M_WIDTH:4 * M_WIDTH + 2 * HEADS],
                     ((0, 0), (0, LANES - 2 * HEADS))).astype(BF16)
        wc = wl[:, 4 * M_WIDTH + 2 * HEADS:].astype(BF16)
        bg = jnp.pad(b_gates[l], (0, LANES - 2 * HEADS)).reshape(1, LANES)
        qkvo, grow, yc = _inproj(x2, sc1, sh1, wq, wg, wc, bg, w_conv[l], bsz, seq)
        hn = _mlstm(qkvo, grow, mh_norm_w[l].reshape(1, M_WIDTH), bsz, seq)

        n_logit = N_GROUPS + n_exp_total
        wr = jnp.pad(jnp.concatenate([w_grp[l], w_router[l]], axis=1).T,
                     ((0, ROUTE_ROWS - n_logit), (0, 0))).astype(BF16)
        br = jnp.pad(jnp.concatenate([b_grp[l], b_router[l]]), (0, ROUTE_ROWS - n_logit))
        br = jnp.broadcast_to(br[:, None], (ROUTE_ROWS, LANES))
        x1, u2, route, cnt = _outproj(hn, yc, x2, gt1, sc2, sh2, w_out[l].astype(BF16),
                                      ln1_g[l].reshape(1, D_MODEL), ln1_b[l].reshape(1, D_MODEL),
                                      wr, br, bsz, seq)
        pos, pads, plan = _route_plan(route, cnt, n_tiles, l)
        xs = _dispatch(pos, pads, u2, n_rows)
        wsel = jnp.concatenate([jnp.broadcast_to(w_grp[l], (N_BUCKETS, D_MODEL, N_GROUPS)),
                                w_router[l][:, _BUCKET_EA].T[:, :, None],
                                w_router[l][:, _BUCKET_EB].T[:, :, None]], axis=2)
        wsel = jnp.pad(wsel, ((0, 0), (0, 0), (0, LANES - N_GROUPS - 2))).astype(BF16)
        bsel = jnp.concatenate([jnp.broadcast_to(b_grp[l], (N_BUCKETS, N_GROUPS)),
                                b_router[l][_BUCKET_EA][:, None], b_router[l][_BUCKET_EB][:, None]], axis=1)
        bsel = jnp.pad(bsel, ((0, 0), (0, LANES - N_GROUPS - 2))).reshape(N_BUCKETS, 1, LANES)
        ys = _moe(plan, xs, wsel, bsel, wg_all, wu_all, wd_all, n_tiles)
        x2 = _combine(pos, ys, x1, gt2, ln2_g[l].reshape(1, D_MODEL),
                      ln2_b[l].reshape(1, D_MODEL), bsz, seq)
    return x2.reshape(bsz, seq, D_MODEL)
```

```python
import math

import jax
import jax.numpy as jnp
import numpy as np
from jax import lax
from jax.experimental import pallas as pl
from jax.experimental.pallas import tpu as pltpu

F32 = jnp.float32
BF16 = jnp.bfloat16
I32 = jnp.int32

D_MODEL = 1024
DEPTH = 2
HEADS = 4
D_HEAD = 128
M_WIDTH = HEADS * D_HEAD
C_WIDTH = D_MODEL - M_WIDTH
N_GROUPS = 4
N_EXP = 4
D_EXPERT = 512
N_PAIRS = 6
N_BUCKETS = N_GROUPS * N_PAIRS
ALPHA = (2 * DEPTH) ** 0.25
LN_EPS = 1e-5
QK_SCALE = D_HEAD ** -0.5
LOG_QK_SCALE = math.log(QK_SCALE)
GATE_ROWS = 24

LANES = 128
ROUTE_ROWS = 32

TOK_TILE = 512
CHUNK = 128
MOE_TILE = 256
VMEM_LIMIT = 56 * 1024 * 1024

_PAIR_A = (0, 0, 0, 1, 1, 2)
_PAIR_B = (1, 2, 3, 2, 3, 3)
_BUCKET_EA = np.array([g * N_EXP + _PAIR_A[p] for g in range(N_GROUPS) for p in range(N_PAIRS)])
_BUCKET_EB = np.array([g * N_EXP + _PAIR_B[p] for g in range(N_GROUPS) for p in range(N_PAIRS)])


def _cparams(sem):
    return pltpu.CompilerParams(dimension_semantics=sem, vmem_limit_bytes=VMEM_LIMIT)


def _ada_kernel(c_ref, w_ref, b_ref, o_ref):
    c = c_ref[...]
    cond = c * jax.nn.sigmoid(c)
    o_ref[0] = jnp.dot(cond, w_ref[0], preferred_element_type=F32,
                       precision=lax.Precision.HIGHEST) + b_ref[0]


def _ada_mod(c, w_ada, b_ada):
    bsz = c.shape[0]
    nblk = w_ada.shape[2] // D_MODEL
    return pl.pallas_call(
        _ada_kernel,
        grid=(DEPTH, nblk),
        in_specs=[
            pl.BlockSpec((bsz, D_MODEL), lambda l, j: (0, 0)),
            pl.BlockSpec((1, D_MODEL, D_MODEL), lambda l, j: (l, 0, j)),
            pl.BlockSpec((1, 1, D_MODEL), lambda l, j: (l, 0, j)),
        ],
        out_specs=pl.BlockSpec((1, bsz, D_MODEL), lambda l, j: (l, 0, j)),
        out_shape=jax.ShapeDtypeStruct((DEPTH, bsz, nblk * D_MODEL), F32),
        compiler_params=_cparams(("arbitrary", "arbitrary")),
        name="ada_mod",
    )(c, w_ada, b_ada.reshape(DEPTH, 1, -1))


def _log_sigmoid(x):
    return jnp.minimum(x, 0.0) - jnp.log1p(jnp.exp(-jnp.abs(x)))


def _chunk_scan(x, op, identity):
    lane = lax.broadcasted_iota(I32, x.shape, 1) & (CHUNK - 1)
    sh = 1
    while sh < CHUNK:
        x = op(x, jnp.where(lane >= sh, pltpu.roll(x, sh, 1), identity))
        sh *= 2
    return x


def _inproj_kernel(x_ref, sc_ref, sh_ref, wq_ref, wg_ref, wc_ref, bg_ref, wconv_ref,
                   qkvo_ref, grow_ref, yc_ref, zbuf):
    tm = x_ref.shape[0]
    u = (x_ref[...] * (1.0 + sc_ref[0]) + sh_ref[0]).astype(BF16)
    qkvo_ref[...] = jnp.dot(u, wq_ref[...], preferred_element_type=F32).astype(BF16)
    g = jnp.dot(u, wg_ref[...], preferred_element_type=F32) + bg_ref[...]
    gt = g.T[:8]
    head_row = lax.broadcasted_iota(I32, gt.shape, 0) < HEADS
    ig = jnp.where(head_row, gt, 0.0)
    logf = jnp.where(head_row, _log_sigmoid(pltpu.roll(gt, HEADS, 0)), 0.0)
    bcum = _chunk_scan(logf, jnp.add, 0.0)
    ug = ig - bcum
    grow_ref[0:8, :] = ug
    grow_ref[8:16, :] = bcum
    grow_ref[16:24, :] = _chunk_scan(ug, jnp.maximum, -jnp.inf)
    pc = jnp.dot(u, wc_ref[...], preferred_element_type=F32)
    z = pc[:, C_WIDTH:2 * C_WIDTH] * pc[:, 2 * C_WIDTH:]

    @pl.when(pl.program_id(1) == 0)
    def _():
        zbuf[0:8, :] = jnp.zeros((8, C_WIDTH), F32)

    zbuf[8:8 + tm, :] = z
    zc = (wconv_ref[0:1, :] * zbuf[6:6 + tm, :] + wconv_ref[1:2, :] * zbuf[7:7 + tm, :]
          + wconv_ref[2:3, :] * z)
    yc_ref[...] = (pc[:, :C_WIDTH] * zc).astype(BF16)
    zbuf[0:8, :] = zbuf[tm:tm + 8, :]


def _inproj(x2, sc, sh, wq, wg, wc, bg, wconv, bsz, seq):
    tm = min(TOK_TILE, seq)
    nt = seq // tm
    tok = bsz * seq
    row = lambda b, s: (b * nt + s, 0)
    const = lambda b, s: (0, 0)
    mod = lambda b, s: (b, 0, 0)
    return pl.pallas_call(
        _inproj_kernel,
        grid=(bsz, nt),
        in_specs=[
            pl.BlockSpec((tm, D_MODEL), row),
            pl.BlockSpec((1, 1, D_MODEL), mod),
            pl.BlockSpec((1, 1, D_MODEL), mod),
            pl.BlockSpec(wq.shape, const),
            pl.BlockSpec(wg.shape, const),
            pl.BlockSpec(wc.shape, const),
            pl.BlockSpec(bg.shape, const),
            pl.BlockSpec(wconv.shape, const),
        ],
        out_specs=[
            pl.BlockSpec((tm, 4 * M_WIDTH), row),
            pl.BlockSpec((GATE_ROWS, tm), lambda b, s: (0, b * nt + s)),
            pl.BlockSpec((tm, C_WIDTH), row),
        ],
        out_shape=[
            jax.ShapeDtypeStruct((tok, 4 * M_WIDTH), BF16),
            jax.ShapeDtypeStruct((GATE_ROWS, tok), F32),
            jax.ShapeDtypeStruct((tok, C_WIDTH), BF16),
        ],
        scratch_shapes=[pltpu.VMEM((tm + 8, C_WIDTH), F32)],
        compiler_params=_cparams(("arbitrary", "arbitrary")),
        name="inproj",
    )(x2, sc, sh, wq, wg, wc, bg, wconv)


N_ROWQ = 4
SPLIT = 3


def _split3(x):
    hi = x.astype(BF16).astype(F32)
    r1 = x - hi
    mid = r1.astype(BF16).astype(F32)
    return [hi, mid, r1 - mid]


def _mlstm_kernel(qkvo_ref, grow_ref, nw_ref, out_ref, cst, mst, sel):
    sb = qkvo_ref.shape[0]
    L = CHUNK

    @pl.when(pl.program_id(1) == 0)
    def _():
        cst[...] = jnp.zeros(cst.shape, F32)
        mst[...] = jnp.zeros(mst.shape, F32)

    @pl.when((pl.program_id(0) == 0) & (pl.program_id(1) == 0))
    def _():
        r = lax.broadcasted_iota(I32, sel.shape, 0)
        c = lax.broadcasted_iota(I32, sel.shape, 1) // LANES
        rq = r // 8
        quantity = ((rq >= SPLIT).astype(I32) + (rq >= 2 * SPLIT).astype(I32)
                    + (rq >= 3 * SPLIT).astype(I32))
        hit = ((r % 8) == c // N_ROWQ) & (quantity == c % N_ROWQ) & (rq < N_ROWQ * SPLIT)
        sel[...] = hit.astype(BF16)

    causal = lax.broadcasted_iota(I32, (L, L), 0) >= lax.broadcasted_iota(I32, (L, L), 1)
    ones_ext = jnp.ones((L, D_HEAD), BF16)
    pad_rows = jnp.zeros((LANES - 8 * N_ROWQ * SPLIT, L), F32)
    eye = (lax.broadcasted_iota(I32, (L, L), 0) == lax.broadcasted_iota(I32, (L, L), 1)).astype(BF16)
    n_chunks = sb // L
    pairs = [(c, h) for c in range(n_chunks) for h in range(HEADS)]

    def cols(part, c, h):
        c0 = part * M_WIDTH + h * D_HEAD
        return qkvo_ref[c * L:(c + 1) * L, c0:c0 + D_HEAD]

    m_prev = mst[...]
    stack_t, us, decay = [], [], []
    for c in range(n_chunks):
        ug = grow_ref[0:8, c * L:(c + 1) * L]
        bcum = grow_ref[8:16, c * L:(c + 1) * L]
        cmax = grow_ref[16:24, c * L:(c + 1) * L]
        big_m = jnp.maximum(m_prev, cmax)
        m_last = jnp.broadcast_to(big_m[:, L - 1:L], (8, L))
        g_tot = jnp.broadcast_to(bcum[:, L - 1:L], (8, L))
        wq = jnp.exp(m_prev - big_m) * QK_SCALE
        log_em = -(bcum + big_m)
        wk = jnp.exp(ug - m_last)
        decay.append(jnp.exp(m_prev - m_last))
        us.append(ug + LOG_QK_SCALE)
        m_prev = g_tot + m_last
        stack = jnp.concatenate(_split3(big_m) + _split3(wq) + _split3(log_em) + _split3(wk)
                                + [pad_rows], axis=0)
        stack_t.append(stack.T.astype(BF16))
    mst[...] = m_prev

    def rep(c, h, j):
        c0 = (h * N_ROWQ + j) * LANES
        return jnp.dot(stack_t[c], sel[:, c0:c0 + LANES], preferred_element_type=F32)

    s_mat = {ch: lax.dot_general(cols(0, *ch), cols(1, *ch), (((1,), (1,)), ((), ())),
                                 preferred_element_type=F32) for ch in pairs}
    kw = {ch: (cols(1, *ch).astype(F32) * rep(*ch, 3)).astype(BF16) for ch in pairs}
    kw_t = {ch: lax.dot_general(eye, kw[ch], (((1,), (1,)), ((), ())),
                                preferred_element_type=F32).astype(BF16) for ch in pairs}
    lhs = {}
    for ch in pairs:
        c, h = ch
        p = jnp.where(causal, s_mat[ch] * jnp.exp(us[c][h:h + 1, :] - rep(c, h, 0)), 0.0)
        qw = cols(0, c, h).astype(F32) * rep(c, h, 1)
        lhs[ch] = jnp.concatenate([p.astype(BF16), qw.astype(BF16)], axis=1)
    vext = {ch: jnp.concatenate([cols(2, *ch), ones_ext], axis=1) for ch in pairs}
    upd = {ch: jnp.dot(kw_t[ch], vext[ch], preferred_element_type=F32) for ch in pairs}
    tot = {}
    for c in range(n_chunks):
        for h in range(HEADS):
            state = cst[h]
            rhs = jnp.concatenate([vext[(c, h)], state.astype(BF16)], axis=0)
            tot[(c, h)] = jnp.dot(lhs[(c, h)], rhs, preferred_element_type=F32)
            dec = jnp.concatenate([decay[c][h:h + 1, :], decay[c][h:h + 1, :]], axis=1)
            cst[h] = dec * state + upd[(c, h)]
    for ch in pairs:
        c, h = ch
        c0 = h * D_HEAD
        num, den = tot[ch][:, :D_HEAD], tot[ch][:, D_HEAD:]
        hh = num / jnp.maximum(jnp.abs(den), jnp.exp(rep(c, h, 2)))
        mu = jnp.mean(hh, axis=1, keepdims=True)
        var = jnp.mean(jnp.square(hh - mu), axis=1, keepdims=True)
        hn = (hh - mu) * lax.rsqrt(var + LN_EPS) * nw_ref[:, c0:c0 + D_HEAD]
        og = jax.nn.sigmoid(cols(3, c, h).astype(F32))
        out_ref[c * L:(c + 1) * L, c0:c0 + D_HEAD] = (hn * og).astype(BF16)


def _mlstm(qkvo, grow, nw, bsz, seq):
    sb = min(TOK_TILE, seq)
    nt = seq // sb
    tok = bsz * seq
    row = lambda b, s: (b * nt + s, 0)
    return pl.pallas_call(
        _mlstm_kernel,
        grid=(bsz, nt),
        in_specs=[
            pl.BlockSpec((sb, 4 * M_WIDTH), row),
            pl.BlockSpec((GATE_ROWS, sb), lambda b, s: (0, b * nt + s)),
            pl.BlockSpec((1, M_WIDTH), lambda b, s: (0, 0)),
        ],
        out_specs=pl.BlockSpec((sb, M_WIDTH), row),
        out_shape=jax.ShapeDtypeStruct((tok, M_WIDTH), BF16),
        scratch_shapes=[pltpu.VMEM((HEADS, D_HEAD, 2 * D_HEAD), F32),
                        pltpu.VMEM((8, CHUNK), F32),
                        pltpu.VMEM((LANES, HEADS * N_ROWQ * LANES), BF16)],
        compiler_params=_cparams(("arbitrary", "arbitrary")),
        name="mlstm",
    )(qkvo, grow, nw)


def _layer_norm(r, g, b):
    mu = jnp.mean(r, axis=-1, keepdims=True)
    var = jnp.mean(jnp.square(r - mu), axis=-1, keepdims=True)
    return (r - mu) * lax.rsqrt(var + LN_EPS) * g + b


TOK_SUB = D_MODEL // LANES


def _store_token_major(ref, val):
    rows = val.shape[0]
    for k in range(TOK_SUB):
        ref[pl.ds(k, rows, stride=TOK_SUB), :] = val[:, k * LANES:(k + 1) * LANES]


def _load_token_major(ref, rows):
    return jnp.concatenate([ref[pl.ds(k, rows, stride=TOK_SUB), :] for k in range(TOK_SUB)], axis=1)


def _outproj_kernel(hn_ref, yc_ref, x_ref, gt_ref, sc_ref, sh_ref, wo_ref, lg_ref, lb_ref,
                    wr_ref, br_ref, x1_ref, u2_ref, route_ref, cnt_ref, base, triu):
    tm = x_ref.shape[0]

    @pl.when(pl.program_id(0) == 0)
    def _():
        base[...] = jnp.zeros(base.shape, F32)
        rid = lax.broadcasted_iota(I32, (tm, tm), 0)
        cid = lax.broadcasted_iota(I32, (tm, tm), 1)
        triu[...] = (rid <= cid).astype(BF16)

    mix = (jnp.dot(hn_ref[...], wo_ref[0:M_WIDTH, :], preferred_element_type=F32)
           + jnp.dot(yc_ref[...], wo_ref[M_WIDTH:, :], preferred_element_type=F32))
    x1 = _layer_norm(ALPHA * x_ref[...] + (1.0 + gt_ref[0]) * mix, lg_ref[...], lb_ref[...])
    x1_ref[...] = x1
    u2 = x1 * (1.0 + sc_ref[0]) + sh_ref[0]
    _store_token_major(u2_ref, u2)

    lt = lax.dot_general(wr_ref[...], u2.astype(BF16), (((1,), (1,)), ((), ())),
                         preferred_element_type=F32) + br_ref[:, 0:1]
    lg = [lt[j:j + 1, :] for j in range(N_GROUPS + N_GROUPS * N_EXP)]
    best = lg[0]
    grp = jnp.zeros((1, tm), I32)
    for j in range(1, N_GROUPS):
        c = lg[j] > best
        grp = jnp.where(c, j, grp)
        best = jnp.where(c, lg[j], best)
    sel = []
    for e in range(N_EXP):
        val = lg[N_GROUPS + e]
        for gg in range(1, N_GROUPS):
            val = jnp.where(grp == gg, lg[N_GROUPS + gg * N_EXP + e], val)
        sel.append(val)
    v1 = sel[0]
    i1 = jnp.zeros((1, tm), I32)
    for e in range(1, N_EXP):
        c = sel[e] > v1
        i1 = jnp.where(c, e, i1)
        v1 = jnp.where(c, sel[e], v1)
    v2 = jnp.full((1, tm), -jnp.inf, F32)
    i2 = jnp.zeros((1, tm), I32)
    for e in range(N_EXP):
        cand = jnp.where(i1 == e, -jnp.inf, sel[e])
        c = cand > v2
        i2 = jnp.where(c, e, i2)
        v2 = jnp.where(c, cand, v2)
    ea = jnp.minimum(i1, i2)
    eb = jnp.maximum(i1, i2)
    pair = jnp.where(ea == 0, 0, jnp.where(ea == 1, 3, 5)) + eb - ea - 1
    bucket = grp * N_PAIRS + pair

    onehot = lax.broadcasted_iota(I32, (ROUTE_ROWS, tm), 0) == bucket
    cum = jnp.dot(onehot.astype(BF16), triu[...], preferred_element_type=F32)
    prev = base[:, 0:1]
    rank = jnp.sum(jnp.where(onehot, cum - 1.0 + prev, 0.0), axis=0, keepdims=True)
    new_base = prev + cum[:, tm - 1:tm]
    base[...] = jnp.broadcast_to(new_base, base.shape)
    cnt_ref[...] = jnp.broadcast_to(new_base, cnt_ref.shape)

    zrow = jnp.zeros((1, tm), F32)
    route_ref[...] = jnp.concatenate(
        [bucket.astype(F32), rank, zrow, zrow, zrow, zrow, zrow, zrow], axis=0)


def _outproj(hn, yc, x2, gt, sc, sh, wo, lg, lb, wr, br, bsz, seq):
    tm = min(TOK_TILE, seq)
    nt = seq // tm
    tok = bsz * seq
    row = lambda i: (i, 0)
    const = lambda i: (0, 0)
    mod = lambda i: (i // nt, 0, 0)
    return pl.pallas_call(
        _outproj_kernel,
        grid=(bsz * nt,),
        in_specs=[
            pl.BlockSpec((tm, M_WIDTH), row),
            pl.BlockSpec((tm, C_WIDTH), row),
            pl.BlockSpec((tm, D_MODEL), row),
            pl.BlockSpec((1, 1, D_MODEL), mod),
            pl.BlockSpec((1, 1, D_MODEL), mod),
            pl.BlockSpec((1, 1, D_MODEL), mod),
            pl.BlockSpec(wo.shape, const),
            pl.BlockSpec(lg.shape, const),
            pl.BlockSpec(lb.shape, const),
            pl.BlockSpec(wr.shape, const),
            pl.BlockSpec(br.shape, const),
        ],
        out_specs=[
            pl.BlockSpec((tm, D_MODEL), row),
            pl.BlockSpec((tm * TOK_SUB, LANES), row),
            pl.BlockSpec((8, tm), lambda i: (0, i)),
            pl.BlockSpec((ROUTE_ROWS, LANES), const),
        ],
        out_shape=[
            jax.ShapeDtypeStruct((tok, D_MODEL), F32),
            jax.ShapeDtypeStruct((tok * TOK_SUB, LANES), F32),
            jax.ShapeDtypeStruct((8, tok), F32),
            jax.ShapeDtypeStruct((ROUTE_ROWS, LANES), F32),
        ],
        scratch_shapes=[pltpu.VMEM((ROUTE_ROWS, LANES), F32), pltpu.VMEM((tm, tm), BF16)],
        compiler_params=_cparams(("arbitrary",)),
        name="outproj",
    )(hn, yc, x2, gt, sc, sh, wo, lg, lb, wr, br)


DISPATCH_TILE = 1024
DMA_UNROLL = 8
COMBINE_HALF = 256
COMBINE_PARTS = 4


def _dispatch_kernel(pos_ref, pad_ref, u_ref, xs_ref, zeros, sem, zsem):
    ts = u_ref.shape[0] // TOK_SUB
    t0 = pl.program_id(0) * ts

    def zero_rows(first_row, n_rows_static):
        dst = pl.multiple_of(first_row * TOK_SUB, TOK_SUB)
        return pltpu.make_async_copy(zeros.at[pl.ds(0, n_rows_static * TOK_SUB), :],
                                     xs_ref.at[pl.ds(dst, n_rows_static * TOK_SUB), :], zsem)

    def fill_pads(wait):
        def bucket(b, carry):
            row = pad_ref[b]
            n = pad_ref[N_BUCKETS + b]
            size = MOE_TILE // 2
            while size >= 1:
                @pl.when((n & size) != 0)
                def _(row=row, size=size):
                    cp = zero_rows(row, size)
                    cp.wait() if wait else cp.start()
                row = row + (n & size)
                size //= 2
            return carry
        lax.fori_loop(0, N_BUCKETS, bucket, 0)

        def idle_tile(j, carry):
            cp = zero_rows(j * MOE_TILE, MOE_TILE)
            cp.wait() if wait else cp.start()
            return carry
        lax.fori_loop(pad_ref[2 * N_BUCKETS], xs_ref.shape[0] // (MOE_TILE * TOK_SUB), idle_tile, 0)

    @pl.when(pl.program_id(0) == 0)
    def _():
        zeros[...] = jnp.zeros(zeros.shape, F32)
        fill_pads(wait=False)
        fill_pads(wait=True)

    def issue(g, carry):
        for u in range(DMA_UNROLL):
            r = g * DMA_UNROLL + u
            dst = pl.multiple_of(pos_ref[t0 + r] * TOK_SUB, TOK_SUB)
            src = pl.multiple_of(r * TOK_SUB, TOK_SUB)
            pltpu.make_async_copy(u_ref.at[pl.ds(src, TOK_SUB), :],
                                  xs_ref.at[pl.ds(dst, TOK_SUB), :], sem).start(priority=u % 2)
        return carry

    lax.fori_loop(0, ts // DMA_UNROLL, issue, 0)
    pltpu.make_async_copy(u_ref, xs_ref.at[pl.ds(0, ts * TOK_SUB), :], sem).wait()


def _dispatch(pos, pad_start, u2, n_rows):
    tok = u2.shape[0] // TOK_SUB
    ts = min(DISPATCH_TILE, tok)
    return pl.pallas_call(
        _dispatch_kernel,
        grid_spec=pltpu.PrefetchScalarGridSpec(
            num_scalar_prefetch=2,
            grid=(tok // ts,),
            in_specs=[pl.BlockSpec((ts * TOK_SUB, LANES), lambda i, pos, pad: (i, 0))],
            out_specs=pl.BlockSpec(memory_space=pl.ANY),
            scratch_shapes=[pltpu.VMEM((MOE_TILE * TOK_SUB, LANES), F32),
                            pltpu.SemaphoreType.DMA(()), pltpu.SemaphoreType.DMA(())],
        ),
        out_shape=jax.ShapeDtypeStruct((n_rows * TOK_SUB, LANES), F32),
        compiler_params=_cparams(("arbitrary",)),
        name="dispatch",
    )(pos, pad_start, u2)


def _moe_kernel(e1_ref, e2_ref, valid_ref, fresh_ref, tb_ref, xb_ref, xs_ref, wsel_ref, bsel_ref,
                wg1, wu1, wd1, wg2, wu2, wd2, y_ref, wgs, wus, wds):
    del tb_ref, xb_ref
    j = pl.program_id(0)
    rows = y_ref.shape[0] // TOK_SUB

    @pl.when(fresh_ref[j] == 1)
    def _():
        wgs[0] = wg1[0].astype(BF16)
        wus[0] = wu1[0].astype(BF16)
        wds[0] = wd1[0].astype(BF16)
        wgs[1] = wg2[0].astype(BF16)
        wus[1] = wu2[0].astype(BF16)
        wds[1] = wd2[0].astype(BF16)

    @pl.when(valid_ref[j] == 1)
    def _():
        x = _load_token_major(xs_ref, rows).astype(BF16)
        lg = jnp.dot(x, wsel_ref[0], preferred_element_type=F32) + bsel_ref[0]
        grp = [lg[:, i:i + 1] for i in range(N_GROUPS)]
        gmax = jnp.maximum(jnp.maximum(grp[0], grp[1]), jnp.maximum(grp[2], grp[3]))
        gsum = sum(jnp.exp(v - gmax) for v in grp)
        la, lb = lg[:, N_GROUPS:N_GROUPS + 1], lg[:, N_GROUPS + 1:N_GROUPS + 2]
        emax = jnp.maximum(la, lb)
        ea, eb = jnp.exp(la - emax), jnp.exp(lb - emax)
        scale = 1.0 / (gsum * (ea + eb))
        wcols = (ea * scale, eb * scale)
        acc = None
        for slot in range(2):
            g = jnp.dot(x, wgs[slot], preferred_element_type=F32)
            u = jnp.dot(x, wus[slot], preferred_element_type=F32)
            hid = (g * jax.nn.sigmoid(g) * u * wcols[slot]).astype(BF16)
            y = jnp.dot(hid, wds[slot], preferred_element_type=F32)
            acc = y if acc is None else acc + y
        _store_token_major(y_ref, acc)

    @pl.when(valid_ref[j] == 0)
    def _():
        y_ref[...] = jnp.zeros(y_ref.shape, F32)


def _moe(plan, xs, wsel, bsel, w_gate, w_up, w_down, n_tiles):
    wsel1 = lambda j, e1, e2, v, f, tb, xb: (e1[j], 0, 0)
    wsel2 = lambda j, e1, e2, v, f, tb, xb: (e2[j], 0, 0)
    by_bucket = lambda j, e1, e2, v, f, tb, xb: (tb[j], 0, 0)
    up_spec = lambda sel: pl.BlockSpec((1, D_MODEL, D_EXPERT), sel)
    dn_spec = lambda sel: pl.BlockSpec((1, D_EXPERT, D_MODEL), sel)
    tile_rows = MOE_TILE * TOK_SUB
    return pl.pallas_call(
        _moe_kernel,
        grid_spec=pltpu.PrefetchScalarGridSpec(
            num_scalar_prefetch=6,
            grid=(n_tiles,),
            in_specs=[pl.BlockSpec((tile_rows, LANES), lambda j, e1, e2, v, f, tb, xb: (xb[j], 0)),
                      pl.BlockSpec((1, D_MODEL, LANES), by_bucket),
                      pl.BlockSpec((1, 1, LANES), by_bucket),
                      up_spec(wsel1), up_spec(wsel1), dn_spec(wsel1),
                      up_spec(wsel2), up_spec(wsel2), dn_spec(wsel2)],
            out_specs=pl.BlockSpec((tile_rows, LANES), lambda j, *_: (j, 0)),
            scratch_shapes=[pltpu.VMEM((2, D_MODEL, D_EXPERT), BF16),
                            pltpu.VMEM((2, D_MODEL, D_EXPERT), BF16),
                            pltpu.VMEM((2, D_EXPERT, D_MODEL), BF16)],
        ),
        out_shape=jax.ShapeDtypeStruct((n_tiles * tile_rows, LANES), F32),
        compiler_params=_cparams(("arbitrary",)),
        name="moe",
    )(*plan, xs, wsel, bsel, w_gate, w_up, w_down, w_gate, w_up, w_down)


def _combine_kernel(pos_ref, ys_ref, x1_ref, gt_ref, lg_ref, lb_ref, out_ref, *scratch):
    bufs, sems = scratch[:COMBINE_PARTS], scratch[COMBINE_PARTS:]
    ts = x1_ref.shape[0] // COMBINE_PARTS
    i = pl.program_id(0)
    last = pl.num_programs(0) - 1
    t0 = i * COMBINE_PARTS * ts

    def request(first_tok, k):
        for r in range(ts):
            src = pl.multiple_of(pos_ref[first_tok + r] * TOK_SUB, TOK_SUB)
            pltpu.make_async_copy(ys_ref.at[pl.ds(src, TOK_SUB), :],
                                  bufs[k].at[pl.ds(r * TOK_SUB, TOK_SUB), :],
                                  sems[k]).start(priority=r % 2)

    def arrived(k):
        pltpu.make_async_copy(ys_ref.at[pl.ds(0, ts * TOK_SUB), :], bufs[k], sems[k]).wait()

    @pl.when(i == 0)
    def _():
        request(t0, 0)
        request(t0 + ts, 1)

    for k in range(COMBINE_PARTS):
        arrived(k)
        ahead = t0 + (k + 2) * ts
        if k + 2 >= COMBINE_PARTS:
            ahead = jnp.where(i == last, t0 + (k + 2 - COMBINE_PARTS) * ts, ahead)
        request(ahead, (k + 2) % COMBINE_PARTS)
        lo = k * ts
        y = _load_token_major(bufs[k], ts)
        out_ref[lo:lo + ts, :] = _layer_norm(ALPHA * x1_ref[lo:lo + ts, :] + (1.0 + gt_ref[0]) * y,
                                             lg_ref[...], lb_ref[...])

    @pl.when(i == last)
    def _():
        arrived(0)
        arrived(1)


def _combine(pos, ys, x1, gt, lg, lb, bsz, seq):
    half = min(COMBINE_HALF, seq // COMBINE_PARTS)
    ts = COMBINE_PARTS * half
    nt = seq // ts
    tok = bsz * seq
    return pl.pallas_call(
        _combine_kernel,
        grid_spec=pltpu.PrefetchScalarGridSpec(
            num_scalar_prefetch=1,
            grid=(tok // ts,),
            in_specs=[pl.BlockSpec(memory_space=pl.ANY),
                      pl.BlockSpec((ts, D_MODEL), lambda i, pos: (i, 0)),
                      pl.BlockSpec((1, 1, D_MODEL), lambda i, pos: (i // nt, 0, 0)),
                      pl.BlockSpec((1, D_MODEL), lambda i, pos: (0, 0)),
                      pl.BlockSpec((1, D_MODEL), lambda i, pos: (0, 0))],
            out_specs=pl.BlockSpec((ts, D_MODEL), lambda i, pos: (i, 0)),
            scratch_shapes=([pltpu.VMEM((half * TOK_SUB, LANES), F32)] * COMBINE_PARTS
                            + [pltpu.SemaphoreType.DMA(())] * COMBINE_PARTS),
        ),
        out_shape=jax.ShapeDtypeStruct((tok, D_MODEL), F32),
        compiler_params=_cparams(("arbitrary",)),
        name="combine",
    )(pos, ys, x1, gt, lg, lb)


def _route_plan(route, cnt, n_tiles, layer):
    bucket = route[0].astype(I32)
    rank = route[1].astype(I32)
    counts = cnt[:N_BUCKETS, 0].astype(I32)
    tiles_b = (counts + MOE_TILE - 1) // MOE_TILE
    tile_end = jnp.cumsum(tiles_b)
    row_off = (tile_end - tiles_b) * MOE_TILE
    bucket_ids = jnp.arange(N_BUCKETS, dtype=I32)
    pos = jnp.sum(jnp.where(bucket[None, :] == bucket_ids[:, None], row_off[:, None], 0), axis=0) + rank
    total = tile_end[-1]
    j = jnp.arange(n_tiles, dtype=I32)
    valid = j < total
    tb = jnp.sum((tile_end[None, :] <= jnp.minimum(j, total - 1)[:, None]).astype(I32), axis=1)
    tb = jnp.clip(tb, 0, N_BUCKETS - 1)
    grp = tb // N_PAIRS
    pair = tb % N_PAIRS
    pair_a = sum((pair == i).astype(I32) * _PAIR_A[i] for i in range(N_PAIRS))
    pair_b = sum((pair == i).astype(I32) * _PAIR_B[i] for i in range(N_PAIRS))
    e1 = layer * (N_GROUPS * N_EXP) + grp * N_EXP + pair_a
    e2 = layer * (N_GROUPS * N_EXP) + grp * N_EXP + pair_b
    fresh = jnp.concatenate([jnp.ones((1,), I32), (tb[1:] != tb[:-1]).astype(I32)])
    xblk = jnp.minimum(j, total - 1)
    pads = jnp.concatenate([row_off + counts, tiles_b * MOE_TILE - counts, total[None]])
    return pos, pads, (e1, e2, valid.astype(I32), fresh, tb, xblk)


def kernel(x, c, w_ada, b_ada, w_in, b_gates, mh_norm_w, w_conv, w_out, ln1_g, ln1_b,
           w_grp, b_grp, w_router, b_router, w_gate, w_up, w_down, ln2_g, ln2_b):
    bsz, seq, _ = x.shape
    tok = bsz * seq
    n_rows = tok + N_BUCKETS * MOE_TILE
    n_tiles = n_rows // MOE_TILE
    n_exp_total = N_GROUPS * N_EXP

    wg_all = w_gate.reshape(DEPTH * n_exp_total, D_MODEL, D_EXPERT)
    wu_all = w_up.reshape(DEPTH * n_exp_total, D_MODEL, D_EXPERT)
    wd_all = w_down.reshape(DEPTH * n_exp_total, D_EXPERT, D_MODEL)
    mod = _ada_mod(c, w_ada, b_ada).reshape(DEPTH, bsz, 6, 1, D_MODEL)
    x2 = x.reshape(tok, D_MODEL)
    for l in range(DEPTH):
        sh1, sc1, gt1, sh2, sc2, gt2 = [mod[l, :, i] for i in range(6)]
        wl = w_in[l]
        wq = wl[:, :4 * M_WIDTH].astype(BF16)
        wg = jnp.pad(wl[:, 4 * M_WIDTH:4 * M_WIDTH + 2 * HEADS],
                     ((0, 0), (0, LANES - 2 * HEADS))).astype(BF16)
        wc = wl[:, 4 * M_WIDTH + 2 * HEADS:].astype(BF16)
        bg = jnp.pad(b_gates[l], (0, LANES - 2 * HEADS)).reshape(1, LANES)
        qkvo, grow, yc = _inproj(x2, sc1, sh1, wq, wg, wc, bg, w_conv[l], bsz, seq)
        hn = _mlstm(qkvo, grow, mh_norm_w[l].reshape(1, M_WIDTH), bsz, seq)

        n_logit = N_GROUPS + n_exp_total
        wr = jnp.pad(jnp.concatenate([w_grp[l], w_router[l]], axis=1).T,
                     ((0, ROUTE_ROWS - n_logit), (0, 0))).astype(BF16)
        br = jnp.pad(jnp.concatenate([b_grp[l], b_router[l]]), (0, ROUTE_ROWS - n_logit))
        br = jnp.broadcast_to(br[:, None], (ROUTE_ROWS, LANES))
        x1, u2, route, cnt = _outproj(hn, yc, x2, gt1, sc2, sh2, w_out[l].astype(BF16),
                                      ln1_g[l].reshape(1, D_MODEL), ln1_b[l].reshape(1, D_MODEL),
                                      wr, br, bsz, seq)
        pos, pads, plan = _route_plan(route, cnt, n_tiles, l)
        xs = _dispatch(pos, pads, u2, n_rows)
        wsel = jnp.concatenate([jnp.broadcast_to(w_grp[l], (N_BUCKETS, D_MODEL, N_GROUPS)),
                                w_router[l][:, _BUCKET_EA].T[:, :, None],
                                w_router[l][:, _BUCKET_EB].T[:, :, None]], axis=2)
        wsel = jnp.pad(wsel, ((0, 0), (0, 0), (0, LANES - N_GROUPS - 2))).astype(BF16)
        bsel = jnp.concatenate([jnp.broadcast_to(b_grp[l], (N_BUCKETS, N_GROUPS)),
                                b_router[l][_BUCKET_EA][:, None], b_router[l][_BUCKET_EB][:, None]], axis=1)
        bsel = jnp.pad(bsel, ((0, 0), (0, LANES - N_GROUPS - 2))).reshape(N_BUCKETS, 1, LANES)
        ys = _moe(plan, xs, wsel, bsel, wg_all, wu_all, wd_all, n_tiles)
        x2 = _combine(pos, ys, x1, gt2, ln2_g[l].reshape(1, D_MODEL),
                      ln2_b[l].reshape(1, D_MODEL), bsz, seq)
    return x2.reshape(bsz, seq, D_MODEL)
```

```python
import math

import jax
import jax.numpy as jnp
import numpy as np
from jax import lax
from jax.experimental import pallas as pl
from jax.experimental.pallas import tpu as pltpu

F32 = jnp.float32
BF16 = jnp.bfloat16
I32 = jnp.int32

D_MODEL = 1024
DEPTH = 2
HEADS = 4
D_HEAD = 128
M_WIDTH = HEADS * D_HEAD
C_WIDTH = D_MODEL - M_WIDTH
N_GROUPS = 4
N_EXP = 4
D_EXPERT = 512
N_PAIRS = 6
N_BUCKETS = N_GROUPS * N_PAIRS
ALPHA = (2 * DEPTH) ** 0.25
LN_EPS = 1e-5
QK_SCALE = D_HEAD ** -0.5
LOG_QK_SCALE = math.log(QK_SCALE)
GATE_ROWS = 24

LANES = 128
ROUTE_ROWS = 32

TOK_TILE = 512
CHUNK = 128
MOE_TILE = 256
VMEM_LIMIT = 56 * 1024 * 1024

_PAIR_A = (0, 0, 0, 1, 1, 2)
_PAIR_B = (1, 2, 3, 2, 3, 3)
_BUCKET_EA = np.array([g * N_EXP + _PAIR_A[p] for g in range(N_GROUPS) for p in range(N_PAIRS)])
_BUCKET_EB = np.array([g * N_EXP + _PAIR_B[p] for g in range(N_GROUPS) for p in range(N_PAIRS)])


def _cparams(sem):
    return pltpu.CompilerParams(dimension_semantics=sem, vmem_limit_bytes=VMEM_LIMIT)


def _ada_kernel(c_ref, w_ref, b_ref, o_ref):
    c = c_ref[...]
    cond = c * jax.nn.sigmoid(c)
    o_ref[0] = jnp.dot(cond, w_ref[0], preferred_element_type=F32,
                       precision=lax.Precision.HIGHEST) + b_ref[0]


def _ada_mod(c, w_ada, b_ada):
    bsz = c.shape[0]
    nblk = w_ada.shape[2] // D_MODEL
    return pl.pallas_call(
        _ada_kernel,
        grid=(DEPTH, nblk),
        in_specs=[
            pl.BlockSpec((bsz, D_MODEL), lambda l, j: (0, 0)),
            pl.BlockSpec((1, D_MODEL, D_MODEL), lambda l, j: (l, 0, j)),
            pl.BlockSpec((1, 1, D_MODEL), lambda l, j: (l, 0, j)),
        ],
        out_specs=pl.BlockSpec((1, bsz, D_MODEL), lambda l, j: (l, 0, j)),
        out_shape=jax.ShapeDtypeStruct((DEPTH, bsz, nblk * D_MODEL), F32),
        compiler_params=_cparams(("arbitrary", "arbitrary")),
        name="ada_mod",
    )(c, w_ada, b_ada.reshape(DEPTH, 1, -1))


def _log_sigmoid(x):
    return jnp.minimum(x, 0.0) - jnp.log1p(jnp.exp(-jnp.abs(x)))


def _chunk_scan(x, op, identity):
    lane = lax.broadcasted_iota(I32, x.shape, 1) & (CHUNK - 1)
    sh = 1
    while sh < CHUNK:
        x = op(x, jnp.where(lane >= sh, pltpu.roll(x, sh, 1), identity))
        sh *= 2
    return x


def _inproj_kernel(x_ref, sc_ref, sh_ref, wq_ref, wg_ref, wc_ref, bg_ref, wconv_ref,
                   qkvo_ref, grow_ref, yc_ref, zbuf):
    tm = x_ref.shape[0]
    u = (x_ref[...] * (1.0 + sc_ref[0]) + sh_ref[0]).astype(BF16)
    qkvo_ref[...] = jnp.dot(u, wq_ref[...], preferred_element_type=F32).astype(BF16)
    g = jnp.dot(u, wg_ref[...], preferred_element_type=F32) + bg_ref[...]
    gt = g.T[:8]
    head_row = lax.broadcasted_iota(I32, gt.shape, 0) < HEADS
    ig = jnp.where(head_row, gt, 0.0)
    logf = jnp.where(head_row, _log_sigmoid(pltpu.roll(gt, HEADS, 0)), 0.0)
    bcum = _chunk_scan(logf, jnp.add, 0.0)
    ug = ig - bcum
    grow_ref[0:8, :] = ug
    grow_ref[8:16, :] = bcum
    grow_ref[16:24, :] = _chunk_scan(ug, jnp.maximum, -jnp.inf)
    pc = jnp.dot(u, wc_ref[...], preferred_element_type=F32)
    z = pc[:, C_WIDTH:2 * C_WIDTH] * pc[:, 2 * C_WIDTH:]

    @pl.when(pl.program_id(1) == 0)
    def _():
        zbuf[0:8, :] = jnp.zeros((8, C_WIDTH), F32)

    zbuf[8:8 + tm, :] = z
    zc = (wconv_ref[0:1, :] * zbuf[6:6 + tm, :] + wconv_ref[1:2, :] * zbuf[7:7 + tm, :]
          + wconv_ref[2:3, :] * z)
    yc_ref[...] = (pc[:, :C_WIDTH] * zc).astype(BF16)
    zbuf[0:8, :] = zbuf[tm:tm + 8, :]


def _inproj(x2, sc, sh, wq, wg, wc, bg, wconv, bsz, seq):
    tm = min(TOK_TILE, seq)
    nt = seq // tm
    tok = bsz * seq
    row = lambda b, s: (b * nt + s, 0)
    const = lambda b, s: (0, 0)
    mod = lambda b, s: (b, 0, 0)
    return pl.pallas_call(
        _inproj_kernel,
        grid=(bsz, nt),
        in_specs=[
            pl.BlockSpec((tm, D_MODEL), row),
            pl.BlockSpec((1, 1, D_MODEL), mod),
            pl.BlockSpec((1, 1, D_MODEL), mod),
            pl.BlockSpec(wq.shape, const),
            pl.BlockSpec(wg.shape, const),
            pl.BlockSpec(wc.shape, const),
            pl.BlockSpec(bg.shape, const),
            pl.BlockSpec(wconv.shape, const),
        ],
        out_specs=[
            pl.BlockSpec((tm, 4 * M_WIDTH), row),
            pl.BlockSpec((GATE_ROWS, tm), lambda b, s: (0, b * nt + s)),
            pl.BlockSpec((tm, C_WIDTH), row),
        ],
        out_shape=[
            jax.ShapeDtypeStruct((tok, 4 * M_WIDTH), BF16),
            jax.ShapeDtypeStruct((GATE_ROWS, tok), F32),
            jax.ShapeDtypeStruct((tok, C_WIDTH), BF16),
        ],
        scratch_shapes=[pltpu.VMEM((tm + 8, C_WIDTH), F32)],
        compiler_params=_cparams(("arbitrary", "arbitrary")),
        name="inproj",
    )(x2, sc, sh, wq, wg, wc, bg, wconv)


N_ROWQ = 4
SPLIT = 3


def _split3(x):
    hi = x.astype(BF16).astype(F32)
    r1 = x - hi
    mid = r1.astype(BF16).astype(F32)
    return [hi, mid, r1 - mid]


def _mlstm_kernel(qkvo_ref, grow_ref, nw_ref, out_ref, cst, mst, sel):
    sb = qkvo_ref.shape[0]
    L = CHUNK

    @pl.when(pl.program_id(1) == 0)
    def _():
        cst[...] = jnp.zeros(cst.shape, F32)
        mst[...] = jnp.zeros(mst.shape, F32)

    @pl.when((pl.program_id(0) == 0) & (pl.program_id(1) == 0))
    def _():
        r = lax.broadcasted_iota(I32, sel.shape, 0)
        c = lax.broadcasted_iota(I32, sel.shape, 1) // LANES
        rq = r // 8
        quantity = ((rq >= SPLIT).astype(I32) + (rq >= 2 * SPLIT).astype(I32)
                    + (rq >= 3 * SPLIT).astype(I32))
        hit = ((r % 8) == c // N_ROWQ) & (quantity == c % N_ROWQ) & (rq < N_ROWQ * SPLIT)
        sel[...] = hit.astype(BF16)

    causal = lax.broadcasted_iota(I32, (L, L), 0) >= lax.broadcasted_iota(I32, (L, L), 1)
    ones_ext = jnp.ones((L, D_HEAD), BF16)
    pad_rows = jnp.zeros((LANES - 8 * N_ROWQ * SPLIT, L), F32)
    eye = (lax.broadcasted_iota(I32, (L, L), 0) == lax.broadcasted_iota(I32, (L, L), 1)).astype(BF16)
    n_chunks = sb // L
    pairs = [(c, h) for c in range(n_chunks) for h in range(HEADS)]

    def cols(part, c, h):
        c0 = part * M_WIDTH + h * D_HEAD
        return qkvo_ref[c * L:(c + 1) * L, c0:c0 + D_HEAD]

    m_prev = mst[...]
    stack_t, us, decay = [], [], []
    for c in range(n_chunks):
        ug = grow_ref[0:8, c * L:(c + 1) * L]
        bcum = grow_ref[8:16, c * L:(c + 1) * L]
        cmax = grow_ref[16:24, c * L:(c + 1) * L]
        big_m = jnp.maximum(m_prev, cmax)
        m_last = jnp.broadcast_to(big_m[:, L - 1:L], (8, L))
        g_tot = jnp.broadcast_to(bcum[:, L - 1:L], (8, L))
        wq = jnp.exp(m_prev - big_m) * QK_SCALE
        log_em = -(bcum + big_m)
        wk = jnp.exp(ug - m_last)
        decay.append(jnp.exp(m_prev - m_last))
        us.append(ug + LOG_QK_SCALE)
        m_prev = g_tot + m_last
        stack = jnp.concatenate(_split3(big_m) + _split3(wq) + _split3(log_em) + _split3(wk)
                                + [pad_rows], axis=0)
        stack_t.append(stack.T.astype(BF16))
    mst[...] = m_prev

    def rep(c, h, j):
        c0 = (h * N_ROWQ + j) * LANES
        return jnp.dot(stack_t[c], sel[:, c0:c0 + LANES], preferred_element_type=F32)

    s_mat = {ch: lax.dot_general(cols(0, *ch), cols(1, *ch), (((1,), (1,)), ((), ())),
                                 preferred_element_type=F32) for ch in pairs}
    kw = {ch: (cols(1, *ch).astype(F32) * rep(*ch, 3)).astype(BF16) for ch in pairs}
    kw_t = {ch: lax.dot_general(eye, kw[ch], (((1,), (1,)), ((), ())),
                                preferred_element_type=F32).astype(BF16) for ch in pairs}
    lhs = {}
    for ch in pairs:
        c, h = ch
        p = jnp.where(causal, s_mat[ch] * jnp.exp(us[c][h:h + 1, :] - rep(c, h, 0)), 0.0)
        qw = cols(0, c, h).astype(F32) * rep(c, h, 1)
        lhs[ch] = jnp.concatenate([p.astype(BF16), qw.astype(BF16)], axis=1)
    vext = {ch: jnp.concatenate([cols(2, *ch), ones_ext], axis=1) for ch in pairs}
    upd = {ch: jnp.dot(kw_t[ch], vext[ch], preferred_element_type=F32) for ch in pairs}
    tot = {}
    for c in range(n_chunks):
        for h in range(HEADS):
            state = cst[h]
            rhs = jnp.concatenate([vext[(c, h)], state.astype(BF16)], axis=0)
            tot[(c, h)] = jnp.dot(lhs[(c, h)], rhs, preferred_element_type=F32)
            dec = jnp.concatenate([decay[c][h:h + 1, :], decay[c][h:h + 1, :]], axis=1)
            cst[h] = dec * state + upd[(c, h)]
    for ch in pairs:
        c, h = ch
        c0 = h * D_HEAD
        num, den = tot[ch][:, :D_HEAD], tot[ch][:, D_HEAD:]
        hh = num / jnp.maximum(jnp.abs(den), jnp.exp(rep(c, h, 2)))
        mu = jnp.mean(hh, axis=1, keepdims=True)
        var = jnp.mean(jnp.square(hh - mu), axis=1, keepdims=True)
        hn = (hh - mu) * lax.rsqrt(var + LN_EPS) * nw_ref[:, c0:c0 + D_HEAD]
        og = jax.nn.sigmoid(cols(3, c, h).astype(F32))
        out_ref[c * L:(c + 1) * L, c0:c0 + D_HEAD] = (hn * og).astype(BF16)


def _mlstm(qkvo, grow, nw, bsz, seq):
    sb = min(TOK_TILE, seq)
    nt = seq // sb
    tok = bsz * seq
    row = lambda b, s: (b * nt + s, 0)
    return pl.pallas_call(
        _mlstm_kernel,
        grid=(bsz, nt),
        in_specs=[
            pl.BlockSpec((sb, 4 * M_WIDTH), row),
            pl.BlockSpec((GATE_ROWS, sb), lambda b, s: (0, b * nt + s)),
            pl.BlockSpec((1, M_WIDTH), lambda b, s: (0, 0)),
        ],
        out_specs=pl.BlockSpec((sb, M_WIDTH), row),
        out_shape=jax.ShapeDtypeStruct((tok, M_WIDTH), BF16),
        scratch_shapes=[pltpu.VMEM((HEADS, D_HEAD, 2 * D_HEAD), F32),
                        pltpu.VMEM((8, CHUNK), F32),
                        pltpu.VMEM((LANES, HEADS * N_ROWQ * LANES), BF16)],
        compiler_params=_cparams(("arbitrary", "arbitrary")),
        name="mlstm",
    )(qkvo, grow, nw)


def _layer_norm(r, g, b):
    mu = jnp.mean(r, axis=-1, keepdims=True)
    var = jnp.mean(jnp.square(r - mu), axis=-1, keepdims=True)
    return (r - mu) * lax.rsqrt(var + LN_EPS) * g + b


TOK_SUB = D_MODEL // LANES


def _store_token_major(ref, val):
    rows = val.shape[0]
    for k in range(TOK_SUB):
        ref[pl.ds(k, rows, stride=TOK_SUB), :] = val[:, k * LANES:(k + 1) * LANES]


def _load_token_major(ref, rows):
    return jnp.concatenate([ref[pl.ds(k, rows, stride=TOK_SUB), :] for k in range(TOK_SUB)], axis=1)


def _outproj_kernel(hn_ref, yc_ref, x_ref, gt_ref, sc_ref, sh_ref, wo_ref, lg_ref, lb_ref,
                    wr_ref, br_ref, x1_ref, u2_ref, route_ref, cnt_ref, base, triu):
    tm = x_ref.shape[0]

    @pl.when(pl.program_id(0) == 0)
    def _():
        base[...] = jnp.zeros(base.shape, F32)
        rid = lax.broadcasted_iota(I32, (tm, tm), 0)
        cid = lax.broadcasted_iota(I32, (tm, tm), 1)
        triu[...] = (rid <= cid).astype(BF16)

    mix = (jnp.dot(hn_ref[...], wo_ref[0:M_WIDTH, :], preferred_element_type=F32)
           + jnp.dot(yc_ref[...], wo_ref[M_WIDTH:, :], preferred_element_type=F32))
    x1 = _layer_norm(ALPHA * x_ref[...] + (1.0 + gt_ref[0]) * mix, lg_ref[...], lb_ref[...])
    x1_ref[...] = x1
    u2 = x1 * (1.0 + sc_ref[0]) + sh_ref[0]
    _store_token_major(u2_ref, u2)

    lt = lax.dot_general(wr_ref[...], u2.astype(BF16), (((1,), (1,)), ((), ())),
                         preferred_element_type=F32) + br_ref[:, 0:1]
    lg = [lt[j:j + 1, :] for j in range(N_GROUPS + N_GROUPS * N_EXP)]
    best = lg[0]
    grp = jnp.zeros((1, tm), I32)
    for j in range(1, N_GROUPS):
        c = lg[j] > best
        grp = jnp.where(c, j, grp)
        best = jnp.where(c, lg[j], best)
    sel = []
    for e in range(N_EXP):
        val = lg[N_GROUPS + e]
        for gg in range(1, N_GROUPS):
            val = jnp.where(grp == gg, lg[N_GROUPS + gg * N_EXP + e], val)
        sel.append(val)
    v1 = sel[0]
    i1 = jnp.zeros((1, tm), I32)
    for e in range(1, N_EXP):
        c = sel[e] > v1
        i1 = jnp.where(c, e, i1)
        v1 = jnp.where(c, sel[e], v1)
    v2 = jnp.full((1, tm), -jnp.inf, F32)
    i2 = jnp.zeros((1, tm), I32)
    for e in range(N_EXP):
        cand = jnp.where(i1 == e, -jnp.inf, sel[e])
        c = cand > v2
        i2 = jnp.where(c, e, i2)
        v2 = jnp.where(c, cand, v2)
    ea = jnp.minimum(i1, i2)
    eb = jnp.maximum(i1, i2)
    pair = jnp.where(ea == 0, 0, jnp.where(ea == 1, 3, 5)) + eb - ea - 1
    bucket = grp * N_PAIRS + pair

    onehot = lax.broadcasted_iota(I32, (ROUTE_ROWS, tm), 0) == bucket
    cum = jnp.dot(onehot.astype(BF16), triu[...], preferred_element_type=F32)
    prev = base[:, 0:1]
    rank = jnp.sum(jnp.where(onehot, cum - 1.0 + prev, 0.0), axis=0, keepdims=True)
    new_base = prev + cum[:, tm - 1:tm]
    base[...] = jnp.broadcast_to(new_base, base.shape)
    cnt_ref[...] = jnp.broadcast_to(new_base, cnt_ref.shape)

    zrow = jnp.zeros((1, tm), F32)
    route_ref[...] = jnp.concatenate(
        [bucket.astype(F32), rank, zrow, zrow, zrow, zrow, zrow, zrow], axis=0)


def _outproj(hn, yc, x2, gt, sc, sh, wo, lg, lb, wr, br, bsz, seq):
    tm = min(TOK_TILE, seq)
    nt = seq // tm
    tok = bsz * seq
    row = lambda i: (i, 0)
    const = lambda i: (0, 0)
    mod = lambda i: (i // nt, 0, 0)
    return pl.pallas_call(
        _outproj_kernel,
        grid=(bsz * nt,),
        in_specs=[
            pl.BlockSpec((tm, M_WIDTH), row),
            pl.BlockSpec((tm, C_WIDTH), row),
            pl.BlockSpec((tm, D_MODEL), row),
            pl.BlockSpec((1, 1, D_MODEL), mod),
            pl.BlockSpec((1, 1, D_MODEL), mod),
            pl.BlockSpec((1, 1, D_MODEL), mod),
            pl.BlockSpec(wo.shape, const),
            pl.BlockSpec(lg.shape, const),
            pl.BlockSpec(lb.shape, const),
            pl.BlockSpec(wr.shape, const),
            pl.BlockSpec(br.shape, const),
        ],
        out_specs=[
            pl.BlockSpec((tm, D_MODEL), row),
            pl.BlockSpec((tm * TOK_SUB, LANES), row),
            pl.BlockSpec((8, tm), lambda i: (0, i)),
            pl.BlockSpec((ROUTE_ROWS, LANES), const),
        ],
        out_shape=[
            jax.ShapeDtypeStruct((tok, D_MODEL), F32),
            jax.ShapeDtypeStruct((tok * TOK_SUB, LANES), F32),
            jax.ShapeDtypeStruct((8, tok), F32),
            jax.ShapeDtypeStruct((ROUTE_ROWS, LANES), F32),
        ],
        scratch_shapes=[pltpu.VMEM((ROUTE_ROWS, LANES), F32), pltpu.VMEM((tm, tm), BF16)],
        compiler_params=_cparams(("arbitrary",)),
        name="outproj",
    )(hn, yc, x2, gt, sc, sh, wo, lg, lb, wr, br)


DMA_UNROLL = 8


def _slot_token_kernel(pos_ref, zero_ref, inv_ref, sem):
    fill = pltpu.make_async_copy(zero_ref, inv_ref, sem)
    fill.start()
    fill.wait()

    def put(g, carry):
        for u in range(DMA_UNROLL):
            t = g * DMA_UNROLL + u
            inv_ref[pos_ref[t]] = t
        return carry

    lax.fori_loop(0, pos_ref.shape[0] // DMA_UNROLL, put, 0)


def _slot_token(pos, n_rows):
    return pl.pallas_call(
        _slot_token_kernel,
        in_specs=[pl.BlockSpec(memory_space=pltpu.SMEM), pl.BlockSpec(memory_space=pl.ANY)],
        out_specs=pl.BlockSpec(memory_space=pltpu.SMEM),
        out_shape=jax.ShapeDtypeStruct((n_rows,), I32),
        scratch_shapes=[pltpu.SemaphoreType.DMA(())],
        name="slot_token",
    )(pos, jnp.zeros((n_rows,), I32))


def _moe_kernel(e1_ref, e2_ref, valid_ref, fresh_ref, tb_ref, inv_ref, u_ref, wsel_ref, bsel_ref,
                wg1, wu1, wd1, wg2, wu2, wd2, y_ref, xin0, xin1, sem0, sem1, wgs, wus, wds):
    del tb_ref
    j = pl.program_id(0)
    n = pl.num_programs(0)
    rows = y_ref.shape[0] // TOK_SUB
    total = fresh_ref[n]
    bufs = ((xin0, sem0), (xin1, sem1))

    def request(tile, buf, sem, written_out):
        base = tile * rows

        def one(r, prio):
            src = pl.multiple_of(inv_ref[base + r] * TOK_SUB, TOK_SUB)
            dst = r * TOK_SUB if isinstance(r, int) else pl.multiple_of(r * TOK_SUB, TOK_SUB)
            pltpu.make_async_copy(u_ref.at[pl.ds(src, TOK_SUB), :],
                                  buf.at[pl.ds(dst, TOK_SUB), :], sem).start(priority=prio)

        if written_out:
            for r in range(rows):
                one(r, r % 2)
        else:
            def group(g, carry):
                for u in range(DMA_UNROLL):
                    one(g * DMA_UNROLL + u, u % 2)
                return carry
            lax.fori_loop(0, rows // DMA_UNROLL, group, 0)

    def arrived(buf, sem):
        pltpu.make_async_copy(u_ref.at[pl.ds(0, rows * TOK_SUB), :], buf, sem).wait()

    @pl.when(j == 0)
    def _():
        request(0, xin0, sem0, written_out=False)

    @pl.when(fresh_ref[j] == 1)
    def _():
        wgs[0] = wg1[0].astype(BF16)
        wus[0] = wu1[0].astype(BF16)
        wds[0] = wd1[0].astype(BF16)
        wgs[1] = wg2[0].astype(BF16)
        wus[1] = wu2[0].astype(BF16)
        wds[1] = wd2[0].astype(BF16)

    def experts(x_ref):
        x = _load_token_major(x_ref, rows).astype(BF16)
        lg = jnp.dot(x, wsel_ref[0], preferred_element_type=F32) + bsel_ref[0]
        grp = [lg[:, i:i + 1] for i in range(N_GROUPS)]
        gmax = jnp.maximum(jnp.maximum(grp[0], grp[1]), jnp.maximum(grp[2], grp[3]))
        gsum = sum(jnp.exp(v - gmax) for v in grp)
        la, lb = lg[:, N_GROUPS:N_GROUPS + 1], lg[:, N_GROUPS + 1:N_GROUPS + 2]
        emax = jnp.maximum(la, lb)
        ea, eb = jnp.exp(la - emax), jnp.exp(lb - emax)
        scale = 1.0 / (gsum * (ea + eb))
        wcols = (ea * scale, eb * scale)
        acc = None
        for slot in range(2):
            g = jnp.dot(x, wgs[slot], preferred_element_type=F32)
            u = jnp.dot(x, wus[slot], preferred_element_type=F32)
            hid = (g * jax.nn.sigmoid(g) * u * wcols[slot]).astype(BF16)
            y = jnp.dot(hid, wds[slot], preferred_element_type=F32)
            acc = y if acc is None else acc + y
        _store_token_major(y_ref, acc)

    def step(parity):
        (cur, csem), (oth, osem) = bufs[parity], bufs[1 - parity]

        @pl.when(valid_ref[j] == 1)
        def _():
            arrived(cur, csem)
            request(jnp.minimum(j + 1, total - 1), oth, osem, written_out=True)
            experts(cur)

        @pl.when((valid_ref[j] == 1) & (j == n - 1))
        def _():
            arrived(oth, osem)

        @pl.when(valid_ref[j] == 0)
        def _():
            @pl.when(j == total)
            def _():
                arrived(cur, csem)
            y_ref[...] = jnp.zeros(y_ref.shape, F32)

    @pl.when(j % 2 == 0)
    def _():
        step(0)

    @pl.when(j % 2 == 1)
    def _():
        step(1)


def _moe(plan, u2, wsel, bsel, w_gate, w_up, w_down, n_tiles):
    wsel1 = lambda j, e1, e2, v, f, tb, inv: (e1[j], 0, 0)
    wsel2 = lambda j, e1, e2, v, f, tb, inv: (e2[j], 0, 0)
    by_bucket = lambda j, e1, e2, v, f, tb, inv: (tb[j], 0, 0)
    up_spec = lambda sel: pl.BlockSpec((1, D_MODEL, D_EXPERT), sel)
    dn_spec = lambda sel: pl.BlockSpec((1, D_EXPERT, D_MODEL), sel)
    tile_rows = MOE_TILE * TOK_SUB
    return pl.pallas_call(
        _moe_kernel,
        grid_spec=pltpu.PrefetchScalarGridSpec(
            num_scalar_prefetch=6,
            grid=(n_tiles,),
            in_specs=[pl.BlockSpec(memory_space=pl.ANY),
                      pl.BlockSpec((1, D_MODEL, LANES), by_bucket),
                      pl.BlockSpec((1, 1, LANES), by_bucket),
                      up_spec(wsel1), up_spec(wsel1), dn_spec(wsel1),
                      up_spec(wsel2), up_spec(wsel2), dn_spec(wsel2)],
            out_specs=pl.BlockSpec((tile_rows, LANES), lambda j, *_: (j, 0)),
            scratch_shapes=[pltpu.VMEM((tile_rows, LANES), F32), pltpu.VMEM((tile_rows, LANES), F32),
                            pltpu.SemaphoreType.DMA(()), pltpu.SemaphoreType.DMA(()),
                            pltpu.VMEM((2, D_MODEL, D_EXPERT), BF16),
                            pltpu.VMEM((2, D_MODEL, D_EXPERT), BF16),
                            pltpu.VMEM((2, D_EXPERT, D_MODEL), BF16)],
        ),
        out_shape=jax.ShapeDtypeStruct((n_tiles * tile_rows, LANES), F32),
        compiler_params=_cparams(("arbitrary",)),
        name="moe",
    )(*plan, u2, wsel, bsel, w_gate, w_up, w_down, w_gate, w_up, w_down)


def _combine_kernel(pos_ref, ys_ref, x1_ref, gt_ref, lg_ref, lb_ref, out_ref, ybuf, sems):
    ts = x1_ref.shape[0]
    i = pl.program_id(0)
    n = pl.num_programs(0)
    slot = i % 2

    def gather(tile, to_slot):
        def issue(g, carry):
            for u in range(DMA_UNROLL):
                r = g * DMA_UNROLL + u
                src = pl.multiple_of(pos_ref[tile * ts + r] * TOK_SUB, TOK_SUB)
                dst = pl.multiple_of(r * TOK_SUB, TOK_SUB)
                pltpu.make_async_copy(ys_ref.at[pl.ds(src, TOK_SUB), :],
                                      ybuf.at[to_slot, pl.ds(dst, TOK_SUB), :],
                                      sems.at[to_slot]).start(priority=u % 2)
            return carry
        lax.fori_loop(0, ts // DMA_UNROLL, issue, 0)

    @pl.when(i == 0)
    def _():
        gather(0, 0)

    @pl.when(i + 1 < n)
    def _():
        gather(i + 1, 1 - slot)

    pltpu.make_async_copy(ys_ref.at[pl.ds(0, ts * TOK_SUB), :], ybuf.at[slot], sems.at[slot]).wait()
    y = _load_token_major(ybuf.at[slot], ts)
    out_ref[...] = _layer_norm(ALPHA * x1_ref[...] + (1.0 + gt_ref[0]) * y, lg_ref[...], lb_ref[...])


def _combine(pos, ys, x1, gt, lg, lb, bsz, seq):
    ts = min(TOK_TILE, seq)
    nt = seq // ts
    tok = bsz * seq
    return pl.pallas_call(
        _combine_kernel,
        grid_spec=pltpu.PrefetchScalarGridSpec(
            num_scalar_prefetch=1,
            grid=(tok // ts,),
            in_specs=[pl.BlockSpec(memory_space=pl.ANY),
                      pl.BlockSpec((ts, D_MODEL), lambda i, pos: (i, 0)),
                      pl.BlockSpec((1, 1, D_MODEL), lambda i, pos: (i // nt, 0, 0)),
                      pl.BlockSpec((1, D_MODEL), lambda i, pos: (0, 0)),
                      pl.BlockSpec((1, D_MODEL), lambda i, pos: (0, 0))],
            out_specs=pl.BlockSpec((ts, D_MODEL), lambda i, pos: (i, 0)),
            scratch_shapes=[pltpu.VMEM((2, ts * TOK_SUB, LANES), F32), pltpu.SemaphoreType.DMA((2,))],
        ),
        out_shape=jax.ShapeDtypeStruct((tok, D_MODEL), F32),
        compiler_params=_cparams(("arbitrary",)),
        name="combine",
    )(pos, ys, x1, gt, lg, lb)


def _route_plan(route, cnt, n_tiles, layer):
    bucket = route[0].astype(I32)
    rank = route[1].astype(I32)
    counts = cnt[:N_BUCKETS, 0].astype(I32)
    tiles_b = (counts + MOE_TILE - 1) // MOE_TILE
    tile_end = jnp.cumsum(tiles_b)
    row_off = (tile_end - tiles_b) * MOE_TILE
    bucket_ids = jnp.arange(N_BUCKETS, dtype=I32)
    pos = jnp.sum(jnp.where(bucket[None, :] == bucket_ids[:, None], row_off[:, None], 0), axis=0) + rank
    total = tile_end[-1]
    j = jnp.arange(n_tiles, dtype=I32)
    valid = j < total
    tb = jnp.sum((tile_end[None, :] <= jnp.minimum(j, total - 1)[:, None]).astype(I32), axis=1)
    tb = jnp.clip(tb, 0, N_BUCKETS - 1)
    grp = tb // N_PAIRS
    pair = tb % N_PAIRS
    pair_a = sum((pair == i).astype(I32) * _PAIR_A[i] for i in range(N_PAIRS))
    pair_b = sum((pair == i).astype(I32) * _PAIR_B[i] for i in range(N_PAIRS))
    e1 = layer * (N_GROUPS * N_EXP) + grp * N_EXP + pair_a
    e2 = layer * (N_GROUPS * N_EXP) + grp * N_EXP + pair_b
    fresh = jnp.concatenate([jnp.ones((1,), I32), (tb[1:] != tb[:-1]).astype(I32), total[None]])
    return pos, (e1, e2, valid.astype(I32), fresh, tb)


def kernel(x, c, w_ada, b_ada, w_in, b_gates, mh_norm_w, w_conv, w_out, ln1_g, ln1_b,
           w_grp, b_grp, w_router, b_router, w_gate, w_up, w_down, ln2_g, ln2_b):
    bsz, seq, _ = x.shape
    tok = bsz * seq
    n_rows = tok + N_BUCKETS * MOE_TILE
    n_tiles = n_rows // MOE_TILE
    n_exp_total = N_GROUPS * N_EXP

    wg_all = w_gate.reshape(DEPTH * n_exp_total, D_MODEL, D_EXPERT)
    wu_all = w_up.reshape(DEPTH * n_exp_total, D_MODEL, D_EXPERT)
    wd_all = w_down.reshape(DEPTH * n_exp_total, D_EXPERT, D_MODEL)
    mod = _ada_mod(c, w_ada, b_ada).reshape(DEPTH, bsz, 6, 1, D_MODEL)
    x2 = x.reshape(tok, D_MODEL)
    for l in range(DEPTH):
        sh1, sc1, gt1, sh2, sc2, gt2 = [mod[l, :, i] for i in range(6)]
        wl = w_in[l]
        wq = wl[:, :4 * M_WIDTH].astype(BF16)
        wg = jnp.pad(wl[:, 4 * M_WIDTH:4 * M_WIDTH + 2 * HEADS],
                     ((0, 0), (0, LANES - 2 * HEADS))).astype(BF16)
        wc = wl[:, 4 * M_WIDTH + 2 * HEADS:].astype(BF16)
        bg = jnp.pad(b_gates[l], (0, LANES - 2 * HEADS)).reshape(1, LANES)
        qkvo, grow, yc = _inproj(x2, sc1, sh1, wq, wg, wc, bg, w_conv[l], bsz, seq)
        hn = _mlstm(qkvo, grow, mh_norm_w[l].reshape(1, M_WIDTH), bsz, seq)

        n_logit = N_GROUPS + n_exp_total
        wr = jnp.pad(jnp.concatenate([w_grp[l], w_router[l]], axis=1).T,
                     ((0, ROUTE_ROWS - n_logit), (0, 0))).astype(BF16)
        br = jnp.pad(jnp.concatenate([b_grp[l], b_router[l]]), (0, ROUTE_ROWS - n_logit))
        br = jnp.broadcast_to(br[:, None], (ROUTE_ROWS, LANES))
        x1, u2, route, cnt = _outproj(hn, yc, x2, gt1, sc2, sh2, w_out[l].astype(BF16),
                                      ln1_g[l].reshape(1, D_MODEL), ln1_b[l].reshape(1, D_MODEL),
                                      wr, br, bsz, seq)
        pos, plan = _route_plan(route, cnt, n_tiles, l)
        plan = plan + (_slot_token(pos, n_rows),)
        wsel = jnp.concatenate([jnp.broadcast_to(w_grp[l], (N_BUCKETS, D_MODEL, N_GROUPS)),
                                w_router[l][:, _BUCKET_EA].T[:, :, None],
                                w_router[l][:, _BUCKET_EB].T[:, :, None]], axis=2)
        wsel = jnp.pad(wsel, ((0, 0), (0, 0), (0, LANES - N_GROUPS - 2))).astype(BF16)
        bsel = jnp.concatenate([jnp.broadcast_to(b_grp[l], (N_BUCKETS, N_GROUPS)),
                                b_router[l][_BUCKET_EA][:, None], b_router[l][_BUCKET_EB][:, None]], axis=1)
        bsel = jnp.pad(bsel, ((0, 0), (0, LANES - N_GROUPS - 2))).reshape(N_BUCKETS, 1, LANES)
        ys = _moe(plan, u2, wsel, bsel, wg_all, wu_all, wd_all, n_tiles)
        x2 = _combine(pos, ys, x1, gt2, ln2_g[l].reshape(1, D_MODEL),
                      ln2_b[l].reshape(1, D_MODEL), bsz, seq)
    return x2.reshape(bsz, seq, D_MODEL)
```

```python
import math

import jax
import jax.numpy as jnp
import numpy as np
from jax import lax
from jax.experimental import pallas as pl
from jax.experimental.pallas import tpu as pltpu

F32 = jnp.float32
BF16 = jnp.bfloat16
I32 = jnp.int32

D_MODEL = 1024
DEPTH = 2
HEADS = 4
D_HEAD = 128
M_WIDTH = HEADS * D_HEAD
C_WIDTH = D_MODEL - M_WIDTH
N_GROUPS = 4
N_EXP = 4
D_EXPERT = 512
N_PAIRS = 6
N_BUCKETS = N_GROUPS * N_PAIRS
ALPHA = (2 * DEPTH) ** 0.25
LN_EPS = 1e-5
QK_SCALE = D_HEAD ** -0.5
LOG_QK_SCALE = math.log(QK_SCALE)
GATE_ROWS = 24

LANES = 128
ROUTE_ROWS = 32

TOK_TILE = 512
CHUNK = 128
MOE_TILE = 256
VMEM_LIMIT = 56 * 1024 * 1024

_PAIR_ORDER = (0, 2, 3, 1, 4, 5)
_SLOT0 = (0, 2, 2, 3, 3, 3)
_SLOT1 = (1, 1, 0, 0, 1, 2)
_BUCKET_S0 = np.array([g * N_EXP + _SLOT0[p] for g in range(N_GROUPS) for p in range(N_PAIRS)])
_BUCKET_S1 = np.array([g * N_EXP + _SLOT1[p] for g in range(N_GROUPS) for p in range(N_PAIRS)])


def _cparams(sem):
    return pltpu.CompilerParams(dimension_semantics=sem, vmem_limit_bytes=VMEM_LIMIT)


def _ada_kernel(c_ref, w_ref, b_ref, o_ref):
    c = c_ref[...]
    cond = c * jax.nn.sigmoid(c)
    o_ref[0] = jnp.dot(cond, w_ref[0], preferred_element_type=F32,
                       precision=lax.Precision.HIGHEST) + b_ref[0]


def _ada_mod(c, w_ada, b_ada):
    bsz = c.shape[0]
    nblk = w_ada.shape[2] // D_MODEL
    return pl.pallas_call(
        _ada_kernel,
        grid=(DEPTH, nblk),
        in_specs=[
            pl.BlockSpec((bsz, D_MODEL), lambda l, j: (0, 0)),
            pl.BlockSpec((1, D_MODEL, D_MODEL), lambda l, j: (l, 0, j)),
            pl.BlockSpec((1, 1, D_MODEL), lambda l, j: (l, 0, j)),
        ],
        out_specs=pl.BlockSpec((1, bsz, D_MODEL), lambda l, j: (l, 0, j)),
        out_shape=jax.ShapeDtypeStruct((DEPTH, bsz, nblk * D_MODEL), F32),
        compiler_params=_cparams(("arbitrary", "arbitrary")),
        name="ada_mod",
    )(c, w_ada, b_ada.reshape(DEPTH, 1, -1))


def _log_sigmoid(x):
    return jnp.minimum(x, 0.0) - jnp.log1p(jnp.exp(-jnp.abs(x)))


def _chunk_scan(x, op, identity):
    lane = lax.broadcasted_iota(I32, x.shape, 1) & (CHUNK - 1)
    sh = 1
    while sh < CHUNK:
        x = op(x, jnp.where(lane >= sh, pltpu.roll(x, sh, 1), identity))
        sh *= 2
    return x


def _inproj_kernel(x_ref, sc_ref, sh_ref, wq_ref, wg_ref, wc_ref, bg_ref, wconv_ref,
                   qkvo_ref, grow_ref, yc_ref, zbuf):
    tm = x_ref.shape[0]
    u = (x_ref[...] * (1.0 + sc_ref[0]) + sh_ref[0]).astype(BF16)
    qkvo_ref[...] = jnp.dot(u, wq_ref[...], preferred_element_type=F32).astype(BF16)
    g = jnp.dot(u, wg_ref[...], preferred_element_type=F32) + bg_ref[...]
    gt = g.T[:8]
    head_row = lax.broadcasted_iota(I32, gt.shape, 0) < HEADS
    ig = jnp.where(head_row, gt, 0.0)
    logf = jnp.where(head_row, _log_sigmoid(pltpu.roll(gt, HEADS, 0)), 0.0)
    bcum = _chunk_scan(logf, jnp.add, 0.0)
    ug = ig - bcum
    grow_ref[0:8, :] = ug
    grow_ref[8:16, :] = bcum
    grow_ref[16:24, :] = _chunk_scan(ug, jnp.maximum, -jnp.inf)
    pc = jnp.dot(u, wc_ref[...], preferred_element_type=F32)
    z = pc[:, C_WIDTH:2 * C_WIDTH] * pc[:, 2 * C_WIDTH:]

    @pl.when(pl.program_id(1) == 0)
    def _():
        zbuf[0:8, :] = jnp.zeros((8, C_WIDTH), F32)

    zbuf[8:8 + tm, :] = z
    zc = (wconv_ref[0:1, :] * zbuf[6:6 + tm, :] + wconv_ref[1:2, :] * zbuf[7:7 + tm, :]
          + wconv_ref[2:3, :] * z)
    yc_ref[...] = (pc[:, :C_WIDTH] * zc).astype(BF16)
    zbuf[0:8, :] = zbuf[tm:tm + 8, :]


def _inproj(x2, sc, sh, wq, wg, wc, bg, wconv, bsz, seq):
    tm = min(TOK_TILE, seq)
    nt = seq // tm
    tok = bsz * seq
    row = lambda b, s: (b * nt + s, 0)
    const = lambda b, s: (0, 0)
    mod = lambda b, s: (b, 0, 0)
    return pl.pallas_call(
        _inproj_kernel,
        grid=(bsz, nt),
        in_specs=[
            pl.BlockSpec((tm, D_MODEL), row),
            pl.BlockSpec((1, 1, D_MODEL), mod),
            pl.BlockSpec((1, 1, D_MODEL), mod),
            pl.BlockSpec(wq.shape, const),
            pl.BlockSpec(wg.shape, const),
            pl.BlockSpec(wc.shape, const),
            pl.BlockSpec(bg.shape, const),
            pl.BlockSpec(wconv.shape, const),
        ],
        out_specs=[
            pl.BlockSpec((tm, 4 * M_WIDTH), row),
            pl.BlockSpec((GATE_ROWS, tm), lambda b, s: (0, b * nt + s)),
            pl.BlockSpec((tm, C_WIDTH), row),
        ],
        out_shape=[
            jax.ShapeDtypeStruct((tok, 4 * M_WIDTH), BF16),
            jax.ShapeDtypeStruct((GATE_ROWS, tok), F32),
            jax.ShapeDtypeStruct((tok, C_WIDTH), BF16),
        ],
        scratch_shapes=[pltpu.VMEM((tm + 8, C_WIDTH), F32)],
        compiler_params=_cparams(("arbitrary", "arbitrary")),
        name="inproj",
    )(x2, sc, sh, wq, wg, wc, bg, wconv)


N_ROWQ = 4
SPLIT = 3


def _split3(x):
    hi = x.astype(BF16).astype(F32)
    r1 = x - hi
    mid = r1.astype(BF16).astype(F32)
    return [hi, mid, r1 - mid]


def _mlstm_kernel(qkvo_ref, grow_ref, nw_ref, out_ref, cst, mst, sel):
    sb = qkvo_ref.shape[0]
    L = CHUNK

    @pl.when(pl.program_id(1) == 0)
    def _():
        cst[...] = jnp.zeros(cst.shape, F32)
        mst[...] = jnp.zeros(mst.shape, F32)

    @pl.when((pl.program_id(0) == 0) & (pl.program_id(1) == 0))
    def _():
        r = lax.broadcasted_iota(I32, sel.shape, 0)
        c = lax.broadcasted_iota(I32, sel.shape, 1) // LANES
        rq = r // 8
        quantity = ((rq >= SPLIT).astype(I32) + (rq >= 2 * SPLIT).astype(I32)
                    + (rq >= 3 * SPLIT).astype(I32))
        hit = ((r % 8) == c // N_ROWQ) & (quantity == c % N_ROWQ) & (rq < N_ROWQ * SPLIT)
        sel[...] = hit.astype(BF16)

    causal = lax.broadcasted_iota(I32, (L, L), 0) >= lax.broadcasted_iota(I32, (L, L), 1)
    ones_ext = jnp.ones((L, D_HEAD), BF16)
    pad_rows = jnp.zeros((LANES - 8 * N_ROWQ * SPLIT, L), F32)
    eye = (lax.broadcasted_iota(I32, (L, L), 0) == lax.broadcasted_iota(I32, (L, L), 1)).astype(BF16)
    n_chunks = sb // L
    pairs = [(c, h) for c in range(n_chunks) for h in range(HEADS)]

    def cols(part, c, h):
        c0 = part * M_WIDTH + h * D_HEAD
        return qkvo_ref[c * L:(c + 1) * L, c0:c0 + D_HEAD]

    m_prev = mst[...]
    stack_t, us, decay = [], [], []
    for c in range(n_chunks):
        ug = grow_ref[0:8, c * L:(c + 1) * L]
        bcum = grow_ref[8:16, c * L:(c + 1) * L]
        cmax = grow_ref[16:24, c * L:(c + 1) * L]
        big_m = jnp.maximum(m_prev, cmax)
        m_last = jnp.broadcast_to(big_m[:, L - 1:L], (8, L))
        g_tot = jnp.broadcast_to(bcum[:, L - 1:L], (8, L))
        wq = jnp.exp(m_prev - big_m) * QK_SCALE
        log_em = -(bcum + big_m)
        wk = jnp.exp(ug - m_last)
        decay.append(jnp.exp(m_prev - m_last))
        us.append(ug + LOG_QK_SCALE)
        m_prev = g_tot + m_last
        stack = jnp.concatenate(_split3(big_m) + _split3(wq) + _split3(log_em) + _split3(wk)
                                + [pad_rows], axis=0)
        stack_t.append(stack.T.astype(BF16))
    mst[...] = m_prev

    def rep(c, h, j):
        c0 = (h * N_ROWQ + j) * LANES
        return jnp.dot(stack_t[c], sel[:, c0:c0 + LANES], preferred_element_type=F32)

    s_mat = {ch: lax.dot_general(cols(0, *ch), cols(1, *ch), (((1,), (1,)), ((), ())),
                                 preferred_element_type=F32) for ch in pairs}
    kw = {ch: (cols(1, *ch).astype(F32) * rep(*ch, 3)).astype(BF16) for ch in pairs}
    kw_t = {ch: lax.dot_general(eye, kw[ch], (((1,), (1,)), ((), ())),
                                preferred_element_type=F32).astype(BF16) for ch in pairs}
    lhs = {}
    for ch in pairs:
        c, h = ch
        p = jnp.where(causal, s_mat[ch] * jnp.exp(us[c][h:h + 1, :] - rep(c, h, 0)), 0.0)
        qw = cols(0, c, h).astype(F32) * rep(c, h, 1)
        lhs[ch] = jnp.concatenate([p.astype(BF16), qw.astype(BF16)], axis=1)
    vext = {ch: jnp.concatenate([cols(2, *ch), ones_ext], axis=1) for ch in pairs}
    upd = {ch: jnp.dot(kw_t[ch], vext[ch], preferred_element_type=F32) for ch in pairs}
    tot = {}
    for c in range(n_chunks):
        for h in range(HEADS):
            state = cst[h]
            rhs = jnp.concatenate([vext[(c, h)], state.astype(BF16)], axis=0)
            tot[(c, h)] = jnp.dot(lhs[(c, h)], rhs, preferred_element_type=F32)
            dec = jnp.concatenate([decay[c][h:h + 1, :], decay[c][h:h + 1, :]], axis=1)
            cst[h] = dec * state + upd[(c, h)]
    for ch in pairs:
        c, h = ch
        c0 = h * D_HEAD
        num, den = tot[ch][:, :D_HEAD], tot[ch][:, D_HEAD:]
        hh = num / jnp.maximum(jnp.abs(den), jnp.exp(rep(c, h, 2)))
        mu = jnp.mean(hh, axis=1, keepdims=True)
        var = jnp.mean(jnp.square(hh - mu), axis=1, keepdims=True)
        hn = (hh - mu) * lax.rsqrt(var + LN_EPS) * nw_ref[:, c0:c0 + D_HEAD]
        og = jax.nn.sigmoid(cols(3, c, h).astype(F32))
        out_ref[c * L:(c + 1) * L, c0:c0 + D_HEAD] = (hn * og).astype(BF16)


def _mlstm(qkvo, grow, nw, bsz, seq):
    sb = min(TOK_TILE, seq)
    nt = seq // sb
    tok = bsz * seq
    row = lambda b, s: (b * nt + s, 0)
    return pl.pallas_call(
        _mlstm_kernel,
        grid=(bsz, nt),
        in_specs=[
            pl.BlockSpec((sb, 4 * M_WIDTH), row),
            pl.BlockSpec((GATE_ROWS, sb), lambda b, s: (0, b * nt + s)),
            pl.BlockSpec((1, M_WIDTH), lambda b, s: (0, 0)),
        ],
        out_specs=pl.BlockSpec((sb, M_WIDTH), row),
        out_shape=jax.ShapeDtypeStruct((tok, M_WIDTH), BF16),
        scratch_shapes=[pltpu.VMEM((HEADS, D_HEAD, 2 * D_HEAD), F32),
                        pltpu.VMEM((8, CHUNK), F32),
                        pltpu.VMEM((LANES, HEADS * N_ROWQ * LANES), BF16)],
        compiler_params=_cparams(("arbitrary", "arbitrary")),
        name="mlstm",
    )(qkvo, grow, nw)


def _layer_norm(r, g, b):
    mu = jnp.mean(r, axis=-1, keepdims=True)
    var = jnp.mean(jnp.square(r - mu), axis=-1, keepdims=True)
    return (r - mu) * lax.rsqrt(var + LN_EPS) * g + b


TOK_SUB = D_MODEL // LANES


def _store_token_major(ref, val):
    rows = val.shape[0]
    for k in range(TOK_SUB):
        ref[pl.ds(k, rows, stride=TOK_SUB), :] = val[:, k * LANES:(k + 1) * LANES]


def _load_token_major(ref, rows):
    return jnp.concatenate([ref[pl.ds(k, rows, stride=TOK_SUB), :] for k in range(TOK_SUB)], axis=1)


def _outproj_kernel(hn_ref, yc_ref, x_ref, gt_ref, sc_ref, sh_ref, wo_ref, lg_ref, lb_ref,
                    wr_ref, br_ref, x1_ref, u2_ref, route_ref, cnt_ref, base, triu):
    tm = x_ref.shape[0]

    @pl.when(pl.program_id(0) == 0)
    def _():
        base[...] = jnp.zeros(base.shape, F32)
        rid = lax.broadcasted_iota(I32, (tm, tm), 0)
        cid = lax.broadcasted_iota(I32, (tm, tm), 1)
        triu[...] = (rid <= cid).astype(BF16)

    mix = (jnp.dot(hn_ref[...], wo_ref[0:M_WIDTH, :], preferred_element_type=F32)
           + jnp.dot(yc_ref[...], wo_ref[M_WIDTH:, :], preferred_element_type=F32))
    x1 = _layer_norm(ALPHA * x_ref[...] + (1.0 + gt_ref[0]) * mix, lg_ref[...], lb_ref[...])
    x1_ref[...] = x1
    u2 = x1 * (1.0 + sc_ref[0]) + sh_ref[0]
    _store_token_major(u2_ref, u2)

    lt = lax.dot_general(wr_ref[...], u2.astype(BF16), (((1,), (1,)), ((), ())),
                         preferred_element_type=F32) + br_ref[:, 0:1]
    lg = [lt[j:j + 1, :] for j in range(N_GROUPS + N_GROUPS * N_EXP)]
    best = lg[0]
    grp = jnp.zeros((1, tm), I32)
    for j in range(1, N_GROUPS):
        c = lg[j] > best
        grp = jnp.where(c, j, grp)
        best = jnp.where(c, lg[j], best)
    sel = []
    for e in range(N_EXP):
        val = lg[N_GROUPS + e]
        for gg in range(1, N_GROUPS):
            val = jnp.where(grp == gg, lg[N_GROUPS + gg * N_EXP + e], val)
        sel.append(val)
    v1 = sel[0]
    i1 = jnp.zeros((1, tm), I32)
    for e in range(1, N_EXP):
        c = sel[e] > v1
        i1 = jnp.where(c, e, i1)
        v1 = jnp.where(c, sel[e], v1)
    v2 = jnp.full((1, tm), -jnp.inf, F32)
    i2 = jnp.zeros((1, tm), I32)
    for e in range(N_EXP):
        cand = jnp.where(i1 == e, -jnp.inf, sel[e])
        c = cand > v2
        i2 = jnp.where(c, e, i2)
        v2 = jnp.where(c, cand, v2)
    ea = jnp.minimum(i1, i2)
    eb = jnp.maximum(i1, i2)
    lex = jnp.where(ea == 0, 0, jnp.where(ea == 1, 3, 5)) + eb - ea - 1
    pair = lex
    for i, p in enumerate(_PAIR_ORDER):
        if i != p:
            pair = jnp.where(lex == i, p, pair)
    bucket = grp * N_PAIRS + pair

    onehot = lax.broadcasted_iota(I32, (ROUTE_ROWS, tm), 0) == bucket
    cum = jnp.dot(onehot.astype(BF16), triu[...], preferred_element_type=F32)
    prev = base[:, 0:1]
    rank = jnp.sum(jnp.where(onehot, cum - 1.0 + prev, 0.0), axis=0, keepdims=True)
    new_base = prev + cum[:, tm - 1:tm]
    base[...] = jnp.broadcast_to(new_base, base.shape)
    cnt_ref[...] = jnp.broadcast_to(new_base, cnt_ref.shape)

    zrow = jnp.zeros((1, tm), F32)
    route_ref[...] = jnp.concatenate(
        [bucket.astype(F32), rank, zrow, zrow, zrow, zrow, zrow, zrow], axis=0)


def _outproj(hn, yc, x2, gt, sc, sh, wo, lg, lb, wr, br, bsz, seq):
    tm = min(TOK_TILE, seq)
    nt = seq // tm
    tok = bsz * seq
    row = lambda i: (i, 0)
    const = lambda i: (0, 0)
    mod = lambda i: (i // nt, 0, 0)
    return pl.pallas_call(
        _outproj_kernel,
        grid=(bsz * nt,),
        in_specs=[
            pl.BlockSpec((tm, M_WIDTH), row),
            pl.BlockSpec((tm, C_WIDTH), row),
            pl.BlockSpec((tm, D_MODEL), row),
            pl.BlockSpec((1, 1, D_MODEL), mod),
            pl.BlockSpec((1, 1, D_MODEL), mod),
            pl.BlockSpec((1, 1, D_MODEL), mod),
            pl.BlockSpec(wo.shape, const),
            pl.BlockSpec(lg.shape, const),
            pl.BlockSpec(lb.shape, const),
            pl.BlockSpec(wr.shape, const),
            pl.BlockSpec(br.shape, const),
        ],
        out_specs=[
            pl.BlockSpec((tm, D_MODEL), row),
            pl.BlockSpec((tm * TOK_SUB, LANES), row),
            pl.BlockSpec((8, tm), lambda i: (0, i)),
            pl.BlockSpec((ROUTE_ROWS, LANES), const),
        ],
        out_shape=[
            jax.ShapeDtypeStruct((tok, D_MODEL), F32),
            jax.ShapeDtypeStruct((tok * TOK_SUB, LANES), F32),
            jax.ShapeDtypeStruct((8, tok), F32),
            jax.ShapeDtypeStruct((ROUTE_ROWS, LANES), F32),
        ],
        scratch_shapes=[pltpu.VMEM((ROUTE_ROWS, LANES), F32), pltpu.VMEM((tm, tm), BF16)],
        compiler_params=_cparams(("arbitrary",)),
        name="outproj",
    )(hn, yc, x2, gt, sc, sh, wo, lg, lb, wr, br)


DISPATCH_TILE = 1024
DMA_UNROLL = 8


def _dispatch_kernel(pos_ref, pad_ref, u_ref, xs_ref, zeros, sem, zsem):
    ts = u_ref.shape[0] // TOK_SUB
    t0 = pl.program_id(0) * ts

    def zero_rows(first_row, n_rows_static):
        dst = pl.multiple_of(first_row * TOK_SUB, TOK_SUB)
        return pltpu.make_async_copy(zeros.at[pl.ds(0, n_rows_static * TOK_SUB), :],
                                     xs_ref.at[pl.ds(dst, n_rows_static * TOK_SUB), :], zsem)

    def fill_pads(wait):
        def bucket(b, carry):
            row = pad_ref[b]
            n = pad_ref[N_BUCKETS + b]
            size = MOE_TILE // 2
            while size >= 1:
                @pl.when((n & size) != 0)
                def _(row=row, size=size):
                    cp = zero_rows(row, size)
                    cp.wait() if wait else cp.start()
                row = row + (n & size)
                size //= 2
            return carry
        lax.fori_loop(0, N_BUCKETS, bucket, 0)

        def idle_tile(j, carry):
            cp = zero_rows(j * MOE_TILE, MOE_TILE)
            cp.wait() if wait else cp.start()
            return carry
        lax.fori_loop(pad_ref[2 * N_BUCKETS], xs_ref.shape[0] // (MOE_TILE * TOK_SUB), idle_tile, 0)

    @pl.when(pl.program_id(0) == 0)
    def _():
        zeros[...] = jnp.zeros(zeros.shape, F32)
        fill_pads(wait=False)
        fill_pads(wait=True)

    def issue(g, carry):
        for u in range(DMA_UNROLL):
            r = g * DMA_UNROLL + u
            dst = pl.multiple_of(pos_ref[t0 + r] * TOK_SUB, TOK_SUB)
            src = pl.multiple_of(r * TOK_SUB, TOK_SUB)
            pltpu.make_async_copy(u_ref.at[pl.ds(src, TOK_SUB), :],
                                  xs_ref.at[pl.ds(dst, TOK_SUB), :], sem).start(priority=u % 2)
        return carry

    lax.fori_loop(0, ts // DMA_UNROLL, issue, 0)
    pltpu.make_async_copy(u_ref, xs_ref.at[pl.ds(0, ts * TOK_SUB), :], sem).wait()


def _dispatch(pos, pad_start, u2, n_rows):
    tok = u2.shape[0] // TOK_SUB
    ts = min(DISPATCH_TILE, tok)
    return pl.pallas_call(
        _dispatch_kernel,
        grid_spec=pltpu.PrefetchScalarGridSpec(
            num_scalar_prefetch=2,
            grid=(tok // ts,),
            in_specs=[pl.BlockSpec((ts * TOK_SUB, LANES), lambda i, pos, pad: (i, 0))],
            out_specs=pl.BlockSpec(memory_space=pl.ANY),
            scratch_shapes=[pltpu.VMEM((MOE_TILE * TOK_SUB, LANES), F32),
                            pltpu.SemaphoreType.DMA(()), pltpu.SemaphoreType.DMA(())],
        ),
        out_shape=jax.ShapeDtypeStruct((n_rows * TOK_SUB, LANES), F32),
        compiler_params=_cparams(("arbitrary",)),
        name="dispatch",
    )(pos, pad_start, u2)


def _moe_kernel(s0_ref, s1_ref, valid_ref, fresh0_ref, fresh1_ref, tb_ref, xb_ref, xs_ref, wsel_ref,
                bsel_ref, wg0, wu0, wd0, wg1, wu1, wd1, y_ref, wgs, wus, wds):
    del s0_ref, s1_ref, tb_ref, xb_ref
    j = pl.program_id(0)
    rows = y_ref.shape[0] // TOK_SUB

    @pl.when(fresh0_ref[j] == 1)
    def _():
        wgs[0] = wg0[0].astype(BF16)
        wus[0] = wu0[0].astype(BF16)
        wds[0] = wd0[0].astype(BF16)

    @pl.when(fresh1_ref[j] == 1)
    def _():
        wgs[1] = wg1[0].astype(BF16)
        wus[1] = wu1[0].astype(BF16)
        wds[1] = wd1[0].astype(BF16)

    @pl.when(valid_ref[j] == 1)
    def _():
        x = _load_token_major(xs_ref, rows).astype(BF16)
        lg = jnp.dot(x, wsel_ref[0], preferred_element_type=F32) + bsel_ref[0]
        grp = [lg[:, i:i + 1] for i in range(N_GROUPS)]
        gmax = jnp.maximum(jnp.maximum(grp[0], grp[1]), jnp.maximum(grp[2], grp[3]))
        gsum = sum(jnp.exp(v - gmax) for v in grp)
        la, lb = lg[:, N_GROUPS:N_GROUPS + 1], lg[:, N_GROUPS + 1:N_GROUPS + 2]
        emax = jnp.maximum(la, lb)
        ea, eb = jnp.exp(la - emax), jnp.exp(lb - emax)
        scale = 1.0 / (gsum * (ea + eb))
        wcols = (ea * scale, eb * scale)
        acc = None
        for slot in range(2):
            g = jnp.dot(x, wgs[slot], preferred_element_type=F32)
            u = jnp.dot(x, wus[slot], preferred_element_type=F32)
            hid = (g * jax.nn.sigmoid(g) * u * wcols[slot]).astype(BF16)
            y = jnp.dot(hid, wds[slot], preferred_element_type=F32)
            acc = y if acc is None else acc + y
        _store_token_major(y_ref, acc)

    @pl.when(valid_ref[j] == 0)
    def _():
        y_ref[...] = jnp.zeros(y_ref.shape, F32)


def _moe(plan, xs, wsel, bsel, w_gate, w_up, w_down, n_tiles):
    wsel1 = lambda j, s0, s1, v, f0, f1, tb, xb: (s0[j], 0, 0)
    wsel2 = lambda j, s0, s1, v, f0, f1, tb, xb: (s1[j], 0, 0)
    by_bucket = lambda j, s0, s1, v, f0, f1, tb, xb: (tb[j], 0, 0)
    up_spec = lambda sel: pl.BlockSpec((1, D_MODEL, D_EXPERT), sel)
    dn_spec = lambda sel: pl.BlockSpec((1, D_EXPERT, D_MODEL), sel)
    tile_rows = MOE_TILE * TOK_SUB
    return pl.pallas_call(
        _moe_kernel,
        grid_spec=pltpu.PrefetchScalarGridSpec(
            num_scalar_prefetch=7,
            grid=(n_tiles,),
            in_specs=[pl.BlockSpec((tile_rows, LANES), lambda j, s0, s1, v, f0, f1, tb, xb: (xb[j], 0)),
                      pl.BlockSpec((1, D_MODEL, LANES), by_bucket),
                      pl.BlockSpec((1, 1, LANES), by_bucket),
                      up_spec(wsel1), up_spec(wsel1), dn_spec(wsel1),
                      up_spec(wsel2), up_spec(wsel2), dn_spec(wsel2)],
            out_specs=pl.BlockSpec((tile_rows, LANES), lambda j, *_: (j, 0)),
            scratch_shapes=[pltpu.VMEM((2, D_MODEL, D_EXPERT), BF16),
                            pltpu.VMEM((2, D_MODEL, D_EXPERT), BF16),
                            pltpu.VMEM((2, D_EXPERT, D_MODEL), BF16)],
        ),
        out_shape=jax.ShapeDtypeStruct((n_tiles * tile_rows, LANES), F32),
        compiler_params=_cparams(("arbitrary",)),
        name="moe",
    )(*plan, xs, wsel, bsel, w_gate, w_up, w_down, w_gate, w_up, w_down)


def _combine_kernel(pos_ref, ys_ref, x1_ref, gt_ref, lg_ref, lb_ref, out_ref, ybuf, sems):
    ts = x1_ref.shape[0]
    i = pl.program_id(0)
    n = pl.num_programs(0)
    slot = i % 2

    def gather(tile, to_slot):
        def issue(g, carry):
            for u in range(DMA_UNROLL):
                r = g * DMA_UNROLL + u
                src = pl.multiple_of(pos_ref[tile * ts + r] * TOK_SUB, TOK_SUB)
                dst = pl.multiple_of(r * TOK_SUB, TOK_SUB)
                pltpu.make_async_copy(ys_ref.at[pl.ds(src, TOK_SUB), :],
                                      ybuf.at[to_slot, pl.ds(dst, TOK_SUB), :],
                                      sems.at[to_slot]).start(priority=u % 2)
            return carry
        lax.fori_loop(0, ts // DMA_UNROLL, issue, 0)

    @pl.when(i == 0)
    def _():
        gather(0, 0)

    @pl.when(i + 1 < n)
    def _():
        gather(i + 1, 1 - slot)

    pltpu.make_async_copy(ys_ref.at[pl.ds(0, ts * TOK_SUB), :], ybuf.at[slot], sems.at[slot]).wait()
    y = _load_token_major(ybuf.at[slot], ts)
    out_ref[...] = _layer_norm(ALPHA * x1_ref[...] + (1.0 + gt_ref[0]) * y, lg_ref[...], lb_ref[...])


def _combine(pos, ys, x1, gt, lg, lb, bsz, seq):
    ts = min(TOK_TILE, seq)
    nt = seq // ts
    tok = bsz * seq
    return pl.pallas_call(
        _combine_kernel,
        grid_spec=pltpu.PrefetchScalarGridSpec(
            num_scalar_prefetch=1,
            grid=(tok // ts,),
            in_specs=[pl.BlockSpec(memory_space=pl.ANY),
                      pl.BlockSpec((ts, D_MODEL), lambda i, pos: (i, 0)),
                      pl.BlockSpec((1, 1, D_MODEL), lambda i, pos: (i // nt, 0, 0)),
                      pl.BlockSpec((1, D_MODEL), lambda i, pos: (0, 0)),
                      pl.BlockSpec((1, D_MODEL), lambda i, pos: (0, 0))],
            out_specs=pl.BlockSpec((ts, D_MODEL), lambda i, pos: (i, 0)),
            scratch_shapes=[pltpu.VMEM((2, ts * TOK_SUB, LANES), F32), pltpu.SemaphoreType.DMA((2,))],
        ),
        out_shape=jax.ShapeDtypeStruct((tok, D_MODEL), F32),
        compiler_params=_cparams(("arbitrary",)),
        name="combine",
    )(pos, ys, x1, gt, lg, lb)


def _route_plan(route, cnt, n_tiles, layer):
    bucket = route[0].astype(I32)
    rank = route[1].astype(I32)
    counts = cnt[:N_BUCKETS, 0].astype(I32)
    tiles_b = (counts + MOE_TILE - 1) // MOE_TILE
    tile_end = jnp.cumsum(tiles_b)
    row_off = (tile_end - tiles_b) * MOE_TILE
    bucket_ids = jnp.arange(N_BUCKETS, dtype=I32)
    pos = jnp.sum(jnp.where(bucket[None, :] == bucket_ids[:, None], row_off[:, None], 0), axis=0) + rank
    total = tile_end[-1]
    j = jnp.arange(n_tiles, dtype=I32)
    valid = j < total
    tb = jnp.sum((tile_end[None, :] <= jnp.minimum(j, total - 1)[:, None]).astype(I32), axis=1)
    tb = jnp.clip(tb, 0, N_BUCKETS - 1)
    grp = tb // N_PAIRS
    pair = tb % N_PAIRS
    first = layer * (N_GROUPS * N_EXP) + grp * N_EXP
    s0 = first + sum((pair == i).astype(I32) * _SLOT0[i] for i in range(N_PAIRS))
    s1 = first + sum((pair == i).astype(I32) * _SLOT1[i] for i in range(N_PAIRS))
    one = jnp.ones((1,), I32)
    fresh0 = jnp.concatenate([one, (s0[1:] != s0[:-1]).astype(I32)])
    fresh1 = jnp.concatenate([one, (s1[1:] != s1[:-1]).astype(I32)])
    xblk = jnp.minimum(j, total - 1)
    pads = jnp.concatenate([row_off + counts, tiles_b * MOE_TILE - counts, total[None]])
    return pos, pads, (s0, s1, valid.astype(I32), fresh0, fresh1, tb, xblk)


def kernel(x, c, w_ada, b_ada, w_in, b_gates, mh_norm_w, w_conv, w_out, ln1_g, ln1_b,
           w_grp, b_grp, w_router, b_router, w_gate, w_up, w_down, ln2_g, ln2_b):
    bsz, seq, _ = x.shape
    tok = bsz * seq
    n_rows = tok + N_BUCKETS * MOE_TILE
    n_tiles = n_rows // MOE_TILE
    n_exp_total = N_GROUPS * N_EXP

    wg_all = w_gate.reshape(DEPTH * n_exp_total, D_MODEL, D_EXPERT)
    wu_all = w_up.reshape(DEPTH * n_exp_total, D_MODEL, D_EXPERT)
    wd_all = w_down.reshape(DEPTH * n_exp_total, D_EXPERT, D_MODEL)
    mod = _ada_mod(c, w_ada, b_ada).reshape(DEPTH, bsz, 6, 1, D_MODEL)
    x2 = x.reshape(tok, D_MODEL)
    for l in range(DEPTH):
        sh1, sc1, gt1, sh2, sc2, gt2 = [mod[l, :, i] for i in range(6)]
        wl = w_in[l]
        wq = wl[:, :4 * M_WIDTH].astype(BF16)
        wg = jnp.pad(wl[:, 4 * M_WIDTH:4 * M_WIDTH + 2 * HEADS],
                     ((0, 0), (0, LANES - 2 * HEADS))).astype(BF16)
        wc = wl[:, 4 * M_WIDTH + 2 * HEADS:].astype(BF16)
        bg = jnp.pad(b_gates[l], (0, LANES - 2 * HEADS)).reshape(1, LANES)
        qkvo, grow, yc = _inproj(x2, sc1, sh1, wq, wg, wc, bg, w_conv[l], bsz, seq)
        hn = _mlstm(qkvo, grow, mh_norm_w[l].reshape(1, M_WIDTH), bsz, seq)

        n_logit = N_GROUPS + n_exp_total
        wr = jnp.pad(jnp.concatenate([w_grp[l], w_router[l]], axis=1).T,
                     ((0, ROUTE_ROWS - n_logit), (0, 0))).astype(BF16)
        br = jnp.pad(jnp.concatenate([b_grp[l], b_router[l]]), (0, ROUTE_ROWS - n_logit))
        br = jnp.broadcast_to(br[:, None], (ROUTE_ROWS, LANES))
        x1, u2, route, cnt = _outproj(hn, yc, x2, gt1, sc2, sh2, w_out[l].astype(BF16),
                                      ln1_g[l].reshape(1, D_MODEL), ln1_b[l].reshape(1, D_MODEL),
                                      wr, br, bsz, seq)
        pos, pads, plan = _route_plan(route, cnt, n_tiles, l)
        xs = _dispatch(pos, pads, u2, n_rows)
        wsel = jnp.concatenate([jnp.broadcast_to(w_grp[l], (N_BUCKETS, D_MODEL, N_GROUPS)),
                                w_router[l][:, _BUCKET_S0].T[:, :, None],
                                w_router[l][:, _BUCKET_S1].T[:, :, None]], axis=2)
        wsel = jnp.pad(wsel, ((0, 0), (0, 0), (0, LANES - N_GROUPS - 2))).astype(BF16)
        bsel = jnp.concatenate([jnp.broadcast_to(b_grp[l], (N_BUCKETS, N_GROUPS)),
                                b_router[l][_BUCKET_S0][:, None], b_router[l][_BUCKET_S1][:, None]], axis=1)
        bsel = jnp.pad(bsel, ((0, 0), (0, LANES - N_GROUPS - 2))).reshape(N_BUCKETS, 1, LANES)
        ys = _moe(plan, xs, wsel, bsel, wg_all, wu_all, wd_all, n_tiles)
        x2 = _combine(pos, ys, x1, gt2, ln2_g[l].reshape(1, D_MODEL),
                      ln2_b[l].reshape(1, D_MODEL), bsz, seq)
    return x2.reshape(bsz, seq, D_MODEL)
```

```python
import functools
import math

import jax
import jax.numpy as jnp
import numpy as np
from jax import lax
from jax.experimental import pallas as pl
from jax.experimental.pallas import tpu as pltpu

F32 = jnp.float32
BF16 = jnp.bfloat16
I32 = jnp.int32

D_MODEL = 1024
DEPTH = 2
HEADS = 4
D_HEAD = 128
M_WIDTH = HEADS * D_HEAD
C_WIDTH = D_MODEL - M_WIDTH
N_GROUPS = 4
N_EXP = 4
D_EXPERT = 512
N_PAIRS = 6
N_BUCKETS = N_GROUPS * N_PAIRS
ALPHA = (2 * DEPTH) ** 0.25
LN_EPS = 1e-5
QK_SCALE = D_HEAD ** -0.5
LOG_QK_SCALE = math.log(QK_SCALE)
GATE_ROWS = 24

LANES = 128
ROUTE_ROWS = 32

TOK_TILE = 512
CHUNK = 128
MOE_TILE = 256
VMEM_LIMIT = 56 * 1024 * 1024

_PAIR_ORDER = (0, 2, 3, 1, 4, 5)
_SLOT0 = (0, 2, 2, 3, 3, 3)
_SLOT1 = (1, 1, 0, 0, 1, 2)
_BUCKET_S0 = np.array([g * N_EXP + _SLOT0[p] for g in range(N_GROUPS) for p in range(N_PAIRS)])
_BUCKET_S1 = np.array([g * N_EXP + _SLOT1[p] for g in range(N_GROUPS) for p in range(N_PAIRS)])


def _cparams(sem):
    return pltpu.CompilerParams(dimension_semantics=sem, vmem_limit_bytes=VMEM_LIMIT)


def _ada_kernel(c_ref, w_ref, b_ref, o_ref):
    c = c_ref[...]
    cond = (c * jax.nn.sigmoid(c)).astype(BF16)
    o_ref[0] = jnp.dot(cond, w_ref[0].astype(BF16), preferred_element_type=F32) + b_ref[0]


def _ada_mod(c, w_ada, b_ada):
    bsz = c.shape[0]
    nblk = w_ada.shape[2] // D_MODEL
    return pl.pallas_call(
        _ada_kernel,
        grid=(DEPTH, nblk),
        in_specs=[
            pl.BlockSpec((bsz, D_MODEL), lambda l, j: (0, 0)),
            pl.BlockSpec((1, D_MODEL, D_MODEL), lambda l, j: (l, 0, j)),
            pl.BlockSpec((1, 1, D_MODEL), lambda l, j: (l, 0, j)),
        ],
        out_specs=pl.BlockSpec((1, bsz, D_MODEL), lambda l, j: (l, 0, j)),
        out_shape=jax.ShapeDtypeStruct((DEPTH, bsz, nblk * D_MODEL), F32),
        compiler_params=_cparams(("arbitrary", "arbitrary")),
        name="ada_mod",
    )(c, w_ada, b_ada.reshape(DEPTH, 1, -1))


def _log_sigmoid(x):
    return jnp.minimum(x, 0.0) - jnp.log1p(jnp.exp(-jnp.abs(x)))


def _chunk_scan(x, op, identity):
    lane = lax.broadcasted_iota(I32, x.shape, 1) & (CHUNK - 1)
    sh = 1
    while sh < CHUNK:
        x = op(x, jnp.where(lane >= sh, pltpu.roll(x, sh, 1), identity))
        sh *= 2
    return x


def _inproj_kernel(x_ref, sc_ref, sh_ref, wq_ref, wg_ref, wc_ref, bg_ref, wconv_ref,
                   qkvo_ref, grow_ref, yc_ref, zbuf):
    tm = x_ref.shape[0]
    u = (x_ref[...] * (1.0 + sc_ref[0]) + sh_ref[0]).astype(BF16)
    qkvo_ref[...] = jnp.dot(u, wq_ref[...], preferred_element_type=F32).astype(BF16)
    g = jnp.dot(u, wg_ref[...], preferred_element_type=F32) + bg_ref[...]
    gt = g.T[:8]
    head_row = lax.broadcasted_iota(I32, gt.shape, 0) < HEADS
    ig = jnp.where(head_row, gt, 0.0)
    logf = jnp.where(head_row, _log_sigmoid(pltpu.roll(gt, HEADS, 0)), 0.0)
    bcum = _chunk_scan(logf, jnp.add, 0.0)
    ug = ig - bcum
    grow_ref[0:8, :] = ug
    grow_ref[8:16, :] = bcum
    grow_ref[16:24, :] = _chunk_scan(ug, jnp.maximum, -jnp.inf)
    pc = jnp.dot(u, wc_ref[...], preferred_element_type=F32)
    z = pc[:, C_WIDTH:2 * C_WIDTH] * pc[:, 2 * C_WIDTH:]

    @pl.when(pl.program_id(1) == 0)
    def _():
        zbuf[0:8, :] = jnp.zeros((8, C_WIDTH), F32)

    zbuf[8:8 + tm, :] = z
    zc = (wconv_ref[0:1, :] * zbuf[6:6 + tm, :] + wconv_ref[1:2, :] * zbuf[7:7 + tm, :]
          + wconv_ref[2:3, :] * z)
    yc_ref[...] = (pc[:, :C_WIDTH] * zc).astype(BF16)
    zbuf[0:8, :] = zbuf[tm:tm + 8, :]


def _inproj(x2, sc, sh, wq, wg, wc, bg, wconv, bsz, seq):
    tm = min(TOK_TILE, seq)
    nt = seq // tm
    tok = bsz * seq
    row = lambda b, s: (b * nt + s, 0)
    const = lambda b, s: (0, 0)
    mod = lambda b, s: (b, 0, 0)
    return pl.pallas_call(
        _inproj_kernel,
        grid=(bsz, nt),
        in_specs=[
            pl.BlockSpec((tm, D_MODEL), row),
            pl.BlockSpec((1, 1, D_MODEL), mod),
            pl.BlockSpec((1, 1, D_MODEL), mod),
            pl.BlockSpec(wq.shape, const),
            pl.BlockSpec(wg.shape, const),
            pl.BlockSpec(wc.shape, const),
            pl.BlockSpec(bg.shape, const),
            pl.BlockSpec(wconv.shape, const),
        ],
        out_specs=[
            pl.BlockSpec((tm, 4 * M_WIDTH), row),
            pl.BlockSpec((GATE_ROWS, tm), lambda b, s: (0, b * nt + s)),
            pl.BlockSpec((tm, C_WIDTH), row),
        ],
        out_shape=[
            jax.ShapeDtypeStruct((tok, 4 * M_WIDTH), BF16),
            jax.ShapeDtypeStruct((GATE_ROWS, tok), F32),
            jax.ShapeDtypeStruct((tok, C_WIDTH), BF16),
        ],
        scratch_shapes=[pltpu.VMEM((tm + 8, C_WIDTH), F32)],
        compiler_params=_cparams(("arbitrary", "arbitrary")),
        name="inproj",
    )(x2, sc, sh, wq, wg, wc, bg, wconv)


N_ROWQ = 4
SPLIT = 3


def _split3(x):
    hi = x.astype(BF16).astype(F32)
    r1 = x - hi
    mid = r1.astype(BF16).astype(F32)
    return [hi, mid, r1 - mid]


def _mlstm_selector(sel):
    r = lax.broadcasted_iota(I32, sel.shape, 0)
    c = lax.broadcasted_iota(I32, sel.shape, 1) // LANES
    rq = r // 8
    quantity = ((rq >= SPLIT).astype(I32) + (rq >= 2 * SPLIT).astype(I32)
                + (rq >= 3 * SPLIT).astype(I32))
    hit = ((r % 8) == c // N_ROWQ) & (quantity == c % N_ROWQ) & (rq < N_ROWQ * SPLIT)
    sel[...] = hit.astype(BF16)


def _mlstm_block(qkvo_ref, grow_ref, nw_ref, out_ref, cst, mst, sel, between=()):
    between = tuple(between) + (None,) * 4
    sb = qkvo_ref.shape[0]
    L = CHUNK
    causal = lax.broadcasted_iota(I32, (L, L), 0) >= lax.broadcasted_iota(I32, (L, L), 1)
    ones_ext = jnp.ones((L, D_HEAD), BF16)
    pad_rows = jnp.zeros((LANES - 8 * N_ROWQ * SPLIT, L), F32)
    eye = (lax.broadcasted_iota(I32, (L, L), 0) == lax.broadcasted_iota(I32, (L, L), 1)).astype(BF16)
    n_chunks = sb // L
    pairs = [(c, h) for c in range(n_chunks) for h in range(HEADS)]

    def cols(part, c, h):
        c0 = part * M_WIDTH + h * D_HEAD
        return qkvo_ref[c * L:(c + 1) * L, c0:c0 + D_HEAD]

    m_prev = mst[...]
    stack_t, us, decay = [], [], []
    for c in range(n_chunks):
        ug = grow_ref[0:8, c * L:(c + 1) * L]
        bcum = grow_ref[8:16, c * L:(c + 1) * L]
        cmax = grow_ref[16:24, c * L:(c + 1) * L]
        big_m = jnp.maximum(m_prev, cmax)
        m_last = jnp.broadcast_to(big_m[:, L - 1:L], (8, L))
        g_tot = jnp.broadcast_to(bcum[:, L - 1:L], (8, L))
        wq = jnp.exp(m_prev - big_m) * QK_SCALE
        log_em = -(bcum + big_m)
        wk = jnp.exp(ug - m_last)
        decay.append(jnp.exp(m_prev - m_last))
        us.append(ug + LOG_QK_SCALE)
        m_prev = g_tot + m_last
        stack = jnp.concatenate(_split3(big_m) + _split3(wq) + _split3(log_em) + _split3(wk)
                                + [pad_rows], axis=0)
        stack_t.append(stack.T.astype(BF16))
    mst[...] = m_prev
    if between[0] is not None:
        between[0]()

    def rep(c, h, j):
        c0 = (h * N_ROWQ + j) * LANES
        return jnp.dot(stack_t[c], sel[:, c0:c0 + LANES], preferred_element_type=F32)

    s_mat = {ch: lax.dot_general(cols(0, *ch), cols(1, *ch), (((1,), (1,)), ((), ())),
                                 preferred_element_type=F32) for ch in pairs}
    kw = {ch: (cols(1, *ch).astype(F32) * rep(*ch, 3)).astype(BF16) for ch in pairs}
    kw_t = {ch: lax.dot_general(eye, kw[ch], (((1,), (1,)), ((), ())),
                                preferred_element_type=F32).astype(BF16) for ch in pairs}
    if between[1] is not None:
        between[1]()
    lhs = {}
    for ch in pairs:
        c, h = ch
        p = jnp.where(causal, s_mat[ch] * jnp.exp(us[c][h:h + 1, :] - rep(c, h, 0)), 0.0)
        qw = cols(0, c, h).astype(F32) * rep(c, h, 1)
        lhs[ch] = jnp.concatenate([p.astype(BF16), qw.astype(BF16)], axis=1)
    vext = {ch: jnp.concatenate([cols(2, *ch), ones_ext], axis=1) for ch in pairs}
    upd = {ch: jnp.dot(kw_t[ch], vext[ch], preferred_element_type=F32) for ch in pairs}
    if between[2] is not None:
        between[2]()
    tot = {}
    for c in range(n_chunks):
        for h in range(HEADS):
            state = cst[h]
            rhs = jnp.concatenate([vext[(c, h)], state.astype(BF16)], axis=0)
            tot[(c, h)] = jnp.dot(lhs[(c, h)], rhs, preferred_element_type=F32)
            dec = jnp.concatenate([decay[c][h:h + 1, :], decay[c][h:h + 1, :]], axis=1)
            cst[h] = dec * state + upd[(c, h)]
    if between[3] is not None:
        between[3]()
    for ch in pairs:
        c, h = ch
        c0 = h * D_HEAD
        num, den = tot[ch][:, :D_HEAD], tot[ch][:, D_HEAD:]
        hh = num / jnp.maximum(jnp.abs(den), jnp.exp(rep(c, h, 2)))
        mu = jnp.mean(hh, axis=1, keepdims=True)
        var = jnp.mean(jnp.square(hh - mu), axis=1, keepdims=True)
        hn = (hh - mu) * lax.rsqrt(var + LN_EPS) * nw_ref[:, c0:c0 + D_HEAD]
        og = jax.nn.sigmoid(cols(3, c, h).astype(F32))
        out_ref[c * L:(c + 1) * L, c0:c0 + D_HEAD] = (hn * og).astype(BF16)


def _layer_norm(r, g, b):
    mu = jnp.mean(r, axis=-1, keepdims=True)
    var = jnp.mean(jnp.square(r - mu), axis=-1, keepdims=True)
    return (r - mu) * lax.rsqrt(var + LN_EPS) * g + b


TOK_SUB = D_MODEL // LANES


def _store_token_major(ref, val):
    rows = val.shape[0]
    for k in range(TOK_SUB):
        ref[pl.ds(k, rows, stride=TOK_SUB), :] = val[:, k * LANES:(k + 1) * LANES]


def _load_token_major(ref, rows):
    return jnp.concatenate([ref[pl.ds(k, rows, stride=TOK_SUB), :] for k in range(TOK_SUB)], axis=1)


def _outproj_parts(hn_val, yc_ref, x_ref, gt_ref, sc_ref, sh_ref, wo_ref, lg_ref, lb_ref,
                   wr_ref, br_ref, x1_ref, u2_ref, route_ref, cnt_ref, base, triu, live):
    tm = x_ref.shape[0]
    held = {}

    def project():
        held["mix"] = (jnp.dot(hn_val, wo_ref[0:M_WIDTH, :], preferred_element_type=F32)
                       + jnp.dot(yc_ref[...], wo_ref[M_WIDTH:, :], preferred_element_type=F32))

    def normalise():
        x1 = _layer_norm(ALPHA * x_ref[...] + (1.0 + gt_ref[0]) * held["mix"], lg_ref[...], lb_ref[...])
        x1_ref[...] = x1
        u2 = x1 * (1.0 + sc_ref[0]) + sh_ref[0]
        _store_token_major(u2_ref, u2)
        held["logits"] = lax.dot_general(wr_ref[...], u2.astype(BF16), (((1,), (1,)), ((), ())),
                                         preferred_element_type=F32) + br_ref[:, 0:1]

    def route():
        _route_block(held["logits"], route_ref, cnt_ref, base, triu, live, tm)

    return project, normalise, route


def _route_block(lt, route_ref, cnt_ref, base, triu, live, tm):
    lg = [lt[j:j + 1, :] for j in range(N_GROUPS + N_GROUPS * N_EXP)]
    best = lg[0]
    grp = jnp.zeros((1, tm), I32)
    for j in range(1, N_GROUPS):
        c = lg[j] > best
        grp = jnp.where(c, j, grp)
        best = jnp.where(c, lg[j], best)
    sel = []
    for e in range(N_EXP):
        val = lg[N_GROUPS + e]
        for gg in range(1, N_GROUPS):
            val = jnp.where(grp == gg, lg[N_GROUPS + gg * N_EXP + e], val)
        sel.append(val)
    v1 = sel[0]
    i1 = jnp.zeros((1, tm), I32)
    for e in range(1, N_EXP):
        c = sel[e] > v1
        i1 = jnp.where(c, e, i1)
        v1 = jnp.where(c, sel[e], v1)
    v2 = jnp.full((1, tm), -jnp.inf, F32)
    i2 = jnp.zeros((1, tm), I32)
    for e in range(N_EXP):
        cand = jnp.where(i1 == e, -jnp.inf, sel[e])
        c = cand > v2
        i2 = jnp.where(c, e, i2)
        v2 = jnp.where(c, cand, v2)
    ea = jnp.minimum(i1, i2)
    eb = jnp.maximum(i1, i2)
    lex = jnp.where(ea == 0, 0, jnp.where(ea == 1, 3, 5)) + eb - ea - 1
    pair = lex
    for i, p in enumerate(_PAIR_ORDER):
        if i != p:
            pair = jnp.where(lex == i, p, pair)
    bucket = grp * N_PAIRS + pair

    onehot = lax.broadcasted_iota(I32, (ROUTE_ROWS, tm), 0) == bucket
    cum = jnp.dot(onehot.astype(BF16), triu[...], preferred_element_type=F32)
    prev = base[:, 0:1]
    rank = jnp.sum(jnp.where(onehot, cum - 1.0 + prev, 0.0), axis=0, keepdims=True)
    new_base = prev + live * cum[:, tm - 1:tm]
    base[...] = jnp.broadcast_to(new_base, base.shape)
    cnt_ref[...] = jnp.broadcast_to(new_base, cnt_ref.shape)

    zrow = jnp.zeros((1, tm), F32)
    route_ref[...] = jnp.concatenate(
        [bucket.astype(F32), rank, zrow, zrow, zrow, zrow, zrow, zrow], axis=0)


def _mixer_kernel(qkvo_ref, grow_ref, nw_ref, yc_ref, x_ref, gt_ref, sc_ref, sh_ref, wo_ref, lg_ref,
                  lb_ref, wr_ref, br_ref, x1_ref, u2_ref, route_ref, cnt_ref,
                  cst, mst, sel, hn, base, triu, *, blocks_per_seq):
    i = pl.program_id(0)
    tm = x_ref.shape[0]

    @pl.when(i == 0)
    def _():
        _mlstm_selector(sel)
        hn[...] = jnp.zeros(hn.shape, BF16)
        base[...] = jnp.zeros(base.shape, F32)
        rid = lax.broadcasted_iota(I32, triu.shape, 0)
        cid = lax.broadcasted_iota(I32, triu.shape, 1)
        triu[...] = (rid <= cid).astype(BF16)

    @pl.when(i % blocks_per_seq == 0)
    def _():
        cst[...] = jnp.zeros(cst.shape, F32)
        mst[...] = jnp.zeros(mst.shape, F32)

    live = jnp.where(i > 0, 1.0, 0.0).astype(F32)
    parts = _outproj_parts(hn[...], yc_ref, x_ref, gt_ref, sc_ref, sh_ref, wo_ref, lg_ref, lb_ref,
                           wr_ref, br_ref, x1_ref, u2_ref, route_ref, cnt_ref, base, triu, live)
    _mlstm_block(qkvo_ref, grow_ref, nw_ref, hn, cst, mst, sel, between=parts)


def _mixer(qkvo, grow, nw, yc, x2, gt, sc, sh, wo, lg, lb, wr, br, bsz, seq):
    tm = min(TOK_TILE, seq)
    nt = seq // tm
    tok = bsz * seq
    n = bsz * nt
    cur = lambda i: (jnp.minimum(i, n - 1), 0)
    prev = lambda i: (jnp.maximum(i - 1, 0), 0)
    const = lambda i: (0, 0)
    mod = lambda i: (jnp.maximum(i - 1, 0) // nt, 0, 0)
    return pl.pallas_call(
        functools.partial(_mixer_kernel, blocks_per_seq=nt),
        grid=(n + 1,),
        in_specs=[
            pl.BlockSpec((tm, 4 * M_WIDTH), cur),
            pl.BlockSpec((GATE_ROWS, tm), lambda i: (0, jnp.minimum(i, n - 1))),
            pl.BlockSpec((1, M_WIDTH), const),
            pl.BlockSpec((tm, C_WIDTH), prev),
            pl.BlockSpec((tm, D_MODEL), prev),
            pl.BlockSpec((1, 1, D_MODEL), mod),
            pl.BlockSpec((1, 1, D_MODEL), mod),
            pl.BlockSpec((1, 1, D_MODEL), mod),
            pl.BlockSpec(wo.shape, const),
            pl.BlockSpec(lg.shape, const),
            pl.BlockSpec(lb.shape, const),
            pl.BlockSpec(wr.shape, const),
            pl.BlockSpec(br.shape, const),
        ],
        out_specs=[
            pl.BlockSpec((tm, D_MODEL), prev),
            pl.BlockSpec((tm * TOK_SUB, LANES), prev),
            pl.BlockSpec((8, tm), lambda i: (0, jnp.maximum(i - 1, 0))),
            pl.BlockSpec((ROUTE_ROWS, LANES), const),
        ],
        out_shape=[
            jax.ShapeDtypeStruct((tok, D_MODEL), F32),
            jax.ShapeDtypeStruct((tok * TOK_SUB, LANES), F32),
            jax.ShapeDtypeStruct((8, tok), F32),
            jax.ShapeDtypeStruct((ROUTE_ROWS, LANES), F32),
        ],
        scratch_shapes=[pltpu.VMEM((HEADS, D_HEAD, 2 * D_HEAD), F32),
                        pltpu.VMEM((8, CHUNK), F32),
                        pltpu.VMEM((LANES, HEADS * N_ROWQ * LANES), BF16),
                        pltpu.VMEM((tm, M_WIDTH), BF16),
                        pltpu.VMEM((ROUTE_ROWS, LANES), F32),
                        pltpu.VMEM((tm, tm), BF16)],
        compiler_params=_cparams(("arbitrary",)),
        name="mixer",
    )(qkvo, grow, nw, yc, x2, gt, sc, sh, wo, lg, lb, wr, br)


DISPATCH_TILE = 1024
DMA_UNROLL = 8


def _dispatch_kernel(pos_ref, pad_ref, u_ref, xs_ref, zeros, sem, zsem):
    ts = u_ref.shape[0] // TOK_SUB
    t0 = pl.program_id(0) * ts

    def zero_rows(first_row, n_rows_static):
        dst = pl.multiple_of(first_row * TOK_SUB, TOK_SUB)
        return pltpu.make_async_copy(zeros.at[pl.ds(0, n_rows_static * TOK_SUB), :],
                                     xs_ref.at[pl.ds(dst, n_rows_static * TOK_SUB), :], zsem)

    def fill_pads(wait):
        def bucket(b, carry):
            row = pad_ref[b]
            n = pad_ref[N_BUCKETS + b]
            size = MOE_TILE // 2
            while size >= 1:
                @pl.when((n & size) != 0)
                def _(row=row, size=size):
                    cp = zero_rows(row, size)
                    cp.wait() if wait else cp.start()
                row = row + (n & size)
                size //= 2
            return carry
        lax.fori_loop(0, N_BUCKETS, bucket, 0)

        def idle_tile(j, carry):
            cp = zero_rows(j * MOE_TILE, MOE_TILE)
            cp.wait() if wait else cp.start()
            return carry
        lax.fori_loop(pad_ref[2 * N_BUCKETS], xs_ref.shape[0] // (MOE_TILE * TOK_SUB), idle_tile, 0)

    @pl.when(pl.program_id(0) == 0)
    def _():
        zeros[...] = jnp.zeros(zeros.shape, F32)
        fill_pads(wait=False)
        fill_pads(wait=True)

    def issue(g, carry):
        for u in range(DMA_UNROLL):
            r = g * DMA_UNROLL + u
            dst = pl.multiple_of(pos_ref[t0 + r] * TOK_SUB, TOK_SUB)
            src = pl.multiple_of(r * TOK_SUB, TOK_SUB)
            pltpu.make_async_copy(u_ref.at[pl.ds(src, TOK_SUB), :],
                                  xs_ref.at[pl.ds(dst, TOK_SUB), :], sem).start(priority=u % 2)
        return carry

    lax.fori_loop(0, ts // DMA_UNROLL, issue, 0)
    pltpu.make_async_copy(u_ref, xs_ref.at[pl.ds(0, ts * TOK_SUB), :], sem).wait()


def _dispatch(pos, pad_start, u2, n_rows):
    tok = u2.shape[0] // TOK_SUB
    ts = min(DISPATCH_TILE, tok)
    return pl.pallas_call(
        _dispatch_kernel,
        grid_spec=pltpu.PrefetchScalarGridSpec(
            num_scalar_prefetch=2,
            grid=(tok // ts,),
            in_specs=[pl.BlockSpec((ts * TOK_SUB, LANES), lambda i, pos, pad: (i, 0))],
            out_specs=pl.BlockSpec(memory_space=pl.ANY),
            scratch_shapes=[pltpu.VMEM((MOE_TILE * TOK_SUB, LANES), F32),
                            pltpu.SemaphoreType.DMA(()), pltpu.SemaphoreType.DMA(())],
        ),
        out_shape=jax.ShapeDtypeStruct((n_rows * TOK_SUB, LANES), F32),
        compiler_params=_cparams(("arbitrary",)),
        name="dispatch",
    )(pos, pad_start, u2)


def _moe_kernel(s0_ref, s1_ref, valid_ref, fresh0_ref, fresh1_ref, tb_ref, xb_ref, xs_ref, wsel_ref,
                bsel_ref, wg0, wu0, wd0, wg1, wu1, wd1, y_ref, wgs, wus, wds):
    del s0_ref, s1_ref, tb_ref, xb_ref
    j = pl.program_id(0)
    rows = y_ref.shape[0] // TOK_SUB

    @pl.when(fresh0_ref[j] == 1)
    def _():
        wgs[0] = wg0[0].astype(BF16)
        wus[0] = wu0[0].astype(BF16)
        wds[0] = wd0[0].astype(BF16)

    @pl.when(fresh1_ref[j] == 1)
    def _():
        wgs[1] = wg1[0].astype(BF16)
        wus[1] = wu1[0].astype(BF16)
        wds[1] = wd1[0].astype(BF16)

    @pl.when(valid_ref[j] == 1)
    def _():
        x = _load_token_major(xs_ref, rows).astype(BF16)
        lg = jnp.dot(x, wsel_ref[0], preferred_element_type=F32) + bsel_ref[0]
        grp = [lg[:, i:i + 1] for i in range(N_GROUPS)]
        gmax = jnp.maximum(jnp.maximum(grp[0], grp[1]), jnp.maximum(grp[2], grp[3]))
        gsum = sum(jnp.exp(v - gmax) for v in grp)
        la, lb = lg[:, N_GROUPS:N_GROUPS + 1], lg[:, N_GROUPS + 1:N_GROUPS + 2]
        emax = jnp.maximum(la, lb)
        ea, eb = jnp.exp(la - emax), jnp.exp(lb - emax)
        scale = 1.0 / (gsum * (ea + eb))
        wcols = (ea * scale, eb * scale)
        acc = None
        for slot in range(2):
            g = jnp.dot(x, wgs[slot], preferred_element_type=F32)
            u = jnp.dot(x, wus[slot], preferred_element_type=F32)
            hid = (g * jax.nn.sigmoid(g) * u * wcols[slot]).astype(BF16)
            y = jnp.dot(hid, wds[slot], preferred_element_type=F32)
            acc = y if acc is None else acc + y
        _store_token_major(y_ref, acc)

    @pl.when(valid_ref[j] == 0)
    def _():
        y_ref[...] = jnp.zeros(y_ref.shape, F32)


def _moe(plan, xs, wsel, bsel, w_gate, w_up, w_down, n_tiles):
    wsel1 = lambda j, s0, s1, v, f0, f1, tb, xb: (s0[j], 0, 0)
    wsel2 = lambda j, s0, s1, v, f0, f1, tb, xb: (s1[j], 0, 0)
    by_bucket = lambda j, s0, s1, v, f0, f1, tb, xb: (tb[j], 0, 0)
    up_spec = lambda sel: pl.BlockSpec((1, D_MODEL, D_EXPERT), sel)
    dn_spec = lambda sel: pl.BlockSpec((1, D_EXPERT, D_MODEL), sel)
    tile_rows = MOE_TILE * TOK_SUB
    return pl.pallas_call(
        _moe_kernel,
        grid_spec=pltpu.PrefetchScalarGridSpec(
            num_scalar_prefetch=7,
            grid=(n_tiles,),
            in_specs=[pl.BlockSpec((tile_rows, LANES), lambda j, s0, s1, v, f0, f1, tb, xb: (xb[j], 0)),
                      pl.BlockSpec((1, D_MODEL, LANES), by_bucket),
                      pl.BlockSpec((1, 1, LANES), by_bucket),
                      up_spec(wsel1), up_spec(wsel1), dn_spec(wsel1),
                      up_spec(wsel2), up_spec(wsel2), dn_spec(wsel2)],
            out_specs=pl.BlockSpec((tile_rows, LANES), lambda j, *_: (j, 0)),
            scratch_shapes=[pltpu.VMEM((2, D_MODEL, D_EXPERT), BF16),
                            pltpu.VMEM((2, D_MODEL, D_EXPERT), BF16),
                            pltpu.VMEM((2, D_EXPERT, D_MODEL), BF16)],
        ),
        out_shape=jax.ShapeDtypeStruct((n_tiles * tile_rows, LANES), F32),
        compiler_params=_cparams(("arbitrary",)),
        name="moe",
    )(*plan, xs, wsel, bsel, w_gate, w_up, w_down, w_gate, w_up, w_down)


def _combine_kernel(pos_ref, ys_ref, x1_ref, gt_ref, lg_ref, lb_ref, out_ref, ybuf, sems):
    ts = x1_ref.shape[0]
    i = pl.program_id(0)
    n = pl.num_programs(0)
    slot = i % 2

    def gather(tile, to_slot):
        def issue(g, carry):
            for u in range(DMA_UNROLL):
                r = g * DMA_UNROLL + u
                src = pl.multiple_of(pos_ref[tile * ts + r] * TOK_SUB, TOK_SUB)
                dst = pl.multiple_of(r * TOK_SUB, TOK_SUB)
                pltpu.make_async_copy(ys_ref.at[pl.ds(src, TOK_SUB), :],
                                      ybuf.at[to_slot, pl.ds(dst, TOK_SUB), :],
                                      sems.at[to_slot]).start(priority=u % 2)
            return carry
        lax.fori_loop(0, ts // DMA_UNROLL, issue, 0)

    @pl.when(i == 0)
    def _():
        gather(0, 0)

    @pl.when(i + 1 < n)
    def _():
        gather(i + 1, 1 - slot)

    pltpu.make_async_copy(ys_ref.at[pl.ds(0, ts * TOK_SUB), :], ybuf.at[slot], sems.at[slot]).wait()
    y = _load_token_major(ybuf.at[slot], ts)
    out_ref[...] = _layer_norm(ALPHA * x1_ref[...] + (1.0 + gt_ref[0]) * y, lg_ref[...], lb_ref[...])


def _combine(pos, ys, x1, gt, lg, lb, bsz, seq):
    ts = min(TOK_TILE, seq)
    nt = seq // ts
    tok = bsz * seq
    return pl.pallas_call(
        _combine_kernel,
        grid_spec=pltpu.PrefetchScalarGridSpec(
            num_scalar_prefetch=1,
            grid=(tok // ts,),
            in_specs=[pl.BlockSpec(memory_space=pl.ANY),
                      pl.BlockSpec((ts, D_MODEL), lambda i, pos: (i, 0)),
                      pl.BlockSpec((1, 1, D_MODEL), lambda i, pos: (i // nt, 0, 0)),
                      pl.BlockSpec((1, D_MODEL), lambda i, pos: (0, 0)),
                      pl.BlockSpec((1, D_MODEL), lambda i, pos: (0, 0))],
            out_specs=pl.BlockSpec((ts, D_MODEL), lambda i, pos: (i, 0)),
            scratch_shapes=[pltpu.VMEM((2, ts * TOK_SUB, LANES), F32), pltpu.SemaphoreType.DMA((2,))],
        ),
        out_shape=jax.ShapeDtypeStruct((tok, D_MODEL), F32),
        compiler_params=_cparams(("arbitrary",)),
        name="combine",
    )(pos, ys, x1, gt, lg, lb)


def _route_plan(route, cnt, n_tiles, layer):
    bucket = route[0].astype(I32)
    rank = route[1].astype(I32)
    counts = cnt[:N_BUCKETS, 0].astype(I32)
    tiles_b = (counts + MOE_TILE - 1) // MOE_TILE
    tile_end = jnp.cumsum(tiles_b)
    row_off = (tile_end - tiles_b) * MOE_TILE
    bucket_ids = jnp.arange(N_BUCKETS, dtype=I32)
    pos = jnp.sum(jnp.where(bucket[None, :] == bucket_ids[:, None], row_off[:, None], 0), axis=0) + rank
    total = tile_end[-1]
    j = jnp.arange(n_tiles, dtype=I32)
    valid = j < total
    tb = jnp.sum((tile_end[None, :] <= jnp.minimum(j, total - 1)[:, None]).astype(I32), axis=1)
    tb = jnp.clip(tb, 0, N_BUCKETS - 1)
    grp = tb // N_PAIRS
    pair = tb % N_PAIRS
    first = layer * (N_GROUPS * N_EXP) + grp * N_EXP
    s0 = first + sum((pair == i).astype(I32) * _SLOT0[i] for i in range(N_PAIRS))
    s1 = first + sum((pair == i).astype(I32) * _SLOT1[i] for i in range(N_PAIRS))
    one = jnp.ones((1,), I32)
    fresh0 = jnp.concatenate([one, (s0[1:] != s0[:-1]).astype(I32)])
    fresh1 = jnp.concatenate([one, (s1[1:] != s1[:-1]).astype(I32)])
    xblk = jnp.minimum(j, total - 1)
    pads = jnp.concatenate([row_off + counts, tiles_b * MOE_TILE - counts, total[None]])
    return pos, pads, (s0, s1, valid.astype(I32), fresh0, fresh1, tb, xblk)


def kernel(x, c, w_ada, b_ada, w_in, b_gates, mh_norm_w, w_conv, w_out, ln1_g, ln1_b,
           w_grp, b_grp, w_router, b_router, w_gate, w_up, w_down, ln2_g, ln2_b):
    bsz, seq, _ = x.shape
    tok = bsz * seq
    n_rows = tok + N_BUCKETS * MOE_TILE
    n_tiles = n_rows // MOE_TILE
    n_exp_total = N_GROUPS * N_EXP

    wg_all = w_gate.reshape(DEPTH * n_exp_total, D_MODEL, D_EXPERT)
    wu_all = w_up.reshape(DEPTH * n_exp_total, D_MODEL, D_EXPERT)
    wd_all = w_down.reshape(DEPTH * n_exp_total, D_EXPERT, D_MODEL)
    mod = _ada_mod(c, w_ada, b_ada).reshape(DEPTH, bsz, 6, 1, D_MODEL)
    x2 = x.reshape(tok, D_MODEL)
    for l in range(DEPTH):
        sh1, sc1, gt1, sh2, sc2, gt2 = [mod[l, :, i] for i in range(6)]
        wl = w_in[l]
        wq = wl[:, :4 * M_WIDTH].astype(BF16)
        wg = jnp.pad(wl[:, 4 * M_WIDTH:4 * M_WIDTH + 2 * HEADS],
                     ((0, 0), (0, LANES - 2 * HEADS))).astype(BF16)
        wc = wl[:, 4 * M_WIDTH + 2 * HEADS:].astype(BF16)
        bg = jnp.pad(b_gates[l], (0, LANES - 2 * HEADS)).reshape(1, LANES)
        qkvo, grow, yc = _inproj(x2, sc1, sh1, wq, wg, wc, bg, w_conv[l], bsz, seq)
        n_logit = N_GROUPS + n_exp_total
        wr = jnp.pad(jnp.concatenate([w_grp[l], w_router[l]], axis=1).T,
                     ((0, ROUTE_ROWS - n_logit), (0, 0))).astype(BF16)
        br = jnp.pad(jnp.concatenate([b_grp[l], b_router[l]]), (0, ROUTE_ROWS - n_logit))
        br = jnp.broadcast_to(br[:, None], (ROUTE_ROWS, LANES))
        x1, u2, route, cnt = _mixer(qkvo, grow, mh_norm_w[l].reshape(1, M_WIDTH), yc, x2, gt1, sc2, sh2,
                                    w_out[l].astype(BF16), ln1_g[l].reshape(1, D_MODEL),
                                    ln1_b[l].reshape(1, D_MODEL), wr, br, bsz, seq)
        pos, pads, plan = _route_plan(route, cnt, n_tiles, l)
        xs = _dispatch(pos, pads, u2, n_rows)
        wsel = jnp.concatenate([jnp.broadcast_to(w_grp[l], (N_BUCKETS, D_MODEL, N_GROUPS)),
                                w_router[l][:, _BUCKET_S0].T[:, :, None],
                                w_router[l][:, _BUCKET_S1].T[:, :, None]], axis=2)
        wsel = jnp.pad(wsel, ((0, 0), (0, 0), (0, LANES - N_GROUPS - 2))).astype(BF16)
        bsel = jnp.concatenate([jnp.broadcast_to(b_grp[l], (N_BUCKETS, N_GROUPS)),
                                b_router[l][_BUCKET_S0][:, None], b_router[l][_BUCKET_S1][:, None]], axis=1)
        bsel = jnp.pad(bsel, ((0, 0), (0, LANES - N_GROUPS - 2))).reshape(N_BUCKETS, 1, LANES)
        ys = _moe(plan, xs, wsel, bsel, wg_all, wu_all, wd_all, n_tiles)
        x2 = _combine(pos, ys, x1, gt2, ln2_g[l].reshape(1, D_MODEL),
                      ln2_b[l].reshape(1, D_MODEL), bsz, seq)
    return x2.reshape(bsz, seq, D_MODEL)
```

```python
import functools
import math

import jax
import jax.numpy as jnp
import numpy as np
from jax import lax
from jax.experimental import pallas as pl
from jax.experimental.pallas import tpu as pltpu

F32 = jnp.float32
BF16 = jnp.bfloat16
I32 = jnp.int32

D_MODEL = 1024
DEPTH = 2
HEADS = 4
D_HEAD = 128
M_WIDTH = HEADS * D_HEAD
C_WIDTH = D_MODEL - M_WIDTH
N_GROUPS = 4
N_EXP = 4
D_EXPERT = 512
N_PAIRS = 6
N_BUCKETS = N_GROUPS * N_PAIRS
ALPHA = (2 * DEPTH) ** 0.25
LN_EPS = 1e-5
QK_SCALE = D_HEAD ** -0.5
LOG_QK_SCALE = math.log(QK_SCALE)
GATE_ROWS = 24

LANES = 128
ROUTE_ROWS = 32

TOK_TILE = 512
INPROJ_TILE = 1024
CHUNK = 128
MOE_TILE = 256
MOE_STEP = 2 * MOE_TILE
VMEM_LIMIT = 56 * 1024 * 1024

_PAIR_ORDER = (0, 2, 3, 1, 4, 5)
_SLOT0 = (0, 2, 2, 3, 3, 3)
_SLOT1 = (1, 1, 0, 0, 1, 2)
_BUCKET_S0 = np.array([g * N_EXP + _SLOT0[p] for g in range(N_GROUPS) for p in range(N_PAIRS)])
_BUCKET_S1 = np.array([g * N_EXP + _SLOT1[p] for g in range(N_GROUPS) for p in range(N_PAIRS)])


def _cparams(sem):
    return pltpu.CompilerParams(dimension_semantics=sem, vmem_limit_bytes=VMEM_LIMIT)


def _ada_kernel(c_ref, w_ref, b_ref, o_ref):
    c = c_ref[...]
    cond = (c * jax.nn.sigmoid(c)).astype(BF16)
    o_ref[0] = jnp.dot(cond, w_ref[0].astype(BF16), preferred_element_type=F32) + b_ref[0]


def _ada_mod(c, w_ada, b_ada):
    bsz = c.shape[0]
    nblk = w_ada.shape[2] // D_MODEL
    return pl.pallas_call(
        _ada_kernel,
        grid=(DEPTH, nblk),
        in_specs=[
            pl.BlockSpec((bsz, D_MODEL), lambda l, j: (0, 0)),
            pl.BlockSpec((1, D_MODEL, D_MODEL), lambda l, j: (l, 0, j)),
            pl.BlockSpec((1, 1, D_MODEL), lambda l, j: (l, 0, j)),
        ],
        out_specs=pl.BlockSpec((1, bsz, D_MODEL), lambda l, j: (l, 0, j)),
        out_shape=jax.ShapeDtypeStruct((DEPTH, bsz, nblk * D_MODEL), F32),
        compiler_params=_cparams(("arbitrary", "arbitrary")),
        name="ada_mod",
    )(c, w_ada, b_ada.reshape(DEPTH, 1, -1))


def _log_sigmoid(x):
    return jnp.minimum(x, 0.0) - jnp.log1p(jnp.exp(-jnp.abs(x)))


def _chunk_scan(x, op, identity):
    lane = lax.broadcasted_iota(I32, x.shape, 1) & (CHUNK - 1)
    sh = 1
    while sh < CHUNK:
        x = op(x, jnp.where(lane >= sh, pltpu.roll(x, sh, 1), identity))
        sh *= 2
    return x


def _inproj_kernel(x_ref, sc_ref, sh_ref, wq_ref, wg_ref, wc_ref, bg_ref, wconv_ref,
                   qkvo_ref, grow_ref, yc_ref, zbuf):
    tm = x_ref.shape[0]
    u = (x_ref[...] * (1.0 + sc_ref[0]) + sh_ref[0]).astype(BF16)
    qkvo_ref[...] = jnp.dot(u, wq_ref[...], preferred_element_type=F32).astype(BF16)
    g = jnp.dot(u, wg_ref[...], preferred_element_type=F32) + bg_ref[...]
    gt = g.T[:8]
    head_row = lax.broadcasted_iota(I32, gt.shape, 0) < HEADS
    ig = jnp.where(head_row, gt, 0.0)
    logf = jnp.where(head_row, _log_sigmoid(pltpu.roll(gt, HEADS, 0)), 0.0)
    bcum = _chunk_scan(logf, jnp.add, 0.0)
    ug = ig - bcum
    grow_ref[0:8, :] = ug
    grow_ref[8:16, :] = bcum
    grow_ref[16:24, :] = _chunk_scan(ug, jnp.maximum, -jnp.inf)
    pc = jnp.dot(u, wc_ref[...], preferred_element_type=F32)
    z = pc[:, C_WIDTH:2 * C_WIDTH] * pc[:, 2 * C_WIDTH:]

    @pl.when(pl.program_id(1) == 0)
    def _():
        zbuf[0:8, :] = jnp.zeros((8, C_WIDTH), F32)

    zbuf[8:8 + tm, :] = z
    zc = (wconv_ref[0:1, :] * zbuf[6:6 + tm, :] + wconv_ref[1:2, :] * zbuf[7:7 + tm, :]
          + wconv_ref[2:3, :] * z)
    yc_ref[...] = (pc[:, :C_WIDTH] * zc).astype(BF16)
    zbuf[0:8, :] = zbuf[tm:tm + 8, :]


def _inproj(x2, sc, sh, wq, wg, wc, bg, wconv, bsz, seq):
    tm = min(INPROJ_TILE, seq)
    nt = seq // tm
    tok = bsz * seq
    row = lambda b, s: (b * nt + s, 0)
    const = lambda b, s: (0, 0)
    mod = lambda b, s: (b, 0, 0)
    return pl.pallas_call(
        _inproj_kernel,
        grid=(bsz, nt),
        in_specs=[
            pl.BlockSpec((tm, D_MODEL), row),
            pl.BlockSpec((1, 1, D_MODEL), mod),
            pl.BlockSpec((1, 1, D_MODEL), mod),
            pl.BlockSpec(wq.shape, const),
            pl.BlockSpec(wg.shape, const),
            pl.BlockSpec(wc.shape, const),
            pl.BlockSpec(bg.shape, const),
            pl.BlockSpec(wconv.shape, const),
        ],
        out_specs=[
            pl.BlockSpec((tm, 4 * M_WIDTH), row),
            pl.BlockSpec((GATE_ROWS, tm), lambda b, s: (0, b * nt + s)),
            pl.BlockSpec((tm, C_WIDTH), row),
        ],
        out_shape=[
            jax.ShapeDtypeStruct((tok, 4 * M_WIDTH), BF16),
            jax.ShapeDtypeStruct((GATE_ROWS, tok), F32),
            jax.ShapeDtypeStruct((tok, C_WIDTH), BF16),
        ],
        scratch_shapes=[pltpu.VMEM((tm + 8, C_WIDTH), F32)],
        compiler_params=_cparams(("arbitrary", "arbitrary")),
        name="inproj",
    )(x2, sc, sh, wq, wg, wc, bg, wconv)


N_ROWQ = 4
SPLIT = 3


def _split3(x):
    hi = x.astype(BF16).astype(F32)
    r1 = x - hi
    mid = r1.astype(BF16).astype(F32)
    return [hi, mid, r1 - mid]


def _mlstm_selector(sel):
    r = lax.broadcasted_iota(I32, sel.shape, 0)
    c = lax.broadcasted_iota(I32, sel.shape, 1) // LANES
    rq = r // 8
    quantity = ((rq >= SPLIT).astype(I32) + (rq >= 2 * SPLIT).astype(I32)
                + (rq >= 3 * SPLIT).astype(I32))
    hit = ((r % 8) == c // N_ROWQ) & (quantity == c % N_ROWQ) & (rq < N_ROWQ * SPLIT)
    sel[...] = hit.astype(BF16)


def _mlstm_block(qkvo_ref, grow_ref, nw_ref, out_ref, cst, mst, sel, between=()):
    between = tuple(between) + (None,) * 4
    sb = qkvo_ref.shape[0]
    L = CHUNK
    causal = lax.broadcasted_iota(I32, (L, L), 0) >= lax.broadcasted_iota(I32, (L, L), 1)
    ones_ext = jnp.ones((L, D_HEAD), BF16)
    pad_rows = jnp.zeros((LANES - 8 * N_ROWQ * SPLIT, L), F32)
    eye = (lax.broadcasted_iota(I32, (L, L), 0) == lax.broadcasted_iota(I32, (L, L), 1)).astype(BF16)
    n_chunks = sb // L
    pairs = [(c, h) for c in range(n_chunks) for h in range(HEADS)]

    def cols(part, c, h):
        c0 = part * M_WIDTH + h * D_HEAD
        return qkvo_ref[c * L:(c + 1) * L, c0:c0 + D_HEAD]

    m_prev = mst[...]
    stack_t, us, decay = [], [], []
    for c in range(n_chunks):
        ug = grow_ref[0:8, c * L:(c + 1) * L]
        bcum = grow_ref[8:16, c * L:(c + 1) * L]
        cmax = grow_ref[16:24, c * L:(c + 1) * L]
        big_m = jnp.maximum(m_prev, cmax)
        m_last = jnp.broadcast_to(big_m[:, L - 1:L], (8, L))
        g_tot = jnp.broadcast_to(bcum[:, L - 1:L], (8, L))
        wq = jnp.exp(m_prev - big_m) * QK_SCALE
        log_em = -(bcum + big_m)
        wk = jnp.exp(ug - m_last)
        decay.append(jnp.exp(m_prev - m_last))
        us.append(ug + LOG_QK_SCALE)
        m_prev = g_tot + m_last
        stack = jnp.concatenate(_split3(big_m) + _split3(wq) + _split3(log_em) + _split3(wk)
                                + [pad_rows], axis=0)
        stack_t.append(stack.T.astype(BF16))
    mst[...] = m_prev
    if between[0] is not None:
        between[0]()

    def rep(c, h, j):
        c0 = (h * N_ROWQ + j) * LANES
        return jnp.dot(stack_t[c], sel[:, c0:c0 + LANES], preferred_element_type=F32)

    s_mat = {ch: lax.dot_general(cols(0, *ch), cols(1, *ch), (((1,), (1,)), ((), ())),
                                 preferred_element_type=F32) for ch in pairs}
    kw = {ch: (cols(1, *ch).astype(F32) * rep(*ch, 3)).astype(BF16) for ch in pairs}
    kw_t = {ch: lax.dot_general(eye, kw[ch], (((1,), (1,)), ((), ())),
                                preferred_element_type=F32).astype(BF16) for ch in pairs}
    if between[1] is not None:
        between[1]()
    lhs = {}
    for ch in pairs:
        c, h = ch
        p = jnp.where(causal, s_mat[ch] * jnp.exp(us[c][h:h + 1, :] - rep(c, h, 0)), 0.0)
        qw = cols(0, c, h).astype(F32) * rep(c, h, 1)
        lhs[ch] = jnp.concatenate([p.astype(BF16), qw.astype(BF16)], axis=1)
    vext = {ch: jnp.concatenate([cols(2, *ch), ones_ext], axis=1) for ch in pairs}
    upd = {ch: jnp.dot(kw_t[ch], vext[ch], preferred_element_type=F32) for ch in pairs}
    if between[2] is not None:
        between[2]()
    tot = {}
    for c in range(n_chunks):
        for h in range(HEADS):
            state = cst[h]
            rhs = jnp.concatenate([vext[(c, h)], state.astype(BF16)], axis=0)
            tot[(c, h)] = jnp.dot(lhs[(c, h)], rhs, preferred_element_type=F32)
            dec = jnp.concatenate([decay[c][h:h + 1, :], decay[c][h:h + 1, :]], axis=1)
            cst[h] = dec * state + upd[(c, h)]
    if between[3] is not None:
        between[3]()
    for ch in pairs:
        c, h = ch
        c0 = h * D_HEAD
        num, den = tot[ch][:, :D_HEAD], tot[ch][:, D_HEAD:]
        hh = num / jnp.maximum(jnp.abs(den), jnp.exp(rep(c, h, 2)))
        mu = jnp.mean(hh, axis=1, keepdims=True)
        var = jnp.mean(jnp.square(hh - mu), axis=1, keepdims=True)
        hn = (hh - mu) * lax.rsqrt(var + LN_EPS) * nw_ref[:, c0:c0 + D_HEAD]
        og = jax.nn.sigmoid(cols(3, c, h).astype(F32))
        out_ref[c * L:(c + 1) * L, c0:c0 + D_HEAD] = (hn * og).astype(BF16)


def _layer_norm(r, g, b):
    mu = jnp.mean(r, axis=-1, keepdims=True)
    var = jnp.mean(jnp.square(r - mu), axis=-1, keepdims=True)
    return (r - mu) * lax.rsqrt(var + LN_EPS) * g + b


TOK_SUB = D_MODEL // LANES


def _store_token_major(ref, val, first_row=0):
    rows = val.shape[0]
    for k in range(TOK_SUB):
        ref[pl.ds(first_row * TOK_SUB + k, rows, stride=TOK_SUB), :] = val[:, k * LANES:(k + 1) * LANES]


def _load_token_major(ref, rows, first_row=0):
    return jnp.concatenate([ref[pl.ds(first_row * TOK_SUB + k, rows, stride=TOK_SUB), :]
                            for k in range(TOK_SUB)], axis=1)


def _outproj_parts(hn_val, yc_ref, x_ref, gt_ref, sc_ref, sh_ref, wo_ref, lg_ref, lb_ref,
                   wr_ref, br_ref, x1_ref, u2_ref, route_ref, cnt_ref, base, triu, live):
    tm = x_ref.shape[0]
    held = {}

    def project():
        held["mix"] = (jnp.dot(hn_val, wo_ref[0:M_WIDTH, :], preferred_element_type=F32)
                       + jnp.dot(yc_ref[...], wo_ref[M_WIDTH:, :], preferred_element_type=F32))

    def normalise():
        x1 = _layer_norm(ALPHA * x_ref[...] + (1.0 + gt_ref[0]) * held["mix"], lg_ref[...], lb_ref[...])
        x1_ref[...] = x1
        u2 = x1 * (1.0 + sc_ref[0]) + sh_ref[0]
        _store_token_major(u2_ref, u2)
        held["logits"] = lax.dot_general(wr_ref[...], u2.astype(BF16), (((1,), (1,)), ((), ())),
                                         preferred_element_type=F32) + br_ref[:, 0:1]

    def route():
        _route_block(held["logits"], route_ref, cnt_ref, base, triu, live, tm)

    return project, normalise, route


def _route_block(lt, route_ref, cnt_ref, base, triu, live, tm):
    lg = [lt[j:j + 1, :] for j in range(N_GROUPS + N_GROUPS * N_EXP)]
    best = lg[0]
    grp = jnp.zeros((1, tm), I32)
    for j in range(1, N_GROUPS):
        c = lg[j] > best
        grp = jnp.where(c, j, grp)
        best = jnp.where(c, lg[j], best)
    sel = []
    for e in range(N_EXP):
        val = lg[N_GROUPS + e]
        for gg in range(1, N_GROUPS):
            val = jnp.where(grp == gg, lg[N_GROUPS + gg * N_EXP + e], val)
        sel.append(val)
    v1 = sel[0]
    i1 = jnp.zeros((1, tm), I32)
    for e in range(1, N_EXP):
        c = sel[e] > v1
        i1 = jnp.where(c, e, i1)
        v1 = jnp.where(c, sel[e], v1)
    v2 = jnp.full((1, tm), -jnp.inf, F32)
    i2 = jnp.zeros((1, tm), I32)
    for e in range(N_EXP):
        cand = jnp.where(i1 == e, -jnp.inf, sel[e])
        c = cand > v2
        i2 = jnp.where(c, e, i2)
        v2 = jnp.where(c, cand, v2)
    ea = jnp.minimum(i1, i2)
    eb = jnp.maximum(i1, i2)
    lex = jnp.where(ea == 0, 0, jnp.where(ea == 1, 3, 5)) + eb - ea - 1
    pair = lex
    for i, p in enumerate(_PAIR_ORDER):
        if i != p:
            pair = jnp.where(lex == i, p, pair)
    bucket = grp * N_PAIRS + pair

    onehot = lax.broadcasted_iota(I32, (ROUTE_ROWS, tm), 0) == bucket
    cum = jnp.dot(onehot.astype(BF16), triu[...], preferred_element_type=F32)
    prev = base[:, 0:1]
    rank = jnp.sum(jnp.where(onehot, cum - 1.0 + prev, 0.0), axis=0, keepdims=True)
    new_base = prev + live * cum[:, tm - 1:tm]
    base[...] = jnp.broadcast_to(new_base, base.shape)
    cnt_ref[...] = jnp.broadcast_to(new_base, cnt_ref.shape)

    zrow = jnp.zeros((1, tm), F32)
    route_ref[...] = jnp.concatenate(
        [bucket.astype(F32), rank, zrow, zrow, zrow, zrow, zrow, zrow], axis=0)


def _mixer_kernel(qkvo_ref, grow_ref, nw_ref, yc_ref, x_ref, gt_ref, sc_ref, sh_ref, wo_ref, lg_ref,
                  lb_ref, wr_ref, br_ref, x1_ref, u2_ref, route_ref, cnt_ref,
                  cst, mst, sel, hn, base, triu, *, blocks_per_seq):
    i = pl.program_id(0)
    tm = x_ref.shape[0]

    @pl.when(i == 0)
    def _():
        _mlstm_selector(sel)
        hn[...] = jnp.zeros(hn.shape, BF16)
        base[...] = jnp.zeros(base.shape, F32)
        rid = lax.broadcasted_iota(I32, triu.shape, 0)
        cid = lax.broadcasted_iota(I32, triu.shape, 1)
        triu[...] = (rid <= cid).astype(BF16)

    @pl.when(i % blocks_per_seq == 0)
    def _():
        cst[...] = jnp.zeros(cst.shape, F32)
        mst[...] = jnp.zeros(mst.shape, F32)

    live = jnp.where(i > 0, 1.0, 0.0).astype(F32)
    parts = _outproj_parts(hn[...], yc_ref, x_ref, gt_ref, sc_ref, sh_ref, wo_ref, lg_ref, lb_ref,
                           wr_ref, br_ref, x1_ref, u2_ref, route_ref, cnt_ref, base, triu, live)
    _mlstm_block(qkvo_ref, grow_ref, nw_ref, hn, cst, mst, sel, between=parts)


def _mixer(qkvo, grow, nw, yc, x2, gt, sc, sh, wo, lg, lb, wr, br, bsz, seq):
    tm = min(TOK_TILE, seq)
    nt = seq // tm
    tok = bsz * seq
    n = bsz * nt
    cur = lambda i: (jnp.minimum(i, n - 1), 0)
    prev = lambda i: (jnp.maximum(i - 1, 0), 0)
    const = lambda i: (0, 0)
    mod = lambda i: (jnp.maximum(i - 1, 0) // nt, 0, 0)
    return pl.pallas_call(
        functools.partial(_mixer_kernel, blocks_per_seq=nt),
        grid=(n + 1,),
        in_specs=[
            pl.BlockSpec((tm, 4 * M_WIDTH), cur),
            pl.BlockSpec((GATE_ROWS, tm), lambda i: (0, jnp.minimum(i, n - 1))),
            pl.BlockSpec((1, M_WIDTH), const),
            pl.BlockSpec((tm, C_WIDTH), prev),
            pl.BlockSpec((tm, D_MODEL), prev),
            pl.BlockSpec((1, 1, D_MODEL), mod),
            pl.BlockSpec((1, 1, D_MODEL), mod),
            pl.BlockSpec((1, 1, D_MODEL), mod),
            pl.BlockSpec(wo.shape, const),
            pl.BlockSpec(lg.shape, const),
            pl.BlockSpec(lb.shape, const),
            pl.BlockSpec(wr.shape, const),
            pl.BlockSpec(br.shape, const),
        ],
        out_specs=[
            pl.BlockSpec((tm, D_MODEL), prev),
            pl.BlockSpec((tm * TOK_SUB, LANES), prev),
            pl.BlockSpec((8, tm), lambda i: (0, jnp.maximum(i - 1, 0))),
            pl.BlockSpec((ROUTE_ROWS, LANES), const),
        ],
        out_shape=[
            jax.ShapeDtypeStruct((tok, D_MODEL), F32),
            jax.ShapeDtypeStruct((tok * TOK_SUB, LANES), F32),
            jax.ShapeDtypeStruct((8, tok), F32),
            jax.ShapeDtypeStruct((ROUTE_ROWS, LANES), F32),
        ],
        scratch_shapes=[pltpu.VMEM((HEADS, D_HEAD, 2 * D_HEAD), F32),
                        pltpu.VMEM((8, CHUNK), F32),
                        pltpu.VMEM((LANES, HEADS * N_ROWQ * LANES), BF16),
                        pltpu.VMEM((tm, M_WIDTH), BF16),
                        pltpu.VMEM((ROUTE_ROWS, LANES), F32),
                        pltpu.VMEM((tm, tm), BF16)],
        compiler_params=_cparams(("arbitrary",)),
        name="mixer",
    )(qkvo, grow, nw, yc, x2, gt, sc, sh, wo, lg, lb, wr, br)


DISPATCH_TILE = 1024
DMA_UNROLL = 8


def _dispatch_kernel(pos_ref, pad_ref, u_ref, xs_ref, zeros, sem, zsem):
    ts = u_ref.shape[0] // TOK_SUB
    t0 = pl.program_id(0) * ts

    def zero_rows(first_row, n_rows_static):
        dst = pl.multiple_of(first_row * TOK_SUB, TOK_SUB)
        return pltpu.make_async_copy(zeros.at[pl.ds(0, n_rows_static * TOK_SUB), :],
                                     xs_ref.at[pl.ds(dst, n_rows_static * TOK_SUB), :], zsem)

    def fill_pads(wait):
        def bucket(b, carry):
            row = pad_ref[b]
            n = pad_ref[N_BUCKETS + b]
            size = MOE_STEP // 2
            while size >= 1:
                @pl.when((n & size) != 0)
                def _(row=row, size=size):
                    cp = zero_rows(row, size)
                    cp.wait() if wait else cp.start()
                row = row + (n & size)
                size //= 2
            return carry
        lax.fori_loop(0, N_BUCKETS, bucket, 0)

        def idle_tile(j, carry):
            cp = zero_rows(j * MOE_TILE, MOE_TILE)
            cp.wait() if wait else cp.start()
            return carry
        lax.fori_loop(pad_ref[2 * N_BUCKETS], xs_ref.shape[0] // (MOE_TILE * TOK_SUB), idle_tile, 0)

    @pl.when(pl.program_id(0) == 0)
    def _():
        zeros[...] = jnp.zeros(zeros.shape, F32)
        fill_pads(wait=False)
        fill_pads(wait=True)

    def issue(g, carry):
        for u in range(DMA_UNROLL):
            r = g * DMA_UNROLL + u
            dst = pl.multiple_of(pos_ref[t0 + r] * TOK_SUB, TOK_SUB)
            src = pl.multiple_of(r * TOK_SUB, TOK_SUB)
            pltpu.make_async_copy(u_ref.at[pl.ds(src, TOK_SUB), :],
                                  xs_ref.at[pl.ds(dst, TOK_SUB), :], sem).start(priority=u % 2)
        return carry

    lax.fori_loop(0, ts // DMA_UNROLL, issue, 0)
    pltpu.make_async_copy(u_ref, xs_ref.at[pl.ds(0, ts * TOK_SUB), :], sem).wait()


def _dispatch(pos, pad_start, u2, n_rows):
    tok = u2.shape[0] // TOK_SUB
    ts = min(DISPATCH_TILE, tok)
    return pl.pallas_call(
        _dispatch_kernel,
        grid_spec=pltpu.PrefetchScalarGridSpec(
            num_scalar_prefetch=2,
            grid=(tok // ts,),
            in_specs=[pl.BlockSpec((ts * TOK_SUB, LANES), lambda i, pos, pad: (i, 0))],
            out_specs=pl.BlockSpec(memory_space=pl.ANY),
            scratch_shapes=[pltpu.VMEM((MOE_TILE * TOK_SUB, LANES), F32),
                            pltpu.SemaphoreType.DMA(()), pltpu.SemaphoreType.DMA(())],
        ),
        out_shape=jax.ShapeDtypeStruct((n_rows * TOK_SUB, LANES), F32),
        compiler_params=_cparams(("arbitrary",)),
        name="dispatch",
    )(pos, pad_start, u2)


def _moe_kernel(s0_ref, s1_ref, tiles_ref, fresh0_ref, fresh1_ref, tb_ref, xb_ref, xs_ref, wsel_ref,
                bsel_ref, wg0, wu0, wd0, wg1, wu1, wd1, y_ref, wgs, wus, wds):
    del s0_ref, s1_ref, tb_ref, xb_ref
    j = pl.program_id(0)
    rows = MOE_TILE

    @pl.when(fresh0_ref[j] == 1)
    def _():
        wgs[0] = wg0[0].astype(BF16)
        wus[0] = wu0[0].astype(BF16)
        wds[0] = wd0[0].astype(BF16)

    @pl.when(fresh1_ref[j] == 1)
    def _():
        wgs[1] = wg1[0].astype(BF16)
        wus[1] = wu1[0].astype(BF16)
        wds[1] = wd1[0].astype(BF16)

    def experts(part):
        x = _load_token_major(xs_ref, rows, part * rows).astype(BF16)
        lg = jnp.dot(x, wsel_ref[0], preferred_element_type=F32) + bsel_ref[0]
        grp = [lg[:, i:i + 1] for i in range(N_GROUPS)]
        gmax = jnp.maximum(jnp.maximum(grp[0], grp[1]), jnp.maximum(grp[2], grp[3]))
        gsum = sum(jnp.exp(v - gmax) for v in grp)
        la, lb = lg[:, N_GROUPS:N_GROUPS + 1], lg[:, N_GROUPS + 1:N_GROUPS + 2]
        emax = jnp.maximum(la, lb)
        ea, eb = jnp.exp(la - emax), jnp.exp(lb - emax)
        scale = 1.0 / (gsum * (ea + eb))
        wcols = (ea * scale, eb * scale)
        acc = None
        for slot in range(2):
            g = jnp.dot(x, wgs[slot], preferred_element_type=F32)
            u = jnp.dot(x, wus[slot], preferred_element_type=F32)
            hid = (g * jax.nn.sigmoid(g) * u * wcols[slot]).astype(BF16)
            y = jnp.dot(hid, wds[slot], preferred_element_type=F32)
            acc = y if acc is None else acc + y
        _store_token_major(y_ref, acc, part * rows)

    for part in range(MOE_STEP // MOE_TILE):
        @pl.when(tiles_ref[j] > part)
        def _(part=part):
            experts(part)

        @pl.when(tiles_ref[j] <= part)
        def _(part=part):
            lo = part * rows * TOK_SUB
            y_ref[lo:lo + rows * TOK_SUB, :] = jnp.zeros((rows * TOK_SUB, LANES), F32)


def _moe(plan, xs, wsel, bsel, w_gate, w_up, w_down, n_tiles):
    wsel1 = lambda j, s0, s1, v, f0, f1, tb, xb: (s0[j], 0, 0)
    wsel2 = lambda j, s0, s1, v, f0, f1, tb, xb: (s1[j], 0, 0)
    by_bucket = lambda j, s0, s1, v, f0, f1, tb, xb: (tb[j], 0, 0)
    up_spec = lambda sel: pl.BlockSpec((1, D_MODEL, D_EXPERT), sel)
    dn_spec = lambda sel: pl.BlockSpec((1, D_EXPERT, D_MODEL), sel)
    tile_rows = MOE_STEP * TOK_SUB
    return pl.pallas_call(
        _moe_kernel,
        grid_spec=pltpu.PrefetchScalarGridSpec(
            num_scalar_prefetch=7,
            grid=(n_tiles,),
            in_specs=[pl.BlockSpec((tile_rows, LANES), lambda j, s0, s1, v, f0, f1, tb, xb: (xb[j], 0)),
                      pl.BlockSpec((1, D_MODEL, LANES), by_bucket),
                      pl.BlockSpec((1, 1, LANES), by_bucket),
                      up_spec(wsel1), up_spec(wsel1), dn_spec(wsel1),
                      up_spec(wsel2), up_spec(wsel2), dn_spec(wsel2)],
            out_specs=pl.BlockSpec((tile_rows, LANES), lambda j, *_: (j, 0)),
            scratch_shapes=[pltpu.VMEM((2, D_MODEL, D_EXPERT), BF16),
                            pltpu.VMEM((2, D_MODEL, D_EXPERT), BF16),
                            pltpu.VMEM((2, D_EXPERT, D_MODEL), BF16)],
        ),
        out_shape=jax.ShapeDtypeStruct((n_tiles * tile_rows, LANES), F32),
        compiler_params=_cparams(("arbitrary",)),
        name="moe",
    )(*plan, xs, wsel, bsel, w_gate, w_up, w_down, w_gate, w_up, w_down)


def _combine_kernel(pos_ref, ys_ref, x1_ref, gt_ref, lg_ref, lb_ref, out_ref, ybuf, sems):
    ts = x1_ref.shape[0]
    i = pl.program_id(0)
    n = pl.num_programs(0)
    slot = i % 2

    def gather(tile, to_slot):
        def issue(g, carry):
            for u in range(DMA_UNROLL):
                r = g * DMA_UNROLL + u
                src = pl.multiple_of(pos_ref[tile * ts + r] * TOK_SUB, TOK_SUB)
                dst = pl.multiple_of(r * TOK_SUB, TOK_SUB)
                pltpu.make_async_copy(ys_ref.at[pl.ds(src, TOK_SUB), :],
                                      ybuf.at[to_slot, pl.ds(dst, TOK_SUB), :],
                                      sems.at[to_slot]).start(priority=u % 2)
            return carry
        lax.fori_loop(0, ts // DMA_UNROLL, issue, 0)

    @pl.when(i == 0)
    def _():
        gather(0, 0)

    @pl.when(i + 1 < n)
    def _():
        gather(i + 1, 1 - slot)

    pltpu.make_async_copy(ys_ref.at[pl.ds(0, ts * TOK_SUB), :], ybuf.at[slot], sems.at[slot]).wait()
    y = _load_token_major(ybuf.at[slot], ts)
    out_ref[...] = _layer_norm(ALPHA * x1_ref[...] + (1.0 + gt_ref[0]) * y, lg_ref[...], lb_ref[...])


def _combine(pos, ys, x1, gt, lg, lb, bsz, seq):
    ts = min(TOK_TILE, seq)
    nt = seq // ts
    tok = bsz * seq
    return pl.pallas_call(
        _combine_kernel,
        grid_spec=pltpu.PrefetchScalarGridSpec(
            num_scalar_prefetch=1,
            grid=(tok // ts,),
            in_specs=[pl.BlockSpec(memory_space=pl.ANY),
                      pl.BlockSpec((ts, D_MODEL), lambda i, pos: (i, 0)),
                      pl.BlockSpec((1, 1, D_MODEL), lambda i, pos: (i // nt, 0, 0)),
                      pl.BlockSpec((1, D_MODEL), lambda i, pos: (0, 0)),
                      pl.BlockSpec((1, D_MODEL), lambda i, pos: (0, 0))],
            out_specs=pl.BlockSpec((ts, D_MODEL), lambda i, pos: (i, 0)),
            scratch_shapes=[pltpu.VMEM((2, ts * TOK_SUB, LANES), F32), pltpu.SemaphoreType.DMA((2,))],
        ),
        out_shape=jax.ShapeDtypeStruct((tok, D_MODEL), F32),
        compiler_params=_cparams(("arbitrary",)),
        name="combine",
    )(pos, ys, x1, gt, lg, lb)


def _route_plan(route, cnt, n_tiles, layer):
    bucket = route[0].astype(I32)
    rank = route[1].astype(I32)
    counts = cnt[:N_BUCKETS, 0].astype(I32)
    tiles_b = (counts + MOE_STEP - 1) // MOE_STEP
    tile_end = jnp.cumsum(tiles_b)
    row_off = (tile_end - tiles_b) * MOE_STEP
    bucket_ids = jnp.arange(N_BUCKETS, dtype=I32)
    pos = jnp.sum(jnp.where(bucket[None, :] == bucket_ids[:, None], row_off[:, None], 0), axis=0) + rank
    total = tile_end[-1]
    j = jnp.arange(n_tiles, dtype=I32)
    tb = jnp.sum((tile_end[None, :] <= jnp.minimum(j, total - 1)[:, None]).astype(I32), axis=1)
    tb = jnp.clip(tb, 0, N_BUCKETS - 1)
    hit = tb[None, :] == bucket_ids[:, None]
    rows_left = jnp.sum(jnp.where(hit, (row_off + counts)[:, None], 0), axis=0) - j * MOE_STEP
    tiles_j = jnp.clip((rows_left + MOE_TILE - 1) // MOE_TILE, 0, MOE_STEP // MOE_TILE)
    tiles_j = jnp.where(j < total, tiles_j, 0)
    grp = tb // N_PAIRS
    pair = tb % N_PAIRS
    first = layer * (N_GROUPS * N_EXP) + grp * N_EXP
    s0 = first + sum((pair == i).astype(I32) * _SLOT0[i] for i in range(N_PAIRS))
    s1 = first + sum((pair == i).astype(I32) * _SLOT1[i] for i in range(N_PAIRS))
    one = jnp.ones((1,), I32)
    fresh0 = jnp.concatenate([one, (s0[1:] != s0[:-1]).astype(I32)])
    fresh1 = jnp.concatenate([one, (s1[1:] != s1[:-1]).astype(I32)])
    xblk = jnp.minimum(j, total - 1)
    pads = jnp.concatenate([row_off + counts, tiles_b * MOE_STEP - counts,
                            (total * (MOE_STEP // MOE_TILE))[None]])
    return pos, pads, (s0, s1, tiles_j, fresh0, fresh1, tb, xblk)


def kernel(x, c, w_ada, b_ada, w_in, b_gates, mh_norm_w, w_conv, w_out, ln1_g, ln1_b,
           w_grp, b_grp, w_router, b_router, w_gate, w_up, w_down, ln2_g, ln2_b):
    bsz, seq, _ = x.shape
    tok = bsz * seq
    n_rows = tok + N_BUCKETS * MOE_STEP
    n_tiles = n_rows // MOE_STEP
    n_exp_total = N_GROUPS * N_EXP

    wg_all = w_gate.reshape(DEPTH * n_exp_total, D_MODEL, D_EXPERT)
    wu_all = w_up.reshape(DEPTH * n_exp_total, D_MODEL, D_EXPERT)
    wd_all = w_down.reshape(DEPTH * n_exp_total, D_EXPERT, D_MODEL)
    mod = _ada_mod(c, w_ada, b_ada).reshape(DEPTH, bsz, 6, 1, D_MODEL)
    x2 = x.reshape(tok, D_MODEL)
    for l in range(DEPTH):
        sh1, sc1, gt1, sh2, sc2, gt2 = [mod[l, :, i] for i in range(6)]
        wl = w_in[l]
        wq = wl[:, :4 * M_WIDTH].astype(BF16)
        wg = jnp.pad(wl[:, 4 * M_WIDTH:4 * M_WIDTH + 2 * HEADS],
                     ((0, 0), (0, LANES - 2 * HEADS))).astype(BF16)
        wc = wl[:, 4 * M_WIDTH + 2 * HEADS:].astype(BF16)
        bg = jnp.pad(b_gates[l], (0, LANES - 2 * HEADS)).reshape(1, LANES)
        qkvo, grow, yc = _inproj(x2, sc1, sh1, wq, wg, wc, bg, w_conv[l], bsz, seq)
        n_logit = N_GROUPS + n_exp_total
        wr = jnp.pad(jnp.concatenate([w_grp[l], w_router[l]], axis=1).T,
                     ((0, ROUTE_ROWS - n_logit), (0, 0))).astype(BF16)
        br = jnp.pad(jnp.concatenate([b_grp[l], b_router[l]]), (0, ROUTE_ROWS - n_logit))
        br = jnp.broadcast_to(br[:, None], (ROUTE_ROWS, LANES))
        x1, u2, route, cnt = _mixer(qkvo, grow, mh_norm_w[l].reshape(1, M_WIDTH), yc, x2, gt1, sc2, sh2,
                                    w_out[l].astype(BF16), ln1_g[l].reshape(1, D_MODEL),
                                    ln1_b[l].reshape(1, D_MODEL), wr, br, bsz, seq)
        pos, pads, plan = _route_plan(route, cnt, n_tiles, l)
        xs = _dispatch(pos, pads, u2, n_rows)
        wsel = jnp.concatenate([jnp.broadcast_to(w_grp[l], (N_BUCKETS, D_MODEL, N_GROUPS)),
                                w_router[l][:, _BUCKET_S0].T[:, :, None],
                                w_router[l][:, _BUCKET_S1].T[:, :, None]], axis=2)
        wsel = jnp.pad(wsel, ((0, 0), (0, 0), (0, LANES - N_GROUPS - 2))).astype(BF16)
        bsel = jnp.concatenate([jnp.broadcast_to(b_grp[l], (N_BUCKETS, N_GROUPS)),
                                b_router[l][_BUCKET_S0][:, None], b_router[l][_BUCKET_S1][:, None]], axis=1)
        bsel = jnp.pad(bsel, ((0, 0), (0, LANES - N_GROUPS - 2))).reshape(N_BUCKETS, 1, LANES)
        ys = _moe(plan, xs, wsel, bsel, wg_all, wu_all, wd_all, n_tiles)
        x2 = _combine(pos, ys, x1, gt2, ln2_g[l].reshape(1, D_MODEL),
                      ln2_b[l].reshape(1, D_MODEL), bsz, seq)
    return x2.reshape(bsz, seq, D_MODEL)
```

```python
import functools
import math

import jax
import jax.numpy as jnp
import numpy as np
from jax import lax
from jax.experimental import pallas as pl
from jax.experimental.pallas import tpu as pltpu

F32 = jnp.float32
BF16 = jnp.bfloat16
I32 = jnp.int32

D_MODEL = 1024
DEPTH = 2
HEADS = 4
D_HEAD = 128
M_WIDTH = HEADS * D_HEAD
C_WIDTH = D_MODEL - M_WIDTH
N_GROUPS = 4
N_EXP = 4
D_EXPERT = 512
N_PAIRS = 6
N_BUCKETS = N_GROUPS * N_PAIRS
ALPHA = (2 * DEPTH) ** 0.25
LN_EPS = 1e-5
QK_SCALE = D_HEAD ** -0.5
LOG_QK_SCALE = math.log(QK_SCALE)
GATE_ROWS = 24

LANES = 128
ROUTE_ROWS = 32

TOK_TILE = 512
INPROJ_TILE = 1024
CHUNK = 256
MOE_TILE = 256
MOE_STEP = 2 * MOE_TILE
VMEM_LIMIT = 56 * 1024 * 1024

_PAIR_ORDER = (0, 2, 3, 1, 4, 5)
_SLOT0 = (0, 2, 2, 3, 3, 3)
_SLOT1 = (1, 1, 0, 0, 1, 2)
_BUCKET_S0 = np.array([g * N_EXP + _SLOT0[p] for g in range(N_GROUPS) for p in range(N_PAIRS)])
_BUCKET_S1 = np.array([g * N_EXP + _SLOT1[p] for g in range(N_GROUPS) for p in range(N_PAIRS)])


def _cparams(sem):
    return pltpu.CompilerParams(dimension_semantics=sem, vmem_limit_bytes=VMEM_LIMIT)


def _ada_kernel(c_ref, w_ref, b_ref, o_ref):
    c = c_ref[...]
    cond = (c * jax.nn.sigmoid(c)).astype(BF16)
    o_ref[0] = jnp.dot(cond, w_ref[0].astype(BF16), preferred_element_type=F32) + b_ref[0]


def _ada_mod(c, w_ada, b_ada):
    bsz = c.shape[0]
    nblk = w_ada.shape[2] // D_MODEL
    return pl.pallas_call(
        _ada_kernel,
        grid=(DEPTH, nblk),
        in_specs=[
            pl.BlockSpec((bsz, D_MODEL), lambda l, j: (0, 0)),
            pl.BlockSpec((1, D_MODEL, D_MODEL), lambda l, j: (l, 0, j)),
            pl.BlockSpec((1, 1, D_MODEL), lambda l, j: (l, 0, j)),
        ],
        out_specs=pl.BlockSpec((1, bsz, D_MODEL), lambda l, j: (l, 0, j)),
        out_shape=jax.ShapeDtypeStruct((DEPTH, bsz, nblk * D_MODEL), F32),
        compiler_params=_cparams(("arbitrary", "arbitrary")),
        name="ada_mod",
    )(c, w_ada, b_ada.reshape(DEPTH, 1, -1))


def _log_sigmoid(x):
    return jnp.minimum(x, 0.0) - jnp.log1p(jnp.exp(-jnp.abs(x)))


def _chunk_scan(x, op, identity):
    lane = lax.broadcasted_iota(I32, x.shape, 1) & (CHUNK - 1)
    sh = 1
    while sh < CHUNK:
        x = op(x, jnp.where(lane >= sh, pltpu.roll(x, sh, 1), identity))
        sh *= 2
    return x


def _inproj_kernel(x_ref, sc_ref, sh_ref, wq_ref, wg_ref, wc_ref, bg_ref, wconv_ref,
                   qkvo_ref, grow_ref, yc_ref, zbuf):
    tm = x_ref.shape[0]
    u = (x_ref[...] * (1.0 + sc_ref[0]) + sh_ref[0]).astype(BF16)
    qkvo_ref[...] = jnp.dot(u, wq_ref[...], preferred_element_type=F32).astype(BF16)
    g = jnp.dot(u, wg_ref[...], preferred_element_type=F32) + bg_ref[...]
    gt = g.T[:8]
    head_row = lax.broadcasted_iota(I32, gt.shape, 0) < HEADS
    ig = jnp.where(head_row, gt, 0.0)
    logf = jnp.where(head_row, _log_sigmoid(pltpu.roll(gt, HEADS, 0)), 0.0)
    bcum = _chunk_scan(logf, jnp.add, 0.0)
    ug = ig - bcum
    grow_ref[0:8, :] = ug
    grow_ref[8:16, :] = bcum
    grow_ref[16:24, :] = _chunk_scan(ug, jnp.maximum, -jnp.inf)
    pc = jnp.dot(u, wc_ref[...], preferred_element_type=F32)
    z = pc[:, C_WIDTH:2 * C_WIDTH] * pc[:, 2 * C_WIDTH:]

    @pl.when(pl.program_id(1) == 0)
    def _():
        zbuf[0:8, :] = jnp.zeros((8, C_WIDTH), F32)

    zbuf[8:8 + tm, :] = z
    zc = (wconv_ref[0:1, :] * zbuf[6:6 + tm, :] + wconv_ref[1:2, :] * zbuf[7:7 + tm, :]
          + wconv_ref[2:3, :] * z)
    yc_ref[...] = (pc[:, :C_WIDTH] * zc).astype(BF16)
    zbuf[0:8, :] = zbuf[tm:tm + 8, :]


def _inproj(x2, sc, sh, wq, wg, wc, bg, wconv, bsz, seq):
    tm = min(INPROJ_TILE, seq)
    nt = seq // tm
    tok = bsz * seq
    row = lambda b, s: (b * nt + s, 0)
    const = lambda b, s: (0, 0)
    mod = lambda b, s: (b, 0, 0)
    return pl.pallas_call(
        _inproj_kernel,
        grid=(bsz, nt),
        in_specs=[
            pl.BlockSpec((tm, D_MODEL), row),
            pl.BlockSpec((1, 1, D_MODEL), mod),
            pl.BlockSpec((1, 1, D_MODEL), mod),
            pl.BlockSpec(wq.shape, const),
            pl.BlockSpec(wg.shape, const),
            pl.BlockSpec(wc.shape, const),
            pl.BlockSpec(bg.shape, const),
            pl.BlockSpec(wconv.shape, const),
        ],
        out_specs=[
            pl.BlockSpec((tm, 4 * M_WIDTH), row),
            pl.BlockSpec((GATE_ROWS, tm), lambda b, s: (0, b * nt + s)),
            pl.BlockSpec((tm, C_WIDTH), row),
        ],
        out_shape=[
            jax.ShapeDtypeStruct((tok, 4 * M_WIDTH), BF16),
            jax.ShapeDtypeStruct((GATE_ROWS, tok), F32),
            jax.ShapeDtypeStruct((tok, C_WIDTH), BF16),
        ],
        scratch_shapes=[pltpu.VMEM((tm + 8, C_WIDTH), F32)],
        compiler_params=_cparams(("arbitrary", "arbitrary")),
        name="inproj",
    )(x2, sc, sh, wq, wg, wc, bg, wconv)


N_ROWQ = 4
SPLIT = 3


def _split3(x):
    hi = x.astype(BF16).astype(F32)
    r1 = x - hi
    mid = r1.astype(BF16).astype(F32)
    return [hi, mid, r1 - mid]


def _mlstm_selector(sel):
    r = lax.broadcasted_iota(I32, sel.shape, 0)
    c = lax.broadcasted_iota(I32, sel.shape, 1) // LANES
    rq = r // 8
    quantity = ((rq >= SPLIT).astype(I32) + (rq >= 2 * SPLIT).astype(I32)
                + (rq >= 3 * SPLIT).astype(I32))
    hit = ((r % 8) == c // N_ROWQ) & (quantity == c % N_ROWQ) & (rq < N_ROWQ * SPLIT)
    sel[...] = hit.astype(BF16)


def _mlstm_block(qkvo_ref, grow_ref, nw_ref, out_ref, cst, mst, sel, between=()):
    between = tuple(between) + (None,) * 3
    sb = qkvo_ref.shape[0]
    L = CHUNK
    causal = lax.broadcasted_iota(I32, (L, L), 0) >= lax.broadcasted_iota(I32, (L, L), 1)
    ones_ext = jnp.ones((L, D_HEAD), BF16)
    pad_rows = jnp.zeros((LANES - 8 * N_ROWQ * SPLIT, L), F32)
    eye = (lax.broadcasted_iota(I32, (D_HEAD, D_HEAD), 0)
           == lax.broadcasted_iota(I32, (D_HEAD, D_HEAD), 1)).astype(BF16)
    n_chunks = sb // L
    pairs = [(c, h) for c in range(n_chunks) for h in range(HEADS)]

    def cols(part, c, h):
        c0 = part * M_WIDTH + h * D_HEAD
        return qkvo_ref[c * L:(c + 1) * L, c0:c0 + D_HEAD]

    m_prev = mst[...]
    stack_t, us, decay = [], [], []
    for c in range(n_chunks):
        ug = grow_ref[0:8, c * L:(c + 1) * L]
        bcum = grow_ref[8:16, c * L:(c + 1) * L]
        cmax = grow_ref[16:24, c * L:(c + 1) * L]
        big_m = jnp.maximum(m_prev, cmax)
        m_last = jnp.broadcast_to(big_m[:, L - 1:L], (8, L))
        g_tot = jnp.broadcast_to(bcum[:, L - 1:L], (8, L))
        wq = jnp.exp(m_prev - big_m) * QK_SCALE
        log_em = -(bcum + big_m)
        wk = jnp.exp(ug - m_last)
        decay.append(jnp.exp(m_prev - m_last))
        us.append(ug + LOG_QK_SCALE)
        m_prev = g_tot + m_last
        stack = jnp.concatenate(_split3(big_m) + _split3(wq) + _split3(log_em) + _split3(wk)
                                + [pad_rows], axis=0)
        stack_t.append(stack.T.astype(BF16))
    mst[...] = m_prev
    if between[0] is not None:
        between[0]()

    reps = {}

    def rep(c, h, j):
        if (c, h) not in reps:
            c0 = h * N_ROWQ * LANES
            reps[(c, h)] = jnp.dot(stack_t[c], sel[:, c0:c0 + N_ROWQ * LANES],
                                   preferred_element_type=F32)
        return reps[(c, h)][:, j * LANES:(j + 1) * LANES]

    s_mat = {ch: lax.dot_general(cols(0, *ch), cols(1, *ch), (((1,), (1,)), ((), ())),
                                 preferred_element_type=F32) for ch in pairs}
    kw = {ch: (cols(1, *ch).astype(F32) * rep(*ch, 3)).astype(BF16) for ch in pairs}
    kw_t = {ch: lax.dot_general(eye, kw[ch], (((1,), (1,)), ((), ())),
                                preferred_element_type=F32).astype(BF16) for ch in pairs}
    if between[1] is not None:
        between[1]()
    lhs = {}
    for ch in pairs:
        c, h = ch
        big_m_rows = jnp.concatenate([rep(c, h, 0)] * (L // LANES), axis=1)
        p = jnp.where(causal, s_mat[ch] * jnp.exp(us[c][h:h + 1, :] - big_m_rows), 0.0)
        qw = cols(0, c, h).astype(F32) * rep(c, h, 1)
        lhs[ch] = jnp.concatenate([p.astype(BF16), qw.astype(BF16)], axis=1)
    vext = {ch: jnp.concatenate([cols(2, *ch), ones_ext], axis=1) for ch in pairs}
    upd = {ch: jnp.dot(kw_t[ch], vext[ch], preferred_element_type=F32) for ch in pairs}
    for c in range(n_chunks):
        tot = {}
        for h in range(HEADS):
            state = cst[h]
            rhs = jnp.concatenate([vext[(c, h)], state.astype(BF16)], axis=0)
            tot[h] = jnp.dot(lhs[(c, h)], rhs, preferred_element_type=F32)
            dec = jnp.concatenate([decay[c][h:h + 1, :LANES]] * 2, axis=1)
            cst[h] = dec * state + upd[(c, h)]
        for h in range(HEADS):
            c0 = h * D_HEAD
            num, den = tot[h][:, :D_HEAD], tot[h][:, D_HEAD:]
            hh = num / jnp.maximum(jnp.abs(den), jnp.exp(rep(c, h, 2)))
            mu = jnp.mean(hh, axis=1, keepdims=True)
            var = jnp.mean(jnp.square(hh - mu), axis=1, keepdims=True)
            hn = (hh - mu) * lax.rsqrt(var + LN_EPS) * nw_ref[:, c0:c0 + D_HEAD]
            og = jax.nn.sigmoid(cols(3, c, h).astype(F32))
            out_ref[c * L:(c + 1) * L, c0:c0 + D_HEAD] = (hn * og).astype(BF16)
        if c == 0 and between[2] is not None:
            between[2]()


def _layer_norm(r, g, b):
    mu = jnp.mean(r, axis=-1, keepdims=True)
    var = jnp.mean(jnp.square(r - mu), axis=-1, keepdims=True)
    return (r - mu) * lax.rsqrt(var + LN_EPS) * g + b


TOK_SUB = D_MODEL // LANES


def _store_token_major(ref, val, first_row=0):
    rows = val.shape[0]
    for k in range(TOK_SUB):
        ref[pl.ds(first_row * TOK_SUB + k, rows, stride=TOK_SUB), :] = val[:, k * LANES:(k + 1) * LANES]


def _load_token_major(ref, rows, first_row=0):
    return jnp.concatenate([ref[pl.ds(first_row * TOK_SUB + k, rows, stride=TOK_SUB), :]
                            for k in range(TOK_SUB)], axis=1)


def _outproj_parts(hn_ref, yc_ref, x_ref, gt_ref, sc_ref, sh_ref, wo_ref, lg_ref, lb_ref,
                   wr_ref, br_ref, x1_ref, u2_ref, route_ref, cnt_ref, base, triu, live):
    tm = x_ref.shape[0]
    held = {}

    def project():
        held["mix"] = (jnp.dot(hn_ref[...], wo_ref[0:M_WIDTH, :], preferred_element_type=F32)
                       + jnp.dot(yc_ref[...], wo_ref[M_WIDTH:, :], preferred_element_type=F32))

    def normalise():
        x1 = _layer_norm(ALPHA * x_ref[...] + (1.0 + gt_ref[0]) * held["mix"], lg_ref[...], lb_ref[...])
        x1_ref[...] = x1
        u2 = x1 * (1.0 + sc_ref[0]) + sh_ref[0]
        _store_token_major(u2_ref, u2)
        held["logits"] = lax.dot_general(wr_ref[...], u2.astype(BF16), (((1,), (1,)), ((), ())),
                                         preferred_element_type=F32) + br_ref[:, 0:1]

    def route():
        _route_block(held["logits"], route_ref, cnt_ref, base, triu, live, tm)

    return project, normalise, route


def _route_block(lt, route_ref, cnt_ref, base, triu, live, tm):
    lg = [lt[j:j + 1, :] for j in range(N_GROUPS + N_GROUPS * N_EXP)]
    best = lg[0]
    grp = jnp.zeros((1, tm), I32)
    for j in range(1, N_GROUPS):
        c = lg[j] > best
        grp = jnp.where(c, j, grp)
        best = jnp.where(c, lg[j], best)
    sel = []
    for e in range(N_EXP):
        val = lg[N_GROUPS + e]
        for gg in range(1, N_GROUPS):
            val = jnp.where(grp == gg, lg[N_GROUPS + gg * N_EXP + e], val)
        sel.append(val)
    v1 = sel[0]
    i1 = jnp.zeros((1, tm), I32)
    for e in range(1, N_EXP):
        c = sel[e] > v1
        i1 = jnp.where(c, e, i1)
        v1 = jnp.where(c, sel[e], v1)
    v2 = jnp.full((1, tm), -jnp.inf, F32)
    i2 = jnp.zeros((1, tm), I32)
    for e in range(N_EXP):
        cand = jnp.where(i1 == e, -jnp.inf, sel[e])
        c = cand > v2
        i2 = jnp.where(c, e, i2)
        v2 = jnp.where(c, cand, v2)
    ea = jnp.minimum(i1, i2)
    eb = jnp.maximum(i1, i2)
    lex = jnp.where(ea == 0, 0, jnp.where(ea == 1, 3, 5)) + eb - ea - 1
    pair = lex
    for i, p in enumerate(_PAIR_ORDER):
        if i != p:
            pair = jnp.where(lex == i, p, pair)
    bucket = grp * N_PAIRS + pair

    onehot = lax.broadcasted_iota(I32, (ROUTE_ROWS, tm), 0) == bucket
    cum = jnp.dot(onehot.astype(BF16), triu[...], preferred_element_type=F32)
    prev = base[:, 0:1]
    rank = jnp.sum(jnp.where(onehot, cum - 1.0 + prev, 0.0), axis=0, keepdims=True)
    new_base = prev + live * cum[:, tm - 1:tm]
    base[...] = jnp.broadcast_to(new_base, base.shape)
    cnt_ref[...] = jnp.broadcast_to(new_base, cnt_ref.shape)

    zrow = jnp.zeros((1, tm), F32)
    route_ref[...] = jnp.concatenate(
        [bucket.astype(F32), rank, zrow, zrow, zrow, zrow, zrow, zrow], axis=0)


def _mixer_kernel(qkvo_ref, grow_ref, nw_ref, yc_ref, x_ref, gt_ref, sc_ref, sh_ref, wo_ref, lg_ref,
                  lb_ref, wr_ref, br_ref, x1_ref, u2_ref, route_ref, cnt_ref,
                  cst, mst, sel, hn, base, triu, *, blocks_per_seq):
    i = pl.program_id(0)
    tm = x_ref.shape[0]

    @pl.when(i == 0)
    def _():
        _mlstm_selector(sel)
        hn[...] = jnp.zeros(hn.shape, BF16)
        base[...] = jnp.zeros(base.shape, F32)
        rid = lax.broadcasted_iota(I32, triu.shape, 0)
        cid = lax.broadcasted_iota(I32, triu.shape, 1)
        triu[...] = (rid <= cid).astype(BF16)

    @pl.when(i % blocks_per_seq == 0)
    def _():
        cst[...] = jnp.zeros(cst.shape, F32)
        mst[...] = jnp.zeros(mst.shape, F32)

    live = jnp.where(i > 0, 1.0, 0.0).astype(F32)
    parts = _outproj_parts(hn, yc_ref, x_ref, gt_ref, sc_ref, sh_ref, wo_ref, lg_ref, lb_ref,
                           wr_ref, br_ref, x1_ref, u2_ref, route_ref, cnt_ref, base, triu, live)
    _mlstm_block(qkvo_ref, grow_ref, nw_ref, hn, cst, mst, sel, between=parts)


def _mixer(qkvo, grow, nw, yc, x2, gt, sc, sh, wo, lg, lb, wr, br, bsz, seq):
    tm = min(TOK_TILE, seq)
    nt = seq // tm
    tok = bsz * seq
    n = bsz * nt
    cur = lambda i: (jnp.minimum(i, n - 1), 0)
    prev = lambda i: (jnp.maximum(i - 1, 0), 0)
    const = lambda i: (0, 0)
    mod = lambda i: (jnp.maximum(i - 1, 0) // nt, 0, 0)
    return pl.pallas_call(
        functools.partial(_mixer_kernel, blocks_per_seq=nt),
        grid=(n + 1,),
        in_specs=[
            pl.BlockSpec((tm, 4 * M_WIDTH), cur),
            pl.BlockSpec((GATE_ROWS, tm), lambda i: (0, jnp.minimum(i, n - 1))),
            pl.BlockSpec((1, M_WIDTH), const),
            pl.BlockSpec((tm, C_WIDTH), prev),
            pl.BlockSpec((tm, D_MODEL), prev),
            pl.BlockSpec((1, 1, D_MODEL), mod),
            pl.BlockSpec((1, 1, D_MODEL), mod),
            pl.BlockSpec((1, 1, D_MODEL), mod),
            pl.BlockSpec(wo.shape, const),
            pl.BlockSpec(lg.shape, const),
            pl.BlockSpec(lb.shape, const),
            pl.BlockSpec(wr.shape, const),
            pl.BlockSpec(br.shape, const),
        ],
        out_specs=[
            pl.BlockSpec((tm, D_MODEL), prev),
            pl.BlockSpec((tm * TOK_SUB, LANES), prev),
            pl.BlockSpec((8, tm), lambda i: (0, jnp.maximum(i - 1, 0))),
            pl.BlockSpec((ROUTE_ROWS, LANES), const),
        ],
        out_shape=[
            jax.ShapeDtypeStruct((tok, D_MODEL), F32),
            jax.ShapeDtypeStruct((tok * TOK_SUB, LANES), F32),
            jax.ShapeDtypeStruct((8, tok), F32),
            jax.ShapeDtypeStruct((ROUTE_ROWS, LANES), F32),
        ],
        scratch_shapes=[pltpu.VMEM((HEADS, D_HEAD, 2 * D_HEAD), F32),
                        pltpu.VMEM((8, CHUNK), F32),
                        pltpu.VMEM((LANES, HEADS * N_ROWQ * LANES), BF16),
                        pltpu.VMEM((tm, M_WIDTH), BF16),
                        pltpu.VMEM((ROUTE_ROWS, LANES), F32),
                        pltpu.VMEM((tm, tm), BF16)],
        compiler_params=_cparams(("arbitrary",)),
        name="mixer",
    )(qkvo, grow, nw, yc, x2, gt, sc, sh, wo, lg, lb, wr, br)


DISPATCH_TILE = 2048
DMA_UNROLL = 8


def _dispatch_kernel(pos_ref, pad_ref, u_ref, xs_ref, zeros, sem, zsem):
    ts = u_ref.shape[0] // TOK_SUB
    t0 = pl.program_id(0) * ts

    def zero_rows(first_row, n_rows_static):
        dst = pl.multiple_of(first_row * TOK_SUB, TOK_SUB)
        return pltpu.make_async_copy(zeros.at[pl.ds(0, n_rows_static * TOK_SUB), :],
                                     xs_ref.at[pl.ds(dst, n_rows_static * TOK_SUB), :], zsem)

    def fill_pads(wait):
        def bucket(b, carry):
            row = pad_ref[b]
            n = pad_ref[N_BUCKETS + b]
            size = MOE_STEP // 2
            while size >= 1:
                @pl.when((n & size) != 0)
                def _(row=row, size=size):
                    cp = zero_rows(row, size)
                    cp.wait() if wait else cp.start()
                row = row + (n & size)
                size //= 2
            return carry
        lax.fori_loop(0, N_BUCKETS, bucket, 0)

        def idle_tile(j, carry):
            cp = zero_rows(j * MOE_TILE, MOE_TILE)
            cp.wait() if wait else cp.start()
            return carry
        lax.fori_loop(pad_ref[2 * N_BUCKETS], xs_ref.shape[0] // (MOE_TILE * TOK_SUB), idle_tile, 0)

    @pl.when(pl.program_id(0) == 0)
    def _():
        zeros[...] = jnp.zeros(zeros.shape, F32)
        fill_pads(wait=False)
        fill_pads(wait=True)

    def issue(g, carry):
        for u in range(DMA_UNROLL):
            r = g * DMA_UNROLL + u
            dst = pl.multiple_of(pos_ref[t0 + r] * TOK_SUB, TOK_SUB)
            src = pl.multiple_of(r * TOK_SUB, TOK_SUB)
            pltpu.make_async_copy(u_ref.at[pl.ds(src, TOK_SUB), :],
                                  xs_ref.at[pl.ds(dst, TOK_SUB), :], sem).start(priority=u % 2)
        return carry

    lax.fori_loop(0, ts // DMA_UNROLL, issue, 0)
    pltpu.make_async_copy(u_ref, xs_ref.at[pl.ds(0, ts * TOK_SUB), :], sem).wait()


def _dispatch(pos, pad_start, u2, n_rows):
    tok = u2.shape[0] // TOK_SUB
    ts = min(DISPATCH_TILE, tok)
    return pl.pallas_call(
        _dispatch_kernel,
        grid_spec=pltpu.PrefetchScalarGridSpec(
            num_scalar_prefetch=2,
            grid=(tok // ts,),
            in_specs=[pl.BlockSpec((ts * TOK_SUB, LANES), lambda i, pos, pad: (i, 0))],
            out_specs=pl.BlockSpec(memory_space=pl.ANY),
            scratch_shapes=[pltpu.VMEM((MOE_TILE * TOK_SUB, LANES), F32),
                            pltpu.SemaphoreType.DMA(()), pltpu.SemaphoreType.DMA(())],
        ),
        out_shape=jax.ShapeDtypeStruct((n_rows * TOK_SUB, LANES), F32),
        compiler_params=_cparams(("arbitrary",)),
        name="dispatch",
    )(pos, pad_start, u2)


def _moe_kernel(s0_ref, s1_ref, tiles_ref, fresh0_ref, fresh1_ref, tb_ref, xb_ref, xs_ref, wsel_ref,
                bsel_ref, wg0, wu0, wd0, wg1, wu1, wd1, y_ref, wgs, wus, wds):
    del s0_ref, s1_ref, tb_ref, xb_ref
    j = pl.program_id(0)
    rows = MOE_TILE

    @pl.when(fresh0_ref[j] == 1)
    def _():
        wgs[0] = wg0[0].astype(BF16)
        wus[0] = wu0[0].astype(BF16)
        wds[0] = wd0[0].astype(BF16)

    @pl.when(fresh1_ref[j] == 1)
    def _():
        wgs[1] = wg1[0].astype(BF16)
        wus[1] = wu1[0].astype(BF16)
        wds[1] = wd1[0].astype(BF16)

    def experts(part):
        x = _load_token_major(xs_ref, rows, part * rows).astype(BF16)
        lg = jnp.dot(x, wsel_ref[0], preferred_element_type=F32) + bsel_ref[0]
        grp = [lg[:, i:i + 1] for i in range(N_GROUPS)]
        gmax = jnp.maximum(jnp.maximum(grp[0], grp[1]), jnp.maximum(grp[2], grp[3]))
        gsum = sum(jnp.exp(v - gmax) for v in grp)
        la, lb = lg[:, N_GROUPS:N_GROUPS + 1], lg[:, N_GROUPS + 1:N_GROUPS + 2]
        emax = jnp.maximum(la, lb)
        ea, eb = jnp.exp(la - emax), jnp.exp(lb - emax)
        scale = 1.0 / (gsum * (ea + eb))
        wcols = (ea * scale, eb * scale)
        acc = None
        for slot in range(2):
            g = jnp.dot(x, wgs[slot], preferred_element_type=F32)
            u = jnp.dot(x, wus[slot], preferred_element_type=F32)
            hid = (g * jax.nn.sigmoid(g) * u * wcols[slot]).astype(BF16)
            y = jnp.dot(hid, wds[slot], preferred_element_type=F32)
            acc = y if acc is None else acc + y
        _store_token_major(y_ref, acc, part * rows)

    for part in range(MOE_STEP // MOE_TILE):
        @pl.when(tiles_ref[j] > part)
        def _(part=part):
            experts(part)

        @pl.when(tiles_ref[j] <= part)
        def _(part=part):
            lo = part * rows * TOK_SUB
            y_ref[lo:lo + rows * TOK_SUB, :] = jnp.zeros((rows * TOK_SUB, LANES), F32)


def _moe(plan, xs, wsel, bsel, w_gate, w_up, w_down, n_tiles):
    wsel1 = lambda j, s0, s1, v, f0, f1, tb, xb: (s0[j], 0, 0)
    wsel2 = lambda j, s0, s1, v, f0, f1, tb, xb: (s1[j], 0, 0)
    by_bucket = lambda j, s0, s1, v, f0, f1, tb, xb: (tb[j], 0, 0)
    up_spec = lambda sel: pl.BlockSpec((1, D_MODEL, D_EXPERT), sel)
    dn_spec = lambda sel: pl.BlockSpec((1, D_EXPERT, D_MODEL), sel)
    tile_rows = MOE_STEP * TOK_SUB
    return pl.pallas_call(
        _moe_kernel,
        grid_spec=pltpu.PrefetchScalarGridSpec(
            num_scalar_prefetch=7,
            grid=(n_tiles,),
            in_specs=[pl.BlockSpec((tile_rows, LANES), lambda j, s0, s1, v, f0, f1, tb, xb: (xb[j], 0)),
                      pl.BlockSpec((1, D_MODEL, LANES), by_bucket),
                      pl.BlockSpec((1, 1, LANES), by_bucket),
                      up_spec(wsel1), up_spec(wsel1), dn_spec(wsel1),
                      up_spec(wsel2), up_spec(wsel2), dn_spec(wsel2)],
            out_specs=pl.BlockSpec((tile_rows, LANES), lambda j, *_: (j, 0)),
            scratch_shapes=[pltpu.VMEM((2, D_MODEL, D_EXPERT), BF16),
                            pltpu.VMEM((2, D_MODEL, D_EXPERT), BF16),
                            pltpu.VMEM((2, D_EXPERT, D_MODEL), BF16)],
        ),
        out_shape=jax.ShapeDtypeStruct((n_tiles * tile_rows, LANES), F32),
        compiler_params=_cparams(("arbitrary",)),
        name="moe",
    )(*plan, xs, wsel, bsel, w_gate, w_up, w_down, w_gate, w_up, w_down)


def _combine_kernel(pos_ref, ys_ref, x1_ref, gt_ref, lg_ref, lb_ref, out_ref, ybuf, sems):
    ts = x1_ref.shape[0]
    i = pl.program_id(0)
    n = pl.num_programs(0)
    slot = i % 2

    def gather(tile, to_slot):
        def issue(g, carry):
            for u in range(DMA_UNROLL):
                r = g * DMA_UNROLL + u
                src = pl.multiple_of(pos_ref[tile * ts + r] * TOK_SUB, TOK_SUB)
                dst = pl.multiple_of(r * TOK_SUB, TOK_SUB)
                pltpu.make_async_copy(ys_ref.at[pl.ds(src, TOK_SUB), :],
                                      ybuf.at[to_slot, pl.ds(dst, TOK_SUB), :],
                                      sems.at[to_slot]).start(priority=u % 2)
            return carry
        lax.fori_loop(0, ts // DMA_UNROLL, issue, 0)

    @pl.when(i == 0)
    def _():
        gather(0, 0)

    @pl.when(i + 1 < n)
    def _():
        gather(i + 1, 1 - slot)

    pltpu.make_async_copy(ys_ref.at[pl.ds(0, ts * TOK_SUB), :], ybuf.at[slot], sems.at[slot]).wait()
    y = _load_token_major(ybuf.at[slot], ts)
    out_ref[...] = _layer_norm(ALPHA * x1_ref[...] + (1.0 + gt_ref[0]) * y, lg_ref[...], lb_ref[...])


def _combine(pos, ys, x1, gt, lg, lb, bsz, seq):
    ts = min(TOK_TILE, seq)
    nt = seq // ts
    tok = bsz * seq
    return pl.pallas_call(
        _combine_kernel,
        grid_spec=pltpu.PrefetchScalarGridSpec(
            num_scalar_prefetch=1,
            grid=(tok // ts,),
            in_specs=[pl.BlockSpec(memory_space=pl.ANY),
                      pl.BlockSpec((ts, D_MODEL), lambda i, pos: (i, 0)),
                      pl.BlockSpec((1, 1, D_MODEL), lambda i, pos: (i // nt, 0, 0)),
                      pl.BlockSpec((1, D_MODEL), lambda i, pos: (0, 0)),
                      pl.BlockSpec((1, D_MODEL), lambda i, pos: (0, 0))],
            out_specs=pl.BlockSpec((ts, D_MODEL), lambda i, pos: (i, 0)),
            scratch_shapes=[pltpu.VMEM((2, ts * TOK_SUB, LANES), F32), pltpu.SemaphoreType.DMA((2,))],
        ),
        out_shape=jax.ShapeDtypeStruct((tok, D_MODEL), F32),
        compiler_params=_cparams(("arbitrary",)),
        name="combine",
    )(pos, ys, x1, gt, lg, lb)


def _route_plan(route, cnt, n_tiles, layer):
    bucket = route[0].astype(I32)
    rank = route[1].astype(I32)
    counts = cnt[:N_BUCKETS, 0].astype(I32)
    tiles_b = (counts + MOE_STEP - 1) // MOE_STEP
    tile_end = jnp.cumsum(tiles_b)
    row_off = (tile_end - tiles_b) * MOE_STEP
    bucket_ids = jnp.arange(N_BUCKETS, dtype=I32)
    pos = jnp.sum(jnp.where(bucket[None, :] == bucket_ids[:, None], row_off[:, None], 0), axis=0) + rank
    total = tile_end[-1]
    j = jnp.arange(n_tiles, dtype=I32)
    tb = jnp.sum((tile_end[None, :] <= jnp.minimum(j, total - 1)[:, None]).astype(I32), axis=1)
    tb = jnp.clip(tb, 0, N_BUCKETS - 1)
    hit = tb[None, :] == bucket_ids[:, None]
    rows_left = jnp.sum(jnp.where(hit, (row_off + counts)[:, None], 0), axis=0) - j * MOE_STEP
    tiles_j = jnp.clip((rows_left + MOE_TILE - 1) // MOE_TILE, 0, MOE_STEP // MOE_TILE)
    tiles_j = jnp.where(j < total, tiles_j, 0)
    grp = tb // N_PAIRS
    pair = tb % N_PAIRS
    first = layer * (N_GROUPS * N_EXP) + grp * N_EXP
    s0 = first + sum((pair == i).astype(I32) * _SLOT0[i] for i in range(N_PAIRS))
    s1 = first + sum((pair == i).astype(I32) * _SLOT1[i] for i in range(N_PAIRS))
    one = jnp.ones((1,), I32)
    fresh0 = jnp.concatenate([one, (s0[1:] != s0[:-1]).astype(I32)])
    fresh1 = jnp.concatenate([one, (s1[1:] != s1[:-1]).astype(I32)])
    xblk = jnp.minimum(j, total - 1)
    pads = jnp.concatenate([row_off + counts, tiles_b * MOE_STEP - counts,
                            (total * (MOE_STEP // MOE_TILE))[None]])
    return pos, pads, (s0, s1, tiles_j, fresh0, fresh1, tb, xblk)


def kernel(x, c, w_ada, b_ada, w_in, b_gates, mh_norm_w, w_conv, w_out, ln1_g, ln1_b,
           w_grp, b_grp, w_router, b_router, w_gate, w_up, w_down, ln2_g, ln2_b):
    bsz, seq, _ = x.shape
    tok = bsz * seq
    n_rows = tok + N_BUCKETS * MOE_STEP
    n_tiles = n_rows // MOE_STEP
    n_exp_total = N_GROUPS * N_EXP

    wg_all = w_gate.reshape(DEPTH * n_exp_total, D_MODEL, D_EXPERT)
    wu_all = w_up.reshape(DEPTH * n_exp_total, D_MODEL, D_EXPERT)
    wd_all = w_down.reshape(DEPTH * n_exp_total, D_EXPERT, D_MODEL)
    mod = _ada_mod(c, w_ada, b_ada).reshape(DEPTH, bsz, 6, 1, D_MODEL)
    x2 = x.reshape(tok, D_MODEL)
    for l in range(DEPTH):
        sh1, sc1, gt1, sh2, sc2, gt2 = [mod[l, :, i] for i in range(6)]
        wl = w_in[l]
        wq = wl[:, :4 * M_WIDTH].astype(BF16)
        wg = jnp.pad(wl[:, 4 * M_WIDTH:4 * M_WIDTH + 2 * HEADS],
                     ((0, 0), (0, LANES - 2 * HEADS))).astype(BF16)
        wc = wl[:, 4 * M_WIDTH + 2 * HEADS:].astype(BF16)
        bg = jnp.pad(b_gates[l], (0, LANES - 2 * HEADS)).reshape(1, LANES)
        qkvo, grow, yc = _inproj(x2, sc1, sh1, wq, wg, wc, bg, w_conv[l], bsz, seq)
        n_logit = N_GROUPS + n_exp_total
        wr = jnp.pad(jnp.concatenate([w_grp[l], w_router[l]], axis=1).T,
                     ((0, ROUTE_ROWS - n_logit), (0, 0))).astype(BF16)
        br = jnp.pad(jnp.concatenate([b_grp[l], b_router[l]]), (0, ROUTE_ROWS - n_logit))
        br = jnp.broadcast_to(br[:, None], (ROUTE_ROWS, LANES))
        x1, u2, route, cnt = _mixer(qkvo, grow, mh_norm_w[l].reshape(1, M_WIDTH), yc, x2, gt1, sc2, sh2,
                                    w_out[l].astype(BF16), ln1_g[l].reshape(1, D_MODEL),
                                    ln1_b[l].reshape(1, D_MODEL), wr, br, bsz, seq)
        pos, pads, plan = _route_plan(route, cnt, n_tiles, l)
        xs = _dispatch(pos, pads, u2, n_rows)
        wsel = jnp.concatenate([jnp.broadcast_to(w_grp[l], (N_BUCKETS, D_MODEL, N_GROUPS)),
                                w_router[l][:, _BUCKET_S0].T[:, :, None],
                                w_router[l][:, _BUCKET_S1].T[:, :, None]], axis=2)
        wsel = jnp.pad(wsel, ((0, 0), (0, 0), (0, LANES - N_GROUPS - 2))).astype(BF16)
        bsel = jnp.concatenate([jnp.broadcast_to(b_grp[l], (N_BUCKETS, N_GROUPS)),
                                b_router[l][_BUCKET_S0][:, None], b_router[l][_BUCKET_S1][:, None]], axis=1)
        bsel = jnp.pad(bsel, ((0, 0), (0, LANES - N_GROUPS - 2))).reshape(N_BUCKETS, 1, LANES)
        ys = _moe(plan, xs, wsel, bsel, wg_all, wu_all, wd_all, n_tiles)
        x2 = _combine(pos, ys, x1, gt2, ln2_g[l].reshape(1, D_MODEL),
                      ln2_b[l].reshape(1, D_MODEL), bsz, seq)
    return x2.reshape(bsz, seq, D_MODEL)
```

```python
import functools
import math

import jax
import jax.numpy as jnp
import numpy as np
from jax import lax
from jax.experimental import pallas as pl
from jax.experimental.pallas import tpu as pltpu

F32 = jnp.float32
BF16 = jnp.bfloat16
I32 = jnp.int32

D_MODEL = 1024
DEPTH = 2
HEADS = 4
D_HEAD = 128
M_WIDTH = HEADS * D_HEAD
C_WIDTH = D_MODEL - M_WIDTH
N_GROUPS = 4
N_EXP = 4
D_EXPERT = 512
N_PAIRS = 6
N_BUCKETS = N_GROUPS * N_PAIRS
ALPHA = (2 * DEPTH) ** 0.25
LN_EPS = 1e-5
QK_SCALE = D_HEAD ** -0.5
LOG_QK_SCALE = math.log(QK_SCALE)
GATE_ROWS = 24

LANES = 128
ROUTE_ROWS = 32

TOK_TILE = 512
INPROJ_TILE = 1024
MIXER_TILE = 1024
CHUNK = 256
MOE_TILE = 256
MOE_STEP = 2 * MOE_TILE
VMEM_LIMIT = 56 * 1024 * 1024

_PAIR_ORDER = (0, 2, 3, 1, 4, 5)
_SLOT0 = (0, 2, 2, 3, 3, 3)
_SLOT1 = (1, 1, 0, 0, 1, 2)
_BUCKET_S0 = np.array([g * N_EXP + _SLOT0[p] for g in range(N_GROUPS) for p in range(N_PAIRS)])
_BUCKET_S1 = np.array([g * N_EXP + _SLOT1[p] for g in range(N_GROUPS) for p in range(N_PAIRS)])


def _cparams(sem):
    return pltpu.CompilerParams(dimension_semantics=sem, vmem_limit_bytes=VMEM_LIMIT)


def _ada_kernel(c_ref, w_ref, b_ref, o_ref):
    c = c_ref[...]
    cond = (c * jax.nn.sigmoid(c)).astype(BF16)
    o_ref[0] = jnp.dot(cond, w_ref[0].astype(BF16), preferred_element_type=F32) + b_ref[0]


def _ada_mod(c, w_ada, b_ada):
    bsz = c.shape[0]
    nblk = w_ada.shape[2] // D_MODEL
    return pl.pallas_call(
        _ada_kernel,
        grid=(DEPTH, nblk),
        in_specs=[
            pl.BlockSpec((bsz, D_MODEL), lambda l, j: (0, 0)),
            pl.BlockSpec((1, D_MODEL, D_MODEL), lambda l, j: (l, 0, j)),
            pl.BlockSpec((1, 1, D_MODEL), lambda l, j: (l, 0, j)),
        ],
        out_specs=pl.BlockSpec((1, bsz, D_MODEL), lambda l, j: (l, 0, j)),
        out_shape=jax.ShapeDtypeStruct((DEPTH, bsz, nblk * D_MODEL), F32),
        compiler_params=_cparams(("arbitrary", "arbitrary")),
        name="ada_mod",
    )(c, w_ada, b_ada.reshape(DEPTH, 1, -1))


def _log_sigmoid(x):
    return jnp.minimum(x, 0.0) - jnp.log1p(jnp.exp(-jnp.abs(x)))


def _chunk_scan(x, op, identity):
    lane = lax.broadcasted_iota(I32, x.shape, 1) & (CHUNK - 1)
    sh = 1
    while sh < CHUNK:
        x = op(x, jnp.where(lane >= sh, pltpu.roll(x, sh, 1), identity))
        sh *= 2
    return x


def _inproj_kernel(x_ref, sc_ref, sh_ref, wq_ref, wg_ref, wc_ref, bg_ref, wconv_ref,
                   qkvo_ref, grow_ref, yc_ref, zbuf):
    tm = x_ref.shape[0]
    u = (x_ref[...] * (1.0 + sc_ref[0]) + sh_ref[0]).astype(BF16)
    qkvo_ref[...] = jnp.dot(u, wq_ref[...], preferred_element_type=F32).astype(BF16)
    g = jnp.dot(u, wg_ref[...], preferred_element_type=F32) + bg_ref[...]
    gt = g.T[:8]
    head_row = lax.broadcasted_iota(I32, gt.shape, 0) < HEADS
    ig = jnp.where(head_row, gt, 0.0)
    logf = jnp.where(head_row, _log_sigmoid(pltpu.roll(gt, HEADS, 0)), 0.0)
    bcum = _chunk_scan(logf, jnp.add, 0.0)
    ug = ig - bcum
    grow_ref[0:8, :] = ug
    grow_ref[8:16, :] = bcum
    grow_ref[16:24, :] = _chunk_scan(ug, jnp.maximum, -jnp.inf)
    pc = jnp.dot(u, wc_ref[...], preferred_element_type=F32)
    z = pc[:, C_WIDTH:2 * C_WIDTH] * pc[:, 2 * C_WIDTH:]

    @pl.when(pl.program_id(1) == 0)
    def _():
        zbuf[0:8, :] = jnp.zeros((8, C_WIDTH), F32)

    zbuf[8:8 + tm, :] = z
    zc = (wconv_ref[0:1, :] * zbuf[6:6 + tm, :] + wconv_ref[1:2, :] * zbuf[7:7 + tm, :]
          + wconv_ref[2:3, :] * z)
    yc_ref[...] = (pc[:, :C_WIDTH] * zc).astype(BF16)
    zbuf[0:8, :] = zbuf[tm:tm + 8, :]


def _inproj(x2, sc, sh, wq, wg, wc, bg, wconv, bsz, seq):
    tm = min(INPROJ_TILE, seq)
    nt = seq // tm
    tok = bsz * seq
    row = lambda b, s: (b * nt + s, 0)
    const = lambda b, s: (0, 0)
    mod = lambda b, s: (b, 0, 0)
    return pl.pallas_call(
        _inproj_kernel,
        grid=(bsz, nt),
        in_specs=[
            pl.BlockSpec((tm, D_MODEL), row),
            pl.BlockSpec((1, 1, D_MODEL), mod),
            pl.BlockSpec((1, 1, D_MODEL), mod),
            pl.BlockSpec(wq.shape, const),
            pl.BlockSpec(wg.shape, const),
            pl.BlockSpec(wc.shape, const),
            pl.BlockSpec(bg.shape, const),
            pl.BlockSpec(wconv.shape, const),
        ],
        out_specs=[
            pl.BlockSpec((tm, 4 * M_WIDTH), row),
            pl.BlockSpec((GATE_ROWS, tm), lambda b, s: (0, b * nt + s)),
            pl.BlockSpec((tm, C_WIDTH), row),
        ],
        out_shape=[
            jax.ShapeDtypeStruct((tok, 4 * M_WIDTH), BF16),
            jax.ShapeDtypeStruct((GATE_ROWS, tok), F32),
            jax.ShapeDtypeStruct((tok, C_WIDTH), BF16),
        ],
        scratch_shapes=[pltpu.VMEM((tm + 8, C_WIDTH), F32)],
        compiler_params=_cparams(("arbitrary", "arbitrary")),
        name="inproj",
    )(x2, sc, sh, wq, wg, wc, bg, wconv)


N_ROWQ = 4
SPLIT = 3


def _split3(x):
    hi = x.astype(BF16).astype(F32)
    r1 = x - hi
    mid = r1.astype(BF16).astype(F32)
    return [hi, mid, r1 - mid]


def _mlstm_selector(sel):
    r = lax.broadcasted_iota(I32, sel.shape, 0)
    c = lax.broadcasted_iota(I32, sel.shape, 1) // LANES
    rq = r // 8
    quantity = ((rq >= SPLIT).astype(I32) + (rq >= 2 * SPLIT).astype(I32)
                + (rq >= 3 * SPLIT).astype(I32))
    hit = ((r % 8) == c // N_ROWQ) & (quantity == c % N_ROWQ) & (rq < N_ROWQ * SPLIT)
    sel[...] = hit.astype(BF16)


def _mlstm_block(qkvo_ref, grow_ref, nw_ref, out_ref, cst, mst, sel, between=()):
    between = tuple(between) + (None,) * 3
    sb = qkvo_ref.shape[0]
    L = CHUNK
    causal = lax.broadcasted_iota(I32, (L, L), 0) >= lax.broadcasted_iota(I32, (L, L), 1)
    ones_ext = jnp.ones((L, D_HEAD), BF16)
    pad_rows = jnp.zeros((LANES - 8 * N_ROWQ * SPLIT, L), F32)
    eye = (lax.broadcasted_iota(I32, (D_HEAD, D_HEAD), 0)
           == lax.broadcasted_iota(I32, (D_HEAD, D_HEAD), 1)).astype(BF16)
    n_chunks = sb // L
    pairs = [(c, h) for c in range(n_chunks) for h in range(HEADS)]

    def cols(part, c, h):
        c0 = part * M_WIDTH + h * D_HEAD
        return qkvo_ref[c * L:(c + 1) * L, c0:c0 + D_HEAD]

    m_prev = mst[...]
    stack_t, us, decay = [], [], []
    for c in range(n_chunks):
        ug = grow_ref[0:8, c * L:(c + 1) * L]
        bcum = grow_ref[8:16, c * L:(c + 1) * L]
        cmax = grow_ref[16:24, c * L:(c + 1) * L]
        big_m = jnp.maximum(m_prev, cmax)
        m_last = jnp.broadcast_to(big_m[:, L - 1:L], (8, L))
        g_tot = jnp.broadcast_to(bcum[:, L - 1:L], (8, L))
        wq = jnp.exp(m_prev - big_m) * QK_SCALE
        log_em = -(bcum + big_m)
        wk = jnp.exp(ug - m_last)
        decay.append(jnp.exp(m_prev - m_last))
        us.append(ug + LOG_QK_SCALE)
        m_prev = g_tot + m_last
        stack = jnp.concatenate(_split3(big_m) + _split3(wq) + _split3(log_em) + _split3(wk)
                                + [pad_rows], axis=0)
        stack_t.append(stack.T.astype(BF16))
    mst[...] = m_prev
    if between[0] is not None:
        between[0]()

    reps = {}

    def rep(c, h, j):
        if (c, h) not in reps:
            c0 = h * N_ROWQ * LANES
            reps[(c, h)] = jnp.dot(stack_t[c], sel[:, c0:c0 + N_ROWQ * LANES],
                                   preferred_element_type=F32)
        return reps[(c, h)][:, j * LANES:(j + 1) * LANES]

    s_mat = {ch: lax.dot_general(cols(0, *ch), cols(1, *ch), (((1,), (1,)), ((), ())),
                                 preferred_element_type=F32) for ch in pairs}
    kw = {ch: (cols(1, *ch).astype(F32) * rep(*ch, 3)).astype(BF16) for ch in pairs}
    kw_t = {ch: lax.dot_general(eye, kw[ch], (((1,), (1,)), ((), ())),
                                preferred_element_type=F32).astype(BF16) for ch in pairs}
    if between[1] is not None:
        between[1]()
    lhs = {}
    for ch in pairs:
        c, h = ch
        big_m_rows = jnp.concatenate([rep(c, h, 0)] * (L // LANES), axis=1)
        p = jnp.where(causal, s_mat[ch] * jnp.exp(us[c][h:h + 1, :] - big_m_rows), 0.0)
        qw = cols(0, c, h).astype(F32) * rep(c, h, 1)
        lhs[ch] = jnp.concatenate([p.astype(BF16), qw.astype(BF16)], axis=1)
    vext = {ch: jnp.concatenate([cols(2, *ch), ones_ext], axis=1) for ch in pairs}
    upd = {ch: jnp.dot(kw_t[ch], vext[ch], preferred_element_type=F32) for ch in pairs}
    for c in range(n_chunks):
        tot = {}
        for h in range(HEADS):
            state = cst[h]
            rhs = jnp.concatenate([vext[(c, h)], state.astype(BF16)], axis=0)
            tot[h] = jnp.dot(lhs[(c, h)], rhs, preferred_element_type=F32)
            dec = jnp.concatenate([decay[c][h:h + 1, :LANES]] * 2, axis=1)
            cst[h] = dec * state + upd[(c, h)]
        for h in range(HEADS):
            c0 = h * D_HEAD
            num, den = tot[h][:, :D_HEAD], tot[h][:, D_HEAD:]
            hh = num / jnp.maximum(jnp.abs(den), jnp.exp(rep(c, h, 2)))
            mu = jnp.mean(hh, axis=1, keepdims=True)
            var = jnp.mean(jnp.square(hh - mu), axis=1, keepdims=True)
            hn = (hh - mu) * lax.rsqrt(var + LN_EPS) * nw_ref[:, c0:c0 + D_HEAD]
            og = jax.nn.sigmoid(cols(3, c, h).astype(F32))
            out_ref[c * L:(c + 1) * L, c0:c0 + D_HEAD] = (hn * og).astype(BF16)
        if c == 0 and between[2] is not None:
            between[2]()


def _layer_norm(r, g, b):
    mu = jnp.mean(r, axis=-1, keepdims=True)
    var = jnp.mean(jnp.square(r - mu), axis=-1, keepdims=True)
    return (r - mu) * lax.rsqrt(var + LN_EPS) * g + b


TOK_SUB = D_MODEL // LANES


def _store_token_major(ref, val, first_row=0):
    rows = val.shape[0]
    for k in range(TOK_SUB):
        ref[pl.ds(first_row * TOK_SUB + k, rows, stride=TOK_SUB), :] = val[:, k * LANES:(k + 1) * LANES]


def _load_token_major(ref, rows, first_row=0):
    return jnp.concatenate([ref[pl.ds(first_row * TOK_SUB + k, rows, stride=TOK_SUB), :]
                            for k in range(TOK_SUB)], axis=1)


def _outproj_parts(hn_ref, yc_ref, x_ref, gt_ref, sc_ref, sh_ref, wo_ref, lg_ref, lb_ref,
                   wr_ref, br_ref, x1_ref, u2_ref, route_ref, cnt_ref, base, triu, live):
    tm = x_ref.shape[0]
    held = {}

    def project():
        held["mix"] = (jnp.dot(hn_ref[...], wo_ref[0:M_WIDTH, :], preferred_element_type=F32)
                       + jnp.dot(yc_ref[...], wo_ref[M_WIDTH:, :], preferred_element_type=F32))

    def normalise():
        x1 = _layer_norm(ALPHA * x_ref[...] + (1.0 + gt_ref[0]) * held["mix"], lg_ref[...], lb_ref[...])
        x1_ref[...] = x1
        u2 = x1 * (1.0 + sc_ref[0]) + sh_ref[0]
        _store_token_major(u2_ref, u2)
        held["logits"] = lax.dot_general(wr_ref[...], u2.astype(BF16), (((1,), (1,)), ((), ())),
                                         preferred_element_type=F32) + br_ref[:, 0:1]

    def route():
        _route_block(held["logits"], route_ref, cnt_ref, base, triu, live, tm)

    return project, normalise, route


def _route_block(lt, route_ref, cnt_ref, base, triu, live, tm):
    lg = [lt[j:j + 1, :] for j in range(N_GROUPS + N_GROUPS * N_EXP)]
    best = lg[0]
    grp = jnp.zeros((1, tm), I32)
    for j in range(1, N_GROUPS):
        c = lg[j] > best
        grp = jnp.where(c, j, grp)
        best = jnp.where(c, lg[j], best)
    sel = []
    for e in range(N_EXP):
        val = lg[N_GROUPS + e]
        for gg in range(1, N_GROUPS):
            val = jnp.where(grp == gg, lg[N_GROUPS + gg * N_EXP + e], val)
        sel.append(val)
    v1 = sel[0]
    i1 = jnp.zeros((1, tm), I32)
    for e in range(1, N_EXP):
        c = sel[e] > v1
        i1 = jnp.where(c, e, i1)
        v1 = jnp.where(c, sel[e], v1)
    v2 = jnp.full((1, tm), -jnp.inf, F32)
    i2 = jnp.zeros((1, tm), I32)
    for e in range(N_EXP):
        cand = jnp.where(i1 == e, -jnp.inf, sel[e])
        c = cand > v2
        i2 = jnp.where(c, e, i2)
        v2 = jnp.where(c, cand, v2)
    ea = jnp.minimum(i1, i2)
    eb = jnp.maximum(i1, i2)
    lex = jnp.where(ea == 0, 0, jnp.where(ea == 1, 3, 5)) + eb - ea - 1
    pair = lex
    for i, p in enumerate(_PAIR_ORDER):
        if i != p:
            pair = jnp.where(lex == i, p, pair)
    bucket = grp * N_PAIRS + pair

    onehot = lax.broadcasted_iota(I32, (ROUTE_ROWS, tm), 0) == bucket
    cum = jnp.dot(onehot.astype(BF16), triu[...], preferred_element_type=F32)
    prev = base[:, 0:1]
    rank = jnp.sum(jnp.where(onehot, cum - 1.0 + prev, 0.0), axis=0, keepdims=True)
    new_base = prev + live * cum[:, tm - 1:tm]
    base[...] = jnp.broadcast_to(new_base, base.shape)
    cnt_ref[...] = jnp.broadcast_to(new_base, cnt_ref.shape)

    zrow = jnp.zeros((1, tm), F32)
    route_ref[...] = jnp.concatenate(
        [bucket.astype(F32), rank, zrow, zrow, zrow, zrow, zrow, zrow], axis=0)


def _mixer_kernel(qkvo_ref, grow_ref, nw_ref, yc_ref, x_ref, gt_ref, sc_ref, sh_ref, wo_ref, lg_ref,
                  lb_ref, wr_ref, br_ref, x1_ref, u2_ref, route_ref, cnt_ref,
                  cst, mst, sel, hn, base, triu, *, blocks_per_seq):
    i = pl.program_id(0)
    tm = x_ref.shape[0]

    @pl.when(i == 0)
    def _():
        _mlstm_selector(sel)
        hn[...] = jnp.zeros(hn.shape, BF16)
        base[...] = jnp.zeros(base.shape, F32)
        rid = lax.broadcasted_iota(I32, triu.shape, 0)
        cid = lax.broadcasted_iota(I32, triu.shape, 1)
        triu[...] = (rid <= cid).astype(BF16)

    @pl.when(i % blocks_per_seq == 0)
    def _():
        cst[...] = jnp.zeros(cst.shape, F32)
        mst[...] = jnp.zeros(mst.shape, F32)

    live = jnp.where(i > 0, 1.0, 0.0).astype(F32)
    parts = _outproj_parts(hn, yc_ref, x_ref, gt_ref, sc_ref, sh_ref, wo_ref, lg_ref, lb_ref,
                           wr_ref, br_ref, x1_ref, u2_ref, route_ref, cnt_ref, base, triu, live)
    _mlstm_block(qkvo_ref, grow_ref, nw_ref, hn, cst, mst, sel, between=parts)


def _mixer(qkvo, grow, nw, yc, x2, gt, sc, sh, wo, lg, lb, wr, br, bsz, seq):
    tm = min(MIXER_TILE, seq)
    nt = seq // tm
    tok = bsz * seq
    n = bsz * nt
    cur = lambda i: (jnp.minimum(i, n - 1), 0)
    prev = lambda i: (jnp.maximum(i - 1, 0), 0)
    const = lambda i: (0, 0)
    mod = lambda i: (jnp.maximum(i - 1, 0) // nt, 0, 0)
    return pl.pallas_call(
        functools.partial(_mixer_kernel, blocks_per_seq=nt),
        grid=(n + 1,),
        in_specs=[
            pl.BlockSpec((tm, 4 * M_WIDTH), cur),
            pl.BlockSpec((GATE_ROWS, tm), lambda i: (0, jnp.minimum(i, n - 1))),
            pl.BlockSpec((1, M_WIDTH), const),
            pl.BlockSpec((tm, C_WIDTH), prev),
            pl.BlockSpec((tm, D_MODEL), prev),
            pl.BlockSpec((1, 1, D_MODEL), mod),
            pl.BlockSpec((1, 1, D_MODEL), mod),
            pl.BlockSpec((1, 1, D_MODEL), mod),
            pl.BlockSpec(wo.shape, const),
            pl.BlockSpec(lg.shape, const),
            pl.BlockSpec(lb.shape, const),
            pl.BlockSpec(wr.shape, const),
            pl.BlockSpec(br.shape, const),
        ],
        out_specs=[
            pl.BlockSpec((tm, D_MODEL), prev),
            pl.BlockSpec((tm * TOK_SUB, LANES), prev),
            pl.BlockSpec((8, tm), lambda i: (0, jnp.maximum(i - 1, 0))),
            pl.BlockSpec((ROUTE_ROWS, LANES), const),
        ],
        out_shape=[
            jax.ShapeDtypeStruct((tok, D_MODEL), F32),
            jax.ShapeDtypeStruct((tok * TOK_SUB, LANES), F32),
            jax.ShapeDtypeStruct((8, tok), F32),
            jax.ShapeDtypeStruct((ROUTE_ROWS, LANES), F32),
        ],
        scratch_shapes=[pltpu.VMEM((HEADS, D_HEAD, 2 * D_HEAD), F32),
                        pltpu.VMEM((8, CHUNK), F32),
                        pltpu.VMEM((LANES, HEADS * N_ROWQ * LANES), BF16),
                        pltpu.VMEM((tm, M_WIDTH), BF16),
                        pltpu.VMEM((ROUTE_ROWS, LANES), F32),
                        pltpu.VMEM((tm, tm), BF16)],
        compiler_params=_cparams(("arbitrary",)),
        name="mixer",
    )(qkvo, grow, nw, yc, x2, gt, sc, sh, wo, lg, lb, wr, br)


DISPATCH_TILE = 2048
DMA_UNROLL = 8


def _dispatch_kernel(pos_ref, pad_ref, u_ref, xs_ref, zeros, sem, zsem):
    ts = u_ref.shape[0] // TOK_SUB
    t0 = pl.program_id(0) * ts

    def zero_rows(first_row, n_rows_static):
        dst = pl.multiple_of(first_row * TOK_SUB, TOK_SUB)
        return pltpu.make_async_copy(zeros.at[pl.ds(0, n_rows_static * TOK_SUB), :],
                                     xs_ref.at[pl.ds(dst, n_rows_static * TOK_SUB), :], zsem)

    def fill_pads(wait):
        def bucket(b, carry):
            row = pad_ref[b]
            n = pad_ref[N_BUCKETS + b]
            size = MOE_STEP // 2
            while size >= 1:
                @pl.when((n & size) != 0)
                def _(row=row, size=size):
                    cp = zero_rows(row, size)
                    cp.wait() if wait else cp.start()
                row = row + (n & size)
                size //= 2
            return carry
        lax.fori_loop(0, N_BUCKETS, bucket, 0)

        def idle_tile(j, carry):
            cp = zero_rows(j * MOE_TILE, MOE_TILE)
            cp.wait() if wait else cp.start()
            return carry
        lax.fori_loop(pad_ref[2 * N_BUCKETS], xs_ref.shape[0] // (MOE_TILE * TOK_SUB), idle_tile, 0)

    @pl.when(pl.program_id(0) == 0)
    def _():
        zeros[...] = jnp.zeros(zeros.shape, F32)
        fill_pads(wait=False)
        fill_pads(wait=True)

    def issue(g, carry):
        for u in range(DMA_UNROLL):
            r = g * DMA_UNROLL + u
            dst = pl.multiple_of(pos_ref[t0 + r] * TOK_SUB, TOK_SUB)
            src = pl.multiple_of(r * TOK_SUB, TOK_SUB)
            pltpu.make_async_copy(u_ref.at[pl.ds(src, TOK_SUB), :],
                                  xs_ref.at[pl.ds(dst, TOK_SUB), :], sem).start(priority=u % 2)
        return carry

    lax.fori_loop(0, ts // DMA_UNROLL, issue, 0)
    pltpu.make_async_copy(u_ref, xs_ref.at[pl.ds(0, ts * TOK_SUB), :], sem).wait()


def _dispatch(pos, pad_start, u2, n_rows):
    tok = u2.shape[0] // TOK_SUB
    ts = min(DISPATCH_TILE, tok)
    return pl.pallas_call(
        _dispatch_kernel,
        grid_spec=pltpu.PrefetchScalarGridSpec(
            num_scalar_prefetch=2,
            grid=(tok // ts,),
            in_specs=[pl.BlockSpec((ts * TOK_SUB, LANES), lambda i, pos, pad: (i, 0))],
            out_specs=pl.BlockSpec(memory_space=pl.ANY),
            scratch_shapes=[pltpu.VMEM((MOE_TILE * TOK_SUB, LANES), F32),
                            pltpu.SemaphoreType.DMA(()), pltpu.SemaphoreType.DMA(())],
        ),
        out_shape=jax.ShapeDtypeStruct((n_rows * TOK_SUB, LANES), F32),
        compiler_params=_cparams(("arbitrary",)),
        name="dispatch",
    )(pos, pad_start, u2)


def _moe_kernel(s0_ref, s1_ref, tiles_ref, fresh0_ref, fresh1_ref, tb_ref, xb_ref, xs_ref, wsel_ref,
                bsel_ref, wg0, wu0, wd0, wg1, wu1, wd1, y_ref, wgs, wus, wds):
    del s0_ref, s1_ref, tb_ref, xb_ref
    j = pl.program_id(0)
    rows = MOE_TILE

    @pl.when(fresh0_ref[j] == 1)
    def _():
        wgs[0] = wg0[0].astype(BF16)
        wus[0] = wu0[0].astype(BF16)
        wds[0] = wd0[0].astype(BF16)

    @pl.when(fresh1_ref[j] == 1)
    def _():
        wgs[1] = wg1[0].astype(BF16)
        wus[1] = wu1[0].astype(BF16)
        wds[1] = wd1[0].astype(BF16)

    def experts(part):
        x = _load_token_major(xs_ref, rows, part * rows).astype(BF16)
        lg = jnp.dot(x, wsel_ref[0], preferred_element_type=F32) + bsel_ref[0]
        grp = [lg[:, i:i + 1] for i in range(N_GROUPS)]
        gmax = jnp.maximum(jnp.maximum(grp[0], grp[1]), jnp.maximum(grp[2], grp[3]))
        gsum = sum(jnp.exp(v - gmax) for v in grp)
        la, lb = lg[:, N_GROUPS:N_GROUPS + 1], lg[:, N_GROUPS + 1:N_GROUPS + 2]
        emax = jnp.maximum(la, lb)
        ea, eb = jnp.exp(la - emax), jnp.exp(lb - emax)
        scale = 1.0 / (gsum * (ea + eb))
        wcols = (ea * scale, eb * scale)
        acc = None
        for slot in range(2):
            g = jnp.dot(x, wgs[slot], preferred_element_type=F32)
            u = jnp.dot(x, wus[slot], preferred_element_type=F32)
            hid = (g * jax.nn.sigmoid(g) * u * wcols[slot]).astype(BF16)
            y = jnp.dot(hid, wds[slot], preferred_element_type=F32)
            acc = y if acc is None else acc + y
        _store_token_major(y_ref, acc, part * rows)

    for part in range(MOE_STEP // MOE_TILE):
        @pl.when(tiles_ref[j] > part)
        def _(part=part):
            experts(part)

        @pl.when(tiles_ref[j] <= part)
        def _(part=part):
            lo = part * rows * TOK_SUB
            y_ref[lo:lo + rows * TOK_SUB, :] = jnp.zeros((rows * TOK_SUB, LANES), F32)


def _moe(plan, xs, wsel, bsel, w_gate, w_up, w_down, n_tiles):
    wsel1 = lambda j, s0, s1, v, f0, f1, tb, xb: (s0[j], 0, 0)
    wsel2 = lambda j, s0, s1, v, f0, f1, tb, xb: (s1[j], 0, 0)
    by_bucket = lambda j, s0, s1, v, f0, f1, tb, xb: (tb[j], 0, 0)
    up_spec = lambda sel: pl.BlockSpec((1, D_MODEL, D_EXPERT), sel)
    dn_spec = lambda sel: pl.BlockSpec((1, D_EXPERT, D_MODEL), sel)
    tile_rows = MOE_STEP * TOK_SUB
    return pl.pallas_call(
        _moe_kernel,
        grid_spec=pltpu.PrefetchScalarGridSpec(
            num_scalar_prefetch=7,
            grid=(n_tiles,),
            in_specs=[pl.BlockSpec((tile_rows, LANES), lambda j, s0, s1, v, f0, f1, tb, xb: (xb[j], 0)),
                      pl.BlockSpec((1, D_MODEL, LANES), by_bucket),
                      pl.BlockSpec((1, 1, LANES), by_bucket),
                      up_spec(wsel1), up_spec(wsel1), dn_spec(wsel1),
                      up_spec(wsel2), up_spec(wsel2), dn_spec(wsel2)],
            out_specs=pl.BlockSpec((tile_rows, LANES), lambda j, *_: (j, 0)),
            scratch_shapes=[pltpu.VMEM((2, D_MODEL, D_EXPERT), BF16),
                            pltpu.VMEM((2, D_MODEL, D_EXPERT), BF16),
                            pltpu.VMEM((2, D_EXPERT, D_MODEL), BF16)],
        ),
        out_shape=jax.ShapeDtypeStruct((n_tiles * tile_rows, LANES), F32),
        compiler_params=_cparams(("arbitrary",)),
        name="moe",
    )(*plan, xs, wsel, bsel, w_gate, w_up, w_down, w_gate, w_up, w_down)


def _combine_kernel(pos_ref, ys_ref, x1_ref, gt_ref, lg_ref, lb_ref, out_ref, ybuf, sems):
    ts = x1_ref.shape[0]
    i = pl.program_id(0)
    n = pl.num_programs(0)
    slot = i % 2

    def gather(tile, to_slot):
        def issue(g, carry):
            for u in range(DMA_UNROLL):
                r = g * DMA_UNROLL + u
                src = pl.multiple_of(pos_ref[tile * ts + r] * TOK_SUB, TOK_SUB)
                dst = pl.multiple_of(r * TOK_SUB, TOK_SUB)
                pltpu.make_async_copy(ys_ref.at[pl.ds(src, TOK_SUB), :],
                                      ybuf.at[to_slot, pl.ds(dst, TOK_SUB), :],
                                      sems.at[to_slot]).start(priority=u % 2)
            return carry
        lax.fori_loop(0, ts // DMA_UNROLL, issue, 0)

    @pl.when(i == 0)
    def _():
        gather(0, 0)

    @pl.when(i + 1 < n)
    def _():
        gather(i + 1, 1 - slot)

    pltpu.make_async_copy(ys_ref.at[pl.ds(0, ts * TOK_SUB), :], ybuf.at[slot], sems.at[slot]).wait()
    y = _load_token_major(ybuf.at[slot], ts)
    out_ref[...] = _layer_norm(ALPHA * x1_ref[...] + (1.0 + gt_ref[0]) * y, lg_ref[...], lb_ref[...])


def _combine(pos, ys, x1, gt, lg, lb, bsz, seq):
    ts = min(TOK_TILE, seq)
    nt = seq // ts
    tok = bsz * seq
    return pl.pallas_call(
        _combine_kernel,
        grid_spec=pltpu.PrefetchScalarGridSpec(
            num_scalar_prefetch=1,
            grid=(tok // ts,),
            in_specs=[pl.BlockSpec(memory_space=pl.ANY),
                      pl.BlockSpec((ts, D_MODEL), lambda i, pos: (i, 0)),
                      pl.BlockSpec((1, 1, D_MODEL), lambda i, pos: (i // nt, 0, 0)),
                      pl.BlockSpec((1, D_MODEL), lambda i, pos: (0, 0)),
                      pl.BlockSpec((1, D_MODEL), lambda i, pos: (0, 0))],
            out_specs=pl.BlockSpec((ts, D_MODEL), lambda i, pos: (i, 0)),
            scratch_shapes=[pltpu.VMEM((2, ts * TOK_SUB, LANES), F32), pltpu.SemaphoreType.DMA((2,))],
        ),
        out_shape=jax.ShapeDtypeStruct((tok, D_MODEL), F32),
        compiler_params=_cparams(("arbitrary",)),
        name="combine",
    )(pos, ys, x1, gt, lg, lb)


def _route_plan(route, cnt, n_tiles, layer):
    bucket = route[0].astype(I32)
    rank = route[1].astype(I32)
    counts = cnt[:N_BUCKETS, 0].astype(I32)
    tiles_b = (counts + MOE_STEP - 1) // MOE_STEP
    tile_end = jnp.cumsum(tiles_b)
    row_off = (tile_end - tiles_b) * MOE_STEP
    bucket_ids = jnp.arange(N_BUCKETS, dtype=I32)
    pos = jnp.sum(jnp.where(bucket[None, :] == bucket_ids[:, None], row_off[:, None], 0), axis=0) + rank
    total = tile_end[-1]
    j = jnp.arange(n_tiles, dtype=I32)
    tb = jnp.sum((tile_end[None, :] <= jnp.minimum(j, total - 1)[:, None]).astype(I32), axis=1)
    tb = jnp.clip(tb, 0, N_BUCKETS - 1)
    hit = tb[None, :] == bucket_ids[:, None]
    rows_left = jnp.sum(jnp.where(hit, (row_off + counts)[:, None], 0), axis=0) - j * MOE_STEP
    tiles_j = jnp.clip((rows_left + MOE_TILE - 1) // MOE_TILE, 0, MOE_STEP // MOE_TILE)
    tiles_j = jnp.where(j < total, tiles_j, 0)
    grp = tb // N_PAIRS
    pair = tb % N_PAIRS
    first = layer * (N_GROUPS * N_EXP) + grp * N_EXP
    s0 = first + sum((pair == i).astype(I32) * _SLOT0[i] for i in range(N_PAIRS))
    s1 = first + sum((pair == i).astype(I32) * _SLOT1[i] for i in range(N_PAIRS))
    one = jnp.ones((1,), I32)
    fresh0 = jnp.concatenate([one, (s0[1:] != s0[:-1]).astype(I32)])
    fresh1 = jnp.concatenate([one, (s1[1:] != s1[:-1]).astype(I32)])
    xblk = jnp.minimum(j, total - 1)
    pads = jnp.concatenate([row_off + counts, tiles_b * MOE_STEP - counts,
                            (total * (MOE_STEP // MOE_TILE))[None]])
    return pos, pads, (s0, s1, tiles_j, fresh0, fresh1, tb, xblk)


def kernel(x, c, w_ada, b_ada, w_in, b_gates, mh_norm_w, w_conv, w_out, ln1_g, ln1_b,
           w_grp, b_grp, w_router, b_router, w_gate, w_up, w_down, ln2_g, ln2_b):
    bsz, seq, _ = x.shape
    tok = bsz * seq
    n_rows = tok + N_BUCKETS * MOE_STEP
    n_tiles = n_rows // MOE_STEP
    n_exp_total = N_GROUPS * N_EXP

    wg_all = w_gate.reshape(DEPTH * n_exp_total, D_MODEL, D_EXPERT)
    wu_all = w_up.reshape(DEPTH * n_exp_total, D_MODEL, D_EXPERT)
    wd_all = w_down.reshape(DEPTH * n_exp_total, D_EXPERT, D_MODEL)
    mod = _ada_mod(c, w_ada, b_ada).reshape(DEPTH, bsz, 6, 1, D_MODEL)
    x2 = x.reshape(tok, D_MODEL)
    for l in range(DEPTH):
        sh1, sc1, gt1, sh2, sc2, gt2 = [mod[l, :, i] for i in range(6)]
        wl = w_in[l]
        wq = wl[:, :4 * M_WIDTH].astype(BF16)
        wg = jnp.pad(wl[:, 4 * M_WIDTH:4 * M_WIDTH + 2 * HEADS],
                     ((0, 0), (0, LANES - 2 * HEADS))).astype(BF16)
        wc = wl[:, 4 * M_WIDTH + 2 * HEADS:].astype(BF16)
        bg = jnp.pad(b_gates[l], (0, LANES - 2 * HEADS)).reshape(1, LANES)
        qkvo, grow, yc = _inproj(x2, sc1, sh1, wq, wg, wc, bg, w_conv[l], bsz, seq)
        n_logit = N_GROUPS + n_exp_total
        wr = jnp.pad(jnp.concatenate([w_grp[l], w_router[l]], axis=1).T,
                     ((0, ROUTE_ROWS - n_logit), (0, 0))).astype(BF16)
        br = jnp.pad(jnp.concatenate([b_grp[l], b_router[l]]), (0, ROUTE_ROWS - n_logit))
        br = jnp.broadcast_to(br[:, None], (ROUTE_ROWS, LANES))
        x1, u2, route, cnt = _mixer(qkvo, grow, mh_norm_w[l].reshape(1, M_WIDTH), yc, x2, gt1, sc2, sh2,
                                    w_out[l].astype(BF16), ln1_g[l].reshape(1, D_MODEL),
                                    ln1_b[l].reshape(1, D_MODEL), wr, br, bsz, seq)
        pos, pads, plan = _route_plan(route, cnt, n_tiles, l)
        xs = _dispatch(pos, pads, u2, n_rows)
        wsel = jnp.concatenate([jnp.broadcast_to(w_grp[l], (N_BUCKETS, D_MODEL, N_GROUPS)),
                                w_router[l][:, _BUCKET_S0].T[:, :, None],
                                w_router[l][:, _BUCKET_S1].T[:, :, None]], axis=2)
        wsel = jnp.pad(wsel, ((0, 0), (0, 0), (0, LANES - N_GROUPS - 2))).astype(BF16)
        bsel = jnp.concatenate([jnp.broadcast_to(b_grp[l], (N_BUCKETS, N_GROUPS)),
                                b_router[l][_BUCKET_S0][:, None], b_router[l][_BUCKET_S1][:, None]], axis=1)
        bsel = jnp.pad(bsel, ((0, 0), (0, LANES - N_GROUPS - 2))).reshape(N_BUCKETS, 1, LANES)
        ys = _moe(plan, xs, wsel, bsel, wg_all, wu_all, wd_all, n_tiles)
        x2 = _combine(pos, ys, x1, gt2, ln2_g[l].reshape(1, D_MODEL),
                      ln2_b[l].reshape(1, D_MODEL), bsz, seq)
    return x2.reshape(bsz, seq, D_MODEL)
```

```python
import functools
import math

import jax
import jax.numpy as jnp
import numpy as np
from jax import lax
from jax.experimental import pallas as pl
from jax.experimental.pallas import tpu as pltpu

F32 = jnp.float32
BF16 = jnp.bfloat16
I32 = jnp.int32

D_MODEL = 1024
DEPTH = 2
HEADS = 4
D_HEAD = 128
M_WIDTH = HEADS * D_HEAD
C_WIDTH = D_MODEL - M_WIDTH
N_GROUPS = 4
N_EXP = 4
D_EXPERT = 512
N_PAIRS = 6
N_BUCKETS = N_GROUPS * N_PAIRS
N_MOD = 6
MOD_SH1, MOD_SC1, MOD_GT1, MOD_SH2, MOD_SC2, MOD_GT2 = range(N_MOD)
ALPHA = (2 * DEPTH) ** 0.25
LN_EPS = 1e-5
QK_SCALE = D_HEAD ** -0.5
LOG_QK_SCALE = math.log(QK_SCALE)
GATE_ROWS = 24

LANES = 128
ROUTE_ROWS = 32

TOK_TILE = 512
INPROJ_TILE = 1024
CHUNK = 256
MOE_TILE = 256
MOE_STEP = 2 * MOE_TILE
VMEM_LIMIT = 56 * 1024 * 1024

_PAIR_ORDER = (0, 2, 3, 1, 4, 5)
_SLOT0 = (0, 2, 2, 3, 3, 3)
_SLOT1 = (1, 1, 0, 0, 1, 2)
_BUCKET_S0 = np.array([g * N_EXP + _SLOT0[p] for g in range(N_GROUPS) for p in range(N_PAIRS)])
_BUCKET_S1 = np.array([g * N_EXP + _SLOT1[p] for g in range(N_GROUPS) for p in range(N_PAIRS)])


def _cparams(sem):
    return pltpu.CompilerParams(dimension_semantics=sem, vmem_limit_bytes=VMEM_LIMIT)


def _ada_kernel(c_ref, w_ref, b_ref, o_ref):
    c = c_ref[...]
    cond = (c * jax.nn.sigmoid(c)).astype(BF16)
    o_ref[0] = jnp.dot(cond, w_ref[0].astype(BF16), preferred_element_type=F32) + b_ref[0]


def _ada_mod(c, w_ada, b_ada):
    bsz = c.shape[0]
    nblk = w_ada.shape[2] // D_MODEL
    return pl.pallas_call(
        _ada_kernel,
        grid=(DEPTH, nblk),
        in_specs=[
            pl.BlockSpec((bsz, D_MODEL), lambda l, j: (0, 0)),
            pl.BlockSpec((1, D_MODEL, D_MODEL), lambda l, j: (l, 0, j)),
            pl.BlockSpec((1, 1, D_MODEL), lambda l, j: (l, 0, j)),
        ],
        out_specs=pl.BlockSpec((1, bsz, D_MODEL), lambda l, j: (l, 0, j)),
        out_shape=jax.ShapeDtypeStruct((DEPTH, bsz, nblk * D_MODEL), F32),
        compiler_params=_cparams(("arbitrary", "arbitrary")),
        name="ada_mod",
    )(c, w_ada, b_ada.reshape(DEPTH, 1, -1))


def _log_sigmoid(x):
    return jnp.minimum(x, 0.0) - jnp.log1p(jnp.exp(-jnp.abs(x)))


def _chunk_scan(x, op, identity):
    lane = lax.broadcasted_iota(I32, x.shape, 1) & (CHUNK - 1)
    sh = 1
    while sh < CHUNK:
        x = op(x, jnp.where(lane >= sh, pltpu.roll(x, sh, 1), identity))
        sh *= 2
    return x


def _inproj_kernel(x_ref, sc_ref, sh_ref, wq_ref, wg_ref, wc_ref, bg_ref, wconv_ref,
                   qkvo_ref, grow_ref, yc_ref, zbuf):
    tm = x_ref.shape[0]
    u = (x_ref[...] * (1.0 + sc_ref[0]) + sh_ref[0]).astype(BF16)
    qkvo_ref[...] = jnp.dot(u, wq_ref[...], preferred_element_type=F32).astype(BF16)
    g = jnp.dot(u, wg_ref[...], preferred_element_type=F32) + bg_ref[...]
    gt = g.T[:8]
    head_row = lax.broadcasted_iota(I32, gt.shape, 0) < HEADS
    ig = jnp.where(head_row, gt, 0.0)
    logf = jnp.where(head_row, _log_sigmoid(pltpu.roll(gt, HEADS, 0)), 0.0)
    bcum = _chunk_scan(logf, jnp.add, 0.0)
    ug = ig - bcum
    grow_ref[0:8, :] = ug
    grow_ref[8:16, :] = bcum
    grow_ref[16:24, :] = _chunk_scan(ug, jnp.maximum, -jnp.inf)
    pc = jnp.dot(u, wc_ref[...], preferred_element_type=F32)
    z = pc[:, C_WIDTH:2 * C_WIDTH] * pc[:, 2 * C_WIDTH:]

    @pl.when(pl.program_id(1) == 0)
    def _():
        zbuf[0:8, :] = jnp.zeros((8, C_WIDTH), F32)

    zbuf[8:8 + tm, :] = z
    zc = (wconv_ref[0:1, :] * zbuf[6:6 + tm, :] + wconv_ref[1:2, :] * zbuf[7:7 + tm, :]
          + wconv_ref[2:3, :] * z)
    yc_ref[...] = (pc[:, :C_WIDTH] * zc).astype(BF16)
    zbuf[0:8, :] = zbuf[tm:tm + 8, :]


def _mod_spec(layer, which, bsz, batch_of):
    return pl.BlockSpec((1, 1, D_MODEL),
                        lambda *idx: ((layer * bsz + batch_of(*idx)) * N_MOD + which, 0, 0))


def _inproj(x2, mod, layer, wq, wg, wc, bg, wconv, bsz, seq):
    tm = min(INPROJ_TILE, seq)
    nt = seq // tm
    tok = bsz * seq
    row = lambda b, s: (b * nt + s, 0)
    const = lambda b, s: (0, 0)
    batch = lambda b, s: b
    return pl.pallas_call(
        _inproj_kernel,
        grid=(bsz, nt),
        in_specs=[
            pl.BlockSpec((tm, D_MODEL), row),
            _mod_spec(layer, MOD_SC1, bsz, batch),
            _mod_spec(layer, MOD_SH1, bsz, batch),
            pl.BlockSpec(wq.shape, const),
            pl.BlockSpec(wg.shape, const),
            pl.BlockSpec(wc.shape, const),
            pl.BlockSpec(bg.shape, const),
            pl.BlockSpec(wconv.shape, const),
        ],
        out_specs=[
            pl.BlockSpec((tm, 4 * M_WIDTH), row),
            pl.BlockSpec((GATE_ROWS, tm), lambda b, s: (0, b * nt + s)),
            pl.BlockSpec((tm, C_WIDTH), row),
        ],
        out_shape=[
            jax.ShapeDtypeStruct((tok, 4 * M_WIDTH), BF16),
            jax.ShapeDtypeStruct((GATE_ROWS, tok), F32),
            jax.ShapeDtypeStruct((tok, C_WIDTH), BF16),
        ],
        scratch_shapes=[pltpu.VMEM((tm + 8, C_WIDTH), F32)],
        compiler_params=_cparams(("arbitrary", "arbitrary")),
        name="inproj",
    )(x2, mod, mod, wq, wg, wc, bg, wconv)


N_ROWQ = 4
SPLIT = 3


def _split3(x):
    hi = x.astype(BF16).astype(F32)
    r1 = x - hi
    mid = r1.astype(BF16).astype(F32)
    return [hi, mid, r1 - mid]


def _mlstm_selector(sel):
    r = lax.broadcasted_iota(I32, sel.shape, 0)
    c = lax.broadcasted_iota(I32, sel.shape, 1) // LANES
    rq = r // 8
    quantity = ((rq >= SPLIT).astype(I32) + (rq >= 2 * SPLIT).astype(I32)
                + (rq >= 3 * SPLIT).astype(I32))
    hit = ((r % 8) == c // N_ROWQ) & (quantity == c % N_ROWQ) & (rq < N_ROWQ * SPLIT)
    sel[...] = hit.astype(BF16)


def _mlstm_block(qkvo_ref, grow_ref, nw_ref, out_ref, cst, mst, sel, between=()):
    between = tuple(between) + (None,) * 3
    sb = qkvo_ref.shape[0]
    L = CHUNK
    causal = lax.broadcasted_iota(I32, (L, L), 0) >= lax.broadcasted_iota(I32, (L, L), 1)
    ones_ext = jnp.ones((L, D_HEAD), BF16)
    pad_rows = jnp.zeros((LANES - 8 * N_ROWQ * SPLIT, L), F32)
    eye = (lax.broadcasted_iota(I32, (D_HEAD, D_HEAD), 0)
           == lax.broadcasted_iota(I32, (D_HEAD, D_HEAD), 1)).astype(BF16)
    n_chunks = sb // L
    pairs = [(c, h) for c in range(n_chunks) for h in range(HEADS)]

    def cols(part, c, h):
        c0 = part * M_WIDTH + h * D_HEAD
        return qkvo_ref[c * L:(c + 1) * L, c0:c0 + D_HEAD]

    m_prev = mst[...]
    stack_t, us, decay = [], [], []
    for c in range(n_chunks):
        ug = grow_ref[0:8, c * L:(c + 1) * L]
        bcum = grow_ref[8:16, c * L:(c + 1) * L]
        cmax = grow_ref[16:24, c * L:(c + 1) * L]
        big_m = jnp.maximum(m_prev, cmax)
        m_last = jnp.broadcast_to(big_m[:, L - 1:L], (8, L))
        g_tot = jnp.broadcast_to(bcum[:, L - 1:L], (8, L))
        wq = jnp.exp(m_prev - big_m) * QK_SCALE
        log_em = -(bcum + big_m)
        wk = jnp.exp(ug - m_last)
        decay.append(jnp.exp(m_prev - m_last))
        us.append(ug + LOG_QK_SCALE)
        m_prev = g_tot + m_last
        stack = jnp.concatenate(_split3(big_m) + _split3(wq) + _split3(log_em) + _split3(wk)
                                + [pad_rows], axis=0)
        stack_t.append(stack.T.astype(BF16))
    mst[...] = m_prev
    if between[0] is not None:
        between[0]()

    reps = {}

    def rep(c, h, j):
        if (c, h) not in reps:
            c0 = h * N_ROWQ * LANES
            reps[(c, h)] = jnp.dot(stack_t[c], sel[:, c0:c0 + N_ROWQ * LANES],
                                   preferred_element_type=F32)
        return reps[(c, h)][:, j * LANES:(j + 1) * LANES]

    s_mat = {ch: lax.dot_general(cols(0, *ch), cols(1, *ch), (((1,), (1,)), ((), ())),
                                 preferred_element_type=F32) for ch in pairs}
    kw = {ch: (cols(1, *ch).astype(F32) * rep(*ch, 3)).astype(BF16) for ch in pairs}
    kw_t = {ch: lax.dot_general(eye, kw[ch], (((1,), (1,)), ((), ())),
                                preferred_element_type=F32).astype(BF16) for ch in pairs}
    if between[1] is not None:
        between[1]()
    lhs = {}
    for ch in pairs:
        c, h = ch
        big_m_rows = jnp.concatenate([rep(c, h, 0)] * (L // LANES), axis=1)
        p = jnp.where(causal, s_mat[ch] * jnp.exp(us[c][h:h + 1, :] - big_m_rows), 0.0)
        qw = cols(0, c, h).astype(F32) * rep(c, h, 1)
        lhs[ch] = jnp.concatenate([p.astype(BF16), qw.astype(BF16)], axis=1)
    vext = {ch: jnp.concatenate([cols(2, *ch), ones_ext], axis=1) for ch in pairs}
    upd = {ch: jnp.dot(kw_t[ch], vext[ch], preferred_element_type=F32) for ch in pairs}
    for c in range(n_chunks):
        tot = {}
        for h in range(HEADS):
            state = cst[h]
            rhs = jnp.concatenate([vext[(c, h)], state.astype(BF16)], axis=0)
            tot[h] = jnp.dot(lhs[(c, h)], rhs, preferred_element_type=F32)
            dec = jnp.concatenate([decay[c][h:h + 1, :LANES]] * 2, axis=1)
            cst[h] = dec * state + upd[(c, h)]
        for h in range(HEADS):
            c0 = h * D_HEAD
            num, den = tot[h][:, :D_HEAD], tot[h][:, D_HEAD:]
            hh = num / jnp.maximum(jnp.abs(den), jnp.exp(rep(c, h, 2)))
            mu = jnp.mean(hh, axis=1, keepdims=True)
            var = jnp.mean(jnp.square(hh - mu), axis=1, keepdims=True)
            hn = (hh - mu) * lax.rsqrt(var + LN_EPS) * nw_ref[:, c0:c0 + D_HEAD]
            og = jax.nn.sigmoid(cols(3, c, h).astype(F32))
            out_ref[c * L:(c + 1) * L, c0:c0 + D_HEAD] = (hn * og).astype(BF16)
        if c == 0 and between[2] is not None:
            between[2]()


def _layer_norm(r, g, b):
    mu = jnp.mean(r, axis=-1, keepdims=True)
    var = jnp.mean(jnp.square(r - mu), axis=-1, keepdims=True)
    return (r - mu) * lax.rsqrt(var + LN_EPS) * g + b


TOK_SUB = D_MODEL // LANES


def _store_token_major(ref, val, first_row=0):
    rows = val.shape[0]
    for k in range(TOK_SUB):
        ref[pl.ds(first_row * TOK_SUB + k, rows, stride=TOK_SUB), :] = val[:, k * LANES:(k + 1) * LANES]


def _load_token_major(ref, rows, first_row=0):
    return jnp.concatenate([ref[pl.ds(first_row * TOK_SUB + k, rows, stride=TOK_SUB), :]
                            for k in range(TOK_SUB)], axis=1)


def _outproj_parts(hn_ref, yc_ref, x_ref, gt_ref, sc_ref, sh_ref, wo_ref, lg_ref, lb_ref,
                   wr_ref, br_ref, x1_ref, u2_ref, route_ref, cnt_ref, base, triu, live):
    tm = x_ref.shape[0]
    held = {}

    def project():
        held["mix"] = (jnp.dot(hn_ref[...], wo_ref[0:M_WIDTH, :], preferred_element_type=F32)
                       + jnp.dot(yc_ref[...], wo_ref[M_WIDTH:, :], preferred_element_type=F32))

    def normalise():
        x1 = _layer_norm(ALPHA * x_ref[...] + (1.0 + gt_ref[0]) * held["mix"], lg_ref[...], lb_ref[...])
        x1_ref[...] = x1
        u2 = x1 * (1.0 + sc_ref[0]) + sh_ref[0]
        _store_token_major(u2_ref, u2)
        held["logits"] = lax.dot_general(wr_ref[...], u2.astype(BF16), (((1,), (1,)), ((), ())),
                                         preferred_element_type=F32) + br_ref[:, 0:1]

    def route():
        _route_block(held["logits"], route_ref, cnt_ref, base, triu, live, tm)

    return project, normalise, route


def _route_block(lt, route_ref, cnt_ref, base, triu, live, tm):
    lg = [lt[j:j + 1, :] for j in range(N_GROUPS + N_GROUPS * N_EXP)]
    best = lg[0]
    grp = jnp.zeros((1, tm), I32)
    for j in range(1, N_GROUPS):
        c = lg[j] > best
        grp = jnp.where(c, j, grp)
        best = jnp.where(c, lg[j], best)
    sel = []
    for e in range(N_EXP):
        val = lg[N_GROUPS + e]
        for gg in range(1, N_GROUPS):
            val = jnp.where(grp == gg, lg[N_GROUPS + gg * N_EXP + e], val)
        sel.append(val)
    v1 = sel[0]
    i1 = jnp.zeros((1, tm), I32)
    for e in range(1, N_EXP):
        c = sel[e] > v1
        i1 = jnp.where(c, e, i1)
        v1 = jnp.where(c, sel[e], v1)
    v2 = jnp.full((1, tm), -jnp.inf, F32)
    i2 = jnp.zeros((1, tm), I32)
    for e in range(N_EXP):
        cand = jnp.where(i1 == e, -jnp.inf, sel[e])
        c = cand > v2
        i2 = jnp.where(c, e, i2)
        v2 = jnp.where(c, cand, v2)
    ea = jnp.minimum(i1, i2)
    eb = jnp.maximum(i1, i2)
    lex = jnp.where(ea == 0, 0, jnp.where(ea == 1, 3, 5)) + eb - ea - 1
    pair = lex
    for i, p in enumerate(_PAIR_ORDER):
        if i != p:
            pair = jnp.where(lex == i, p, pair)
    bucket = grp * N_PAIRS + pair

    onehot = lax.broadcasted_iota(I32, (ROUTE_ROWS, tm), 0) == bucket
    cum = jnp.dot(onehot.astype(BF16), triu[...], preferred_element_type=F32)
    prev = base[:, 0:1]
    rank = jnp.sum(jnp.where(onehot, cum - 1.0 + prev, 0.0), axis=0, keepdims=True)
    new_base = prev + live * cum[:, tm - 1:tm]
    base[...] = jnp.broadcast_to(new_base, base.shape)
    cnt_ref[...] = jnp.broadcast_to(new_base, cnt_ref.shape)

    zrow = jnp.zeros((1, tm), F32)
    route_ref[...] = jnp.concatenate(
        [bucket.astype(F32), rank, zrow, zrow, zrow, zrow, zrow, zrow], axis=0)


def _mixer_kernel(qkvo_ref, grow_ref, nw_ref, yc_ref, x_ref, gt_ref, sc_ref, sh_ref, wo_ref, lg_ref,
                  lb_ref, wr_ref, br_ref, x1_ref, u2_ref, route_ref, cnt_ref,
                  cst, mst, sel, hn, base, triu, *, blocks_per_seq):
    i = pl.program_id(0)
    tm = x_ref.shape[0]

    @pl.when(i == 0)
    def _():
        _mlstm_selector(sel)
        hn[...] = jnp.zeros(hn.shape, BF16)
        base[...] = jnp.zeros(base.shape, F32)
        rid = lax.broadcasted_iota(I32, triu.shape, 0)
        cid = lax.broadcasted_iota(I32, triu.shape, 1)
        triu[...] = (rid <= cid).astype(BF16)

    @pl.when(i % blocks_per_seq == 0)
    def _():
        cst[...] = jnp.zeros(cst.shape, F32)
        mst[...] = jnp.zeros(mst.shape, F32)

    live = jnp.where(i > 0, 1.0, 0.0).astype(F32)
    parts = _outproj_parts(hn, yc_ref, x_ref, gt_ref, sc_ref, sh_ref, wo_ref, lg_ref, lb_ref,
                           wr_ref, br_ref, x1_ref, u2_ref, route_ref, cnt_ref, base, triu, live)
    _mlstm_block(qkvo_ref, grow_ref, nw_ref, hn, cst, mst, sel, between=parts)


def _mixer(qkvo, grow, nw, yc, x2, mod, layer, wo, lg, lb, wr, br, bsz, seq):
    tm = min(TOK_TILE, seq)
    nt = seq // tm
    tok = bsz * seq
    n = bsz * nt
    cur = lambda i: (jnp.minimum(i, n - 1), 0)
    prev = lambda i: (jnp.maximum(i - 1, 0), 0)
    const = lambda i: (0, 0)
    batch = lambda i: jnp.maximum(i - 1, 0) // nt
    return pl.pallas_call(
        functools.partial(_mixer_kernel, blocks_per_seq=nt),
        grid=(n + 1,),
        in_specs=[
            pl.BlockSpec((tm, 4 * M_WIDTH), cur),
            pl.BlockSpec((GATE_ROWS, tm), lambda i: (0, jnp.minimum(i, n - 1))),
            pl.BlockSpec((1, M_WIDTH), const),
            pl.BlockSpec((tm, C_WIDTH), prev),
            pl.BlockSpec((tm, D_MODEL), prev),
            _mod_spec(layer, MOD_GT1, bsz, batch),
            _mod_spec(layer, MOD_SC2, bsz, batch),
            _mod_spec(layer, MOD_SH2, bsz, batch),
            pl.BlockSpec(wo.shape, const),
            pl.BlockSpec(lg.shape, const),
            pl.BlockSpec(lb.shape, const),
            pl.BlockSpec(wr.shape, const),
            pl.BlockSpec(br.shape, const),
        ],
        out_specs=[
            pl.BlockSpec((tm, D_MODEL), prev),
            pl.BlockSpec((tm * TOK_SUB, LANES), prev),
            pl.BlockSpec((8, tm), lambda i: (0, jnp.maximum(i - 1, 0))),
            pl.BlockSpec((ROUTE_ROWS, LANES), const),
        ],
        out_shape=[
            jax.ShapeDtypeStruct((tok, D_MODEL), F32),
            jax.ShapeDtypeStruct((tok * TOK_SUB, LANES), F32),
            jax.ShapeDtypeStruct((8, tok), F32),
            jax.ShapeDtypeStruct((ROUTE_ROWS, LANES), F32),
        ],
        scratch_shapes=[pltpu.VMEM((HEADS, D_HEAD, 2 * D_HEAD), F32),
                        pltpu.VMEM((8, CHUNK), F32),
                        pltpu.VMEM((LANES, HEADS * N_ROWQ * LANES), BF16),
                        pltpu.VMEM((tm, M_WIDTH), BF16),
                        pltpu.VMEM((ROUTE_ROWS, LANES), F32),
                        pltpu.VMEM((tm, tm), BF16)],
        compiler_params=_cparams(("arbitrary",)),
        name="mixer",
    )(qkvo, grow, nw, yc, x2, mod, mod, mod, wo, lg, lb, wr, br)


DISPATCH_TILE = 2048
DMA_UNROLL = 8


def _dispatch_kernel(pos_ref, pad_ref, u_ref, xs_ref, zeros, sem, zsem):
    ts = u_ref.shape[0] // TOK_SUB
    t0 = pl.program_id(0) * ts

    def zero_rows(first_row, n_rows_static):
        dst = pl.multiple_of(first_row * TOK_SUB, TOK_SUB)
        return pltpu.make_async_copy(zeros.at[pl.ds(0, n_rows_static * TOK_SUB), :],
                                     xs_ref.at[pl.ds(dst, n_rows_static * TOK_SUB), :], zsem)

    def fill_pads(wait):
        def bucket(b, carry):
            row = pad_ref[b]
            n = pad_ref[N_BUCKETS + b]
            size = MOE_STEP // 2
            while size >= 1:
                @pl.when((n & size) != 0)
                def _(row=row, size=size):
                    cp = zero_rows(row, size)
                    cp.wait() if wait else cp.start()
                row = row + (n & size)
                size //= 2
            return carry
        lax.fori_loop(0, N_BUCKETS, bucket, 0)

        def idle_tile(j, carry):
            cp = zero_rows(j * MOE_TILE, MOE_TILE)
            cp.wait() if wait else cp.start()
            return carry
        lax.fori_loop(pad_ref[2 * N_BUCKETS], xs_ref.shape[0] // (MOE_TILE * TOK_SUB), idle_tile, 0)

    @pl.when(pl.program_id(0) == 0)
    def _():
        zeros[...] = jnp.zeros(zeros.shape, F32)
        fill_pads(wait=False)
        fill_pads(wait=True)

    def issue(g, carry):
        for u in range(DMA_UNROLL):
            r = g * DMA_UNROLL + u
            dst = pl.multiple_of(pos_ref[t0 + r] * TOK_SUB, TOK_SUB)
            src = pl.multiple_of(r * TOK_SUB, TOK_SUB)
            pltpu.make_async_copy(u_ref.at[pl.ds(src, TOK_SUB), :],
                                  xs_ref.at[pl.ds(dst, TOK_SUB), :], sem).start(priority=u % 2)
        return carry

    lax.fori_loop(0, ts // DMA_UNROLL, issue, 0)
    pltpu.make_async_copy(u_ref, xs_ref.at[pl.ds(0, ts * TOK_SUB), :], sem).wait()


def _dispatch(pos, pad_start, u2, n_rows):
    tok = u2.shape[0] // TOK_SUB
    ts = min(DISPATCH_TILE, tok)
    return pl.pallas_call(
        _dispatch_kernel,
        grid_spec=pltpu.PrefetchScalarGridSpec(
            num_scalar_prefetch=2,
            grid=(tok // ts,),
            in_specs=[pl.BlockSpec((ts * TOK_SUB, LANES), lambda i, pos, pad: (i, 0))],
            out_specs=pl.BlockSpec(memory_space=pl.ANY),
            scratch_shapes=[pltpu.VMEM((MOE_TILE * TOK_SUB, LANES), F32),
                            pltpu.SemaphoreType.DMA(()), pltpu.SemaphoreType.DMA(())],
        ),
        out_shape=jax.ShapeDtypeStruct((n_rows * TOK_SUB, LANES), F32),
        compiler_params=_cparams(("arbitrary",)),
        name="dispatch",
    )(pos, pad_start, u2)


def _moe_kernel(s0_ref, s1_ref, tiles_ref, fresh0_ref, fresh1_ref, tb_ref, xb_ref, xs_ref, wsel_ref,
                bsel_ref, wg0, wu0, wd0, wg1, wu1, wd1, y_ref, wgs, wus, wds):
    del s0_ref, s1_ref, tb_ref, xb_ref
    j = pl.program_id(0)
    rows = MOE_TILE

    @pl.when(fresh0_ref[j] == 1)
    def _():
        wgs[0] = wg0[0].astype(BF16)
        wus[0] = wu0[0].astype(BF16)
        wds[0] = wd0[0].astype(BF16)

    @pl.when(fresh1_ref[j] == 1)
    def _():
        wgs[1] = wg1[0].astype(BF16)
        wus[1] = wu1[0].astype(BF16)
        wds[1] = wd1[0].astype(BF16)

    def experts(part):
        x = _load_token_major(xs_ref, rows, part * rows).astype(BF16)
        lg = jnp.dot(x, wsel_ref[0], preferred_element_type=F32) + bsel_ref[0]
        grp = [lg[:, i:i + 1] for i in range(N_GROUPS)]
        gmax = jnp.maximum(jnp.maximum(grp[0], grp[1]), jnp.maximum(grp[2], grp[3]))
        gsum = sum(jnp.exp(v - gmax) for v in grp)
        la, lb = lg[:, N_GROUPS:N_GROUPS + 1], lg[:, N_GROUPS + 1:N_GROUPS + 2]
        emax = jnp.maximum(la, lb)
        ea, eb = jnp.exp(la - emax), jnp.exp(lb - emax)
        scale = 1.0 / (gsum * (ea + eb))
        wcols = (ea * scale, eb * scale)
        acc = None
        for slot in range(2):
            g = jnp.dot(x, wgs[slot], preferred_element_type=F32)
            u = jnp.dot(x, wus[slot], preferred_element_type=F32)
            hid = (g * jax.nn.sigmoid(g) * u * wcols[slot]).astype(BF16)
            y = jnp.dot(hid, wds[slot], preferred_element_type=F32)
            acc = y if acc is None else acc + y
        _store_token_major(y_ref, acc, part * rows)

    for part in range(MOE_STEP // MOE_TILE):
        @pl.when(tiles_ref[j] > part)
        def _(part=part):
            experts(part)

        @pl.when(tiles_ref[j] <= part)
        def _(part=part):
            lo = part * rows * TOK_SUB
            y_ref[lo:lo + rows * TOK_SUB, :] = jnp.zeros((rows * TOK_SUB, LANES), F32)


def _moe(plan, xs, wsel, bsel, w_gate, w_up, w_down, n_tiles):
    wsel1 = lambda j, s0, s1, v, f0, f1, tb, xb: (s0[j], 0, 0)
    wsel2 = lambda j, s0, s1, v, f0, f1, tb, xb: (s1[j], 0, 0)
    by_bucket = lambda j, s0, s1, v, f0, f1, tb, xb: (tb[j], 0, 0)
    up_spec = lambda sel: pl.BlockSpec((1, D_MODEL, D_EXPERT), sel)
    dn_spec = lambda sel: pl.BlockSpec((1, D_EXPERT, D_MODEL), sel)
    tile_rows = MOE_STEP * TOK_SUB
    return pl.pallas_call(
        _moe_kernel,
        grid_spec=pltpu.PrefetchScalarGridSpec(
            num_scalar_prefetch=7,
            grid=(n_tiles,),
            in_specs=[pl.BlockSpec((tile_rows, LANES), lambda j, s0, s1, v, f0, f1, tb, xb: (xb[j], 0)),
                      pl.BlockSpec((1, D_MODEL, LANES), by_bucket),
                      pl.BlockSpec((1, 1, LANES), by_bucket),
                      up_spec(wsel1), up_spec(wsel1), dn_spec(wsel1),
                      up_spec(wsel2), up_spec(wsel2), dn_spec(wsel2)],
            out_specs=pl.BlockSpec((tile_rows, LANES), lambda j, *_: (j, 0)),
            scratch_shapes=[pltpu.VMEM((2, D_MODEL, D_EXPERT), BF16),
                            pltpu.VMEM((2, D_MODEL, D_EXPERT), BF16),
                            pltpu.VMEM((2, D_EXPERT, D_MODEL), BF16)],
        ),
        out_shape=jax.ShapeDtypeStruct((n_tiles * tile_rows, LANES), F32),
        compiler_params=_cparams(("arbitrary",)),
        name="moe",
    )(*plan, xs, wsel, bsel, w_gate, w_up, w_down, w_gate, w_up, w_down)


def _combine_kernel(pos_ref, ys_ref, x1_ref, gt_ref, lg_ref, lb_ref, out_ref, ybuf, sems):
    ts = x1_ref.shape[0]
    i = pl.program_id(0)
    n = pl.num_programs(0)
    slot = i % 2

    def gather(tile, to_slot):
        def issue(g, carry):
            for u in range(DMA_UNROLL):
                r = g * DMA_UNROLL + u
                src = pl.multiple_of(pos_ref[tile * ts + r] * TOK_SUB, TOK_SUB)
                dst = pl.multiple_of(r * TOK_SUB, TOK_SUB)
                pltpu.make_async_copy(ys_ref.at[pl.ds(src, TOK_SUB), :],
                                      ybuf.at[to_slot, pl.ds(dst, TOK_SUB), :],
                                      sems.at[to_slot]).start(priority=u % 2)
            return carry
        lax.fori_loop(0, ts // DMA_UNROLL, issue, 0)

    @pl.when(i == 0)
    def _():
        gather(0, 0)

    @pl.when(i + 1 < n)
    def _():
        gather(i + 1, 1 - slot)

    pltpu.make_async_copy(ys_ref.at[pl.ds(0, ts * TOK_SUB), :], ybuf.at[slot], sems.at[slot]).wait()
    y = _load_token_major(ybuf.at[slot], ts)
    out_ref[...] = _layer_norm(ALPHA * x1_ref[...] + (1.0 + gt_ref[0]) * y, lg_ref[...], lb_ref[...])


def _combine(pos, ys, x1, mod, layer, lg, lb, bsz, seq):
    ts = min(TOK_TILE, seq)
    nt = seq // ts
    tok = bsz * seq
    return pl.pallas_call(
        _combine_kernel,
        grid_spec=pltpu.PrefetchScalarGridSpec(
            num_scalar_prefetch=1,
            grid=(tok // ts,),
            in_specs=[pl.BlockSpec(memory_space=pl.ANY),
                      pl.BlockSpec((ts, D_MODEL), lambda i, pos: (i, 0)),
                      _mod_spec(layer, MOD_GT2, bsz, lambda i, pos: i // nt),
                      pl.BlockSpec((1, D_MODEL), lambda i, pos: (0, 0)),
                      pl.BlockSpec((1, D_MODEL), lambda i, pos: (0, 0))],
            out_specs=pl.BlockSpec((ts, D_MODEL), lambda i, pos: (i, 0)),
            scratch_shapes=[pltpu.VMEM((2, ts * TOK_SUB, LANES), F32), pltpu.SemaphoreType.DMA((2,))],
        ),
        out_shape=jax.ShapeDtypeStruct((tok, D_MODEL), F32),
        compiler_params=_cparams(("arbitrary",)),
        name="combine",
    )(pos, ys, x1, mod, lg, lb)


def _route_plan(route, cnt, n_tiles, layer):
    bucket = route[0].astype(I32)
    rank = route[1].astype(I32)
    counts = cnt[:N_BUCKETS, 0].astype(I32)
    tiles_b = (counts + MOE_STEP - 1) // MOE_STEP
    tile_end = jnp.cumsum(tiles_b)
    row_off = (tile_end - tiles_b) * MOE_STEP
    bucket_ids = jnp.arange(N_BUCKETS, dtype=I32)
    pos = jnp.sum(jnp.where(bucket[None, :] == bucket_ids[:, None], row_off[:, None], 0), axis=0) + rank
    total = tile_end[-1]
    j = jnp.arange(n_tiles, dtype=I32)
    tb = jnp.sum((tile_end[None, :] <= jnp.minimum(j, total - 1)[:, None]).astype(I32), axis=1)
    tb = jnp.clip(tb, 0, N_BUCKETS - 1)
    hit = tb[None, :] == bucket_ids[:, None]
    rows_left = jnp.sum(jnp.where(hit, (row_off + counts)[:, None], 0), axis=0) - j * MOE_STEP
    tiles_j = jnp.clip((rows_left + MOE_TILE - 1) // MOE_TILE, 0, MOE_STEP // MOE_TILE)
    tiles_j = jnp.where(j < total, tiles_j, 0)
    grp = tb // N_PAIRS
    pair = tb % N_PAIRS
    first = layer * (N_GROUPS * N_EXP) + grp * N_EXP
    s0 = first + sum((pair == i).astype(I32) * _SLOT0[i] for i in range(N_PAIRS))
    s1 = first + sum((pair == i).astype(I32) * _SLOT1[i] for i in range(N_PAIRS))
    one = jnp.ones((1,), I32)
    fresh0 = jnp.concatenate([one, (s0[1:] != s0[:-1]).astype(I32)])
    fresh1 = jnp.concatenate([one, (s1[1:] != s1[:-1]).astype(I32)])
    xblk = jnp.minimum(j, total - 1)
    pads = jnp.concatenate([row_off + counts, tiles_b * MOE_STEP - counts,
                            (total * (MOE_STEP // MOE_TILE))[None]])
    return pos, pads, (s0, s1, tiles_j, fresh0, fresh1, tb, xblk)


def kernel(x, c, w_ada, b_ada, w_in, b_gates, mh_norm_w, w_conv, w_out, ln1_g, ln1_b,
           w_grp, b_grp, w_router, b_router, w_gate, w_up, w_down, ln2_g, ln2_b):
    bsz, seq, _ = x.shape
    tok = bsz * seq
    n_rows = tok + N_BUCKETS * MOE_STEP
    n_tiles = n_rows // MOE_STEP
    n_exp_total = N_GROUPS * N_EXP

    wg_all = w_gate.reshape(DEPTH * n_exp_total, D_MODEL, D_EXPERT)
    wu_all = w_up.reshape(DEPTH * n_exp_total, D_MODEL, D_EXPERT)
    wd_all = w_down.reshape(DEPTH * n_exp_total, D_EXPERT, D_MODEL)
    mod = _ada_mod(c, w_ada, b_ada).reshape(DEPTH * bsz * N_MOD, 1, D_MODEL)
    x2 = x.reshape(tok, D_MODEL)
    for l in range(DEPTH):
        wl = w_in[l]
        wq = wl[:, :4 * M_WIDTH].astype(BF16)
        wg = jnp.pad(wl[:, 4 * M_WIDTH:4 * M_WIDTH + 2 * HEADS],
                     ((0, 0), (0, LANES - 2 * HEADS))).astype(BF16)
        wc = wl[:, 4 * M_WIDTH + 2 * HEADS:].astype(BF16)
        bg = jnp.pad(b_gates[l], (0, LANES - 2 * HEADS)).reshape(1, LANES)
        qkvo, grow, yc = _inproj(x2, mod, l, wq, wg, wc, bg, w_conv[l], bsz, seq)
        n_logit = N_GROUPS + n_exp_total
        wr = jnp.pad(jnp.concatenate([w_grp[l], w_router[l]], axis=1).T,
                     ((0, ROUTE_ROWS - n_logit), (0, 0))).astype(BF16)
        br = jnp.pad(jnp.concatenate([b_grp[l], b_router[l]]), (0, ROUTE_ROWS - n_logit))
        br = jnp.broadcast_to(br[:, None], (ROUTE_ROWS, LANES))
        x1, u2, route, cnt = _mixer(qkvo, grow, mh_norm_w[l].reshape(1, M_WIDTH), yc, x2, mod, l,
                                    w_out[l].astype(BF16), ln1_g[l].reshape(1, D_MODEL),
                                    ln1_b[l].reshape(1, D_MODEL), wr, br, bsz, seq)
        pos, pads, plan = _route_plan(route, cnt, n_tiles, l)
        xs = _dispatch(pos, pads, u2, n_rows)
        wsel = jnp.concatenate([jnp.broadcast_to(w_grp[l], (N_BUCKETS, D_MODEL, N_GROUPS)),
                                w_router[l][:, _BUCKET_S0].T[:, :, None],
                                w_router[l][:, _BUCKET_S1].T[:, :, None]], axis=2)
        wsel = jnp.pad(wsel, ((0, 0), (0, 0), (0, LANES - N_GROUPS - 2))).astype(BF16)
        bsel = jnp.concatenate([jnp.broadcast_to(b_grp[l], (N_BUCKETS, N_GROUPS)),
                                b_router[l][_BUCKET_S0][:, None], b_router[l][_BUCKET_S1][:, None]], axis=1)
        bsel = jnp.pad(bsel, ((0, 0), (0, LANES - N_GROUPS - 2))).reshape(N_BUCKETS, 1, LANES)
        ys = _moe(plan, xs, wsel, bsel, wg_all, wu_all, wd_all, n_tiles)
        x2 = _combine(pos, ys, x1, mod, l, ln2_g[l].reshape(1, D_MODEL),
                      ln2_b[l].reshape(1, D_MODEL), bsz, seq)
    return x2.reshape(bsz, seq, D_MODEL)
```

```python
import functools
import math

import jax
import jax.numpy as jnp
import numpy as np
from jax import lax
from jax.experimental import pallas as pl
from jax.experimental.pallas import tpu as pltpu

F32 = jnp.float32
BF16 = jnp.bfloat16
I32 = jnp.int32

D_MODEL = 1024
DEPTH = 2
HEADS = 4
D_HEAD = 128
M_WIDTH = HEADS * D_HEAD
C_WIDTH = D_MODEL - M_WIDTH
N_GROUPS = 4
N_EXP = 4
D_EXPERT = 512
N_PAIRS = 6
N_BUCKETS = N_GROUPS * N_PAIRS
N_MOD = 6
MOD_SH1, MOD_SC1, MOD_GT1, MOD_SH2, MOD_SC2, MOD_GT2 = range(N_MOD)
ALPHA = (2 * DEPTH) ** 0.25
LN_EPS = 1e-5
QK_SCALE = D_HEAD ** -0.5
LOG_QK_SCALE = math.log(QK_SCALE)
GATE_ROWS = 24

LANES = 128
ROUTE_ROWS = 32

TOK_TILE = 512
INPROJ_TILE = 1024
CHUNK = 256
MOE_TILE = 256
MOE_STEP = 2 * MOE_TILE
VMEM_LIMIT = 56 * 1024 * 1024

_PAIR_ORDER = (0, 2, 3, 1, 4, 5)
_SLOT0 = (0, 2, 2, 3, 3, 3)
_SLOT1 = (1, 1, 0, 0, 1, 2)
_BUCKET_S0 = np.array([g * N_EXP + _SLOT0[p] for g in range(N_GROUPS) for p in range(N_PAIRS)])
_BUCKET_S1 = np.array([g * N_EXP + _SLOT1[p] for g in range(N_GROUPS) for p in range(N_PAIRS)])


def _cparams(sem):
    return pltpu.CompilerParams(dimension_semantics=sem, vmem_limit_bytes=VMEM_LIMIT)


def _ada_kernel(c_ref, w_ref, b_ref, o_ref):
    c = c_ref[...]
    cond = (c * jax.nn.sigmoid(c)).astype(BF16)
    o_ref[0] = jnp.dot(cond, w_ref[0].astype(BF16), preferred_element_type=F32) + b_ref[0]


def _ada_mod(c, w_ada, b_ada):
    bsz = c.shape[0]
    nblk = w_ada.shape[2] // D_MODEL
    return pl.pallas_call(
        _ada_kernel,
        grid=(DEPTH, nblk),
        in_specs=[
            pl.BlockSpec((bsz, D_MODEL), lambda l, j: (0, 0)),
            pl.BlockSpec((1, D_MODEL, D_MODEL), lambda l, j: (l, 0, j)),
            pl.BlockSpec((1, 1, D_MODEL), lambda l, j: (l, 0, j)),
        ],
        out_specs=pl.BlockSpec((1, bsz, D_MODEL), lambda l, j: (l, 0, j)),
        out_shape=jax.ShapeDtypeStruct((DEPTH, bsz, nblk * D_MODEL), F32),
        compiler_params=_cparams(("arbitrary", "arbitrary")),
        name="ada_mod",
    )(c, w_ada, b_ada.reshape(DEPTH, 1, -1))


def _log_sigmoid(x):
    return jnp.minimum(x, 0.0) - jnp.log1p(jnp.exp(-jnp.abs(x)))


def _chunk_scan(x, op, identity):
    lane = lax.broadcasted_iota(I32, x.shape, 1) & (CHUNK - 1)
    sh = 1
    while sh < CHUNK:
        x = op(x, jnp.where(lane >= sh, pltpu.roll(x, sh, 1), identity))
        sh *= 2
    return x


def _inproj_kernel(x_ref, sc_ref, sh_ref, wq_ref, wg_ref, wc_ref, bg_ref, wconv_ref,
                   qkvo_ref, grow_ref, yc_ref, zbuf):
    tm = x_ref.shape[0]
    u = (x_ref[...] * (1.0 + sc_ref[0]) + sh_ref[0]).astype(BF16)
    qkvo_ref[...] = jnp.dot(u, wq_ref[...], preferred_element_type=F32).astype(BF16)
    g = jnp.dot(u, wg_ref[...], preferred_element_type=F32) + bg_ref[...]
    gt = g.T[:8]
    head_row = lax.broadcasted_iota(I32, gt.shape, 0) < HEADS
    ig = jnp.where(head_row, gt, 0.0)
    logf = jnp.where(head_row, _log_sigmoid(pltpu.roll(gt, HEADS, 0)), 0.0)
    bcum = _chunk_scan(logf, jnp.add, 0.0)
    ug = ig - bcum
    grow_ref[0:8, :] = ug
    grow_ref[8:16, :] = bcum
    grow_ref[16:24, :] = _chunk_scan(ug, jnp.maximum, -jnp.inf)
    pc = jnp.dot(u, wc_ref[...], preferred_element_type=F32)
    z = pc[:, C_WIDTH:2 * C_WIDTH] * pc[:, 2 * C_WIDTH:]

    @pl.when(pl.program_id(1) == 0)
    def _():
        zbuf[0:8, :] = jnp.zeros((8, C_WIDTH), F32)

    zbuf[8:8 + tm, :] = z
    zc = (wconv_ref[0:1, :] * zbuf[6:6 + tm, :] + wconv_ref[1:2, :] * zbuf[7:7 + tm, :]
          + wconv_ref[2:3, :] * z)
    yc_ref[...] = (pc[:, :C_WIDTH] * zc).astype(BF16)
    zbuf[0:8, :] = zbuf[tm:tm + 8, :]


def _mod_spec(layer, which, bsz, batch_of):
    return pl.BlockSpec((1, 1, D_MODEL),
                        lambda *idx: ((layer * bsz + batch_of(*idx)) * N_MOD + which, 0, 0))


def _layer_spec(stacked, layer):
    return pl.BlockSpec((None,) + stacked.shape[1:], lambda *idx: (layer, 0, 0))


def _inproj(x2, mod, layer, wq, wg, wc, bg, wconv, bsz, seq):
    tm = min(INPROJ_TILE, seq)
    nt = seq // tm
    tok = bsz * seq
    row = lambda b, s: (b * nt + s, 0)
    const = lambda b, s: (0, 0)
    batch = lambda b, s: b
    return pl.pallas_call(
        _inproj_kernel,
        grid=(bsz, nt),
        in_specs=[
            pl.BlockSpec((tm, D_MODEL), row),
            _mod_spec(layer, MOD_SC1, bsz, batch),
            _mod_spec(layer, MOD_SH1, bsz, batch),
            _layer_spec(wq, layer),
            _layer_spec(wg, layer),
            _layer_spec(wc, layer),
            pl.BlockSpec(bg.shape, const),
            pl.BlockSpec(wconv.shape, const),
        ],
        out_specs=[
            pl.BlockSpec((tm, 4 * M_WIDTH), row),
            pl.BlockSpec((GATE_ROWS, tm), lambda b, s: (0, b * nt + s)),
            pl.BlockSpec((tm, C_WIDTH), row),
        ],
        out_shape=[
            jax.ShapeDtypeStruct((tok, 4 * M_WIDTH), BF16),
            jax.ShapeDtypeStruct((GATE_ROWS, tok), F32),
            jax.ShapeDtypeStruct((tok, C_WIDTH), BF16),
        ],
        scratch_shapes=[pltpu.VMEM((tm + 8, C_WIDTH), F32)],
        compiler_params=_cparams(("arbitrary", "arbitrary")),
        name="inproj",
    )(x2, mod, mod, wq, wg, wc, bg, wconv)


N_ROWQ = 4
SPLIT = 3


def _split3(x):
    hi = x.astype(BF16).astype(F32)
    r1 = x - hi
    mid = r1.astype(BF16).astype(F32)
    return [hi, mid, r1 - mid]


def _mlstm_selector(sel):
    r = lax.broadcasted_iota(I32, sel.shape, 0)
    c = lax.broadcasted_iota(I32, sel.shape, 1) // LANES
    rq = r // 8
    quantity = ((rq >= SPLIT).astype(I32) + (rq >= 2 * SPLIT).astype(I32)
                + (rq >= 3 * SPLIT).astype(I32))
    hit = ((r % 8) == c // N_ROWQ) & (quantity == c % N_ROWQ) & (rq < N_ROWQ * SPLIT)
    sel[...] = hit.astype(BF16)


def _mlstm_block(qkvo_ref, grow_ref, nw_ref, out_ref, cst, mst, sel, between=()):
    between = tuple(between) + (None,) * 3
    sb = qkvo_ref.shape[0]
    L = CHUNK
    causal = lax.broadcasted_iota(I32, (L, L), 0) >= lax.broadcasted_iota(I32, (L, L), 1)
    ones_ext = jnp.ones((L, D_HEAD), BF16)
    pad_rows = jnp.zeros((LANES - 8 * N_ROWQ * SPLIT, L), F32)
    eye = (lax.broadcasted_iota(I32, (D_HEAD, D_HEAD), 0)
           == lax.broadcasted_iota(I32, (D_HEAD, D_HEAD), 1)).astype(BF16)
    n_chunks = sb // L
    pairs = [(c, h) for c in range(n_chunks) for h in range(HEADS)]

    def cols(part, c, h):
        c0 = part * M_WIDTH + h * D_HEAD
        return qkvo_ref[c * L:(c + 1) * L, c0:c0 + D_HEAD]

    m_prev = mst[...]
    stack_t, us, decay = [], [], []
    for c in range(n_chunks):
        ug = grow_ref[0:8, c * L:(c + 1) * L]
        bcum = grow_ref[8:16, c * L:(c + 1) * L]
        cmax = grow_ref[16:24, c * L:(c + 1) * L]
        big_m = jnp.maximum(m_prev, cmax)
        m_last = jnp.broadcast_to(big_m[:, L - 1:L], (8, L))
        g_tot = jnp.broadcast_to(bcum[:, L - 1:L], (8, L))
        wq = jnp.exp(m_prev - big_m) * QK_SCALE
        log_em = -(bcum + big_m)
        wk = jnp.exp(ug - m_last)
        decay.append(jnp.exp(m_prev - m_last))
        us.append(ug + LOG_QK_SCALE)
        m_prev = g_tot + m_last
        stack = jnp.concatenate(_split3(big_m) + _split3(wq) + _split3(log_em) + _split3(wk)
                                + [pad_rows], axis=0)
        stack_t.append(stack.T.astype(BF16))
    mst[...] = m_prev
    if between[0] is not None:
        between[0]()

    reps = {}

    def rep(c, h, j):
        if (c, h) not in reps:
            c0 = h * N_ROWQ * LANES
            reps[(c, h)] = jnp.dot(stack_t[c], sel[:, c0:c0 + N_ROWQ * LANES],
                                   preferred_element_type=F32)
        return reps[(c, h)][:, j * LANES:(j + 1) * LANES]

    s_mat = {ch: lax.dot_general(cols(0, *ch), cols(1, *ch), (((1,), (1,)), ((), ())),
                                 preferred_element_type=F32) for ch in pairs}
    kw = {ch: (cols(1, *ch).astype(F32) * rep(*ch, 3)).astype(BF16) for ch in pairs}
    kw_t = {ch: lax.dot_general(eye, kw[ch], (((1,), (1,)), ((), ())),
                                preferred_element_type=F32).astype(BF16) for ch in pairs}
    if between[1] is not None:
        between[1]()
    lhs = {}
    for ch in pairs:
        c, h = ch
        big_m_rows = jnp.concatenate([rep(c, h, 0)] * (L // LANES), axis=1)
        p = jnp.where(causal, s_mat[ch] * jnp.exp(us[c][h:h + 1, :] - big_m_rows), 0.0)
        qw = cols(0, c, h).astype(F32) * rep(c, h, 1)
        lhs[ch] = jnp.concatenate([p.astype(BF16), qw.astype(BF16)], axis=1)
    vext = {ch: jnp.concatenate([cols(2, *ch), ones_ext], axis=1) for ch in pairs}
    upd = {ch: jnp.dot(kw_t[ch], vext[ch], preferred_element_type=F32) for ch in pairs}
    for c in range(n_chunks):
        tot = {}
        for h in range(HEADS):
            state = cst[h]
            rhs = jnp.concatenate([vext[(c, h)], state.astype(BF16)], axis=0)
            tot[h] = jnp.dot(lhs[(c, h)], rhs, preferred_element_type=F32)
            dec = jnp.concatenate([decay[c][h:h + 1, :LANES]] * 2, axis=1)
            cst[h] = dec * state + upd[(c, h)]
        for h in range(HEADS):
            c0 = h * D_HEAD
            num, den = tot[h][:, :D_HEAD], tot[h][:, D_HEAD:]
            hh = num / jnp.maximum(jnp.abs(den), jnp.exp(rep(c, h, 2)))
            mu = jnp.mean(hh, axis=1, keepdims=True)
            var = jnp.mean(jnp.square(hh - mu), axis=1, keepdims=True)
            hn = (hh - mu) * lax.rsqrt(var + LN_EPS) * nw_ref[:, c0:c0 + D_HEAD]
            og = jax.nn.sigmoid(cols(3, c, h).astype(F32))
            out_ref[c * L:(c + 1) * L, c0:c0 + D_HEAD] = (hn * og).astype(BF16)
        if c == 0 and between[2] is not None:
            between[2]()


def _layer_norm(r, g, b):
    mu = jnp.mean(r, axis=-1, keepdims=True)
    var = jnp.mean(jnp.square(r - mu), axis=-1, keepdims=True)
    return (r - mu) * lax.rsqrt(var + LN_EPS) * g + b


TOK_SUB = D_MODEL // LANES


def _store_token_major(ref, val, first_row=0):
    rows = val.shape[0]
    for k in range(TOK_SUB):
        ref[pl.ds(first_row * TOK_SUB + k, rows, stride=TOK_SUB), :] = val[:, k * LANES:(k + 1) * LANES]


def _load_token_major(ref, rows, first_row=0):
    return jnp.concatenate([ref[pl.ds(first_row * TOK_SUB + k, rows, stride=TOK_SUB), :]
                            for k in range(TOK_SUB)], axis=1)


def _outproj_parts(hn_ref, yc_ref, x_ref, gt_ref, sc_ref, sh_ref, wo_ref, lg_ref, lb_ref,
                   wr_ref, br_ref, x1_ref, u2_ref, route_ref, cnt_ref, base, triu, live):
    tm = x_ref.shape[0]
    held = {}

    def project():
        held["mix"] = (jnp.dot(hn_ref[...], wo_ref[0:M_WIDTH, :], preferred_element_type=F32)
                       + jnp.dot(yc_ref[...], wo_ref[M_WIDTH:, :], preferred_element_type=F32))

    def normalise():
        x1 = _layer_norm(ALPHA * x_ref[...] + (1.0 + gt_ref[0]) * held["mix"], lg_ref[...], lb_ref[...])
        x1_ref[...] = x1
        u2 = x1 * (1.0 + sc_ref[0]) + sh_ref[0]
        _store_token_major(u2_ref, u2)
        held["logits"] = lax.dot_general(wr_ref[...], u2.astype(BF16), (((1,), (1,)), ((), ())),
                                         preferred_element_type=F32) + br_ref[:, 0:1]

    def route():
        _route_block(held["logits"], route_ref, cnt_ref, base, triu, live, tm)

    return project, normalise, route


def _route_block(lt, route_ref, cnt_ref, base, triu, live, tm):
    lg = [lt[j:j + 1, :] for j in range(N_GROUPS + N_GROUPS * N_EXP)]
    best = lg[0]
    grp = jnp.zeros((1, tm), I32)
    for j in range(1, N_GROUPS):
        c = lg[j] > best
        grp = jnp.where(c, j, grp)
        best = jnp.where(c, lg[j], best)
    sel = []
    for e in range(N_EXP):
        val = lg[N_GROUPS + e]
        for gg in range(1, N_GROUPS):
            val = jnp.where(grp == gg, lg[N_GROUPS + gg * N_EXP + e], val)
        sel.append(val)
    v1 = sel[0]
    i1 = jnp.zeros((1, tm), I32)
    for e in range(1, N_EXP):
        c = sel[e] > v1
        i1 = jnp.where(c, e, i1)
        v1 = jnp.where(c, sel[e], v1)
    v2 = jnp.full((1, tm), -jnp.inf, F32)
    i2 = jnp.zeros((1, tm), I32)
    for e in range(N_EXP):
        cand = jnp.where(i1 == e, -jnp.inf, sel[e])
        c = cand > v2
        i2 = jnp.where(c, e, i2)
        v2 = jnp.where(c, cand, v2)
    ea = jnp.minimum(i1, i2)
    eb = jnp.maximum(i1, i2)
    lex = jnp.where(ea == 0, 0, jnp.where(ea == 1, 3, 5)) + eb - ea - 1
    pair = lex
    for i, p in enumerate(_PAIR_ORDER):
        if i != p:
            pair = jnp.where(lex == i, p, pair)
    bucket = grp * N_PAIRS + pair

    onehot = lax.broadcasted_iota(I32, (ROUTE_ROWS, tm), 0) == bucket
    cum = jnp.dot(onehot.astype(BF16), triu[...], preferred_element_type=F32)
    prev = base[:, 0:1]
    rank = jnp.sum(jnp.where(onehot, cum - 1.0 + prev, 0.0), axis=0, keepdims=True)
    new_base = prev + live * cum[:, tm - 1:tm]
    base[...] = jnp.broadcast_to(new_base, base.shape)
    cnt_ref[...] = jnp.broadcast_to(new_base, cnt_ref.shape)

    zrow = jnp.zeros((1, tm), F32)
    route_ref[...] = jnp.concatenate(
        [bucket.astype(F32), rank, zrow, zrow, zrow, zrow, zrow, zrow], axis=0)


def _mixer_kernel(qkvo_ref, grow_ref, nw_ref, yc_ref, x_ref, gt_ref, sc_ref, sh_ref, wo_ref, lg_ref,
                  lb_ref, wr_ref, br_ref, x1_ref, u2_ref, route_ref, cnt_ref,
                  cst, mst, sel, hn, base, triu, *, blocks_per_seq):
    i = pl.program_id(0)
    tm = x_ref.shape[0]

    @pl.when(i == 0)
    def _():
        _mlstm_selector(sel)
        hn[...] = jnp.zeros(hn.shape, BF16)
        base[...] = jnp.zeros(base.shape, F32)
        rid = lax.broadcasted_iota(I32, triu.shape, 0)
        cid = lax.broadcasted_iota(I32, triu.shape, 1)
        triu[...] = (rid <= cid).astype(BF16)

    @pl.when(i % blocks_per_seq == 0)
    def _():
        cst[...] = jnp.zeros(cst.shape, F32)
        mst[...] = jnp.zeros(mst.shape, F32)

    live = jnp.where(i > 0, 1.0, 0.0).astype(F32)
    parts = _outproj_parts(hn, yc_ref, x_ref, gt_ref, sc_ref, sh_ref, wo_ref, lg_ref, lb_ref,
                           wr_ref, br_ref, x1_ref, u2_ref, route_ref, cnt_ref, base, triu, live)
    _mlstm_block(qkvo_ref, grow_ref, nw_ref, hn, cst, mst, sel, between=parts)


def _mixer(qkvo, grow, nw, yc, x2, mod, layer, wo, lg, lb, wr, br, bsz, seq):
    tm = min(TOK_TILE, seq)
    nt = seq // tm
    tok = bsz * seq
    n = bsz * nt
    cur = lambda i: (jnp.minimum(i, n - 1), 0)
    prev = lambda i: (jnp.maximum(i - 1, 0), 0)
    const = lambda i: (0, 0)
    batch = lambda i: jnp.maximum(i - 1, 0) // nt
    return pl.pallas_call(
        functools.partial(_mixer_kernel, blocks_per_seq=nt),
        grid=(n + 1,),
        in_specs=[
            pl.BlockSpec((tm, 4 * M_WIDTH), cur),
            pl.BlockSpec((GATE_ROWS, tm), lambda i: (0, jnp.minimum(i, n - 1))),
            pl.BlockSpec((1, M_WIDTH), const),
            pl.BlockSpec((tm, C_WIDTH), prev),
            pl.BlockSpec((tm, D_MODEL), prev),
            _mod_spec(layer, MOD_GT1, bsz, batch),
            _mod_spec(layer, MOD_SC2, bsz, batch),
            _mod_spec(layer, MOD_SH2, bsz, batch),
            _layer_spec(wo, layer),
            pl.BlockSpec(lg.shape, const),
            pl.BlockSpec(lb.shape, const),
            pl.BlockSpec(wr.shape, const),
            pl.BlockSpec(br.shape, const),
        ],
        out_specs=[
            pl.BlockSpec((tm, D_MODEL), prev),
            pl.BlockSpec((tm * TOK_SUB, LANES), prev),
            pl.BlockSpec((8, tm), lambda i: (0, jnp.maximum(i - 1, 0))),
            pl.BlockSpec((ROUTE_ROWS, LANES), const),
        ],
        out_shape=[
            jax.ShapeDtypeStruct((tok, D_MODEL), F32),
            jax.ShapeDtypeStruct((tok * TOK_SUB, LANES), F32),
            jax.ShapeDtypeStruct((8, tok), F32),
            jax.ShapeDtypeStruct((ROUTE_ROWS, LANES), F32),
        ],
        scratch_shapes=[pltpu.VMEM((HEADS, D_HEAD, 2 * D_HEAD), F32),
                        pltpu.VMEM((8, CHUNK), F32),
                        pltpu.VMEM((LANES, HEADS * N_ROWQ * LANES), BF16),
                        pltpu.VMEM((tm, M_WIDTH), BF16),
                        pltpu.VMEM((ROUTE_ROWS, LANES), F32),
                        pltpu.VMEM((tm, tm), BF16)],
        compiler_params=_cparams(("arbitrary",)),
        name="mixer",
    )(qkvo, grow, nw, yc, x2, mod, mod, mod, wo, lg, lb, wr, br)


DISPATCH_TILE = 2048
DMA_UNROLL = 8


def _dispatch_kernel(pos_ref, pad_ref, u_ref, xs_ref, zeros, sem, zsem):
    ts = u_ref.shape[0] // TOK_SUB
    t0 = pl.program_id(0) * ts

    def zero_rows(first_row, n_rows_static):
        dst = pl.multiple_of(first_row * TOK_SUB, TOK_SUB)
        return pltpu.make_async_copy(zeros.at[pl.ds(0, n_rows_static * TOK_SUB), :],
                                     xs_ref.at[pl.ds(dst, n_rows_static * TOK_SUB), :], zsem)

    def fill_pads(wait):
        def bucket(b, carry):
            row = pad_ref[b]
            n = pad_ref[N_BUCKETS + b]
            size = MOE_STEP // 2
            while size >= 1:
                @pl.when((n & size) != 0)
                def _(row=row, size=size):
                    cp = zero_rows(row, size)
                    cp.wait() if wait else cp.start()
                row = row + (n & size)
                size //= 2
            return carry
        lax.fori_loop(0, N_BUCKETS, bucket, 0)

        def idle_tile(j, carry):
            cp = zero_rows(j * MOE_TILE, MOE_TILE)
            cp.wait() if wait else cp.start()
            return carry
        lax.fori_loop(pad_ref[2 * N_BUCKETS], xs_ref.shape[0] // (MOE_TILE * TOK_SUB), idle_tile, 0)

    @pl.when(pl.program_id(0) == 0)
    def _():
        zeros[...] = jnp.zeros(zeros.shape, F32)
        fill_pads(wait=False)
        fill_pads(wait=True)

    def issue(g, carry):
        for u in range(DMA_UNROLL):
            r = g * DMA_UNROLL + u
            dst = pl.multiple_of(pos_ref[t0 + r] * TOK_SUB, TOK_SUB)
            src = pl.multiple_of(r * TOK_SUB, TOK_SUB)
            pltpu.make_async_copy(u_ref.at[pl.ds(src, TOK_SUB), :],
                                  xs_ref.at[pl.ds(dst, TOK_SUB), :], sem).start(priority=u % 2)
        return carry

    lax.fori_loop(0, ts // DMA_UNROLL, issue, 0)
    pltpu.make_async_copy(u_ref, xs_ref.at[pl.ds(0, ts * TOK_SUB), :], sem).wait()


def _dispatch(pos, pad_start, u2, n_rows):
    tok = u2.shape[0] // TOK_SUB
    ts = min(DISPATCH_TILE, tok)
    return pl.pallas_call(
        _dispatch_kernel,
        grid_spec=pltpu.PrefetchScalarGridSpec(
            num_scalar_prefetch=2,
            grid=(tok // ts,),
            in_specs=[pl.BlockSpec((ts * TOK_SUB, LANES), lambda i, pos, pad: (i, 0))],
            out_specs=pl.BlockSpec(memory_space=pl.ANY),
            scratch_shapes=[pltpu.VMEM((MOE_TILE * TOK_SUB, LANES), F32),
                            pltpu.SemaphoreType.DMA(()), pltpu.SemaphoreType.DMA(())],
        ),
        out_shape=jax.ShapeDtypeStruct((n_rows * TOK_SUB, LANES), F32),
        compiler_params=_cparams(("arbitrary",)),
        name="dispatch",
    )(pos, pad_start, u2)


def _moe_kernel(s0_ref, s1_ref, tiles_ref, fresh0_ref, fresh1_ref, tb_ref, xb_ref, xs_ref, wsel_ref,
                bsel_ref, wg0, wu0, wd0, wg1, wu1, wd1, y_ref, wgs, wus, wds):
    del s0_ref, s1_ref, tb_ref, xb_ref
    j = pl.program_id(0)
    rows = MOE_TILE

    @pl.when(fresh0_ref[j] == 1)
    def _():
        wgs[0] = wg0[0].astype(BF16)
        wus[0] = wu0[0].astype(BF16)
        wds[0] = wd0[0].astype(BF16)

    @pl.when(fresh1_ref[j] == 1)
    def _():
        wgs[1] = wg1[0].astype(BF16)
        wus[1] = wu1[0].astype(BF16)
        wds[1] = wd1[0].astype(BF16)

    def experts(part):
        x = _load_token_major(xs_ref, rows, part * rows).astype(BF16)
        lg = jnp.dot(x, wsel_ref[0], preferred_element_type=F32) + bsel_ref[0]
        grp = [lg[:, i:i + 1] for i in range(N_GROUPS)]
        gmax = jnp.maximum(jnp.maximum(grp[0], grp[1]), jnp.maximum(grp[2], grp[3]))
        gsum = sum(jnp.exp(v - gmax) for v in grp)
        la, lb = lg[:, N_GROUPS:N_GROUPS + 1], lg[:, N_GROUPS + 1:N_GROUPS + 2]
        emax = jnp.maximum(la, lb)
        ea, eb = jnp.exp(la - emax), jnp.exp(lb - emax)
        scale = 1.0 / (gsum * (ea + eb))
        wcols = (ea * scale, eb * scale)
        acc = None
        for slot in range(2):
            g = jnp.dot(x, wgs[slot], preferred_element_type=F32)
            u = jnp.dot(x, wus[slot], preferred_element_type=F32)
            hid = (g * jax.nn.sigmoid(g) * u * wcols[slot]).astype(BF16)
            y = jnp.dot(hid, wds[slot], preferred_element_type=F32)
            acc = y if acc is None else acc + y
        _store_token_major(y_ref, acc, part * rows)

    for part in range(MOE_STEP // MOE_TILE):
        @pl.when(tiles_ref[j] > part)
        def _(part=part):
            experts(part)

        @pl.when(tiles_ref[j] <= part)
        def _(part=part):
            lo = part * rows * TOK_SUB
            y_ref[lo:lo + rows * TOK_SUB, :] = jnp.zeros((rows * TOK_SUB, LANES), F32)


def _moe(plan, xs, wsel, bsel, w_gate, w_up, w_down, n_tiles):
    wsel1 = lambda j, s0, s1, v, f0, f1, tb, xb: (s0[j], 0, 0)
    wsel2 = lambda j, s0, s1, v, f0, f1, tb, xb: (s1[j], 0, 0)
    by_bucket = lambda j, s0, s1, v, f0, f1, tb, xb: (tb[j], 0, 0)
    up_spec = lambda sel: pl.BlockSpec((1, D_MODEL, D_EXPERT), sel)
    dn_spec = lambda sel: pl.BlockSpec((1, D_EXPERT, D_MODEL), sel)
    tile_rows = MOE_STEP * TOK_SUB
    return pl.pallas_call(
        _moe_kernel,
        grid_spec=pltpu.PrefetchScalarGridSpec(
            num_scalar_prefetch=7,
            grid=(n_tiles,),
            in_specs=[pl.BlockSpec((tile_rows, LANES), lambda j, s0, s1, v, f0, f1, tb, xb: (xb[j], 0)),
                      pl.BlockSpec((1, D_MODEL, LANES), by_bucket),
                      pl.BlockSpec((1, 1, LANES), by_bucket),
                      up_spec(wsel1), up_spec(wsel1), dn_spec(wsel1),
                      up_spec(wsel2), up_spec(wsel2), dn_spec(wsel2)],
            out_specs=pl.BlockSpec((tile_rows, LANES), lambda j, *_: (j, 0)),
            scratch_shapes=[pltpu.VMEM((2, D_MODEL, D_EXPERT), BF16),
                            pltpu.VMEM((2, D_MODEL, D_EXPERT), BF16),
                            pltpu.VMEM((2, D_EXPERT, D_MODEL), BF16)],
        ),
        out_shape=jax.ShapeDtypeStruct((n_tiles * tile_rows, LANES), F32),
        compiler_params=_cparams(("arbitrary",)),
        name="moe",
    )(*plan, xs, wsel, bsel, w_gate, w_up, w_down, w_gate, w_up, w_down)


def _combine_kernel(pos_ref, ys_ref, x1_ref, gt_ref, lg_ref, lb_ref, out_ref, ybuf, sems):
    ts = x1_ref.shape[0]
    i = pl.program_id(0)
    n = pl.num_programs(0)
    slot = i % 2

    def gather(tile, to_slot):
        def issue(g, carry):
            for u in range(DMA_UNROLL):
                r = g * DMA_UNROLL + u
                src = pl.multiple_of(pos_ref[tile * ts + r] * TOK_SUB, TOK_SUB)
                dst = pl.multiple_of(r * TOK_SUB, TOK_SUB)
                pltpu.make_async_copy(ys_ref.at[pl.ds(src, TOK_SUB), :],
                                      ybuf.at[to_slot, pl.ds(dst, TOK_SUB), :],
                                      sems.at[to_slot]).start(priority=u % 2)
            return carry
        lax.fori_loop(0, ts // DMA_UNROLL, issue, 0)

    @pl.when(i == 0)
    def _():
        gather(0, 0)

    @pl.when(i + 1 < n)
    def _():
        gather(i + 1, 1 - slot)

    pltpu.make_async_copy(ys_ref.at[pl.ds(0, ts * TOK_SUB), :], ybuf.at[slot], sems.at[slot]).wait()
    y = _load_token_major(ybuf.at[slot], ts)
    out_ref[...] = _layer_norm(ALPHA * x1_ref[...] + (1.0 + gt_ref[0]) * y, lg_ref[...], lb_ref[...])


def _combine(pos, ys, x1, mod, layer, lg, lb, bsz, seq):
    ts = min(TOK_TILE, seq)
    nt = seq // ts
    tok = bsz * seq
    return pl.pallas_call(
        _combine_kernel,
        grid_spec=pltpu.PrefetchScalarGridSpec(
            num_scalar_prefetch=1,
            grid=(tok // ts,),
            in_specs=[pl.BlockSpec(memory_space=pl.ANY),
                      pl.BlockSpec((ts, D_MODEL), lambda i, pos: (i, 0)),
                      _mod_spec(layer, MOD_GT2, bsz, lambda i, pos: i // nt),
                      pl.BlockSpec((1, D_MODEL), lambda i, pos: (0, 0)),
                      pl.BlockSpec((1, D_MODEL), lambda i, pos: (0, 0))],
            out_specs=pl.BlockSpec((ts, D_MODEL), lambda i, pos: (i, 0)),
            scratch_shapes=[pltpu.VMEM((2, ts * TOK_SUB, LANES), F32), pltpu.SemaphoreType.DMA((2,))],
        ),
        out_shape=jax.ShapeDtypeStruct((tok, D_MODEL), F32),
        compiler_params=_cparams(("arbitrary",)),
        name="combine",
    )(pos, ys, x1, mod, lg, lb)


def _route_plan(route, cnt, n_tiles, layer):
    bucket = route[0].astype(I32)
    rank = route[1].astype(I32)
    counts = cnt[:N_BUCKETS, 0].astype(I32)
    tiles_b = (counts + MOE_STEP - 1) // MOE_STEP
    tile_end = jnp.cumsum(tiles_b)
    row_off = (tile_end - tiles_b) * MOE_STEP
    bucket_ids = jnp.arange(N_BUCKETS, dtype=I32)
    pos = jnp.sum(jnp.where(bucket[None, :] == bucket_ids[:, None], row_off[:, None], 0), axis=0) + rank
    total = tile_end[-1]
    j = jnp.arange(n_tiles, dtype=I32)
    tb = jnp.sum((tile_end[None, :] <= jnp.minimum(j, total - 1)[:, None]).astype(I32), axis=1)
    tb = jnp.clip(tb, 0, N_BUCKETS - 1)
    hit = tb[None, :] == bucket_ids[:, None]
    rows_left = jnp.sum(jnp.where(hit, (row_off + counts)[:, None], 0), axis=0) - j * MOE_STEP
    tiles_j = jnp.clip((rows_left + MOE_TILE - 1) // MOE_TILE, 0, MOE_STEP // MOE_TILE)
    tiles_j = jnp.where(j < total, tiles_j, 0)
    grp = tb // N_PAIRS
    pair = tb % N_PAIRS
    first = layer * (N_GROUPS * N_EXP) + grp * N_EXP
    s0 = first + sum((pair == i).astype(I32) * _SLOT0[i] for i in range(N_PAIRS))
    s1 = first + sum((pair == i).astype(I32) * _SLOT1[i] for i in range(N_PAIRS))
    one = jnp.ones((1,), I32)
    fresh0 = jnp.concatenate([one, (s0[1:] != s0[:-1]).astype(I32)])
    fresh1 = jnp.concatenate([one, (s1[1:] != s1[:-1]).astype(I32)])
    xblk = jnp.minimum(j, total - 1)
    pads = jnp.concatenate([row_off + counts, tiles_b * MOE_STEP - counts,
                            (total * (MOE_STEP // MOE_TILE))[None]])
    return pos, pads, (s0, s1, tiles_j, fresh0, fresh1, layer * N_BUCKETS + tb, xblk)


def kernel(x, c, w_ada, b_ada, w_in, b_gates, mh_norm_w, w_conv, w_out, ln1_g, ln1_b,
           w_grp, b_grp, w_router, b_router, w_gate, w_up, w_down, ln2_g, ln2_b):
    bsz, seq, _ = x.shape
    tok = bsz * seq
    n_rows = tok + N_BUCKETS * MOE_STEP
    n_tiles = n_rows // MOE_STEP
    n_exp_total = N_GROUPS * N_EXP

    wg_all = w_gate.reshape(DEPTH * n_exp_total, D_MODEL, D_EXPERT)
    wu_all = w_up.reshape(DEPTH * n_exp_total, D_MODEL, D_EXPERT)
    wd_all = w_down.reshape(DEPTH * n_exp_total, D_EXPERT, D_MODEL)
    mod = _ada_mod(c, w_ada, b_ada).reshape(DEPTH * bsz * N_MOD, 1, D_MODEL)
    x2 = x.reshape(tok, D_MODEL)
    wq = w_in[:, :, :4 * M_WIDTH].astype(BF16)
    wg = jnp.pad(w_in[:, :, 4 * M_WIDTH:4 * M_WIDTH + 2 * HEADS],
                 ((0, 0), (0, 0), (0, LANES - 2 * HEADS))).astype(BF16)
    wc = w_in[:, :, 4 * M_WIDTH + 2 * HEADS:].astype(BF16)
    wo = w_out.astype(BF16)
    lb_shape = (DEPTH, N_BUCKETS)
    wsel = jnp.concatenate([jnp.broadcast_to(w_grp[:, None], lb_shape + (D_MODEL, N_GROUPS)),
                            jnp.swapaxes(w_router[:, :, _BUCKET_S0], 1, 2)[..., None],
                            jnp.swapaxes(w_router[:, :, _BUCKET_S1], 1, 2)[..., None]], axis=3)
    wsel = jnp.pad(wsel, ((0, 0), (0, 0), (0, 0), (0, LANES - N_GROUPS - 2))).astype(BF16)
    wsel = wsel.reshape(DEPTH * N_BUCKETS, D_MODEL, LANES)
    bsel = jnp.concatenate([jnp.broadcast_to(b_grp[:, None], lb_shape + (N_GROUPS,)),
                            b_router[:, _BUCKET_S0][..., None], b_router[:, _BUCKET_S1][..., None]], axis=2)
    bsel = jnp.pad(bsel, ((0, 0), (0, 0), (0, LANES - N_GROUPS - 2))).reshape(DEPTH * N_BUCKETS, 1, LANES)
    for l in range(DEPTH):
        bg = jnp.pad(b_gates[l], (0, LANES - 2 * HEADS)).reshape(1, LANES)
        qkvo, grow, yc = _inproj(x2, mod, l, wq, wg, wc, bg, w_conv[l], bsz, seq)
        n_logit = N_GROUPS + n_exp_total
        wr = jnp.pad(jnp.concatenate([w_grp[l], w_router[l]], axis=1).T,
                     ((0, ROUTE_ROWS - n_logit), (0, 0))).astype(BF16)
        br = jnp.pad(jnp.concatenate([b_grp[l], b_router[l]]), (0, ROUTE_ROWS - n_logit))
        br = jnp.broadcast_to(br[:, None], (ROUTE_ROWS, LANES))
        x1, u2, route, cnt = _mixer(qkvo, grow, mh_norm_w[l].reshape(1, M_WIDTH), yc, x2, mod, l,
                                    wo, ln1_g[l].reshape(1, D_MODEL),
                                    ln1_b[l].reshape(1, D_MODEL), wr, br, bsz, seq)
        pos, pads, plan = _route_plan(route, cnt, n_tiles, l)
        xs = _dispatch(pos, pads, u2, n_rows)
        ys = _moe(plan, xs, wsel, bsel, wg_all, wu_all, wd_all, n_tiles)
        x2 = _combine(pos, ys, x1, mod, l, ln2_g[l].reshape(1, D_MODEL),
                      ln2_b[l].reshape(1, D_MODEL), bsz, seq)
    return x2.reshape(bsz, seq, D_MODEL)
```

```python
import functools
import math

import jax
import jax.numpy as jnp
import numpy as np
from jax import lax
from jax.experimental import pallas as pl
from jax.experimental.pallas import tpu as pltpu

F32 = jnp.float32
BF16 = jnp.bfloat16
I32 = jnp.int32

D_MODEL = 1024
DEPTH = 2
HEADS = 4
D_HEAD = 128
M_WIDTH = HEADS * D_HEAD
C_WIDTH = D_MODEL - M_WIDTH
N_GROUPS = 4
N_EXP = 4
D_EXPERT = 512
N_PAIRS = 6
N_BUCKETS = N_GROUPS * N_PAIRS
N_MOD = 6
MOD_SH1, MOD_SC1, MOD_GT1, MOD_SH2, MOD_SC2, MOD_GT2 = range(N_MOD)
ALPHA = (2 * DEPTH) ** 0.25
LN_EPS = 1e-5
QK_SCALE = D_HEAD ** -0.5
LOG_QK_SCALE = math.log(QK_SCALE)
GATE_ROWS = 24

LANES = 128
ROUTE_ROWS = 32

TOK_TILE = 512
INPROJ_TILE = 1024
CHUNK = 256
MOE_TILE = 256
MOE_STEP = 2 * MOE_TILE
VMEM_LIMIT = 56 * 1024 * 1024

_PAIR_ORDER = (0, 2, 3, 1, 4, 5)
_SLOT0 = (0, 2, 2, 3, 3, 3)
_SLOT1 = (1, 1, 0, 0, 1, 2)
_BUCKET_S0 = np.array([g * N_EXP + _SLOT0[p] for g in range(N_GROUPS) for p in range(N_PAIRS)])
_BUCKET_S1 = np.array([g * N_EXP + _SLOT1[p] for g in range(N_GROUPS) for p in range(N_PAIRS)])


def _cparams(sem):
    return pltpu.CompilerParams(dimension_semantics=sem, vmem_limit_bytes=VMEM_LIMIT)


def _ada_kernel(c_ref, w_ref, b_ref, o_ref):
    c = c_ref[...]
    cond = (c * jax.nn.sigmoid(c)).astype(BF16)
    o_ref[0] = jnp.dot(cond, w_ref[0].astype(BF16), preferred_element_type=F32) + b_ref[0]


def _ada_mod(c, w_ada, b_ada):
    bsz = c.shape[0]
    nblk = w_ada.shape[2] // D_MODEL
    return pl.pallas_call(
        _ada_kernel,
        grid=(DEPTH, nblk),
        in_specs=[
            pl.BlockSpec((bsz, D_MODEL), lambda l, j: (0, 0)),
            pl.BlockSpec((1, D_MODEL, D_MODEL), lambda l, j: (l, 0, j)),
            pl.BlockSpec((1, 1, D_MODEL), lambda l, j: (l, 0, j)),
        ],
        out_specs=pl.BlockSpec((1, bsz, D_MODEL), lambda l, j: (l, 0, j)),
        out_shape=jax.ShapeDtypeStruct((DEPTH, bsz, nblk * D_MODEL), F32),
        compiler_params=_cparams(("arbitrary", "arbitrary")),
        name="ada_mod",
    )(c, w_ada, b_ada.reshape(DEPTH, 1, -1))


def _log_sigmoid(x):
    return jnp.minimum(x, 0.0) - jnp.log1p(jnp.exp(-jnp.abs(x)))


def _chunk_scan(x, op, identity):
    lane = lax.broadcasted_iota(I32, x.shape, 1) & (CHUNK - 1)
    sh = 1
    while sh < CHUNK:
        x = op(x, jnp.where(lane >= sh, pltpu.roll(x, sh, 1), identity))
        sh *= 2
    return x


def _inproj_kernel(x_ref, sc_ref, sh_ref, wq_ref, wg_ref, wc_ref, bg_ref, wconv_ref,
                   qkvo_ref, grow_ref, yc_ref, zbuf):
    tm = x_ref.shape[0]
    u = (x_ref[...] * (1.0 + sc_ref[0]) + sh_ref[0]).astype(BF16)
    qkvo_ref[...] = jnp.dot(u, wq_ref[...], preferred_element_type=F32).astype(BF16)
    g = jnp.dot(u, wg_ref[...], preferred_element_type=F32) + bg_ref[...]
    gt = g.T[:8]
    head_row = lax.broadcasted_iota(I32, gt.shape, 0) < HEADS
    ig = jnp.where(head_row, gt, 0.0)
    logf = jnp.where(head_row, _log_sigmoid(pltpu.roll(gt, HEADS, 0)), 0.0)
    bcum = _chunk_scan(logf, jnp.add, 0.0)
    ug = ig - bcum
    grow_ref[0:8, :] = ug
    grow_ref[8:16, :] = bcum
    grow_ref[16:24, :] = _chunk_scan(ug, jnp.maximum, -jnp.inf)
    pc = jnp.dot(u, wc_ref[...], preferred_element_type=F32)
    z = pc[:, C_WIDTH:2 * C_WIDTH] * pc[:, 2 * C_WIDTH:]

    @pl.when(pl.program_id(1) == 0)
    def _():
        zbuf[0:8, :] = jnp.zeros((8, C_WIDTH), F32)

    zbuf[8:8 + tm, :] = z
    zc = (wconv_ref[0:1, :] * zbuf[6:6 + tm, :] + wconv_ref[1:2, :] * zbuf[7:7 + tm, :]
          + wconv_ref[2:3, :] * z)
    yc_ref[...] = (pc[:, :C_WIDTH] * zc).astype(BF16)
    zbuf[0:8, :] = zbuf[tm:tm + 8, :]


def _mod_spec(layer, which, bsz, batch_of):
    return pl.BlockSpec((1, 1, D_MODEL),
                        lambda *idx: ((layer * bsz + batch_of(*idx)) * N_MOD + which, 0, 0))


def _layer_spec(stacked, layer):
    return pl.BlockSpec((None,) + stacked.shape[1:], lambda *idx: (layer, 0, 0))


def _inproj(x2, mod, layer, wq, wg, wc, bg, wconv, bsz, seq):
    tm = min(INPROJ_TILE, seq)
    nt = seq // tm
    tok = bsz * seq
    row = lambda b, s: (b * nt + s, 0)
    const = lambda b, s: (0, 0)
    batch = lambda b, s: b
    return pl.pallas_call(
        _inproj_kernel,
        grid=(bsz, nt),
        in_specs=[
            pl.BlockSpec((tm, D_MODEL), row),
            _mod_spec(layer, MOD_SC1, bsz, batch),
            _mod_spec(layer, MOD_SH1, bsz, batch),
            _layer_spec(wq, layer),
            _layer_spec(wg, layer),
            _layer_spec(wc, layer),
            pl.BlockSpec(bg.shape, const),
            pl.BlockSpec(wconv.shape, const),
        ],
        out_specs=[
            pl.BlockSpec((tm, 4 * M_WIDTH), row),
            pl.BlockSpec((GATE_ROWS, tm), lambda b, s: (0, b * nt + s)),
            pl.BlockSpec((tm, C_WIDTH), row),
        ],
        out_shape=[
            jax.ShapeDtypeStruct((tok, 4 * M_WIDTH), BF16),
            jax.ShapeDtypeStruct((GATE_ROWS, tok), F32),
            jax.ShapeDtypeStruct((tok, C_WIDTH), BF16),
        ],
        scratch_shapes=[pltpu.VMEM((tm + 8, C_WIDTH), F32)],
        compiler_params=_cparams(("arbitrary", "arbitrary")),
        name="inproj",
    )(x2, mod, mod, wq, wg, wc, bg, wconv)


N_ROWQ = 4
SPLIT = 3


def _split3(x):
    hi = x.astype(BF16).astype(F32)
    r1 = x - hi
    mid = r1.astype(BF16).astype(F32)
    return [hi, mid, r1 - mid]


def _mlstm_selector(sel):
    r = lax.broadcasted_iota(I32, sel.shape, 0)
    c = lax.broadcasted_iota(I32, sel.shape, 1) // LANES
    rq = r // 8
    quantity = ((rq >= SPLIT).astype(I32) + (rq >= 2 * SPLIT).astype(I32)
                + (rq >= 3 * SPLIT).astype(I32))
    hit = ((r % 8) == c // N_ROWQ) & (quantity == c % N_ROWQ) & (rq < N_ROWQ * SPLIT)
    sel[...] = hit.astype(BF16)


def _mlstm_block(qkvo_ref, grow_ref, nw_ref, out_ref, cst, mst, sel, between=()):
    between = tuple(between) + (None,) * 3
    sb = qkvo_ref.shape[0]
    L = CHUNK
    causal = lax.broadcasted_iota(I32, (L, L), 0) >= lax.broadcasted_iota(I32, (L, L), 1)
    ones_ext = jnp.ones((L, D_HEAD), BF16)
    pad_rows = jnp.zeros((LANES - 8 * N_ROWQ * SPLIT, L), F32)
    eye = (lax.broadcasted_iota(I32, (D_HEAD, D_HEAD), 0)
           == lax.broadcasted_iota(I32, (D_HEAD, D_HEAD), 1)).astype(BF16)
    n_chunks = sb // L
    pairs = [(c, h) for c in range(n_chunks) for h in range(HEADS)]

    def cols(part, c, h):
        c0 = part * M_WIDTH + h * D_HEAD
        return qkvo_ref[c * L:(c + 1) * L, c0:c0 + D_HEAD]

    m_prev = mst[...]
    stack_t, us, decay = [], [], []
    for c in range(n_chunks):
        ug = grow_ref[0:8, c * L:(c + 1) * L]
        bcum = grow_ref[8:16, c * L:(c + 1) * L]
        cmax = grow_ref[16:24, c * L:(c + 1) * L]
        big_m = jnp.maximum(m_prev, cmax)
        m_last = jnp.broadcast_to(big_m[:, L - 1:L], (8, L))
        g_tot = jnp.broadcast_to(bcum[:, L - 1:L], (8, L))
        wq = jnp.exp(m_prev - big_m) * QK_SCALE
        log_em = -(bcum + big_m)
        wk = jnp.exp(ug - m_last)
        decay.append(jnp.exp(m_prev - m_last))
        us.append(ug + LOG_QK_SCALE)
        m_prev = g_tot + m_last
        stack = jnp.concatenate(_split3(big_m) + _split3(wq) + _split3(log_em) + _split3(wk)
                                + [pad_rows], axis=0)
        stack_t.append(stack.T.astype(BF16))
    mst[...] = m_prev
    if between[0] is not None:
        between[0]()

    reps = {}

    def rep(c, h, j):
        if (c, h) not in reps:
            c0 = h * N_ROWQ * LANES
            reps[(c, h)] = jnp.dot(stack_t[c], sel[:, c0:c0 + N_ROWQ * LANES],
                                   preferred_element_type=F32)
        return reps[(c, h)][:, j * LANES:(j + 1) * LANES]

    s_mat = {ch: lax.dot_general(cols(0, *ch), cols(1, *ch), (((1,), (1,)), ((), ())),
                                 preferred_element_type=F32) for ch in pairs}
    kw = {ch: (cols(1, *ch).astype(F32) * rep(*ch, 3)).astype(BF16) for ch in pairs}
    kw_t = {ch: lax.dot_general(eye, kw[ch], (((1,), (1,)), ((), ())),
                                preferred_element_type=F32).astype(BF16) for ch in pairs}
    if between[1] is not None:
        between[1]()
    lhs = {}
    for ch in pairs:
        c, h = ch
        big_m_rows = jnp.concatenate([rep(c, h, 0)] * (L // LANES), axis=1)
        p = jnp.where(causal, s_mat[ch] * jnp.exp(us[c][h:h + 1, :] - big_m_rows), 0.0)
        qw = cols(0, c, h).astype(F32) * rep(c, h, 1)
        lhs[ch] = jnp.concatenate([p.astype(BF16), qw.astype(BF16)], axis=1)
    vext = {ch: jnp.concatenate([cols(2, *ch), ones_ext], axis=1) for ch in pairs}
    upd = {ch: jnp.dot(kw_t[ch], vext[ch], preferred_element_type=F32) for ch in pairs}
    for c in range(n_chunks):
        tot = {}
        for h in range(HEADS):
            state = cst[h]
            rhs = jnp.concatenate([vext[(c, h)], state.astype(BF16)], axis=0)
            tot[h] = jnp.dot(lhs[(c, h)], rhs, preferred_element_type=F32)
            dec = jnp.concatenate([decay[c][h:h + 1, :LANES]] * 2, axis=1)
            cst[h] = dec * state + upd[(c, h)]
        for h in range(HEADS):
            c0 = h * D_HEAD
            num, den = tot[h][:, :D_HEAD], tot[h][:, D_HEAD:]
            hh = num / jnp.maximum(jnp.abs(den), jnp.exp(rep(c, h, 2)))
            mu = jnp.mean(hh, axis=1, keepdims=True)
            var = jnp.mean(jnp.square(hh - mu), axis=1, keepdims=True)
            hn = (hh - mu) * lax.rsqrt(var + LN_EPS) * nw_ref[:, c0:c0 + D_HEAD]
            og = jax.nn.sigmoid(cols(3, c, h).astype(F32))
            out_ref[c * L:(c + 1) * L, c0:c0 + D_HEAD] = (hn * og).astype(BF16)
        if c == 0 and between[2] is not None:
            between[2]()


def _layer_norm(r, g, b):
    mu = jnp.mean(r, axis=-1, keepdims=True)
    var = jnp.mean(jnp.square(r - mu), axis=-1, keepdims=True)
    return (r - mu) * lax.rsqrt(var + LN_EPS) * g + b


TOK_SUB = D_MODEL // LANES


def _store_token_major(ref, val, first_row=0):
    rows = val.shape[0]
    for k in range(TOK_SUB):
        ref[pl.ds(first_row * TOK_SUB + k, rows, stride=TOK_SUB), :] = val[:, k * LANES:(k + 1) * LANES]


def _load_token_major(ref, rows, first_row=0):
    return jnp.concatenate([ref[pl.ds(first_row * TOK_SUB + k, rows, stride=TOK_SUB), :]
                            for k in range(TOK_SUB)], axis=1)


def _outproj_parts(hn_ref, yc_ref, x_ref, gt_ref, sc_ref, sh_ref, wo_ref, lg_ref, lb_ref,
                   wr_ref, br_ref, x1_ref, u2_ref, route_ref, cnt_ref, base, triu, live):
    tm = x_ref.shape[0]
    held = {}

    def project():
        held["mix"] = (jnp.dot(hn_ref[...], wo_ref[0:M_WIDTH, :], preferred_element_type=F32)
                       + jnp.dot(yc_ref[...], wo_ref[M_WIDTH:, :], preferred_element_type=F32))

    def normalise():
        x1 = _layer_norm(ALPHA * x_ref[...] + (1.0 + gt_ref[0]) * held["mix"], lg_ref[...], lb_ref[...])
        x1_ref[...] = x1
        u2 = x1 * (1.0 + sc_ref[0]) + sh_ref[0]
        _store_token_major(u2_ref, u2)
        held["logits"] = lax.dot_general(wr_ref[...], u2.astype(BF16), (((1,), (1,)), ((), ())),
                                         preferred_element_type=F32) + br_ref[:, 0:1]

    def route():
        _route_block(held["logits"], route_ref, cnt_ref, base, triu, live, tm)

    return project, normalise, route


def _route_block(lt, route_ref, cnt_ref, base, triu, live, tm):
    lg = [lt[j:j + 1, :] for j in range(N_GROUPS + N_GROUPS * N_EXP)]
    best = lg[0]
    grp = jnp.zeros((1, tm), I32)
    for j in range(1, N_GROUPS):
        c = lg[j] > best
        grp = jnp.where(c, j, grp)
        best = jnp.where(c, lg[j], best)
    sel = []
    for e in range(N_EXP):
        val = lg[N_GROUPS + e]
        for gg in range(1, N_GROUPS):
            val = jnp.where(grp == gg, lg[N_GROUPS + gg * N_EXP + e], val)
        sel.append(val)
    v1 = sel[0]
    i1 = jnp.zeros((1, tm), I32)
    for e in range(1, N_EXP):
        c = sel[e] > v1
        i1 = jnp.where(c, e, i1)
        v1 = jnp.where(c, sel[e], v1)
    v2 = jnp.full((1, tm), -jnp.inf, F32)
    i2 = jnp.zeros((1, tm), I32)
    for e in range(N_EXP):
        cand = jnp.where(i1 == e, -jnp.inf, sel[e])
        c = cand > v2
        i2 = jnp.where(c, e, i2)
        v2 = jnp.where(c, cand, v2)
    ea = jnp.minimum(i1, i2)
    eb = jnp.maximum(i1, i2)
    lex = jnp.where(ea == 0, 0, jnp.where(ea == 1, 3, 5)) + eb - ea - 1
    pair = lex
    for i, p in enumerate(_PAIR_ORDER):
        if i != p:
            pair = jnp.where(lex == i, p, pair)
    bucket = grp * N_PAIRS + pair

    onehot = lax.broadcasted_iota(I32, (ROUTE_ROWS, tm), 0) == bucket
    cum = jnp.dot(onehot.astype(BF16), triu[...], preferred_element_type=F32)
    prev = base[:, 0:1]
    rank = jnp.sum(jnp.where(onehot, cum - 1.0 + prev, 0.0), axis=0, keepdims=True)
    new_base = prev + live * cum[:, tm - 1:tm]
    base[...] = jnp.broadcast_to(new_base, base.shape)
    cnt_ref[...] = jnp.broadcast_to(new_base, cnt_ref.shape)

    zrow = jnp.zeros((1, tm), F32)
    route_ref[...] = jnp.concatenate(
        [bucket.astype(F32), rank, zrow, zrow, zrow, zrow, zrow, zrow], axis=0)


def _mixer_kernel(qkvo_ref, grow_ref, nw_ref, yc_ref, x_ref, gt_ref, sc_ref, sh_ref, wo_ref, lg_ref,
                  lb_ref, wr_ref, br_ref, x1_ref, u2_ref, route_ref, cnt_ref,
                  cst, mst, sel, hn, base, triu, *, blocks_per_seq):
    i = pl.program_id(0)
    tm = x_ref.shape[0]

    @pl.when(i == 0)
    def _():
        _mlstm_selector(sel)
        hn[...] = jnp.zeros(hn.shape, BF16)
        base[...] = jnp.zeros(base.shape, F32)
        rid = lax.broadcasted_iota(I32, triu.shape, 0)
        cid = lax.broadcasted_iota(I32, triu.shape, 1)
        triu[...] = (rid <= cid).astype(BF16)

    @pl.when(i % blocks_per_seq == 0)
    def _():
        cst[...] = jnp.zeros(cst.shape, F32)
        mst[...] = jnp.zeros(mst.shape, F32)

    live = jnp.where(i > 0, 1.0, 0.0).astype(F32)
    parts = _outproj_parts(hn, yc_ref, x_ref, gt_ref, sc_ref, sh_ref, wo_ref, lg_ref, lb_ref,
                           wr_ref, br_ref, x1_ref, u2_ref, route_ref, cnt_ref, base, triu, live)
    _mlstm_block(qkvo_ref, grow_ref, nw_ref, hn, cst, mst, sel, between=parts)


def _mixer(qkvo, grow, nw, yc, x2, mod, layer, wo, lg, lb, wr, br, bsz, seq):
    tm = min(TOK_TILE, seq)
    nt = seq // tm
    tok = bsz * seq
    n = bsz * nt
    cur = lambda i: (jnp.minimum(i, n - 1), 0)
    prev = lambda i: (jnp.maximum(i - 1, 0), 0)
    const = lambda i: (0, 0)
    batch = lambda i: jnp.maximum(i - 1, 0) // nt
    return pl.pallas_call(
        functools.partial(_mixer_kernel, blocks_per_seq=nt),
        grid=(n + 1,),
        in_specs=[
            pl.BlockSpec((tm, 4 * M_WIDTH), cur),
            pl.BlockSpec((GATE_ROWS, tm), lambda i: (0, jnp.minimum(i, n - 1))),
            pl.BlockSpec((1, M_WIDTH), const),
            pl.BlockSpec((tm, C_WIDTH), prev),
            pl.BlockSpec((tm, D_MODEL), prev),
            _mod_spec(layer, MOD_GT1, bsz, batch),
            _mod_spec(layer, MOD_SC2, bsz, batch),
            _mod_spec(layer, MOD_SH2, bsz, batch),
            _layer_spec(wo, layer),
            pl.BlockSpec(lg.shape, const),
            pl.BlockSpec(lb.shape, const),
            pl.BlockSpec(wr.shape, const),
            pl.BlockSpec(br.shape, const),
        ],
        out_specs=[
            pl.BlockSpec((tm, D_MODEL), prev),
            pl.BlockSpec((tm * TOK_SUB, LANES), prev),
            pl.BlockSpec((8, tm), lambda i: (0, jnp.maximum(i - 1, 0))),
            pl.BlockSpec((ROUTE_ROWS, LANES), const),
        ],
        out_shape=[
            jax.ShapeDtypeStruct((tok, D_MODEL), F32),
            jax.ShapeDtypeStruct((tok * TOK_SUB, LANES), F32),
            jax.ShapeDtypeStruct((8, tok), F32),
            jax.ShapeDtypeStruct((ROUTE_ROWS, LANES), F32),
        ],
        scratch_shapes=[pltpu.VMEM((HEADS, D_HEAD, 2 * D_HEAD), F32),
                        pltpu.VMEM((8, CHUNK), F32),
                        pltpu.VMEM((LANES, HEADS * N_ROWQ * LANES), BF16),
                        pltpu.VMEM((tm, M_WIDTH), BF16),
                        pltpu.VMEM((ROUTE_ROWS, LANES), F32),
                        pltpu.VMEM((tm, tm), BF16)],
        compiler_params=_cparams(("arbitrary",)),
        name="mixer",
    )(qkvo, grow, nw, yc, x2, mod, mod, mod, wo, lg, lb, wr, br)


DISPATCH_TILE = 2048
DMA_UNROLL = 8


def _dispatch_kernel(pos_ref, pad_ref, u_ref, xs_ref, zeros, sem, zsem):
    ts = u_ref.shape[0] // TOK_SUB
    t0 = pl.program_id(0) * ts

    def zero_rows(first_row, n_rows_static):
        dst = pl.multiple_of(first_row * TOK_SUB, TOK_SUB)
        return pltpu.make_async_copy(zeros.at[pl.ds(0, n_rows_static * TOK_SUB), :],
                                     xs_ref.at[pl.ds(dst, n_rows_static * TOK_SUB), :], zsem)

    def fill_pads(wait):
        def bucket(b, carry):
            row = pad_ref[b]
            n = pad_ref[N_BUCKETS + b]
            size = MOE_STEP // 2
            while size >= 1:
                @pl.when((n & size) != 0)
                def _(row=row, size=size):
                    cp = zero_rows(row, size)
                    cp.wait() if wait else cp.start()
                row = row + (n & size)
                size //= 2
            return carry
        lax.fori_loop(0, N_BUCKETS, bucket, 0)

        def idle_tile(j, carry):
            cp = zero_rows(j * MOE_TILE, MOE_TILE)
            cp.wait() if wait else cp.start()
            return carry
        lax.fori_loop(pad_ref[2 * N_BUCKETS], xs_ref.shape[0] // (MOE_TILE * TOK_SUB), idle_tile, 0)

    @pl.when(pl.program_id(0) == 0)
    def _():
        zeros[...] = jnp.zeros(zeros.shape, F32)
        fill_pads(wait=False)
        fill_pads(wait=True)

    def issue(g, carry):
        for u in range(DMA_UNROLL):
            r = g * DMA_UNROLL + u
            dst = pl.multiple_of(pos_ref[t0 + r] * TOK_SUB, TOK_SUB)
            src = pl.multiple_of(r * TOK_SUB, TOK_SUB)
            pltpu.make_async_copy(u_ref.at[pl.ds(src, TOK_SUB), :],
                                  xs_ref.at[pl.ds(dst, TOK_SUB), :], sem).start(priority=u % 2)
        return carry

    lax.fori_loop(0, ts // DMA_UNROLL, issue, 0)
    pltpu.make_async_copy(u_ref, xs_ref.at[pl.ds(0, ts * TOK_SUB), :], sem).wait()


def _dispatch(pos, pad_start, u2, n_rows):
    tok = u2.shape[0] // TOK_SUB
    ts = min(DISPATCH_TILE, tok)
    return pl.pallas_call(
        _dispatch_kernel,
        grid_spec=pltpu.PrefetchScalarGridSpec(
            num_scalar_prefetch=2,
            grid=(tok // ts,),
            in_specs=[pl.BlockSpec((ts * TOK_SUB, LANES), lambda i, pos, pad: (i, 0))],
            out_specs=pl.BlockSpec(memory_space=pl.ANY),
            scratch_shapes=[pltpu.VMEM((MOE_TILE * TOK_SUB, LANES), F32),
                            pltpu.SemaphoreType.DMA(()), pltpu.SemaphoreType.DMA(())],
        ),
        out_shape=jax.ShapeDtypeStruct((n_rows * TOK_SUB, LANES), F32),
        compiler_params=_cparams(("arbitrary",)),
        name="dispatch",
    )(pos, pad_start, u2)


def _moe_kernel(s0_ref, s1_ref, tiles_ref, fresh0_ref, fresh1_ref, tb_ref, xb_ref, xs_ref, wsel_ref,
                bsel_ref, wg0, wu0, wd0, wg1, wu1, wd1, y_ref, wgs, wus, wds):
    del s0_ref, s1_ref, tb_ref, xb_ref
    j = pl.program_id(0)
    rows = MOE_TILE

    @pl.when(fresh0_ref[j] == 1)
    def _():
        wgs[0] = wg0[0].astype(BF16)
        wus[0] = wu0[0].astype(BF16)
        wds[0] = wd0[0].astype(BF16)

    @pl.when(fresh1_ref[j] == 1)
    def _():
        wgs[1] = wg1[0].astype(BF16)
        wus[1] = wu1[0].astype(BF16)
        wds[1] = wd1[0].astype(BF16)

    def experts(part):
        x = _load_token_major(xs_ref, rows, part * rows).astype(BF16)
        lg = jnp.dot(x, wsel_ref[0], preferred_element_type=F32) + bsel_ref[0]
        grp = [lg[:, i:i + 1] for i in range(N_GROUPS)]
        gmax = jnp.maximum(jnp.maximum(grp[0], grp[1]), jnp.maximum(grp[2], grp[3]))
        gsum = sum(jnp.exp(v - gmax) for v in grp)
        la, lb = lg[:, N_GROUPS:N_GROUPS + 1], lg[:, N_GROUPS + 1:N_GROUPS + 2]
        emax = jnp.maximum(la, lb)
        ea, eb = jnp.exp(la - emax), jnp.exp(lb - emax)
        scale = 1.0 / (gsum * (ea + eb))
        wcols = (ea * scale, eb * scale)
        acc = None
        for slot in range(2):
            g = jnp.dot(x, wgs[slot], preferred_element_type=F32)
            u = jnp.dot(x, wus[slot], preferred_element_type=F32)
            hid = (g * jax.nn.sigmoid(g) * u * wcols[slot]).astype(BF16)
            y = jnp.dot(hid, wds[slot], preferred_element_type=F32)
            acc = y if acc is None else acc + y
        _store_token_major(y_ref, acc, part * rows)

    for part in range(MOE_STEP // MOE_TILE):
        @pl.when(tiles_ref[j] > part)
        def _(part=part):
            experts(part)

        @pl.when(tiles_ref[j] <= part)
        def _(part=part):
            lo = part * rows * TOK_SUB
            y_ref[lo:lo + rows * TOK_SUB, :] = jnp.zeros((rows * TOK_SUB, LANES), F32)


def _moe(plan, xs, wsel, bsel, w_gate, w_up, w_down, n_tiles):
    wsel1 = lambda j, s0, s1, v, f0, f1, tb, xb: (s0[j], 0, 0)
    wsel2 = lambda j, s0, s1, v, f0, f1, tb, xb: (s1[j], 0, 0)
    by_bucket = lambda j, s0, s1, v, f0, f1, tb, xb: (tb[j], 0, 0)
    up_spec = lambda sel: pl.BlockSpec((1, D_MODEL, D_EXPERT), sel)
    dn_spec = lambda sel: pl.BlockSpec((1, D_EXPERT, D_MODEL), sel)
    tile_rows = MOE_STEP * TOK_SUB
    return pl.pallas_call(
        _moe_kernel,
        grid_spec=pltpu.PrefetchScalarGridSpec(
            num_scalar_prefetch=7,
            grid=(n_tiles,),
            in_specs=[pl.BlockSpec((tile_rows, LANES), lambda j, s0, s1, v, f0, f1, tb, xb: (xb[j], 0)),
                      pl.BlockSpec((1, D_MODEL, LANES), by_bucket),
                      pl.BlockSpec((1, 1, LANES), by_bucket),
                      up_spec(wsel1), up_spec(wsel1), dn_spec(wsel1),
                      up_spec(wsel2), up_spec(wsel2), dn_spec(wsel2)],
            out_specs=pl.BlockSpec((tile_rows, LANES), lambda j, *_: (j, 0)),
            scratch_shapes=[pltpu.VMEM((2, D_MODEL, D_EXPERT), BF16),
                            pltpu.VMEM((2, D_MODEL, D_EXPERT), BF16),
                            pltpu.VMEM((2, D_EXPERT, D_MODEL), BF16)],
        ),
        out_shape=jax.ShapeDtypeStruct((n_tiles * tile_rows, LANES), F32),
        compiler_params=_cparams(("arbitrary",)),
        name="moe",
    )(*plan, xs, wsel, bsel, w_gate, w_up, w_down, w_gate, w_up, w_down)


def _combine_kernel(pos_ref, ys_ref, x1_ref, gt_ref, lg_ref, lb_ref, out_ref, ybuf, sems):
    ts = x1_ref.shape[0]
    i = pl.program_id(0)
    n = pl.num_programs(0)
    slot = i % 2

    def gather(tile, to_slot):
        def issue(g, carry):
            for u in range(DMA_UNROLL):
                r = g * DMA_UNROLL + u
                src = pl.multiple_of(pos_ref[tile * ts + r] * TOK_SUB, TOK_SUB)
                dst = pl.multiple_of(r * TOK_SUB, TOK_SUB)
                pltpu.make_async_copy(ys_ref.at[pl.ds(src, TOK_SUB), :],
                                      ybuf.at[to_slot, pl.ds(dst, TOK_SUB), :],
                                      sems.at[to_slot]).start(priority=1)
            return carry
        lax.fori_loop(0, ts // DMA_UNROLL, issue, 0)

    @pl.when(i == 0)
    def _():
        gather(0, 0)

    @pl.when(i + 1 < n)
    def _():
        gather(i + 1, 1 - slot)

    pltpu.make_async_copy(ys_ref.at[pl.ds(0, ts * TOK_SUB), :], ybuf.at[slot], sems.at[slot]).wait()
    y = _load_token_major(ybuf.at[slot], ts)
    out_ref[...] = _layer_norm(ALPHA * x1_ref[...] + (1.0 + gt_ref[0]) * y, lg_ref[...], lb_ref[...])


def _combine(pos, ys, x1, mod, layer, lg, lb, bsz, seq):
    ts = min(TOK_TILE, seq)
    nt = seq // ts
    tok = bsz * seq
    return pl.pallas_call(
        _combine_kernel,
        grid_spec=pltpu.PrefetchScalarGridSpec(
            num_scalar_prefetch=1,
            grid=(tok // ts,),
            in_specs=[pl.BlockSpec(memory_space=pl.ANY),
                      pl.BlockSpec((ts, D_MODEL), lambda i, pos: (i, 0)),
                      _mod_spec(layer, MOD_GT2, bsz, lambda i, pos: i // nt),
                      pl.BlockSpec((1, D_MODEL), lambda i, pos: (0, 0)),
                      pl.BlockSpec((1, D_MODEL), lambda i, pos: (0, 0))],
            out_specs=pl.BlockSpec((ts, D_MODEL), lambda i, pos: (i, 0)),
            scratch_shapes=[pltpu.VMEM((2, ts * TOK_SUB, LANES), F32), pltpu.SemaphoreType.DMA((2,))],
        ),
        out_shape=jax.ShapeDtypeStruct((tok, D_MODEL), F32),
        compiler_params=_cparams(("arbitrary",)),
        name="combine",
    )(pos, ys, x1, mod, lg, lb)


def _route_plan(route, cnt, n_tiles, layer):
    bucket = route[0].astype(I32)
    rank = route[1].astype(I32)
    counts = cnt[:N_BUCKETS, 0].astype(I32)
    tiles_b = (counts + MOE_STEP - 1) // MOE_STEP
    tile_end = jnp.cumsum(tiles_b)
    row_off = (tile_end - tiles_b) * MOE_STEP
    bucket_ids = jnp.arange(N_BUCKETS, dtype=I32)
    pos = jnp.sum(jnp.where(bucket[None, :] == bucket_ids[:, None], row_off[:, None], 0), axis=0) + rank
    total = tile_end[-1]
    j = jnp.arange(n_tiles, dtype=I32)
    tb = jnp.sum((tile_end[None, :] <= jnp.minimum(j, total - 1)[:, None]).astype(I32), axis=1)
    tb = jnp.clip(tb, 0, N_BUCKETS - 1)
    hit = tb[None, :] == bucket_ids[:, None]
    rows_left = jnp.sum(jnp.where(hit, (row_off + counts)[:, None], 0), axis=0) - j * MOE_STEP
    tiles_j = jnp.clip((rows_left + MOE_TILE - 1) // MOE_TILE, 0, MOE_STEP // MOE_TILE)
    tiles_j = jnp.where(j < total, tiles_j, 0)
    grp = tb // N_PAIRS
    pair = tb % N_PAIRS
    first = layer * (N_GROUPS * N_EXP) + grp * N_EXP
    s0 = first + sum((pair == i).astype(I32) * _SLOT0[i] for i in range(N_PAIRS))
    s1 = first + sum((pair == i).astype(I32) * _SLOT1[i] for i in range(N_PAIRS))
    one = jnp.ones((1,), I32)
    fresh0 = jnp.concatenate([one, (s0[1:] != s0[:-1]).astype(I32)])
    fresh1 = jnp.concatenate([one, (s1[1:] != s1[:-1]).astype(I32)])
    xblk = jnp.minimum(j, total - 1)
    pads = jnp.concatenate([row_off + counts, tiles_b * MOE_STEP - counts,
                            (total * (MOE_STEP // MOE_TILE))[None]])
    return pos, pads, (s0, s1, tiles_j, fresh0, fresh1, layer * N_BUCKETS + tb, xblk)


def kernel(x, c, w_ada, b_ada, w_in, b_gates, mh_norm_w, w_conv, w_out, ln1_g, ln1_b,
           w_grp, b_grp, w_router, b_router, w_gate, w_up, w_down, ln2_g, ln2_b):
    bsz, seq, _ = x.shape
    tok = bsz * seq
    n_rows = tok + N_BUCKETS * MOE_STEP
    n_tiles = n_rows // MOE_STEP
    n_exp_total = N_GROUPS * N_EXP

    wg_all = w_gate.reshape(DEPTH * n_exp_total, D_MODEL, D_EXPERT)
    wu_all = w_up.reshape(DEPTH * n_exp_total, D_MODEL, D_EXPERT)
    wd_all = w_down.reshape(DEPTH * n_exp_total, D_EXPERT, D_MODEL)
    mod = _ada_mod(c, w_ada, b_ada).reshape(DEPTH * bsz * N_MOD, 1, D_MODEL)
    x2 = x.reshape(tok, D_MODEL)
    wq = w_in[:, :, :4 * M_WIDTH].astype(BF16)
    wg = jnp.pad(w_in[:, :, 4 * M_WIDTH:4 * M_WIDTH + 2 * HEADS],
                 ((0, 0), (0, 0), (0, LANES - 2 * HEADS))).astype(BF16)
    wc = w_in[:, :, 4 * M_WIDTH + 2 * HEADS:].astype(BF16)
    wo = w_out.astype(BF16)
    lb_shape = (DEPTH, N_BUCKETS)
    wsel = jnp.concatenate([jnp.broadcast_to(w_grp[:, None], lb_shape + (D_MODEL, N_GROUPS)),
                            jnp.swapaxes(w_router[:, :, _BUCKET_S0], 1, 2)[..., None],
                            jnp.swapaxes(w_router[:, :, _BUCKET_S1], 1, 2)[..., None]], axis=3)
    wsel = jnp.pad(wsel, ((0, 0), (0, 0), (0, 0), (0, LANES - N_GROUPS - 2))).astype(BF16)
    wsel = wsel.reshape(DEPTH * N_BUCKETS, D_MODEL, LANES)
    bsel = jnp.concatenate([jnp.broadcast_to(b_grp[:, None], lb_shape + (N_GROUPS,)),
                            b_router[:, _BUCKET_S0][..., None], b_router[:, _BUCKET_S1][..., None]], axis=2)
    bsel = jnp.pad(bsel, ((0, 0), (0, 0), (0, LANES - N_GROUPS - 2))).reshape(DEPTH * N_BUCKETS, 1, LANES)
    for l in range(DEPTH):
        bg = jnp.pad(b_gates[l], (0, LANES - 2 * HEADS)).reshape(1, LANES)
        qkvo, grow, yc = _inproj(x2, mod, l, wq, wg, wc, bg, w_conv[l], bsz, seq)
        n_logit = N_GROUPS + n_exp_total
        wr = jnp.pad(jnp.concatenate([w_grp[l], w_router[l]], axis=1).T,
                     ((0, ROUTE_ROWS - n_logit), (0, 0))).astype(BF16)
        br = jnp.pad(jnp.concatenate([b_grp[l], b_router[l]]), (0, ROUTE_ROWS - n_logit))
        br = jnp.broadcast_to(br[:, None], (ROUTE_ROWS, LANES))
        x1, u2, route, cnt = _mixer(qkvo, grow, mh_norm_w[l].reshape(1, M_WIDTH), yc, x2, mod, l,
                                    wo, ln1_g[l].reshape(1, D_MODEL),
                                    ln1_b[l].reshape(1, D_MODEL), wr, br, bsz, seq)
        pos, pads, plan = _route_plan(route, cnt, n_tiles, l)
        xs = _dispatch(pos, pads, u2, n_rows)
        ys = _moe(plan, xs, wsel, bsel, wg_all, wu_all, wd_all, n_tiles)
        x2 = _combine(pos, ys, x1, mod, l, ln2_g[l].reshape(1, D_MODEL),
                      ln2_b[l].reshape(1, D_MODEL), bsz, seq)
    return x2.reshape(bsz, seq, D_MODEL)
```

```python
import functools
import math

import jax
import jax.numpy as jnp
import numpy as np
from jax import lax
from jax.experimental import pallas as pl
from jax.experimental.pallas import tpu as pltpu

F32 = jnp.float32
BF16 = jnp.bfloat16
I32 = jnp.int32

D_MODEL = 1024
DEPTH = 2
HEADS = 4
D_HEAD = 128
M_WIDTH = HEADS * D_HEAD
C_WIDTH = D_MODEL - M_WIDTH
N_GROUPS = 4
N_EXP = 4
D_EXPERT = 512
N_PAIRS = 6
N_BUCKETS = N_GROUPS * N_PAIRS
N_MOD = 6
MOD_SH1, MOD_SC1, MOD_GT1, MOD_SH2, MOD_SC2, MOD_GT2 = range(N_MOD)
ALPHA = (2 * DEPTH) ** 0.25
LN_EPS = 1e-5
QK_SCALE = D_HEAD ** -0.5
LOG_QK_SCALE = math.log(QK_SCALE)
GATE_ROWS = 24

LANES = 128
ROUTE_ROWS = 32

TOK_TILE = 512
INPROJ_TILE = 1024
CHUNK = 256
MOE_TILE = 256
MOE_STEP = 2 * MOE_TILE
VMEM_LIMIT = 56 * 1024 * 1024

_PAIR_ORDER = (0, 2, 3, 1, 4, 5)
_SLOT0 = (0, 2, 2, 3, 3, 3)
_SLOT1 = (1, 1, 0, 0, 1, 2)
_BUCKET_S0 = np.array([g * N_EXP + _SLOT0[p] for g in range(N_GROUPS) for p in range(N_PAIRS)])
_BUCKET_S1 = np.array([g * N_EXP + _SLOT1[p] for g in range(N_GROUPS) for p in range(N_PAIRS)])


def _cparams(sem):
    return pltpu.CompilerParams(dimension_semantics=sem, vmem_limit_bytes=VMEM_LIMIT)


def _ada_kernel(c_ref, w_ref, b_ref, o_ref):
    c = c_ref[...]
    cond = (c * jax.nn.sigmoid(c)).astype(BF16)
    o_ref[0] = jnp.dot(cond, w_ref[0].astype(BF16), preferred_element_type=F32) + b_ref[0]


def _ada_mod(c, w_ada, b_ada):
    bsz = c.shape[0]
    nblk = w_ada.shape[2] // D_MODEL
    return pl.pallas_call(
        _ada_kernel,
        grid=(DEPTH, nblk),
        in_specs=[
            pl.BlockSpec((bsz, D_MODEL), lambda l, j: (0, 0)),
            pl.BlockSpec((1, D_MODEL, D_MODEL), lambda l, j: (l, 0, j)),
            pl.BlockSpec((1, 1, D_MODEL), lambda l, j: (l, 0, j)),
        ],
        out_specs=pl.BlockSpec((1, bsz, D_MODEL), lambda l, j: (l, 0, j)),
        out_shape=jax.ShapeDtypeStruct((DEPTH, bsz, nblk * D_MODEL), F32),
        compiler_params=_cparams(("arbitrary", "arbitrary")),
        name="ada_mod",
    )(c, w_ada, b_ada.reshape(DEPTH, 1, -1))


def _log_sigmoid(x):
    return jnp.minimum(x, 0.0) - jnp.log1p(jnp.exp(-jnp.abs(x)))


def _chunk_scan(x, op, identity):
    lane = lax.broadcasted_iota(I32, x.shape, 1) & (CHUNK - 1)
    sh = 1
    while sh < CHUNK:
        x = op(x, jnp.where(lane >= sh, pltpu.roll(x, sh, 1), identity))
        sh *= 2
    return x


def _inproj_kernel(x_ref, sc_ref, sh_ref, wq_ref, wg_ref, wc_ref, bg_ref, wconv_ref,
                   qkvo_ref, grow_ref, yc_ref, zbuf):
    tm = x_ref.shape[0]
    u = (x_ref[...] * (1.0 + sc_ref[0]) + sh_ref[0]).astype(BF16)
    qkvo_ref[...] = jnp.dot(u, wq_ref[...], preferred_element_type=F32).astype(BF16)
    g = jnp.dot(u, wg_ref[...], preferred_element_type=F32) + bg_ref[...]
    gt = g.T[:8]
    head_row = lax.broadcasted_iota(I32, gt.shape, 0) < HEADS
    ig = jnp.where(head_row, gt, 0.0)
    logf = jnp.where(head_row, _log_sigmoid(pltpu.roll(gt, HEADS, 0)), 0.0)
    bcum = _chunk_scan(logf, jnp.add, 0.0)
    ug = ig - bcum
    grow_ref[0:8, :] = ug
    grow_ref[8:16, :] = bcum
    grow_ref[16:24, :] = _chunk_scan(ug, jnp.maximum, -jnp.inf)
    pc = jnp.dot(u, wc_ref[...], preferred_element_type=F32)
    z = pc[:, C_WIDTH:2 * C_WIDTH] * pc[:, 2 * C_WIDTH:]

    @pl.when(pl.program_id(1) == 0)
    def _():
        zbuf[0:8, :] = jnp.zeros((8, C_WIDTH), F32)

    zbuf[8:8 + tm, :] = z
    zc = (wconv_ref[0:1, :] * zbuf[6:6 + tm, :] + wconv_ref[1:2, :] * zbuf[7:7 + tm, :]
          + wconv_ref[2:3, :] * z)
    yc_ref[...] = (pc[:, :C_WIDTH] * zc).astype(BF16)
    zbuf[0:8, :] = zbuf[tm:tm + 8, :]


def _mod_spec(layer, which, bsz, batch_of):
    return pl.BlockSpec((1, 1, D_MODEL),
                        lambda *idx: ((layer * bsz + batch_of(*idx)) * N_MOD + which, 0, 0))


def _layer_spec(stacked, layer):
    return pl.BlockSpec((None,) + stacked.shape[1:], lambda *idx: (layer, 0, 0))


def _inproj(x2, mod, layer, wq, wg, wc, bg, wconv, bsz, seq):
    tm = min(INPROJ_TILE, seq)
    nt = seq // tm
    tok = bsz * seq
    row = lambda b, s: (b * nt + s, 0)
    const = lambda b, s: (0, 0)
    batch = lambda b, s: b
    return pl.pallas_call(
        _inproj_kernel,
        grid=(bsz, nt),
        in_specs=[
            pl.BlockSpec((tm, D_MODEL), row),
            _mod_spec(layer, MOD_SC1, bsz, batch),
            _mod_spec(layer, MOD_SH1, bsz, batch),
            _layer_spec(wq, layer),
            _layer_spec(wg, layer),
            _layer_spec(wc, layer),
            pl.BlockSpec(bg.shape, const),
            pl.BlockSpec(wconv.shape, const),
        ],
        out_specs=[
            pl.BlockSpec((tm, 4 * M_WIDTH), row),
            pl.BlockSpec((GATE_ROWS, tm), lambda b, s: (0, b * nt + s)),
            pl.BlockSpec((tm, C_WIDTH), row),
        ],
        out_shape=[
            jax.ShapeDtypeStruct((tok, 4 * M_WIDTH), BF16),
            jax.ShapeDtypeStruct((GATE_ROWS, tok), F32),
            jax.ShapeDtypeStruct((tok, C_WIDTH), BF16),
        ],
        scratch_shapes=[pltpu.VMEM((tm + 8, C_WIDTH), F32)],
        compiler_params=_cparams(("arbitrary", "arbitrary")),
        name="inproj",
    )(x2, mod, mod, wq, wg, wc, bg, wconv)


N_ROWQ = 4
SPLIT = 3


def _split3(x):
    hi = x.astype(BF16).astype(F32)
    r1 = x - hi
    mid = r1.astype(BF16).astype(F32)
    return [hi, mid, r1 - mid]


def _mlstm_selector(sel):
    r = lax.broadcasted_iota(I32, sel.shape, 0)
    c = lax.broadcasted_iota(I32, sel.shape, 1) // LANES
    rq = r // 8
    quantity = ((rq >= SPLIT).astype(I32) + (rq >= 2 * SPLIT).astype(I32)
                + (rq >= 3 * SPLIT).astype(I32))
    hit = ((r % 8) == c // N_ROWQ) & (quantity == c % N_ROWQ) & (rq < N_ROWQ * SPLIT)
    sel[...] = hit.astype(BF16)


def _mlstm_block(qkvo_ref, grow_ref, nw_ref, out_ref, cst, mst, sel, between=()):
    between = tuple(between) + (None,) * 3
    sb = qkvo_ref.shape[0]
    L = CHUNK
    causal = lax.broadcasted_iota(I32, (L, L), 0) >= lax.broadcasted_iota(I32, (L, L), 1)
    ones_ext = jnp.ones((L, D_HEAD), BF16)
    pad_rows = jnp.zeros((LANES - 8 * N_ROWQ * SPLIT, L), F32)
    eye = (lax.broadcasted_iota(I32, (D_HEAD, D_HEAD), 0)
           == lax.broadcasted_iota(I32, (D_HEAD, D_HEAD), 1)).astype(BF16)
    n_chunks = sb // L
    pairs = [(c, h) for c in range(n_chunks) for h in range(HEADS)]

    def cols(part, c, h):
        c0 = part * M_WIDTH + h * D_HEAD
        return qkvo_ref[c * L:(c + 1) * L, c0:c0 + D_HEAD]

    m_prev = mst[...]
    stack_t, us, decay = [], [], []
    for c in range(n_chunks):
        ug = grow_ref[0:8, c * L:(c + 1) * L]
        bcum = grow_ref[8:16, c * L:(c + 1) * L]
        cmax = grow_ref[16:24, c * L:(c + 1) * L]
        big_m = jnp.maximum(m_prev, cmax)
        m_last = jnp.broadcast_to(big_m[:, L - 1:L], (8, L))
        g_tot = jnp.broadcast_to(bcum[:, L - 1:L], (8, L))
        wq = jnp.exp(m_prev - big_m) * QK_SCALE
        log_em = -(bcum + big_m)
        wk = jnp.exp(ug - m_last)
        decay.append(jnp.exp(m_prev - m_last))
        us.append(ug + LOG_QK_SCALE)
        m_prev = g_tot + m_last
        stack = jnp.concatenate(_split3(big_m) + _split3(wq) + _split3(log_em) + _split3(wk)
                                + [pad_rows], axis=0)
        stack_t.append(stack.T.astype(BF16))
    mst[...] = m_prev
    if between[0] is not None:
        between[0]()

    reps = {}

    def rep(c, h, j):
        if (c, h) not in reps:
            c0 = h * N_ROWQ * LANES
            reps[(c, h)] = jnp.dot(stack_t[c], sel[:, c0:c0 + N_ROWQ * LANES],
                                   preferred_element_type=F32)
        return reps[(c, h)][:, j * LANES:(j + 1) * LANES]

    s_mat = {ch: lax.dot_general(cols(0, *ch), cols(1, *ch), (((1,), (1,)), ((), ())),
                                 preferred_element_type=F32) for ch in pairs}
    kw = {ch: (cols(1, *ch).astype(F32) * rep(*ch, 3)).astype(BF16) for ch in pairs}
    kw_t = {ch: lax.dot_general(eye, kw[ch], (((1,), (1,)), ((), ())),
                                preferred_element_type=F32).astype(BF16) for ch in pairs}
    if between[1] is not None:
        between[1]()
    lhs = {}
    for ch in pairs:
        c, h = ch
        big_m_rows = jnp.concatenate([rep(c, h, 0)] * (L // LANES), axis=1)
        p = jnp.where(causal, s_mat[ch] * jnp.exp(us[c][h:h + 1, :] - big_m_rows), 0.0)
        qw = cols(0, c, h).astype(F32) * rep(c, h, 1)
        lhs[ch] = jnp.concatenate([p.astype(BF16), qw.astype(BF16)], axis=1)
    vext = {ch: jnp.concatenate([cols(2, *ch), ones_ext], axis=1) for ch in pairs}
    upd = {ch: jnp.dot(kw_t[ch], vext[ch], preferred_element_type=F32) for ch in pairs}
    for c in range(n_chunks):
        tot = {}
        for h in range(HEADS):
            state = cst[h]
            rhs = jnp.concatenate([vext[(c, h)], state.astype(BF16)], axis=0)
            tot[h] = jnp.dot(lhs[(c, h)], rhs, preferred_element_type=F32)
            dec = jnp.concatenate([decay[c][h:h + 1, :LANES]] * 2, axis=1)
            cst[h] = dec * state + upd[(c, h)]
        for h in range(HEADS):
            c0 = h * D_HEAD
            num, den = tot[h][:, :D_HEAD], tot[h][:, D_HEAD:]
            hh = num / jnp.maximum(jnp.abs(den), jnp.exp(rep(c, h, 2)))
            mu = jnp.mean(hh, axis=1, keepdims=True)
            var = jnp.mean(jnp.square(hh - mu), axis=1, keepdims=True)
            hn = (hh - mu) * lax.rsqrt(var + LN_EPS) * nw_ref[:, c0:c0 + D_HEAD]
            og = jax.nn.sigmoid(cols(3, c, h).astype(F32))
            out_ref[c * L:(c + 1) * L, c0:c0 + D_HEAD] = (hn * og).astype(BF16)
        if c == 0 and between[2] is not None:
            between[2]()


def _layer_norm(r, g, b):
    mu = jnp.mean(r, axis=-1, keepdims=True)
    var = jnp.mean(jnp.square(r - mu), axis=-1, keepdims=True)
    return (r - mu) * lax.rsqrt(var + LN_EPS) * g + b


TOK_SUB = D_MODEL // LANES


def _store_token_major(ref, val, first_row=0):
    rows = val.shape[0]
    for k in range(TOK_SUB):
        ref[pl.ds(first_row * TOK_SUB + k, rows, stride=TOK_SUB), :] = val[:, k * LANES:(k + 1) * LANES]


def _load_token_major(ref, rows, first_row=0):
    return jnp.concatenate([ref[pl.ds(first_row * TOK_SUB + k, rows, stride=TOK_SUB), :]
                            for k in range(TOK_SUB)], axis=1)


def _outproj_parts(hn_ref, yc_ref, x_ref, gt_ref, sc_ref, sh_ref, wo_ref, lg_ref, lb_ref,
                   wr_ref, br_ref, x1_ref, u2_ref, route_ref, cnt_ref, base, triu, live):
    tm = x_ref.shape[0]
    held = {}

    def project():
        held["mix"] = (jnp.dot(hn_ref[...], wo_ref[0:M_WIDTH, :], preferred_element_type=F32)
                       + jnp.dot(yc_ref[...], wo_ref[M_WIDTH:, :], preferred_element_type=F32))

    def normalise():
        x1 = _layer_norm(ALPHA * x_ref[...] + (1.0 + gt_ref[0]) * held["mix"], lg_ref[...], lb_ref[...])
        x1_ref[...] = x1
        u2 = x1 * (1.0 + sc_ref[0]) + sh_ref[0]
        _store_token_major(u2_ref, u2)
        held["logits"] = lax.dot_general(wr_ref[...], u2.astype(BF16), (((1,), (1,)), ((), ())),
                                         preferred_element_type=F32) + br_ref[:, 0:1]

    def route():
        _route_block(held["logits"], route_ref, cnt_ref, base, triu, live, tm)

    return project, normalise, route


def _route_block(lt, route_ref, cnt_ref, base, triu, live, tm):
    lg = [lt[j:j + 1, :] for j in range(N_GROUPS + N_GROUPS * N_EXP)]
    best = lg[0]
    grp = jnp.zeros((1, tm), I32)
    for j in range(1, N_GROUPS):
        c = lg[j] > best
        grp = jnp.where(c, j, grp)
        best = jnp.where(c, lg[j], best)
    sel = []
    for e in range(N_EXP):
        val = lg[N_GROUPS + e]
        for gg in range(1, N_GROUPS):
            val = jnp.where(grp == gg, lg[N_GROUPS + gg * N_EXP + e], val)
        sel.append(val)
    v1 = sel[0]
    i1 = jnp.zeros((1, tm), I32)
    for e in range(1, N_EXP):
        c = sel[e] > v1
        i1 = jnp.where(c, e, i1)
        v1 = jnp.where(c, sel[e], v1)
    v2 = jnp.full((1, tm), -jnp.inf, F32)
    i2 = jnp.zeros((1, tm), I32)
    for e in range(N_EXP):
        cand = jnp.where(i1 == e, -jnp.inf, sel[e])
        c = cand > v2
        i2 = jnp.where(c, e, i2)
        v2 = jnp.where(c, cand, v2)
    ea = jnp.minimum(i1, i2)
    eb = jnp.maximum(i1, i2)
    lex = jnp.where(ea == 0, 0, jnp.where(ea == 1, 3, 5)) + eb - ea - 1
    pair = lex
    for i, p in enumerate(_PAIR_ORDER):
        if i != p:
            pair = jnp.where(lex == i, p, pair)
    bucket = grp * N_PAIRS + pair

    onehot = lax.broadcasted_iota(I32, (ROUTE_ROWS, tm), 0) == bucket
    cum = jnp.dot(onehot.astype(BF16), triu[...], preferred_element_type=F32)
    prev = base[:, 0:1]
    rank = jnp.sum(jnp.where(onehot, cum - 1.0 + prev, 0.0), axis=0, keepdims=True)
    new_base = prev + live * cum[:, tm - 1:tm]
    base[...] = jnp.broadcast_to(new_base, base.shape)
    cnt_ref[...] = jnp.broadcast_to(new_base, cnt_ref.shape)

    zrow = jnp.zeros((1, tm), F32)
    route_ref[...] = jnp.concatenate(
        [bucket.astype(F32), rank, zrow, zrow, zrow, zrow, zrow, zrow], axis=0)


def _mixer_kernel(qkvo_ref, grow_ref, nw_ref, yc_ref, x_ref, gt_ref, sc_ref, sh_ref, wo_ref, lg_ref,
                  lb_ref, wr_ref, br_ref, x1_ref, u2_ref, route_ref, cnt_ref,
                  cst, mst, sel, hn, base, triu, *, blocks_per_seq):
    i = pl.program_id(0)
    tm = x_ref.shape[0]

    @pl.when(i == 0)
    def _():
        _mlstm_selector(sel)
        hn[...] = jnp.zeros(hn.shape, BF16)
        base[...] = jnp.zeros(base.shape, F32)
        rid = lax.broadcasted_iota(I32, triu.shape, 0)
        cid = lax.broadcasted_iota(I32, triu.shape, 1)
        triu[...] = (rid <= cid).astype(BF16)

    @pl.when(i % blocks_per_seq == 0)
    def _():
        cst[...] = jnp.zeros(cst.shape, F32)
        mst[...] = jnp.zeros(mst.shape, F32)

    live = jnp.where(i > 0, 1.0, 0.0).astype(F32)
    parts = _outproj_parts(hn, yc_ref, x_ref, gt_ref, sc_ref, sh_ref, wo_ref, lg_ref, lb_ref,
                           wr_ref, br_ref, x1_ref, u2_ref, route_ref, cnt_ref, base, triu, live)
    _mlstm_block(qkvo_ref, grow_ref, nw_ref, hn, cst, mst, sel, between=parts)


def _mixer(qkvo, grow, nw, yc, x2, mod, layer, wo, lg, lb, wr, br, bsz, seq):
    tm = min(TOK_TILE, seq)
    nt = seq // tm
    tok = bsz * seq
    n = bsz * nt
    cur = lambda i: (jnp.minimum(i, n - 1), 0)
    prev = lambda i: (jnp.maximum(i - 1, 0), 0)
    const = lambda i: (0, 0)
    batch = lambda i: jnp.maximum(i - 1, 0) // nt
    return pl.pallas_call(
        functools.partial(_mixer_kernel, blocks_per_seq=nt),
        grid=(n + 1,),
        in_specs=[
            pl.BlockSpec((tm, 4 * M_WIDTH), cur),
            pl.BlockSpec((GATE_ROWS, tm), lambda i: (0, jnp.minimum(i, n - 1))),
            pl.BlockSpec((1, M_WIDTH), const),
            pl.BlockSpec((tm, C_WIDTH), prev),
            pl.BlockSpec((tm, D_MODEL), prev),
            _mod_spec(layer, MOD_GT1, bsz, batch),
            _mod_spec(layer, MOD_SC2, bsz, batch),
            _mod_spec(layer, MOD_SH2, bsz, batch),
            _layer_spec(wo, layer),
            pl.BlockSpec(lg.shape, const),
            pl.BlockSpec(lb.shape, const),
            pl.BlockSpec(wr.shape, const),
            pl.BlockSpec(br.shape, const),
        ],
        out_specs=[
            pl.BlockSpec((tm, D_MODEL), prev),
            pl.BlockSpec((tm * TOK_SUB, LANES), prev),
            pl.BlockSpec((8, tm), lambda i: (0, jnp.maximum(i - 1, 0))),
            pl.BlockSpec((ROUTE_ROWS, LANES), const),
        ],
        out_shape=[
            jax.ShapeDtypeStruct((tok, D_MODEL), F32),
            jax.ShapeDtypeStruct((tok * TOK_SUB, LANES), F32),
            jax.ShapeDtypeStruct((8, tok), F32),
            jax.ShapeDtypeStruct((ROUTE_ROWS, LANES), F32),
        ],
        scratch_shapes=[pltpu.VMEM((HEADS, D_HEAD, 2 * D_HEAD), F32),
                        pltpu.VMEM((8, CHUNK), F32),
                        pltpu.VMEM((LANES, HEADS * N_ROWQ * LANES), BF16),
                        pltpu.VMEM((tm, M_WIDTH), BF16),
                        pltpu.VMEM((ROUTE_ROWS, LANES), F32),
                        pltpu.VMEM((tm, tm), BF16)],
        compiler_params=_cparams(("arbitrary",)),
        name="mixer",
    )(qkvo, grow, nw, yc, x2, mod, mod, mod, wo, lg, lb, wr, br)


DISPATCH_TILE = 2048
DMA_UNROLL = 8


def _dispatch_kernel(pos_ref, pad_ref, u_ref, xs_ref, zeros, sem, zsem):
    ts = u_ref.shape[0] // TOK_SUB
    t0 = pl.program_id(0) * ts

    def zero_rows(first_row, n_rows_static):
        dst = pl.multiple_of(first_row * TOK_SUB, TOK_SUB)
        return pltpu.make_async_copy(zeros.at[pl.ds(0, n_rows_static * TOK_SUB), :],
                                     xs_ref.at[pl.ds(dst, n_rows_static * TOK_SUB), :], zsem)

    def fill_pads(wait):
        def bucket(b, carry):
            row = pad_ref[b]
            n = pad_ref[N_BUCKETS + b]
            size = MOE_STEP // 2
            while size >= 1:
                @pl.when((n & size) != 0)
                def _(row=row, size=size):
                    cp = zero_rows(row, size)
                    cp.wait() if wait else cp.start()
                row = row + (n & size)
                size //= 2
            return carry
        lax.fori_loop(0, N_BUCKETS, bucket, 0)

        def idle_tile(j, carry):
            cp = zero_rows(j * MOE_TILE, MOE_TILE)
            cp.wait() if wait else cp.start()
            return carry
        lax.fori_loop(pad_ref[2 * N_BUCKETS], xs_ref.shape[0] // (MOE_TILE * TOK_SUB), idle_tile, 0)

    @pl.when(pl.program_id(0) == 0)
    def _():
        zeros[...] = jnp.zeros(zeros.shape, F32)
        fill_pads(wait=False)
        fill_pads(wait=True)

    def issue(g, carry):
        for u in range(DMA_UNROLL):
            r = g * DMA_UNROLL + u
            dst = pl.multiple_of(pos_ref[t0 + r] * TOK_SUB, TOK_SUB)
            src = pl.multiple_of(r * TOK_SUB, TOK_SUB)
            pltpu.make_async_copy(u_ref.at[pl.ds(src, TOK_SUB), :],
                                  xs_ref.at[pl.ds(dst, TOK_SUB), :], sem).start(priority=u % 2)
        return carry

    lax.fori_loop(0, ts // DMA_UNROLL, issue, 0)
    pltpu.make_async_copy(u_ref, xs_ref.at[pl.ds(0, ts * TOK_SUB), :], sem).wait()


def _dispatch(pos, pad_start, u2, n_rows):
    tok = u2.shape[0] // TOK_SUB
    ts = min(DISPATCH_TILE, tok)
    return pl.pallas_call(
        _dispatch_kernel,
        grid_spec=pltpu.PrefetchScalarGridSpec(
            num_scalar_prefetch=2,
            grid=(tok // ts,),
            in_specs=[pl.BlockSpec((ts * TOK_SUB, LANES), lambda i, pos, pad: (i, 0))],
            out_specs=pl.BlockSpec(memory_space=pl.ANY),
            scratch_shapes=[pltpu.VMEM((MOE_TILE * TOK_SUB, LANES), F32),
                            pltpu.SemaphoreType.DMA(()), pltpu.SemaphoreType.DMA(())],
        ),
        out_shape=jax.ShapeDtypeStruct((n_rows * TOK_SUB, LANES), F32),
        compiler_params=_cparams(("arbitrary",)),
        name="dispatch",
    )(pos, pad_start, u2)


def _moe_kernel(s0_ref, s1_ref, tiles_ref, fresh0_ref, fresh1_ref, tb_ref, xb_ref, xs_ref, wsel_ref,
                bsel_ref, wg0, wu0, wd0, wg1, wu1, wd1, y_ref, wgs, wus, wds):
    del s0_ref, s1_ref, tb_ref, xb_ref
    j = pl.program_id(0)
    rows = MOE_TILE

    @pl.when(fresh0_ref[j] == 1)
    def _():
        wgs[0] = wg0[0].astype(BF16)
        wus[0] = wu0[0].astype(BF16)
        wds[0] = wd0[0].astype(BF16)

    @pl.when(fresh1_ref[j] == 1)
    def _():
        wgs[1] = wg1[0].astype(BF16)
        wus[1] = wu1[0].astype(BF16)
        wds[1] = wd1[0].astype(BF16)

    def experts(part):
        x = _load_token_major(xs_ref, rows, part * rows).astype(BF16)
        lg = jnp.dot(x, wsel_ref[0], preferred_element_type=F32) + bsel_ref[0]
        grp = [lg[:, i:i + 1] for i in range(N_GROUPS)]
        gmax = jnp.maximum(jnp.maximum(grp[0], grp[1]), jnp.maximum(grp[2], grp[3]))
        gsum = sum(jnp.exp(v - gmax) for v in grp)
        la, lb = lg[:, N_GROUPS:N_GROUPS + 1], lg[:, N_GROUPS + 1:N_GROUPS + 2]
        emax = jnp.maximum(la, lb)
        ea, eb = jnp.exp(la - emax), jnp.exp(lb - emax)
        scale = 1.0 / (gsum * (ea + eb))
        wcols = (ea * scale, eb * scale)
        acc = None
        for slot in range(2):
            g = jnp.dot(x, wgs[slot], preferred_element_type=F32)
            u = jnp.dot(x, wus[slot], preferred_element_type=F32)
            hid = (g * jax.nn.sigmoid(g) * u * wcols[slot]).astype(BF16)
            y = jnp.dot(hid, wds[slot], preferred_element_type=F32)
            acc = y if acc is None else acc + y
        _store_token_major(y_ref, acc, part * rows)

    n_parts = MOE_STEP // MOE_TILE
    for live in range(n_parts + 1):
        @pl.when(tiles_ref[j] == live)
        def _(live=live):
            for part in range(live):
                experts(part)
            if live < n_parts:
                lo = live * rows * TOK_SUB
                y_ref[lo:, :] = jnp.zeros((y_ref.shape[0] - lo, LANES), F32)


def _moe(plan, xs, wsel, bsel, w_gate, w_up, w_down, n_tiles):
    wsel1 = lambda j, s0, s1, v, f0, f1, tb, xb: (s0[j], 0, 0)
    wsel2 = lambda j, s0, s1, v, f0, f1, tb, xb: (s1[j], 0, 0)
    by_bucket = lambda j, s0, s1, v, f0, f1, tb, xb: (tb[j], 0, 0)
    up_spec = lambda sel: pl.BlockSpec((1, D_MODEL, D_EXPERT), sel)
    dn_spec = lambda sel: pl.BlockSpec((1, D_EXPERT, D_MODEL), sel)
    tile_rows = MOE_STEP * TOK_SUB
    return pl.pallas_call(
        _moe_kernel,
        grid_spec=pltpu.PrefetchScalarGridSpec(
            num_scalar_prefetch=7,
            grid=(n_tiles,),
            in_specs=[pl.BlockSpec((tile_rows, LANES), lambda j, s0, s1, v, f0, f1, tb, xb: (xb[j], 0)),
                      pl.BlockSpec((1, D_MODEL, LANES), by_bucket),
                      pl.BlockSpec((1, 1, LANES), by_bucket),
                      up_spec(wsel1), up_spec(wsel1), dn_spec(wsel1),
                      up_spec(wsel2), up_spec(wsel2), dn_spec(wsel2)],
            out_specs=pl.BlockSpec((tile_rows, LANES), lambda j, *_: (j, 0)),
            scratch_shapes=[pltpu.VMEM((2, D_MODEL, D_EXPERT), BF16),
                            pltpu.VMEM((2, D_MODEL, D_EXPERT), BF16),
                            pltpu.VMEM((2, D_EXPERT, D_MODEL), BF16)],
        ),
        out_shape=jax.ShapeDtypeStruct((n_tiles * tile_rows, LANES), F32),
        compiler_params=_cparams(("arbitrary",)),
        name="moe",
    )(*plan, xs, wsel, bsel, w_gate, w_up, w_down, w_gate, w_up, w_down)


def _combine_kernel(pos_ref, ys_ref, x1_ref, gt_ref, lg_ref, lb_ref, out_ref, ybuf, sems):
    ts = x1_ref.shape[0]
    i = pl.program_id(0)
    n = pl.num_programs(0)
    slot = i % 2

    def gather(tile, to_slot):
        def issue(g, carry):
            for u in range(DMA_UNROLL):
                r = g * DMA_UNROLL + u
                src = pl.multiple_of(pos_ref[tile * ts + r] * TOK_SUB, TOK_SUB)
                dst = pl.multiple_of(r * TOK_SUB, TOK_SUB)
                pltpu.make_async_copy(ys_ref.at[pl.ds(src, TOK_SUB), :],
                                      ybuf.at[to_slot, pl.ds(dst, TOK_SUB), :],
                                      sems.at[to_slot]).start(priority=u % 2)
            return carry
        lax.fori_loop(0, ts // DMA_UNROLL, issue, 0)

    @pl.when(i == 0)
    def _():
        gather(0, 0)

    @pl.when(i + 1 < n)
    def _():
        gather(i + 1, 1 - slot)

    pltpu.make_async_copy(ys_ref.at[pl.ds(0, ts * TOK_SUB), :], ybuf.at[slot], sems.at[slot]).wait()
    y = _load_token_major(ybuf.at[slot], ts)
    out_ref[...] = _layer_norm(ALPHA * x1_ref[...] + (1.0 + gt_ref[0]) * y, lg_ref[...], lb_ref[...])


def _combine(pos, ys, x1, mod, layer, lg, lb, bsz, seq):
    ts = min(TOK_TILE, seq)
    nt = seq // ts
    tok = bsz * seq
    return pl.pallas_call(
        _combine_kernel,
        grid_spec=pltpu.PrefetchScalarGridSpec(
            num_scalar_prefetch=1,
            grid=(tok // ts,),
            in_specs=[pl.BlockSpec(memory_space=pl.ANY),
                      pl.BlockSpec((ts, D_MODEL), lambda i, pos: (i, 0)),
                      _mod_spec(layer, MOD_GT2, bsz, lambda i, pos: i // nt),
                      pl.BlockSpec((1, D_MODEL), lambda i, pos: (0, 0)),
                      pl.BlockSpec((1, D_MODEL), lambda i, pos: (0, 0))],
            out_specs=pl.BlockSpec((ts, D_MODEL), lambda i, pos: (i, 0)),
            scratch_shapes=[pltpu.VMEM((2, ts * TOK_SUB, LANES), F32), pltpu.SemaphoreType.DMA((2,))],
        ),
        out_shape=jax.ShapeDtypeStruct((tok, D_MODEL), F32),
        compiler_params=_cparams(("arbitrary",)),
        name="combine",
    )(pos, ys, x1, mod, lg, lb)


def _route_plan(route, cnt, n_tiles, layer):
    bucket = route[0].astype(I32)
    rank = route[1].astype(I32)
    counts = cnt[:N_BUCKETS, 0].astype(I32)
    tiles_b = (counts + MOE_STEP - 1) // MOE_STEP
    tile_end = jnp.cumsum(tiles_b)
    row_off = (tile_end - tiles_b) * MOE_STEP
    bucket_ids = jnp.arange(N_BUCKETS, dtype=I32)
    pos = jnp.sum(jnp.where(bucket[None, :] == bucket_ids[:, None], row_off[:, None], 0), axis=0) + rank
    total = tile_end[-1]
    j = jnp.arange(n_tiles, dtype=I32)
    tb = jnp.sum((tile_end[None, :] <= jnp.minimum(j, total - 1)[:, None]).astype(I32), axis=1)
    tb = jnp.clip(tb, 0, N_BUCKETS - 1)
    hit = tb[None, :] == bucket_ids[:, None]
    rows_left = jnp.sum(jnp.where(hit, (row_off + counts)[:, None], 0), axis=0) - j * MOE_STEP
    tiles_j = jnp.clip((rows_left + MOE_TILE - 1) // MOE_TILE, 0, MOE_STEP // MOE_TILE)
    tiles_j = jnp.where(j < total, tiles_j, 0)
    grp = tb // N_PAIRS
    pair = tb % N_PAIRS
    first = layer * (N_GROUPS * N_EXP) + grp * N_EXP
    s0 = first + sum((pair == i).astype(I32) * _SLOT0[i] for i in range(N_PAIRS))
    s1 = first + sum((pair == i).astype(I32) * _SLOT1[i] for i in range(N_PAIRS))
    one = jnp.ones((1,), I32)
    fresh0 = jnp.concatenate([one, (s0[1:] != s0[:-1]).astype(I32)])
    fresh1 = jnp.concatenate([one, (s1[1:] != s1[:-1]).astype(I32)])
    xblk = jnp.minimum(j, total - 1)
    pads = jnp.concatenate([row_off + counts, tiles_b * MOE_STEP - counts,
                            (total * (MOE_STEP // MOE_TILE))[None]])
    return pos, pads, (s0, s1, tiles_j, fresh0, fresh1, layer * N_BUCKETS + tb, xblk)


def kernel(x, c, w_ada, b_ada, w_in, b_gates, mh_norm_w, w_conv, w_out, ln1_g, ln1_b,
           w_grp, b_grp, w_router, b_router, w_gate, w_up, w_down, ln2_g, ln2_b):
    bsz, seq, _ = x.shape
    tok = bsz * seq
    n_rows = tok + N_BUCKETS * MOE_STEP
    n_tiles = n_rows // MOE_STEP
    n_exp_total = N_GROUPS * N_EXP

    wg_all = w_gate.reshape(DEPTH * n_exp_total, D_MODEL, D_EXPERT)
    wu_all = w_up.reshape(DEPTH * n_exp_total, D_MODEL, D_EXPERT)
    wd_all = w_down.reshape(DEPTH * n_exp_total, D_EXPERT, D_MODEL)
    mod = _ada_mod(c, w_ada, b_ada).reshape(DEPTH * bsz * N_MOD, 1, D_MODEL)
    x2 = x.reshape(tok, D_MODEL)
    wq = w_in[:, :, :4 * M_WIDTH].astype(BF16)
    wg = jnp.pad(w_in[:, :, 4 * M_WIDTH:4 * M_WIDTH + 2 * HEADS],
                 ((0, 0), (0, 0), (0, LANES - 2 * HEADS))).astype(BF16)
    wc = w_in[:, :, 4 * M_WIDTH + 2 * HEADS:].astype(BF16)
    wo = w_out.astype(BF16)
    lb_shape = (DEPTH, N_BUCKETS)
    wsel = jnp.concatenate([jnp.broadcast_to(w_grp[:, None], lb_shape + (D_MODEL, N_GROUPS)),
                            jnp.swapaxes(w_router[:, :, _BUCKET_S0], 1, 2)[..., None],
                            jnp.swapaxes(w_router[:, :, _BUCKET_S1], 1, 2)[..., None]], axis=3)
    wsel = jnp.pad(wsel, ((0, 0), (0, 0), (0, 0), (0, LANES - N_GROUPS - 2))).astype(BF16)
    wsel = wsel.reshape(DEPTH * N_BUCKETS, D_MODEL, LANES)
    bsel = jnp.concatenate([jnp.broadcast_to(b_grp[:, None], lb_shape + (N_GROUPS,)),
                            b_router[:, _BUCKET_S0][..., None], b_router[:, _BUCKET_S1][..., None]], axis=2)
    bsel = jnp.pad(bsel, ((0, 0), (0, 0), (0, LANES - N_GROUPS - 2))).reshape(DEPTH * N_BUCKETS, 1, LANES)
    for l in range(DEPTH):
        bg = jnp.pad(b_gates[l], (0, LANES - 2 * HEADS)).reshape(1, LANES)
        qkvo, grow, yc = _inproj(x2, mod, l, wq, wg, wc, bg, w_conv[l], bsz, seq)
        n_logit = N_GROUPS + n_exp_total
        wr = jnp.pad(jnp.concatenate([w_grp[l], w_router[l]], axis=1).T,
                     ((0, ROUTE_ROWS - n_logit), (0, 0))).astype(BF16)
        br = jnp.pad(jnp.concatenate([b_grp[l], b_router[l]]), (0, ROUTE_ROWS - n_logit))
        br = jnp.broadcast_to(br[:, None], (ROUTE_ROWS, LANES))
        x1, u2, route, cnt = _mixer(qkvo, grow, mh_norm_w[l].reshape(1, M_WIDTH), yc, x2, mod, l,
                                    wo, ln1_g[l].reshape(1, D_MODEL),
                                    ln1_b[l].reshape(1, D_MODEL), wr, br, bsz, seq)
        pos, pads, plan = _route_plan(route, cnt, n_tiles, l)
        xs = _dispatch(pos, pads, u2, n_rows)
        ys = _moe(plan, xs, wsel, bsel, wg_all, wu_all, wd_all, n_tiles)
        x2 = _combine(pos, ys, x1, mod, l, ln2_g[l].reshape(1, D_MODEL),
                      ln2_b[l].reshape(1, D_MODEL), bsz, seq)
    return x2.reshape(bsz, seq, D_MODEL)
```

```python
import functools
import math

import jax
import jax.numpy as jnp
import numpy as np
from jax import lax
from jax.experimental import pallas as pl
from jax.experimental.pallas import tpu as pltpu

F32 = jnp.float32
BF16 = jnp.bfloat16
I32 = jnp.int32

D_MODEL = 1024
DEPTH = 2
HEADS = 4
D_HEAD = 128
M_WIDTH = HEADS * D_HEAD
C_WIDTH = D_MODEL - M_WIDTH
N_GROUPS = 4
N_EXP = 4
D_EXPERT = 512
N_PAIRS = 6
N_BUCKETS = N_GROUPS * N_PAIRS
N_MOD = 6
MOD_SH1, MOD_SC1, MOD_GT1, MOD_SH2, MOD_SC2, MOD_GT2 = range(N_MOD)
ALPHA = (2 * DEPTH) ** 0.25
LN_EPS = 1e-5
QK_SCALE = D_HEAD ** -0.5
LOG_QK_SCALE = math.log(QK_SCALE)
GATE_ROWS = 24

LANES = 128
ROUTE_ROWS = 32

TOK_TILE = 512
INPROJ_TILE = 1024
CHUNK = 256
MOE_TILE = 256
MOE_STEP = 2 * MOE_TILE
VMEM_LIMIT = 56 * 1024 * 1024

_PAIR_ORDER = (0, 2, 3, 1, 4, 5)
_SLOT0 = (0, 2, 2, 3, 3, 3)
_SLOT1 = (1, 1, 0, 0, 1, 2)
_BUCKET_S0 = np.array([g * N_EXP + _SLOT0[p] for g in range(N_GROUPS) for p in range(N_PAIRS)])
_BUCKET_S1 = np.array([g * N_EXP + _SLOT1[p] for g in range(N_GROUPS) for p in range(N_PAIRS)])


def _cparams(sem):
    return pltpu.CompilerParams(dimension_semantics=sem, vmem_limit_bytes=VMEM_LIMIT)


def _ada_kernel(c_ref, w_ref, b_ref, o_ref):
    c = c_ref[...]
    cond = (c * jax.nn.sigmoid(c)).astype(BF16)
    o_ref[0] = jnp.dot(cond, w_ref[0].astype(BF16), preferred_element_type=F32) + b_ref[0]


def _ada_mod(c, w_ada, b_ada):
    bsz = c.shape[0]
    nblk = w_ada.shape[2] // D_MODEL
    return pl.pallas_call(
        _ada_kernel,
        grid=(DEPTH, nblk),
        in_specs=[
            pl.BlockSpec((bsz, D_MODEL), lambda l, j: (0, 0)),
            pl.BlockSpec((1, D_MODEL, D_MODEL), lambda l, j: (l, 0, j)),
            pl.BlockSpec((1, 1, D_MODEL), lambda l, j: (l, 0, j)),
        ],
        out_specs=pl.BlockSpec((1, bsz, D_MODEL), lambda l, j: (l, 0, j)),
        out_shape=jax.ShapeDtypeStruct((DEPTH, bsz, nblk * D_MODEL), F32),
        compiler_params=_cparams(("arbitrary", "arbitrary")),
        name="ada_mod",
    )(c, w_ada, b_ada.reshape(DEPTH, 1, -1))


def _log_sigmoid(x):
    return jnp.minimum(x, 0.0) - jnp.log1p(jnp.exp(-jnp.abs(x)))


def _chunk_scan(x, op, identity):
    lane = lax.broadcasted_iota(I32, x.shape, 1) & (CHUNK - 1)
    sh = 1
    while sh < CHUNK:
        x = op(x, jnp.where(lane >= sh, pltpu.roll(x, sh, 1), identity))
        sh *= 2
    return x


def _inproj_kernel(x_ref, sc_ref, sh_ref, wq_ref, wg_ref, wc_ref, bg_ref, wconv_ref,
                   qkvo_ref, grow_ref, yc_ref, zbuf):
    tm = x_ref.shape[0]
    u = (x_ref[...] * (1.0 + sc_ref[0]) + sh_ref[0]).astype(BF16)
    qkvo_ref[...] = jnp.dot(u, wq_ref[...], preferred_element_type=F32).astype(BF16)
    g = jnp.dot(u, wg_ref[...], preferred_element_type=F32) + bg_ref[...]
    gt = g.T[:8]
    head_row = lax.broadcasted_iota(I32, gt.shape, 0) < HEADS
    ig = jnp.where(head_row, gt, 0.0)
    logf = jnp.where(head_row, _log_sigmoid(pltpu.roll(gt, HEADS, 0)), 0.0)
    bcum = _chunk_scan(logf, jnp.add, 0.0)
    ug = ig - bcum
    grow_ref[0:8, :] = ug
    grow_ref[8:16, :] = bcum
    grow_ref[16:24, :] = _chunk_scan(ug, jnp.maximum, -jnp.inf)
    pc = jnp.dot(u, wc_ref[...], preferred_element_type=F32)
    z = pc[:, C_WIDTH:2 * C_WIDTH] * pc[:, 2 * C_WIDTH:]

    @pl.when(pl.program_id(1) == 0)
    def _():
        zbuf[0:8, :] = jnp.zeros((8, C_WIDTH), F32)

    zbuf[8:8 + tm, :] = z
    zc = (wconv_ref[0:1, :] * zbuf[6:6 + tm, :] + wconv_ref[1:2, :] * zbuf[7:7 + tm, :]
          + wconv_ref[2:3, :] * z)
    yc_ref[...] = (pc[:, :C_WIDTH] * zc).astype(BF16)
    zbuf[0:8, :] = zbuf[tm:tm + 8, :]


def _mod_spec(layer, which, bsz, batch_of):
    return pl.BlockSpec((1, 1, D_MODEL),
                        lambda *idx: ((layer * bsz + batch_of(*idx)) * N_MOD + which, 0, 0))


def _layer_spec(stacked, layer):
    return pl.BlockSpec((None,) + stacked.shape[1:], lambda *idx: (layer, 0, 0))


def _inproj(x2, mod, layer, wq, wg, wc, bg, wconv, bsz, seq):
    tm = min(INPROJ_TILE, seq)
    nt = seq // tm
    tok = bsz * seq
    row = lambda b, s: (b * nt + s, 0)
    const = lambda b, s: (0, 0)
    batch = lambda b, s: b
    return pl.pallas_call(
        _inproj_kernel,
        grid=(bsz, nt),
        in_specs=[
            pl.BlockSpec((tm, D_MODEL), row),
            _mod_spec(layer, MOD_SC1, bsz, batch),
            _mod_spec(layer, MOD_SH1, bsz, batch),
            _layer_spec(wq, layer),
            _layer_spec(wg, layer),
            _layer_spec(wc, layer),
            pl.BlockSpec(bg.shape, const),
            pl.BlockSpec(wconv.shape, const),
        ],
        out_specs=[
            pl.BlockSpec((tm, 4 * M_WIDTH), row),
            pl.BlockSpec((GATE_ROWS, tm), lambda b, s: (0, b * nt + s)),
            pl.BlockSpec((tm, C_WIDTH), row),
        ],
        out_shape=[
            jax.ShapeDtypeStruct((tok, 4 * M_WIDTH), BF16),
            jax.ShapeDtypeStruct((GATE_ROWS, tok), F32),
            jax.ShapeDtypeStruct((tok, C_WIDTH), BF16),
        ],
        scratch_shapes=[pltpu.VMEM((tm + 8, C_WIDTH), F32)],
        compiler_params=_cparams(("arbitrary", "arbitrary")),
        name="inproj",
    )(x2, mod, mod, wq, wg, wc, bg, wconv)


N_ROWQ = 4
SPLIT = 3


def _split3(x):
    hi = x.astype(BF16).astype(F32)
    r1 = x - hi
    mid = r1.astype(BF16).astype(F32)
    return [hi, mid, r1 - mid]


def _mlstm_selector(sel):
    r = lax.broadcasted_iota(I32, sel.shape, 0)
    c = lax.broadcasted_iota(I32, sel.shape, 1) // LANES
    rq = r // 8
    quantity = ((rq >= SPLIT).astype(I32) + (rq >= 2 * SPLIT).astype(I32)
                + (rq >= 3 * SPLIT).astype(I32))
    hit = ((r % 8) == c // N_ROWQ) & (quantity == c % N_ROWQ) & (rq < N_ROWQ * SPLIT)
    sel[...] = hit.astype(BF16)


def _mlstm_block(qkvo_ref, grow_ref, nw_ref, out_ref, cst, mst, sel, between=()):
    between = tuple(between) + (None,) * 3
    sb = qkvo_ref.shape[0]
    L = CHUNK
    causal = lax.broadcasted_iota(I32, (L, L), 0) >= lax.broadcasted_iota(I32, (L, L), 1)
    ones_ext = jnp.ones((L, D_HEAD), BF16)
    pad_rows = jnp.zeros((LANES - 8 * N_ROWQ * SPLIT, L), F32)
    eye = (lax.broadcasted_iota(I32, (D_HEAD, D_HEAD), 0)
           == lax.broadcasted_iota(I32, (D_HEAD, D_HEAD), 1)).astype(BF16)
    n_chunks = sb // L
    pairs = [(c, h) for c in range(n_chunks) for h in range(HEADS)]

    def cols(part, c, h):
        c0 = part * M_WIDTH + h * D_HEAD
        return qkvo_ref[c * L:(c + 1) * L, c0:c0 + D_HEAD]

    m_prev = mst[...]
    stack_t, us, decay = [], [], []
    for c in range(n_chunks):
        ug = grow_ref[0:8, c * L:(c + 1) * L]
        bcum = grow_ref[8:16, c * L:(c + 1) * L]
        cmax = grow_ref[16:24, c * L:(c + 1) * L]
        big_m = jnp.maximum(m_prev, cmax)
        m_last = jnp.broadcast_to(big_m[:, L - 1:L], (8, L))
        g_tot = jnp.broadcast_to(bcum[:, L - 1:L], (8, L))
        wq = jnp.exp(m_prev - big_m) * QK_SCALE
        log_em = -(bcum + big_m)
        wk = jnp.exp(ug - m_last)
        decay.append(jnp.exp(m_prev - m_last))
        us.append(ug + LOG_QK_SCALE)
        m_prev = g_tot + m_last
        stack = jnp.concatenate(_split3(big_m) + _split3(wq) + _split3(log_em) + _split3(wk)
                                + [pad_rows], axis=0)
        stack_t.append(stack.T.astype(BF16))
    mst[...] = m_prev
    if between[0] is not None:
        between[0]()

    reps = {}

    def rep(c, h, j):
        if (c, h) not in reps:
            c0 = h * N_ROWQ * LANES
            reps[(c, h)] = jnp.dot(stack_t[c], sel[:, c0:c0 + N_ROWQ * LANES],
                                   preferred_element_type=F32)
        return reps[(c, h)][:, j * LANES:(j + 1) * LANES]

    s_mat = {ch: lax.dot_general(cols(0, *ch), cols(1, *ch), (((1,), (1,)), ((), ())),
                                 preferred_element_type=F32) for ch in pairs}
    kw = {ch: (cols(1, *ch).astype(F32) * rep(*ch, 3)).astype(BF16) for ch in pairs}
    kw_t = {ch: lax.dot_general(eye, kw[ch], (((1,), (1,)), ((), ())),
                                preferred_element_type=F32).astype(BF16) for ch in pairs}
    if between[1] is not None:
        between[1]()
    lhs = {}
    for ch in pairs:
        c, h = ch
        big_m_rows = jnp.concatenate([rep(c, h, 0)] * (L // LANES), axis=1)
        p = jnp.where(causal, s_mat[ch] * jnp.exp(us[c][h:h + 1, :] - big_m_rows), 0.0)
        qw = cols(0, c, h).astype(F32) * rep(c, h, 1)
        lhs[ch] = jnp.concatenate([p.astype(BF16), qw.astype(BF16)], axis=1)
    vext = {ch: jnp.concatenate([cols(2, *ch), ones_ext], axis=1) for ch in pairs}
    upd = {ch: jnp.dot(kw_t[ch], vext[ch], preferred_element_type=F32) for ch in pairs}
    for c in range(n_chunks):
        tot = {}
        for h in range(HEADS):
            state = cst[h]
            rhs = jnp.concatenate([vext[(c, h)], state.astype(BF16)], axis=0)
            tot[h] = jnp.dot(lhs[(c, h)], rhs, preferred_element_type=F32)
            dec = jnp.concatenate([decay[c][h:h + 1, :LANES]] * 2, axis=1)
            cst[h] = dec * state + upd[(c, h)]
        for h in range(HEADS):
            c0 = h * D_HEAD
            num, den = tot[h][:, :D_HEAD], tot[h][:, D_HEAD:]
            hh = num / jnp.maximum(jnp.abs(den), jnp.exp(rep(c, h, 2)))
            mu = jnp.mean(hh, axis=1, keepdims=True)
            var = jnp.mean(jnp.square(hh - mu), axis=1, keepdims=True)
            hn = (hh - mu) * lax.rsqrt(var + LN_EPS) * nw_ref[:, c0:c0 + D_HEAD]
            og = jax.nn.sigmoid(cols(3, c, h).astype(F32))
            out_ref[c * L:(c + 1) * L, c0:c0 + D_HEAD] = (hn * og).astype(BF16)
        if c == 0 and between[2] is not None:
            between[2]()


def _layer_norm(r, g, b):
    mu = jnp.mean(r, axis=-1, keepdims=True)
    var = jnp.mean(jnp.square(r - mu), axis=-1, keepdims=True)
    return (r - mu) * lax.rsqrt(var + LN_EPS) * g + b


TOK_SUB = D_MODEL // LANES


def _store_token_major(ref, val, first_row=0):
    rows = val.shape[0]
    for k in range(TOK_SUB):
        ref[pl.ds(first_row * TOK_SUB + k, rows, stride=TOK_SUB), :] = val[:, k * LANES:(k + 1) * LANES]


def _load_token_major(ref, rows, first_row=0):
    return jnp.concatenate([ref[pl.ds(first_row * TOK_SUB + k, rows, stride=TOK_SUB), :]
                            for k in range(TOK_SUB)], axis=1)


def _outproj_parts(hn_ref, yc_ref, x_ref, gt_ref, sc_ref, sh_ref, wo_ref, lg_ref, lb_ref,
                   wr_ref, br_ref, x1_ref, u2_ref, route_ref, cnt_ref, base, triu, live):
    tm = x_ref.shape[0]
    held = {}

    def project():
        held["mix"] = (jnp.dot(hn_ref[...], wo_ref[0:M_WIDTH, :], preferred_element_type=F32)
                       + jnp.dot(yc_ref[...], wo_ref[M_WIDTH:, :], preferred_element_type=F32))

    def normalise():
        x1 = _layer_norm(ALPHA * x_ref[...] + (1.0 + gt_ref[0]) * held["mix"], lg_ref[...], lb_ref[...])
        x1_ref[...] = x1
        u2 = x1 * (1.0 + sc_ref[0]) + sh_ref[0]
        _store_token_major(u2_ref, u2)
        held["logits"] = lax.dot_general(wr_ref[...], u2.astype(BF16), (((1,), (1,)), ((), ())),
                                         preferred_element_type=F32) + br_ref[:, 0:1]

    def route():
        _route_block(held["logits"], route_ref, cnt_ref, base, triu, live, tm)

    return project, normalise, route


def _route_block(lt, route_ref, cnt_ref, base, triu, live, tm):
    lg = [lt[j:j + 1, :] for j in range(N_GROUPS + N_GROUPS * N_EXP)]
    best = lg[0]
    grp = jnp.zeros((1, tm), I32)
    for j in range(1, N_GROUPS):
        c = lg[j] > best
        grp = jnp.where(c, j, grp)
        best = jnp.where(c, lg[j], best)
    sel = []
    for e in range(N_EXP):
        val = lg[N_GROUPS + e]
        for gg in range(1, N_GROUPS):
            val = jnp.where(grp == gg, lg[N_GROUPS + gg * N_EXP + e], val)
        sel.append(val)
    v1 = sel[0]
    i1 = jnp.zeros((1, tm), I32)
    for e in range(1, N_EXP):
        c = sel[e] > v1
        i1 = jnp.where(c, e, i1)
        v1 = jnp.where(c, sel[e], v1)
    v2 = jnp.full((1, tm), -jnp.inf, F32)
    i2 = jnp.zeros((1, tm), I32)
    for e in range(N_EXP):
        cand = jnp.where(i1 == e, -jnp.inf, sel[e])
        c = cand > v2
        i2 = jnp.where(c, e, i2)
        v2 = jnp.where(c, cand, v2)
    ea = jnp.minimum(i1, i2)
    eb = jnp.maximum(i1, i2)
    lex = jnp.where(ea == 0, 0, jnp.where(ea == 1, 3, 5)) + eb - ea - 1
    pair = lex
    for i, p in enumerate(_PAIR_ORDER):
        if i != p:
            pair = jnp.where(lex == i, p, pair)
    bucket = grp * N_PAIRS + pair

    onehot = lax.broadcasted_iota(I32, (ROUTE_ROWS, tm), 0) == bucket
    cum = jnp.dot(onehot.astype(BF16), triu[...], preferred_element_type=F32)
    prev = base[:, 0:1]
    rank = jnp.sum(jnp.where(onehot, cum - 1.0 + prev, 0.0), axis=0, keepdims=True)
    new_base = prev + live * cum[:, tm - 1:tm]
    base[...] = jnp.broadcast_to(new_base, base.shape)
    cnt_ref[...] = jnp.broadcast_to(new_base, cnt_ref.shape)

    zrow = jnp.zeros((1, tm), F32)
    route_ref[...] = jnp.concatenate(
        [bucket.astype(F32), rank, zrow, zrow, zrow, zrow, zrow, zrow], axis=0)


def _mixer_kernel(qkvo_ref, grow_ref, nw_ref, yc_ref, x_ref, gt_ref, sc_ref, sh_ref, wo_ref, lg_ref,
                  lb_ref, wr_ref, br_ref, x1_ref, u2_ref, route_ref, cnt_ref,
                  cst, mst, sel, hn, base, triu, *, blocks_per_seq):
    i = pl.program_id(0)
    tm = x_ref.shape[0]

    @pl.when(i == 0)
    def _():
        _mlstm_selector(sel)
        hn[...] = jnp.zeros(hn.shape, BF16)
        base[...] = jnp.zeros(base.shape, F32)
        rid = lax.broadcasted_iota(I32, triu.shape, 0)
        cid = lax.broadcasted_iota(I32, triu.shape, 1)
        triu[...] = (rid <= cid).astype(BF16)

    @pl.when(i % blocks_per_seq == 0)
    def _():
        cst[...] = jnp.zeros(cst.shape, F32)
        mst[...] = jnp.zeros(mst.shape, F32)

    live = jnp.where(i > 0, 1.0, 0.0).astype(F32)
    parts = _outproj_parts(hn, yc_ref, x_ref, gt_ref, sc_ref, sh_ref, wo_ref, lg_ref, lb_ref,
                           wr_ref, br_ref, x1_ref, u2_ref, route_ref, cnt_ref, base, triu, live)
    _mlstm_block(qkvo_ref, grow_ref, nw_ref, hn, cst, mst, sel, between=parts)


def _mixer(qkvo, grow, nw, yc, x2, mod, layer, wo, lg, lb, wr, br, bsz, seq):
    tm = min(TOK_TILE, seq)
    nt = seq // tm
    tok = bsz * seq
    n = bsz * nt
    cur = lambda i: (jnp.minimum(i, n - 1), 0)
    prev = lambda i: (jnp.maximum(i - 1, 0), 0)
    const = lambda i: (0, 0)
    batch = lambda i: jnp.maximum(i - 1, 0) // nt
    return pl.pallas_call(
        functools.partial(_mixer_kernel, blocks_per_seq=nt),
        grid=(n + 1,),
        in_specs=[
            pl.BlockSpec((tm, 4 * M_WIDTH), cur),
            pl.BlockSpec((GATE_ROWS, tm), lambda i: (0, jnp.minimum(i, n - 1))),
            pl.BlockSpec((1, M_WIDTH), const),
            pl.BlockSpec((tm, C_WIDTH), prev),
            pl.BlockSpec((tm, D_MODEL), prev),
            _mod_spec(layer, MOD_GT1, bsz, batch),
            _mod_spec(layer, MOD_SC2, bsz, batch),
            _mod_spec(layer, MOD_SH2, bsz, batch),
            _layer_spec(wo, layer),
            pl.BlockSpec(lg.shape, const),
            pl.BlockSpec(lb.shape, const),
            pl.BlockSpec(wr.shape, const),
            pl.BlockSpec(br.shape, const),
        ],
        out_specs=[
            pl.BlockSpec((tm, D_MODEL), prev),
            pl.BlockSpec((tm * TOK_SUB, LANES), prev),
            pl.BlockSpec((8, tm), lambda i: (0, jnp.maximum(i - 1, 0))),
            pl.BlockSpec((ROUTE_ROWS, LANES), const),
        ],
        out_shape=[
            jax.ShapeDtypeStruct((tok, D_MODEL), F32),
            jax.ShapeDtypeStruct((tok * TOK_SUB, LANES), F32),
            jax.ShapeDtypeStruct((8, tok), F32),
            jax.ShapeDtypeStruct((ROUTE_ROWS, LANES), F32),
        ],
        scratch_shapes=[pltpu.VMEM((HEADS, D_HEAD, 2 * D_HEAD), F32),
                        pltpu.VMEM((8, CHUNK), F32),
                        pltpu.VMEM((LANES, HEADS * N_ROWQ * LANES), BF16),
                        pltpu.VMEM((tm, M_WIDTH), BF16),
                        pltpu.VMEM((ROUTE_ROWS, LANES), F32),
                        pltpu.VMEM((tm, tm), BF16)],
        compiler_params=_cparams(("arbitrary",)),
        name="mixer",
    )(qkvo, grow, nw, yc, x2, mod, mod, mod, wo, lg, lb, wr, br)


DISPATCH_TILE = 2048
DMA_UNROLL = 8


def _dispatch_kernel(pos_ref, pad_ref, u_ref, xs_ref, zeros, sem, zsem):
    ts = u_ref.shape[0] // TOK_SUB
    t0 = pl.program_id(0) * ts

    def zero_rows(first_row, n_rows_static):
        dst = pl.multiple_of(first_row * TOK_SUB, TOK_SUB)
        return pltpu.make_async_copy(zeros.at[pl.ds(0, n_rows_static * TOK_SUB), :],
                                     xs_ref.at[pl.ds(dst, n_rows_static * TOK_SUB), :], zsem)

    def fill_pads(wait):
        def bucket(b, carry):
            row = pad_ref[b]
            n = pad_ref[N_BUCKETS + b]
            size = MOE_STEP // 2
            while size >= 1:
                @pl.when((n & size) != 0)
                def _(row=row, size=size):
                    cp = zero_rows(row, size)
                    cp.wait() if wait else cp.start()
                row = row + (n & size)
                size //= 2
            return carry
        lax.fori_loop(0, N_BUCKETS, bucket, 0)

        def idle_tile(j, carry):
            cp = zero_rows(j * MOE_TILE, MOE_TILE)
            cp.wait() if wait else cp.start()
            return carry
        lax.fori_loop(pad_ref[2 * N_BUCKETS], xs_ref.shape[0] // (MOE_TILE * TOK_SUB), idle_tile, 0)

    @pl.when(pl.program_id(0) == 0)
    def _():
        zeros[...] = jnp.zeros(zeros.shape, F32)
        fill_pads(wait=False)
        fill_pads(wait=True)

    def issue(g, carry):
        for u in range(DMA_UNROLL):
            r = g * DMA_UNROLL + u
            dst = pl.multiple_of(pos_ref[t0 + r] * TOK_SUB, TOK_SUB)
            src = pl.multiple_of(r * TOK_SUB, TOK_SUB)
            pltpu.make_async_copy(u_ref.at[pl.ds(src, TOK_SUB), :],
                                  xs_ref.at[pl.ds(dst, TOK_SUB), :], sem).start(priority=u % 2)
        return carry

    lax.fori_loop(0, ts // DMA_UNROLL, issue, 0)
    pltpu.make_async_copy(u_ref, xs_ref.at[pl.ds(0, ts * TOK_SUB), :], sem).wait()


def _dispatch(pos, pad_start, u2, n_rows):
    tok = u2.shape[0] // TOK_SUB
    ts = min(DISPATCH_TILE, tok)
    return pl.pallas_call(
        _dispatch_kernel,
        grid_spec=pltpu.PrefetchScalarGridSpec(
            num_scalar_prefetch=2,
            grid=(tok // ts,),
            in_specs=[pl.BlockSpec((ts * TOK_SUB, LANES), lambda i, pos, pad: (i, 0))],
            out_specs=pl.BlockSpec(memory_space=pl.ANY),
            scratch_shapes=[pltpu.VMEM((MOE_TILE * TOK_SUB, LANES), F32),
                            pltpu.SemaphoreType.DMA(()), pltpu.SemaphoreType.DMA(())],
        ),
        out_shape=jax.ShapeDtypeStruct((n_rows * TOK_SUB, LANES), F32),
        compiler_params=_cparams(("arbitrary",)),
        name="dispatch",
    )(pos, pad_start, u2)


def _moe_kernel(s0_ref, s1_ref, tiles_ref, fresh0_ref, fresh1_ref, tb_ref, xb_ref, xs_ref, wsel_ref,
                bsel_ref, wg0, wu0, wd0, wg1, wu1, wd1, y_ref, wgs, wus, wds):
    del s0_ref, s1_ref, tb_ref, xb_ref
    j = pl.program_id(0)
    rows = MOE_TILE

    @pl.when(fresh0_ref[j] == 1)
    def _():
        wgs[0] = wg0[0].astype(BF16)
        wus[0] = wu0[0].astype(BF16)
        wds[0] = wd0[0].astype(BF16)

    @pl.when(fresh1_ref[j] == 1)
    def _():
        wgs[1] = wg1[0].astype(BF16)
        wus[1] = wu1[0].astype(BF16)
        wds[1] = wd1[0].astype(BF16)

    def experts(first_row, n_rows):
        x = _load_token_major(xs_ref, n_rows, first_row).astype(BF16)
        lg = jnp.dot(x, wsel_ref[0], preferred_element_type=F32) + bsel_ref[0]
        grp = [lg[:, i:i + 1] for i in range(N_GROUPS)]
        gmax = jnp.maximum(jnp.maximum(grp[0], grp[1]), jnp.maximum(grp[2], grp[3]))
        gsum = sum(jnp.exp(v - gmax) for v in grp)
        la, lb = lg[:, N_GROUPS:N_GROUPS + 1], lg[:, N_GROUPS + 1:N_GROUPS + 2]
        emax = jnp.maximum(la, lb)
        ea, eb = jnp.exp(la - emax), jnp.exp(lb - emax)
        scale = 1.0 / (gsum * (ea + eb))
        wcols = (ea * scale, eb * scale)
        acc = None
        for slot in range(2):
            g = jnp.dot(x, wgs[slot], preferred_element_type=F32)
            u = jnp.dot(x, wus[slot], preferred_element_type=F32)
            hid = (g * jax.nn.sigmoid(g) * u * wcols[slot]).astype(BF16)
            y = jnp.dot(hid, wds[slot], preferred_element_type=F32)
            acc = y if acc is None else acc + y
        _store_token_major(y_ref, acc, first_row)

    half = rows // 2
    for live in range(2 * MOE_STEP // MOE_TILE + 1):
        @pl.when(tiles_ref[j] == live)
        def _(live=live):
            for part in range(live // 2):
                experts(part * rows, rows)
            if live % 2:
                experts((live // 2) * rows, half)
            lo = live * half * TOK_SUB
            if lo < y_ref.shape[0]:
                y_ref[lo:, :] = jnp.zeros((y_ref.shape[0] - lo, LANES), F32)


def _moe(plan, xs, wsel, bsel, w_gate, w_up, w_down, n_tiles):
    wsel1 = lambda j, s0, s1, v, f0, f1, tb, xb: (s0[j], 0, 0)
    wsel2 = lambda j, s0, s1, v, f0, f1, tb, xb: (s1[j], 0, 0)
    by_bucket = lambda j, s0, s1, v, f0, f1, tb, xb: (tb[j], 0, 0)
    up_spec = lambda sel: pl.BlockSpec((1, D_MODEL, D_EXPERT), sel)
    dn_spec = lambda sel: pl.BlockSpec((1, D_EXPERT, D_MODEL), sel)
    tile_rows = MOE_STEP * TOK_SUB
    return pl.pallas_call(
        _moe_kernel,
        grid_spec=pltpu.PrefetchScalarGridSpec(
            num_scalar_prefetch=7,
            grid=(n_tiles,),
            in_specs=[pl.BlockSpec((tile_rows, LANES), lambda j, s0, s1, v, f0, f1, tb, xb: (xb[j], 0)),
                      pl.BlockSpec((1, D_MODEL, LANES), by_bucket),
                      pl.BlockSpec((1, 1, LANES), by_bucket),
                      up_spec(wsel1), up_spec(wsel1), dn_spec(wsel1),
                      up_spec(wsel2), up_spec(wsel2), dn_spec(wsel2)],
            out_specs=pl.BlockSpec((tile_rows, LANES), lambda j, *_: (j, 0)),
            scratch_shapes=[pltpu.VMEM((2, D_MODEL, D_EXPERT), BF16),
                            pltpu.VMEM((2, D_MODEL, D_EXPERT), BF16),
                            pltpu.VMEM((2, D_EXPERT, D_MODEL), BF16)],
        ),
        out_shape=jax.ShapeDtypeStruct((n_tiles * tile_rows, LANES), F32),
        compiler_params=_cparams(("arbitrary",)),
        name="moe",
    )(*plan, xs, wsel, bsel, w_gate, w_up, w_down, w_gate, w_up, w_down)


def _combine_kernel(pos_ref, ys_ref, x1_ref, gt_ref, lg_ref, lb_ref, out_ref, ybuf, sems):
    ts = x1_ref.shape[0]
    i = pl.program_id(0)
    n = pl.num_programs(0)
    slot = i % 2

    def gather(tile, to_slot):
        def issue(g, carry):
            for u in range(DMA_UNROLL):
                r = g * DMA_UNROLL + u
                src = pl.multiple_of(pos_ref[tile * ts + r] * TOK_SUB, TOK_SUB)
                dst = pl.multiple_of(r * TOK_SUB, TOK_SUB)
                pltpu.make_async_copy(ys_ref.at[pl.ds(src, TOK_SUB), :],
                                      ybuf.at[to_slot, pl.ds(dst, TOK_SUB), :],
                                      sems.at[to_slot]).start(priority=u % 2)
            return carry
        lax.fori_loop(0, ts // DMA_UNROLL, issue, 0)

    @pl.when(i == 0)
    def _():
        gather(0, 0)

    @pl.when(i + 1 < n)
    def _():
        gather(i + 1, 1 - slot)

    pltpu.make_async_copy(ys_ref.at[pl.ds(0, ts * TOK_SUB), :], ybuf.at[slot], sems.at[slot]).wait()
    y = _load_token_major(ybuf.at[slot], ts)
    out_ref[...] = _layer_norm(ALPHA * x1_ref[...] + (1.0 + gt_ref[0]) * y, lg_ref[...], lb_ref[...])


def _combine(pos, ys, x1, mod, layer, lg, lb, bsz, seq):
    ts = min(TOK_TILE, seq)
    nt = seq // ts
    tok = bsz * seq
    return pl.pallas_call(
        _combine_kernel,
        grid_spec=pltpu.PrefetchScalarGridSpec(
            num_scalar_prefetch=1,
            grid=(tok // ts,),
            in_specs=[pl.BlockSpec(memory_space=pl.ANY),
                      pl.BlockSpec((ts, D_MODEL), lambda i, pos: (i, 0)),
                      _mod_spec(layer, MOD_GT2, bsz, lambda i, pos: i // nt),
                      pl.BlockSpec((1, D_MODEL), lambda i, pos: (0, 0)),
                      pl.BlockSpec((1, D_MODEL), lambda i, pos: (0, 0))],
            out_specs=pl.BlockSpec((ts, D_MODEL), lambda i, pos: (i, 0)),
            scratch_shapes=[pltpu.VMEM((2, ts * TOK_SUB, LANES), F32), pltpu.SemaphoreType.DMA((2,))],
        ),
        out_shape=jax.ShapeDtypeStruct((tok, D_MODEL), F32),
        compiler_params=_cparams(("arbitrary",)),
        name="combine",
    )(pos, ys, x1, mod, lg, lb)


def _route_plan(route, cnt, n_tiles, layer):
    bucket = route[0].astype(I32)
    rank = route[1].astype(I32)
    counts = cnt[:N_BUCKETS, 0].astype(I32)
    tiles_b = (counts + MOE_STEP - 1) // MOE_STEP
    tile_end = jnp.cumsum(tiles_b)
    row_off = (tile_end - tiles_b) * MOE_STEP
    bucket_ids = jnp.arange(N_BUCKETS, dtype=I32)
    pos = jnp.sum(jnp.where(bucket[None, :] == bucket_ids[:, None], row_off[:, None], 0), axis=0) + rank
    total = tile_end[-1]
    j = jnp.arange(n_tiles, dtype=I32)
    tb = jnp.sum((tile_end[None, :] <= jnp.minimum(j, total - 1)[:, None]).astype(I32), axis=1)
    tb = jnp.clip(tb, 0, N_BUCKETS - 1)
    hit = tb[None, :] == bucket_ids[:, None]
    rows_left = jnp.sum(jnp.where(hit, (row_off + counts)[:, None], 0), axis=0) - j * MOE_STEP
    half = MOE_TILE // 2
    tiles_j = jnp.clip((rows_left + half - 1) // half, 0, MOE_STEP // half)
    tiles_j = jnp.where(j < total, tiles_j, 0)
    grp = tb // N_PAIRS
    pair = tb % N_PAIRS
    first = layer * (N_GROUPS * N_EXP) + grp * N_EXP
    s0 = first + sum((pair == i).astype(I32) * _SLOT0[i] for i in range(N_PAIRS))
    s1 = first + sum((pair == i).astype(I32) * _SLOT1[i] for i in range(N_PAIRS))
    one = jnp.ones((1,), I32)
    fresh0 = jnp.concatenate([one, (s0[1:] != s0[:-1]).astype(I32)])
    fresh1 = jnp.concatenate([one, (s1[1:] != s1[:-1]).astype(I32)])
    xblk = jnp.minimum(j, total - 1)
    pads = jnp.concatenate([row_off + counts, tiles_b * MOE_STEP - counts,
                            (total * (MOE_STEP // MOE_TILE))[None]])
    return pos, pads, (s0, s1, tiles_j, fresh0, fresh1, layer * N_BUCKETS + tb, xblk)


def kernel(x, c, w_ada, b_ada, w_in, b_gates, mh_norm_w, w_conv, w_out, ln1_g, ln1_b,
           w_grp, b_grp, w_router, b_router, w_gate, w_up, w_down, ln2_g, ln2_b):
    bsz, seq, _ = x.shape
    tok = bsz * seq
    n_rows = tok + N_BUCKETS * MOE_STEP
    n_tiles = n_rows // MOE_STEP
    n_exp_total = N_GROUPS * N_EXP

    wg_all = w_gate.reshape(DEPTH * n_exp_total, D_MODEL, D_EXPERT)
    wu_all = w_up.reshape(DEPTH * n_exp_total, D_MODEL, D_EXPERT)
    wd_all = w_down.reshape(DEPTH * n_exp_total, D_EXPERT, D_MODEL)
    mod = _ada_mod(c, w_ada, b_ada).reshape(DEPTH * bsz * N_MOD, 1, D_MODEL)
    x2 = x.reshape(tok, D_MODEL)
    wq = w_in[:, :, :4 * M_WIDTH].astype(BF16)
    wg = jnp.pad(w_in[:, :, 4 * M_WIDTH:4 * M_WIDTH + 2 * HEADS],
                 ((0, 0), (0, 0), (0, LANES - 2 * HEADS))).astype(BF16)
    wc = w_in[:, :, 4 * M_WIDTH + 2 * HEADS:].astype(BF16)
    wo = w_out.astype(BF16)
    lb_shape = (DEPTH, N_BUCKETS)
    wsel = jnp.concatenate([jnp.broadcast_to(w_grp[:, None], lb_shape + (D_MODEL, N_GROUPS)),
                            jnp.swapaxes(w_router[:, :, _BUCKET_S0], 1, 2)[..., None],
                            jnp.swapaxes(w_router[:, :, _BUCKET_S1], 1, 2)[..., None]], axis=3)
    wsel = jnp.pad(wsel, ((0, 0), (0, 0), (0, 0), (0, LANES - N_GROUPS - 2))).astype(BF16)
    wsel = wsel.reshape(DEPTH * N_BUCKETS, D_MODEL, LANES)
    bsel = jnp.concatenate([jnp.broadcast_to(b_grp[:, None], lb_shape + (N_GROUPS,)),
                            b_router[:, _BUCKET_S0][..., None], b_router[:, _BUCKET_S1][..., None]], axis=2)
    bsel = jnp.pad(bsel, ((0, 0), (0, 0), (0, LANES - N_GROUPS - 2))).reshape(DEPTH * N_BUCKETS, 1, LANES)
    for l in range(DEPTH):
        bg = jnp.pad(b_gates[l], (0, LANES - 2 * HEADS)).reshape(1, LANES)
        qkvo, grow, yc = _inproj(x2, mod, l, wq, wg, wc, bg, w_conv[l], bsz, seq)
        n_logit = N_GROUPS + n_exp_total
        wr = jnp.pad(jnp.concatenate([w_grp[l], w_router[l]], axis=1).T,
                     ((0, ROUTE_ROWS - n_logit), (0, 0))).astype(BF16)
        br = jnp.pad(jnp.concatenate([b_grp[l], b_router[l]]), (0, ROUTE_ROWS - n_logit))
        br = jnp.broadcast_to(br[:, None], (ROUTE_ROWS, LANES))
        x1, u2, route, cnt = _mixer(qkvo, grow, mh_norm_w[l].reshape(1, M_WIDTH), yc, x2, mod, l,
                                    wo, ln1_g[l].reshape(1, D_MODEL),
                                    ln1_b[l].reshape(1, D_MODEL), wr, br, bsz, seq)
        pos, pads, plan = _route_plan(route, cnt, n_tiles, l)
        xs = _dispatch(pos, pads, u2, n_rows)
        ys = _moe(plan, xs, wsel, bsel, wg_all, wu_all, wd_all, n_tiles)
        x2 = _combine(pos, ys, x1, mod, l, ln2_g[l].reshape(1, D_MODEL),
                      ln2_b[l].reshape(1, D_MODEL), bsz, seq)
    return x2.reshape(bsz, seq, D_MODEL)
```

```python
import functools
import math

import jax
import jax.numpy as jnp
import numpy as np
from jax import lax
from jax.experimental import pallas as pl
from jax.experimental.pallas import tpu as pltpu

F32 = jnp.float32
BF16 = jnp.bfloat16
I32 = jnp.int32

D_MODEL = 1024
DEPTH = 2
HEADS = 4
D_HEAD = 128
M_WIDTH = HEADS * D_HEAD
C_WIDTH = D_MODEL - M_WIDTH
N_GROUPS = 4
N_EXP = 4
D_EXPERT = 512
N_PAIRS = 6
N_BUCKETS = N_GROUPS * N_PAIRS
N_MOD = 6
MOD_SH1, MOD_SC1, MOD_GT1, MOD_SH2, MOD_SC2, MOD_GT2 = range(N_MOD)
ALPHA = (2 * DEPTH) ** 0.25
LN_EPS = 1e-5
QK_SCALE = D_HEAD ** -0.5
LOG_QK_SCALE = math.log(QK_SCALE)
GATE_ROWS = 24

LANES = 128
ROUTE_ROWS = 32

TOK_TILE = 512
INPROJ_TILE = 1024
CHUNK = 256
MOE_TILE = 256
MOE_STEP = 2 * MOE_TILE
VMEM_LIMIT = 56 * 1024 * 1024

_PAIR_ORDER = (0, 2, 3, 1, 4, 5)
_SLOT0 = (0, 2, 2, 3, 3, 3)
_SLOT1 = (1, 1, 0, 0, 1, 2)
_BUCKET_S0 = np.array([g * N_EXP + _SLOT0[p] for g in range(N_GROUPS) for p in range(N_PAIRS)])
_BUCKET_S1 = np.array([g * N_EXP + _SLOT1[p] for g in range(N_GROUPS) for p in range(N_PAIRS)])


def _cparams(sem):
    return pltpu.CompilerParams(dimension_semantics=sem, vmem_limit_bytes=VMEM_LIMIT)


def _ada_kernel(c_ref, w_ref, b_ref, o_ref):
    c = c_ref[...]
    cond = (c * jax.nn.sigmoid(c)).astype(BF16)
    o_ref[0] = jnp.dot(cond, w_ref[0].astype(BF16), preferred_element_type=F32) + b_ref[0]


def _ada_mod(c, w_ada, b_ada):
    bsz = c.shape[0]
    nblk = w_ada.shape[2] // D_MODEL
    return pl.pallas_call(
        _ada_kernel,
        grid=(DEPTH, nblk),
        in_specs=[
            pl.BlockSpec((bsz, D_MODEL), lambda l, j: (0, 0)),
            pl.BlockSpec((1, D_MODEL, D_MODEL), lambda l, j: (l, 0, j)),
            pl.BlockSpec((1, 1, D_MODEL), lambda l, j: (l, 0, j)),
        ],
        out_specs=pl.BlockSpec((1, bsz, D_MODEL), lambda l, j: (l, 0, j)),
        out_shape=jax.ShapeDtypeStruct((DEPTH, bsz, nblk * D_MODEL), F32),
        compiler_params=_cparams(("arbitrary", "arbitrary")),
        name="ada_mod",
    )(c, w_ada, b_ada.reshape(DEPTH, 1, -1))


def _log_sigmoid(x):
    return jnp.minimum(x, 0.0) - jnp.log1p(jnp.exp(-jnp.abs(x)))


def _chunk_scan(x, op, identity):
    lane = lax.broadcasted_iota(I32, x.shape, 1) & (CHUNK - 1)
    sh = 1
    while sh < CHUNK:
        x = op(x, jnp.where(lane >= sh, pltpu.roll(x, sh, 1), identity))
        sh *= 2
    return x


def _inproj_kernel(x_ref, sc_ref, sh_ref, wq_ref, wg_ref, wc_ref, bg_ref, wconv_ref,
                   qkvo_ref, grow_ref, yc_ref, zbuf):
    tm = x_ref.shape[0]
    u = (x_ref[...] * (1.0 + sc_ref[0]) + sh_ref[0]).astype(BF16)
    qkvo_ref[...] = jnp.dot(u, wq_ref[...], preferred_element_type=F32).astype(BF16)
    uf = u.astype(F32)
    lane = lax.broadcasted_iota(I32, (tm, LANES), 1)
    g = jnp.broadcast_to(bg_ref[...], (tm, LANES))
    for col in range(2 * HEADS):
        dot_col = jnp.sum(uf * wg_ref[col:col + 1, :], axis=1, keepdims=True)
        g = jnp.where(lane == col, g + dot_col, g)
    gt = g.T[:8]
    head_row = lax.broadcasted_iota(I32, gt.shape, 0) < HEADS
    ig = jnp.where(head_row, gt, 0.0)
    logf = jnp.where(head_row, _log_sigmoid(pltpu.roll(gt, HEADS, 0)), 0.0)
    bcum = _chunk_scan(logf, jnp.add, 0.0)
    ug = ig - bcum
    grow_ref[0:8, :] = ug
    grow_ref[8:16, :] = bcum
    grow_ref[16:24, :] = _chunk_scan(ug, jnp.maximum, -jnp.inf)
    pc = jnp.dot(u, wc_ref[...], preferred_element_type=F32)
    z = pc[:, C_WIDTH:2 * C_WIDTH] * pc[:, 2 * C_WIDTH:]

    @pl.when(pl.program_id(1) == 0)
    def _():
        zbuf[0:8, :] = jnp.zeros((8, C_WIDTH), F32)

    zbuf[8:8 + tm, :] = z
    zc = (wconv_ref[0:1, :] * zbuf[6:6 + tm, :] + wconv_ref[1:2, :] * zbuf[7:7 + tm, :]
          + wconv_ref[2:3, :] * z)
    yc_ref[...] = (pc[:, :C_WIDTH] * zc).astype(BF16)
    zbuf[0:8, :] = zbuf[tm:tm + 8, :]


def _mod_spec(layer, which, bsz, batch_of):
    return pl.BlockSpec((1, 1, D_MODEL),
                        lambda *idx: ((layer * bsz + batch_of(*idx)) * N_MOD + which, 0, 0))


def _layer_spec(stacked, layer):
    return pl.BlockSpec((None,) + stacked.shape[1:], lambda *idx: (layer, 0, 0))


def _inproj(x2, mod, layer, wq, wg, wc, bg, wconv, bsz, seq):
    tm = min(INPROJ_TILE, seq)
    nt = seq // tm
    tok = bsz * seq
    row = lambda b, s: (b * nt + s, 0)
    const = lambda b, s: (0, 0)
    batch = lambda b, s: b
    return pl.pallas_call(
        _inproj_kernel,
        grid=(bsz, nt),
        in_specs=[
            pl.BlockSpec((tm, D_MODEL), row),
            _mod_spec(layer, MOD_SC1, bsz, batch),
            _mod_spec(layer, MOD_SH1, bsz, batch),
            _layer_spec(wq, layer),
            _layer_spec(wg, layer),
            _layer_spec(wc, layer),
            pl.BlockSpec(bg.shape, const),
            pl.BlockSpec(wconv.shape, const),
        ],
        out_specs=[
            pl.BlockSpec((tm, 4 * M_WIDTH), row),
            pl.BlockSpec((GATE_ROWS, tm), lambda b, s: (0, b * nt + s)),
            pl.BlockSpec((tm, C_WIDTH), row),
        ],
        out_shape=[
            jax.ShapeDtypeStruct((tok, 4 * M_WIDTH), BF16),
            jax.ShapeDtypeStruct((GATE_ROWS, tok), F32),
            jax.ShapeDtypeStruct((tok, C_WIDTH), BF16),
        ],
        scratch_shapes=[pltpu.VMEM((tm + 8, C_WIDTH), F32)],
        compiler_params=_cparams(("arbitrary", "arbitrary")),
        name="inproj",
    )(x2, mod, mod, wq, wg, wc, bg, wconv)


N_ROWQ = 4
SPLIT = 3


def _split3(x):
    hi = x.astype(BF16).astype(F32)
    r1 = x - hi
    mid = r1.astype(BF16).astype(F32)
    return [hi, mid, r1 - mid]


def _mlstm_selector(sel):
    r = lax.broadcasted_iota(I32, sel.shape, 0)
    c = lax.broadcasted_iota(I32, sel.shape, 1) // LANES
    rq = r // 8
    quantity = ((rq >= SPLIT).astype(I32) + (rq >= 2 * SPLIT).astype(I32)
                + (rq >= 3 * SPLIT).astype(I32))
    hit = ((r % 8) == c // N_ROWQ) & (quantity == c % N_ROWQ) & (rq < N_ROWQ * SPLIT)
    sel[...] = hit.astype(BF16)


def _mlstm_block(qkvo_ref, grow_ref, nw_ref, out_ref, cst, mst, sel, between=()):
    between = tuple(between) + (None,) * 3
    sb = qkvo_ref.shape[0]
    L = CHUNK
    causal = lax.broadcasted_iota(I32, (L, L), 0) >= lax.broadcasted_iota(I32, (L, L), 1)
    ones_ext = jnp.ones((L, D_HEAD), BF16)
    pad_rows = jnp.zeros((LANES - 8 * N_ROWQ * SPLIT, L), F32)
    eye = (lax.broadcasted_iota(I32, (D_HEAD, D_HEAD), 0)
           == lax.broadcasted_iota(I32, (D_HEAD, D_HEAD), 1)).astype(BF16)
    n_chunks = sb // L
    pairs = [(c, h) for c in range(n_chunks) for h in range(HEADS)]

    def cols(part, c, h):
        c0 = part * M_WIDTH + h * D_HEAD
        return qkvo_ref[c * L:(c + 1) * L, c0:c0 + D_HEAD]

    m_prev = mst[...]
    stack_t, us, decay = [], [], []
    for c in range(n_chunks):
        ug = grow_ref[0:8, c * L:(c + 1) * L]
        bcum = grow_ref[8:16, c * L:(c + 1) * L]
        cmax = grow_ref[16:24, c * L:(c + 1) * L]
        big_m = jnp.maximum(m_prev, cmax)
        m_last = jnp.broadcast_to(big_m[:, L - 1:L], (8, L))
        g_tot = jnp.broadcast_to(bcum[:, L - 1:L], (8, L))
        wq = jnp.exp(m_prev - big_m) * QK_SCALE
        log_em = -(bcum + big_m)
        wk = jnp.exp(ug - m_last)
        decay.append(jnp.exp(m_prev - m_last))
        us.append(ug + LOG_QK_SCALE)
        m_prev = g_tot + m_last
        stack = jnp.concatenate(_split3(big_m) + _split3(wq) + _split3(log_em) + _split3(wk)
                                + [pad_rows], axis=0)
        stack_t.append(stack.T.astype(BF16))
    mst[...] = m_prev
    if between[0] is not None:
        between[0]()

    reps = {}

    def rep(c, h, j):
        if (c, h) not in reps:
            c0 = h * N_ROWQ * LANES
            reps[(c, h)] = jnp.dot(stack_t[c], sel[:, c0:c0 + N_ROWQ * LANES],
                                   preferred_element_type=F32)
        return reps[(c, h)][:, j * LANES:(j + 1) * LANES]

    s_mat = {ch: lax.dot_general(cols(0, *ch), cols(1, *ch), (((1,), (1,)), ((), ())),
                                 preferred_element_type=F32) for ch in pairs}
    kw = {ch: (cols(1, *ch).astype(F32) * rep(*ch, 3)).astype(BF16) for ch in pairs}
    kw_t = {ch: lax.dot_general(eye, kw[ch], (((1,), (1,)), ((), ())),
                                preferred_element_type=F32).astype(BF16) for ch in pairs}
    if between[1] is not None:
        between[1]()
    lhs = {}
    for ch in pairs:
        c, h = ch
        big_m_rows = jnp.concatenate([rep(c, h, 0)] * (L // LANES), axis=1)
        p = jnp.where(causal, s_mat[ch] * jnp.exp(us[c][h:h + 1, :] - big_m_rows), 0.0)
        qw = cols(0, c, h).astype(F32) * rep(c, h, 1)
        lhs[ch] = jnp.concatenate([p.astype(BF16), qw.astype(BF16)], axis=1)
    vext = {ch: jnp.concatenate([cols(2, *ch), ones_ext], axis=1) for ch in pairs}
    upd = {ch: jnp.dot(kw_t[ch], vext[ch], preferred_element_type=F32) for ch in pairs}
    for c in range(n_chunks):
        tot = {}
        for h in range(HEADS):
            state = cst[h]
            rhs = jnp.concatenate([vext[(c, h)], state.astype(BF16)], axis=0)
            tot[h] = jnp.dot(lhs[(c, h)], rhs, preferred_element_type=F32)
            dec = jnp.concatenate([decay[c][h:h + 1, :LANES]] * 2, axis=1)
            cst[h] = dec * state + upd[(c, h)]
        for h in range(HEADS):
            c0 = h * D_HEAD
            num, den = tot[h][:, :D_HEAD], tot[h][:, D_HEAD:]
            hh = num / jnp.maximum(jnp.abs(den), jnp.exp(rep(c, h, 2)))
            mu = jnp.mean(hh, axis=1, keepdims=True)
            var = jnp.mean(jnp.square(hh - mu), axis=1, keepdims=True)
            hn = (hh - mu) * lax.rsqrt(var + LN_EPS) * nw_ref[:, c0:c0 + D_HEAD]
            og = jax.nn.sigmoid(cols(3, c, h).astype(F32))
            out_ref[c * L:(c + 1) * L, c0:c0 + D_HEAD] = (hn * og).astype(BF16)
        if c == 0 and between[2] is not None:
            between[2]()


def _layer_norm(r, g, b):
    mu = jnp.mean(r, axis=-1, keepdims=True)
    var = jnp.mean(jnp.square(r - mu), axis=-1, keepdims=True)
    return (r - mu) * lax.rsqrt(var + LN_EPS) * g + b


TOK_SUB = D_MODEL // LANES


def _store_token_major(ref, val, first_row=0):
    rows = val.shape[0]
    for k in range(TOK_SUB):
        ref[pl.ds(first_row * TOK_SUB + k, rows, stride=TOK_SUB), :] = val[:, k * LANES:(k + 1) * LANES]


def _load_token_major(ref, rows, first_row=0):
    return jnp.concatenate([ref[pl.ds(first_row * TOK_SUB + k, rows, stride=TOK_SUB), :]
                            for k in range(TOK_SUB)], axis=1)


def _outproj_parts(hn_ref, yc_ref, x_ref, gt_ref, sc_ref, sh_ref, wo_ref, lg_ref, lb_ref,
                   wr_ref, br_ref, x1_ref, u2_ref, route_ref, cnt_ref, base, triu, live):
    tm = x_ref.shape[0]
    held = {}

    def project():
        held["mix"] = (jnp.dot(hn_ref[...], wo_ref[0:M_WIDTH, :], preferred_element_type=F32)
                       + jnp.dot(yc_ref[...], wo_ref[M_WIDTH:, :], preferred_element_type=F32))

    def normalise():
        x1 = _layer_norm(ALPHA * x_ref[...] + (1.0 + gt_ref[0]) * held["mix"], lg_ref[...], lb_ref[...])
        x1_ref[...] = x1
        u2 = x1 * (1.0 + sc_ref[0]) + sh_ref[0]
        _store_token_major(u2_ref, u2)
        held["logits"] = lax.dot_general(wr_ref[...], u2.astype(BF16), (((1,), (1,)), ((), ())),
                                         preferred_element_type=F32) + br_ref[:, 0:1]

    def route():
        _route_block(held["logits"], route_ref, cnt_ref, base, triu, live, tm)

    return project, normalise, route


def _route_block(lt, route_ref, cnt_ref, base, triu, live, tm):
    lg = [lt[j:j + 1, :] for j in range(N_GROUPS + N_GROUPS * N_EXP)]
    best = lg[0]
    grp = jnp.zeros((1, tm), I32)
    for j in range(1, N_GROUPS):
        c = lg[j] > best
        grp = jnp.where(c, j, grp)
        best = jnp.where(c, lg[j], best)
    sel = []
    for e in range(N_EXP):
        val = lg[N_GROUPS + e]
        for gg in range(1, N_GROUPS):
            val = jnp.where(grp == gg, lg[N_GROUPS + gg * N_EXP + e], val)
        sel.append(val)
    v1 = sel[0]
    i1 = jnp.zeros((1, tm), I32)
    for e in range(1, N_EXP):
        c = sel[e] > v1
        i1 = jnp.where(c, e, i1)
        v1 = jnp.where(c, sel[e], v1)
    v2 = jnp.full((1, tm), -jnp.inf, F32)
    i2 = jnp.zeros((1, tm), I32)
    for e in range(N_EXP):
        cand = jnp.where(i1 == e, -jnp.inf, sel[e])
        c = cand > v2
        i2 = jnp.where(c, e, i2)
        v2 = jnp.where(c, cand, v2)
    ea = jnp.minimum(i1, i2)
    eb = jnp.maximum(i1, i2)
    lex = jnp.where(ea == 0, 0, jnp.where(ea == 1, 3, 5)) + eb - ea - 1
    pair = lex
    for i, p in enumerate(_PAIR_ORDER):
        if i != p:
            pair = jnp.where(lex == i, p, pair)
    bucket = grp * N_PAIRS + pair

    onehot = lax.broadcasted_iota(I32, (ROUTE_ROWS, tm), 0) == bucket
    cum = jnp.dot(onehot.astype(BF16), triu[...], preferred_element_type=F32)
    prev = base[:, 0:1]
    rank = jnp.sum(jnp.where(onehot, cum - 1.0 + prev, 0.0), axis=0, keepdims=True)
    new_base = prev + live * cum[:, tm - 1:tm]
    base[...] = jnp.broadcast_to(new_base, base.shape)
    cnt_ref[...] = jnp.broadcast_to(new_base, cnt_ref.shape)

    zrow = jnp.zeros((1, tm), F32)
    route_ref[...] = jnp.concatenate(
        [bucket.astype(F32), rank, zrow, zrow, zrow, zrow, zrow, zrow], axis=0)


def _mixer_kernel(qkvo_ref, grow_ref, nw_ref, yc_ref, x_ref, gt_ref, sc_ref, sh_ref, wo_ref, lg_ref,
                  lb_ref, wr_ref, br_ref, x1_ref, u2_ref, route_ref, cnt_ref,
                  cst, mst, sel, hn, base, triu, *, blocks_per_seq):
    i = pl.program_id(0)
    tm = x_ref.shape[0]

    @pl.when(i == 0)
    def _():
        _mlstm_selector(sel)
        hn[...] = jnp.zeros(hn.shape, BF16)
        base[...] = jnp.zeros(base.shape, F32)
        rid = lax.broadcasted_iota(I32, triu.shape, 0)
        cid = lax.broadcasted_iota(I32, triu.shape, 1)
        triu[...] = (rid <= cid).astype(BF16)

    @pl.when(i % blocks_per_seq == 0)
    def _():
        cst[...] = jnp.zeros(cst.shape, F32)
        mst[...] = jnp.zeros(mst.shape, F32)

    live = jnp.where(i > 0, 1.0, 0.0).astype(F32)
    parts = _outproj_parts(hn, yc_ref, x_ref, gt_ref, sc_ref, sh_ref, wo_ref, lg_ref, lb_ref,
                           wr_ref, br_ref, x1_ref, u2_ref, route_ref, cnt_ref, base, triu, live)
    _mlstm_block(qkvo_ref, grow_ref, nw_ref, hn, cst, mst, sel, between=parts)


def _mixer(qkvo, grow, nw, yc, x2, mod, layer, wo, lg, lb, wr, br, bsz, seq):
    tm = min(TOK_TILE, seq)
    nt = seq // tm
    tok = bsz * seq
    n = bsz * nt
    cur = lambda i: (jnp.minimum(i, n - 1), 0)
    prev = lambda i: (jnp.maximum(i - 1, 0), 0)
    const = lambda i: (0, 0)
    batch = lambda i: jnp.maximum(i - 1, 0) // nt
    return pl.pallas_call(
        functools.partial(_mixer_kernel, blocks_per_seq=nt),
        grid=(n + 1,),
        in_specs=[
            pl.BlockSpec((tm, 4 * M_WIDTH), cur),
            pl.BlockSpec((GATE_ROWS, tm), lambda i: (0, jnp.minimum(i, n - 1))),
            pl.BlockSpec((1, M_WIDTH), const),
            pl.BlockSpec((tm, C_WIDTH), prev),
            pl.BlockSpec((tm, D_MODEL), prev),
            _mod_spec(layer, MOD_GT1, bsz, batch),
            _mod_spec(layer, MOD_SC2, bsz, batch),
            _mod_spec(layer, MOD_SH2, bsz, batch),
            _layer_spec(wo, layer),
            pl.BlockSpec(lg.shape, const),
            pl.BlockSpec(lb.shape, const),
            pl.BlockSpec(wr.shape, const),
            pl.BlockSpec(br.shape, const),
        ],
        out_specs=[
            pl.BlockSpec((tm, D_MODEL), prev),
            pl.BlockSpec((tm * TOK_SUB, LANES), prev),
            pl.BlockSpec((8, tm), lambda i: (0, jnp.maximum(i - 1, 0))),
            pl.BlockSpec((ROUTE_ROWS, LANES), const),
        ],
        out_shape=[
            jax.ShapeDtypeStruct((tok, D_MODEL), F32),
            jax.ShapeDtypeStruct((tok * TOK_SUB, LANES), F32),
            jax.ShapeDtypeStruct((8, tok), F32),
            jax.ShapeDtypeStruct((ROUTE_ROWS, LANES), F32),
        ],
        scratch_shapes=[pltpu.VMEM((HEADS, D_HEAD, 2 * D_HEAD), F32),
                        pltpu.VMEM((8, CHUNK), F32),
                        pltpu.VMEM((LANES, HEADS * N_ROWQ * LANES), BF16),
                        pltpu.VMEM((tm, M_WIDTH), BF16),
                        pltpu.VMEM((ROUTE_ROWS, LANES), F32),
                        pltpu.VMEM((tm, tm), BF16)],
        compiler_params=_cparams(("arbitrary",)),
        name="mixer",
    )(qkvo, grow, nw, yc, x2, mod, mod, mod, wo, lg, lb, wr, br)


DISPATCH_TILE = 2048
DMA_UNROLL = 8


def _dispatch_kernel(pos_ref, pad_ref, u_ref, xs_ref, zeros, sem, zsem):
    ts = u_ref.shape[0] // TOK_SUB
    t0 = pl.program_id(0) * ts

    def zero_rows(first_row, n_rows_static):
        dst = pl.multiple_of(first_row * TOK_SUB, TOK_SUB)
        return pltpu.make_async_copy(zeros.at[pl.ds(0, n_rows_static * TOK_SUB), :],
                                     xs_ref.at[pl.ds(dst, n_rows_static * TOK_SUB), :], zsem)

    def fill_pads(wait):
        def bucket(b, carry):
            row = pad_ref[b]
            n = pad_ref[N_BUCKETS + b]
            size = MOE_STEP // 2
            while size >= 1:
                @pl.when((n & size) != 0)
                def _(row=row, size=size):
                    cp = zero_rows(row, size)
                    cp.wait() if wait else cp.start()
                row = row + (n & size)
                size //= 2
            return carry
        lax.fori_loop(0, N_BUCKETS, bucket, 0)

        def idle_tile(j, carry):
            cp = zero_rows(j * MOE_TILE, MOE_TILE)
            cp.wait() if wait else cp.start()
            return carry
        lax.fori_loop(pad_ref[2 * N_BUCKETS], xs_ref.shape[0] // (MOE_TILE * TOK_SUB), idle_tile, 0)

    @pl.when(pl.program_id(0) == 0)
    def _():
        zeros[...] = jnp.zeros(zeros.shape, F32)
        fill_pads(wait=False)
        fill_pads(wait=True)

    def issue(g, carry):
        for u in range(DMA_UNROLL):
            r = g * DMA_UNROLL + u
            dst = pl.multiple_of(pos_ref[t0 + r] * TOK_SUB, TOK_SUB)
            src = pl.multiple_of(r * TOK_SUB, TOK_SUB)
            pltpu.make_async_copy(u_ref.at[pl.ds(src, TOK_SUB), :],
                                  xs_ref.at[pl.ds(dst, TOK_SUB), :], sem).start(priority=u % 2)
        return carry

    lax.fori_loop(0, ts // DMA_UNROLL, issue, 0)
    pltpu.make_async_copy(u_ref, xs_ref.at[pl.ds(0, ts * TOK_SUB), :], sem).wait()


def _dispatch(pos, pad_start, u2, n_rows):
    tok = u2.shape[0] // TOK_SUB
    ts = min(DISPATCH_TILE, tok)
    return pl.pallas_call(
        _dispatch_kernel,
        grid_spec=pltpu.PrefetchScalarGridSpec(
            num_scalar_prefetch=2,
            grid=(tok // ts,),
            in_specs=[pl.BlockSpec((ts * TOK_SUB, LANES), lambda i, pos, pad: (i, 0))],
            out_specs=pl.BlockSpec(memory_space=pl.ANY),
            scratch_shapes=[pltpu.VMEM((MOE_TILE * TOK_SUB, LANES), F32),
                            pltpu.SemaphoreType.DMA(()), pltpu.SemaphoreType.DMA(())],
        ),
        out_shape=jax.ShapeDtypeStruct((n_rows * TOK_SUB, LANES), F32),
        compiler_params=_cparams(("arbitrary",)),
        name="dispatch",
    )(pos, pad_start, u2)


def _moe_kernel(s0_ref, s1_ref, tiles_ref, fresh0_ref, fresh1_ref, tb_ref, xb_ref, xs_ref, wsel_ref,
                bsel_ref, wg0, wu0, wd0, wg1, wu1, wd1, y_ref, wgs, wus, wds):
    del s0_ref, s1_ref, tb_ref, xb_ref
    j = pl.program_id(0)
    rows = MOE_TILE

    @pl.when(fresh0_ref[j] == 1)
    def _():
        wgs[0] = wg0[0].astype(BF16)
        wus[0] = wu0[0].astype(BF16)
        wds[0] = wd0[0].astype(BF16)

    @pl.when(fresh1_ref[j] == 1)
    def _():
        wgs[1] = wg1[0].astype(BF16)
        wus[1] = wu1[0].astype(BF16)
        wds[1] = wd1[0].astype(BF16)

    def experts(part):
        x = _load_token_major(xs_ref, rows, part * rows).astype(BF16)
        lg = jnp.dot(x, wsel_ref[0], preferred_element_type=F32) + bsel_ref[0]
        grp = [lg[:, i:i + 1] for i in range(N_GROUPS)]
        gmax = jnp.maximum(jnp.maximum(grp[0], grp[1]), jnp.maximum(grp[2], grp[3]))
        gsum = sum(jnp.exp(v - gmax) for v in grp)
        la, lb = lg[:, N_GROUPS:N_GROUPS + 1], lg[:, N_GROUPS + 1:N_GROUPS + 2]
        emax = jnp.maximum(la, lb)
        ea, eb = jnp.exp(la - emax), jnp.exp(lb - emax)
        scale = 1.0 / (gsum * (ea + eb))
        wcols = (ea * scale, eb * scale)
        acc = None
        for slot in range(2):
            g = jnp.dot(x, wgs[slot], preferred_element_type=F32)
            u = jnp.dot(x, wus[slot], preferred_element_type=F32)
            hid = (g * jax.nn.sigmoid(g) * u * wcols[slot]).astype(BF16)
            y = jnp.dot(hid, wds[slot], preferred_element_type=F32)
            acc = y if acc is None else acc + y
        _store_token_major(y_ref, acc, part * rows)

    n_parts = MOE_STEP // MOE_TILE
    for live in range(n_parts + 1):
        @pl.when(tiles_ref[j] == live)
        def _(live=live):
            for part in range(live):
                experts(part)
            if live < n_parts:
                lo = live * rows * TOK_SUB
                y_ref[lo:, :] = jnp.zeros((y_ref.shape[0] - lo, LANES), F32)


def _moe(plan, xs, wsel, bsel, w_gate, w_up, w_down, n_tiles):
    wsel1 = lambda j, s0, s1, v, f0, f1, tb, xb: (s0[j], 0, 0)
    wsel2 = lambda j, s0, s1, v, f0, f1, tb, xb: (s1[j], 0, 0)
    by_bucket = lambda j, s0, s1, v, f0, f1, tb, xb: (tb[j], 0, 0)
    up_spec = lambda sel: pl.BlockSpec((1, D_MODEL, D_EXPERT), sel)
    dn_spec = lambda sel: pl.BlockSpec((1, D_EXPERT, D_MODEL), sel)
    tile_rows = MOE_STEP * TOK_SUB
    return pl.pallas_call(
        _moe_kernel,
        grid_spec=pltpu.PrefetchScalarGridSpec(
            num_scalar_prefetch=7,
            grid=(n_tiles,),
            in_specs=[pl.BlockSpec((tile_rows, LANES), lambda j, s0, s1, v, f0, f1, tb, xb: (xb[j], 0)),
                      pl.BlockSpec((1, D_MODEL, LANES), by_bucket),
                      pl.BlockSpec((1, 1, LANES), by_bucket),
                      up_spec(wsel1), up_spec(wsel1), dn_spec(wsel1),
                      up_spec(wsel2), up_spec(wsel2), dn_spec(wsel2)],
            out_specs=pl.BlockSpec((tile_rows, LANES), lambda j, *_: (j, 0)),
            scratch_shapes=[pltpu.VMEM((2, D_MODEL, D_EXPERT), BF16),
                            pltpu.VMEM((2, D_MODEL, D_EXPERT), BF16),
                            pltpu.VMEM((2, D_EXPERT, D_MODEL), BF16)],
        ),
        out_shape=jax.ShapeDtypeStruct((n_tiles * tile_rows, LANES), F32),
        compiler_params=_cparams(("arbitrary",)),
        name="moe",
    )(*plan, xs, wsel, bsel, w_gate, w_up, w_down, w_gate, w_up, w_down)


def _combine_kernel(pos_ref, ys_ref, x1_ref, gt_ref, lg_ref, lb_ref, out_ref, ybuf, sems):
    ts = x1_ref.shape[0]
    i = pl.program_id(0)
    n = pl.num_programs(0)
    slot = i % 2

    def gather(tile, to_slot):
        def issue(g, carry):
            for u in range(DMA_UNROLL):
                r = g * DMA_UNROLL + u
                src = pl.multiple_of(pos_ref[tile * ts + r] * TOK_SUB, TOK_SUB)
                dst = pl.multiple_of(r * TOK_SUB, TOK_SUB)
                pltpu.make_async_copy(ys_ref.at[pl.ds(src, TOK_SUB), :],
                                      ybuf.at[to_slot, pl.ds(dst, TOK_SUB), :],
                                      sems.at[to_slot]).start(priority=u % 2)
            return carry
        lax.fori_loop(0, ts // DMA_UNROLL, issue, 0)

    @pl.when(i == 0)
    def _():
        gather(0, 0)

    @pl.when(i + 1 < n)
    def _():
        gather(i + 1, 1 - slot)

    pltpu.make_async_copy(ys_ref.at[pl.ds(0, ts * TOK_SUB), :], ybuf.at[slot], sems.at[slot]).wait()
    y = _load_token_major(ybuf.at[slot], ts)
    out_ref[...] = _layer_norm(ALPHA * x1_ref[...] + (1.0 + gt_ref[0]) * y, lg_ref[...], lb_ref[...])


def _combine(pos, ys, x1, mod, layer, lg, lb, bsz, seq):
    ts = min(TOK_TILE, seq)
    nt = seq // ts
    tok = bsz * seq
    return pl.pallas_call(
        _combine_kernel,
        grid_spec=pltpu.PrefetchScalarGridSpec(
            num_scalar_prefetch=1,
            grid=(tok // ts,),
            in_specs=[pl.BlockSpec(memory_space=pl.ANY),
                      pl.BlockSpec((ts, D_MODEL), lambda i, pos: (i, 0)),
                      _mod_spec(layer, MOD_GT2, bsz, lambda i, pos: i // nt),
                      pl.BlockSpec((1, D_MODEL), lambda i, pos: (0, 0)),
                      pl.BlockSpec((1, D_MODEL), lambda i, pos: (0, 0))],
            out_specs=pl.BlockSpec((ts, D_MODEL), lambda i, pos: (i, 0)),
            scratch_shapes=[pltpu.VMEM((2, ts * TOK_SUB, LANES), F32), pltpu.SemaphoreType.DMA((2,))],
        ),
        out_shape=jax.ShapeDtypeStruct((tok, D_MODEL), F32),
        compiler_params=_cparams(("arbitrary",)),
        name="combine",
    )(pos, ys, x1, mod, lg, lb)


def _route_plan(route, cnt, n_tiles, layer):
    bucket = route[0].astype(I32)
    rank = route[1].astype(I32)
    counts = cnt[:N_BUCKETS, 0].astype(I32)
    tiles_b = (counts + MOE_STEP - 1) // MOE_STEP
    tile_end = jnp.cumsum(tiles_b)
    row_off = (tile_end - tiles_b) * MOE_STEP
    bucket_ids = jnp.arange(N_BUCKETS, dtype=I32)
    pos = jnp.sum(jnp.where(bucket[None, :] == bucket_ids[:, None], row_off[:, None], 0), axis=0) + rank
    total = tile_end[-1]
    j = jnp.arange(n_tiles, dtype=I32)
    tb = jnp.sum((tile_end[None, :] <= jnp.minimum(j, total - 1)[:, None]).astype(I32), axis=1)
    tb = jnp.clip(tb, 0, N_BUCKETS - 1)
    hit = tb[None, :] == bucket_ids[:, None]
    rows_left = jnp.sum(jnp.where(hit, (row_off + counts)[:, None], 0), axis=0) - j * MOE_STEP
    tiles_j = jnp.clip((rows_left + MOE_TILE - 1) // MOE_TILE, 0, MOE_STEP // MOE_TILE)
    tiles_j = jnp.where(j < total, tiles_j, 0)
    grp = tb // N_PAIRS
    pair = tb % N_PAIRS
    first = layer * (N_GROUPS * N_EXP) + grp * N_EXP
    s0 = first + sum((pair == i).astype(I32) * _SLOT0[i] for i in range(N_PAIRS))
    s1 = first + sum((pair == i).astype(I32) * _SLOT1[i] for i in range(N_PAIRS))
    one = jnp.ones((1,), I32)
    fresh0 = jnp.concatenate([one, (s0[1:] != s0[:-1]).astype(I32)])
    fresh1 = jnp.concatenate([one, (s1[1:] != s1[:-1]).astype(I32)])
    xblk = jnp.minimum(j, total - 1)
    pads = jnp.concatenate([row_off + counts, tiles_b * MOE_STEP - counts,
                            (total * (MOE_STEP // MOE_TILE))[None]])
    return pos, pads, (s0, s1, tiles_j, fresh0, fresh1, layer * N_BUCKETS + tb, xblk)


def kernel(x, c, w_ada, b_ada, w_in, b_gates, mh_norm_w, w_conv, w_out, ln1_g, ln1_b,
           w_grp, b_grp, w_router, b_router, w_gate, w_up, w_down, ln2_g, ln2_b):
    bsz, seq, _ = x.shape
    tok = bsz * seq
    n_rows = tok + N_BUCKETS * MOE_STEP
    n_tiles = n_rows // MOE_STEP
    n_exp_total = N_GROUPS * N_EXP

    wg_all = w_gate.reshape(DEPTH * n_exp_total, D_MODEL, D_EXPERT)
    wu_all = w_up.reshape(DEPTH * n_exp_total, D_MODEL, D_EXPERT)
    wd_all = w_down.reshape(DEPTH * n_exp_total, D_EXPERT, D_MODEL)
    mod = _ada_mod(c, w_ada, b_ada).reshape(DEPTH * bsz * N_MOD, 1, D_MODEL)
    x2 = x.reshape(tok, D_MODEL)
    wq = w_in[:, :, :4 * M_WIDTH].astype(BF16)
    wg = jnp.swapaxes(w_in[:, :, 4 * M_WIDTH:4 * M_WIDTH + 2 * HEADS], 1, 2)
    wg = wg.astype(BF16).astype(F32)
    wc = w_in[:, :, 4 * M_WIDTH + 2 * HEADS:].astype(BF16)
    wo = w_out.astype(BF16)
    lb_shape = (DEPTH, N_BUCKETS)
    wsel = jnp.concatenate([jnp.broadcast_to(w_grp[:, None], lb_shape + (D_MODEL, N_GROUPS)),
                            jnp.swapaxes(w_router[:, :, _BUCKET_S0], 1, 2)[..., None],
                            jnp.swapaxes(w_router[:, :, _BUCKET_S1], 1, 2)[..., None]], axis=3)
    wsel = jnp.pad(wsel, ((0, 0), (0, 0), (0, 0), (0, LANES - N_GROUPS - 2))).astype(BF16)
    wsel = wsel.reshape(DEPTH * N_BUCKETS, D_MODEL, LANES)
    bsel = jnp.concatenate([jnp.broadcast_to(b_grp[:, None], lb_shape + (N_GROUPS,)),
                            b_router[:, _BUCKET_S0][..., None], b_router[:, _BUCKET_S1][..., None]], axis=2)
    bsel = jnp.pad(bsel, ((0, 0), (0, 0), (0, LANES - N_GROUPS - 2))).reshape(DEPTH * N_BUCKETS, 1, LANES)
    for l in range(DEPTH):
        bg = jnp.pad(b_gates[l], (0, LANES - 2 * HEADS)).reshape(1, LANES)
        qkvo, grow, yc = _inproj(x2, mod, l, wq, wg, wc, bg, w_conv[l], bsz, seq)
        n_logit = N_GROUPS + n_exp_total
        wr = jnp.pad(jnp.concatenate([w_grp[l], w_router[l]], axis=1).T,
                     ((0, ROUTE_ROWS - n_logit), (0, 0))).astype(BF16)
        br = jnp.pad(jnp.concatenate([b_grp[l], b_router[l]]), (0, ROUTE_ROWS - n_logit))
        br = jnp.broadcast_to(br[:, None], (ROUTE_ROWS, LANES))
        x1, u2, route, cnt = _mixer(qkvo, grow, mh_norm_w[l].reshape(1, M_WIDTH), yc, x2, mod, l,
                                    wo, ln1_g[l].reshape(1, D_MODEL),
                                    ln1_b[l].reshape(1, D_MODEL), wr, br, bsz, seq)
        pos, pads, plan = _route_plan(route, cnt, n_tiles, l)
        xs = _dispatch(pos, pads, u2, n_rows)
        ys = _moe(plan, xs, wsel, bsel, wg_all, wu_all, wd_all, n_tiles)
        x2 = _combine(pos, ys, x1, mod, l, ln2_g[l].reshape(1, D_MODEL),
                      ln2_b[l].reshape(1, D_MODEL), bsz, seq)
    return x2.reshape(bsz, seq, D_MODEL)
```

```python
import functools
import math

import jax
import jax.numpy as jnp
import numpy as np
from jax import lax
from jax.experimental import pallas as pl
from jax.experimental.pallas import tpu as pltpu

F32 = jnp.float32
BF16 = jnp.bfloat16
I32 = jnp.int32

D_MODEL = 1024
DEPTH = 2
HEADS = 4
D_HEAD = 128
M_WIDTH = HEADS * D_HEAD
C_WIDTH = D_MODEL - M_WIDTH
N_GROUPS = 4
N_EXP = 4
D_EXPERT = 512
N_PAIRS = 6
N_BUCKETS = N_GROUPS * N_PAIRS
N_MOD = 6
MOD_SH1, MOD_SC1, MOD_GT1, MOD_SH2, MOD_SC2, MOD_GT2 = range(N_MOD)
ALPHA = (2 * DEPTH) ** 0.25
LN_EPS = 1e-5
QK_SCALE = D_HEAD ** -0.5
LOG_QK_SCALE = math.log(QK_SCALE)
GATE_ROWS = 24

LANES = 128
ROUTE_ROWS = 32

TOK_TILE = 512
INPROJ_TILE = 1024
CHUNK = 256
MOE_TILE = 256
MOE_STEP = 2 * MOE_TILE
VMEM_LIMIT = 56 * 1024 * 1024

_PAIR_ORDER = (0, 2, 3, 1, 4, 5)
_SLOT0 = (0, 2, 2, 3, 3, 3)
_SLOT1 = (1, 1, 0, 0, 1, 2)
_BUCKET_S0 = np.array([g * N_EXP + _SLOT0[p] for g in range(N_GROUPS) for p in range(N_PAIRS)])
_BUCKET_S1 = np.array([g * N_EXP + _SLOT1[p] for g in range(N_GROUPS) for p in range(N_PAIRS)])


def _cparams(sem):
    return pltpu.CompilerParams(dimension_semantics=sem, vmem_limit_bytes=VMEM_LIMIT)


def _ada_kernel(c_ref, w_ref, b_ref, o_ref):
    c = c_ref[...]
    cond = (c * jax.nn.sigmoid(c)).astype(BF16)
    o_ref[0] = jnp.dot(cond, w_ref[0].astype(BF16), preferred_element_type=F32) + b_ref[0]


def _ada_mod(c, w_ada, b_ada):
    bsz = c.shape[0]
    nblk = w_ada.shape[2] // D_MODEL
    return pl.pallas_call(
        _ada_kernel,
        grid=(DEPTH, nblk),
        in_specs=[
            pl.BlockSpec((bsz, D_MODEL), lambda l, j: (0, 0)),
            pl.BlockSpec((1, D_MODEL, D_MODEL), lambda l, j: (l, 0, j)),
            pl.BlockSpec((1, 1, D_MODEL), lambda l, j: (l, 0, j)),
        ],
        out_specs=pl.BlockSpec((1, bsz, D_MODEL), lambda l, j: (l, 0, j)),
        out_shape=jax.ShapeDtypeStruct((DEPTH, bsz, nblk * D_MODEL), F32),
        compiler_params=_cparams(("arbitrary", "arbitrary")),
        name="ada_mod",
    )(c, w_ada, b_ada.reshape(DEPTH, 1, -1))


def _log_sigmoid(x):
    return jnp.minimum(x, 0.0) - jnp.log1p(jnp.exp(-jnp.abs(x)))


def _chunk_scan(x, op, identity):
    lane = lax.broadcasted_iota(I32, x.shape, 1) & (CHUNK - 1)
    sh = 1
    while sh < CHUNK:
        x = op(x, jnp.where(lane >= sh, pltpu.roll(x, sh, 1), identity))
        sh *= 2
    return x


def _inproj_kernel(x_ref, sc_ref, sh_ref, wq_ref, wg_ref, wc_ref, bg_ref, wconv_ref,
                   qkvo_ref, grow_ref, yc_ref, zbuf):
    tm = x_ref.shape[0]
    u = (x_ref[...] * (1.0 + sc_ref[0]) + sh_ref[0]).astype(BF16)
    qkvo_ref[...] = jnp.dot(u, wq_ref[...], preferred_element_type=F32).astype(BF16)
    g = jnp.dot(u, wg_ref[...], preferred_element_type=F32) + bg_ref[...]
    gt = g.T[:8]
    head_row = lax.broadcasted_iota(I32, gt.shape, 0) < HEADS
    ig = jnp.where(head_row, gt, 0.0)
    logf = jnp.where(head_row, _log_sigmoid(pltpu.roll(gt, HEADS, 0)), 0.0)
    bcum = _chunk_scan(logf, jnp.add, 0.0)
    ug = ig - bcum
    grow_ref[0:8, :] = ug
    grow_ref[8:16, :] = bcum
    grow_ref[16:24, :] = _chunk_scan(ug, jnp.maximum, -jnp.inf)
    pc = jnp.dot(u, wc_ref[...], preferred_element_type=F32)
    z = pc[:, C_WIDTH:2 * C_WIDTH] * pc[:, 2 * C_WIDTH:]

    @pl.when(pl.program_id(1) == 0)
    def _():
        zbuf[0:8, :] = jnp.zeros((8, C_WIDTH), F32)

    zbuf[8:8 + tm, :] = z
    zc = (wconv_ref[0:1, :] * zbuf[6:6 + tm, :] + wconv_ref[1:2, :] * zbuf[7:7 + tm, :]
          + wconv_ref[2:3, :] * z)
    yc_ref[...] = (pc[:, :C_WIDTH] * zc).astype(BF16)
    zbuf[0:8, :] = zbuf[tm:tm + 8, :]


def _mod_spec(layer, which, bsz, batch_of):
    return pl.BlockSpec((1, 1, D_MODEL),
                        lambda *idx: ((layer * bsz + batch_of(*idx)) * N_MOD + which, 0, 0))


def _layer_spec(stacked, layer):
    return pl.BlockSpec((None,) + stacked.shape[1:], lambda *idx: (layer, 0, 0))


def _inproj(x2, mod, layer, wq, wg, wc, bg, wconv, bsz, seq):
    tm = min(INPROJ_TILE, seq)
    nt = seq // tm
    tok = bsz * seq
    row = lambda b, s: (b * nt + s, 0)
    const = lambda b, s: (0, 0)
    batch = lambda b, s: b
    return pl.pallas_call(
        _inproj_kernel,
        grid=(bsz, nt),
        in_specs=[
            pl.BlockSpec((tm, D_MODEL), row),
            _mod_spec(layer, MOD_SC1, bsz, batch),
            _mod_spec(layer, MOD_SH1, bsz, batch),
            _layer_spec(wq, layer),
            _layer_spec(wg, layer),
            _layer_spec(wc, layer),
            pl.BlockSpec(bg.shape, const),
            pl.BlockSpec(wconv.shape, const),
        ],
        out_specs=[
            pl.BlockSpec((tm, 4 * M_WIDTH), row),
            pl.BlockSpec((GATE_ROWS, tm), lambda b, s: (0, b * nt + s)),
            pl.BlockSpec((tm, C_WIDTH), row),
        ],
        out_shape=[
            jax.ShapeDtypeStruct((tok, 4 * M_WIDTH), BF16),
            jax.ShapeDtypeStruct((GATE_ROWS, tok), F32),
            jax.ShapeDtypeStruct((tok, C_WIDTH), BF16),
        ],
        scratch_shapes=[pltpu.VMEM((tm + 8, C_WIDTH), F32)],
        compiler_params=_cparams(("arbitrary", "arbitrary")),
        name="inproj",
    )(x2, mod, mod, wq, wg, wc, bg, wconv)


N_ROWQ = 4
SPLIT = 3


def _split3(x):
    hi = x.astype(BF16).astype(F32)
    r1 = x - hi
    mid = r1.astype(BF16).astype(F32)
    return [hi, mid, r1 - mid]


def _mlstm_selector(sel):
    r = lax.broadcasted_iota(I32, sel.shape, 0)
    c = lax.broadcasted_iota(I32, sel.shape, 1) // LANES
    rq = r // 8
    quantity = ((rq >= SPLIT).astype(I32) + (rq >= 2 * SPLIT).astype(I32)
                + (rq >= 3 * SPLIT).astype(I32))
    hit = ((r % 8) == c // N_ROWQ) & (quantity == c % N_ROWQ) & (rq < N_ROWQ * SPLIT)
    sel[...] = hit.astype(BF16)


def _mlstm_block(qkvo_ref, grow_ref, nw_ref, out_ref, cst, mst, sel, between=()):
    between = tuple(between) + (None,) * 3
    sb = qkvo_ref.shape[0]
    L = CHUNK
    causal = lax.broadcasted_iota(I32, (L, L), 0) >= lax.broadcasted_iota(I32, (L, L), 1)
    ones_ext = jnp.ones((L, D_HEAD), BF16)
    pad_rows = jnp.zeros((LANES - 8 * N_ROWQ * SPLIT, L), F32)
    eye = (lax.broadcasted_iota(I32, (D_HEAD, D_HEAD), 0)
           == lax.broadcasted_iota(I32, (D_HEAD, D_HEAD), 1)).astype(BF16)
    n_chunks = sb // L
    pairs = [(c, h) for c in range(n_chunks) for h in range(HEADS)]

    def cols(part, c, h):
        c0 = part * M_WIDTH + h * D_HEAD
        return qkvo_ref[c * L:(c + 1) * L, c0:c0 + D_HEAD]

    m_prev = mst[...]
    stack_t, us, decay = [], [], []
    for c in range(n_chunks):
        ug = grow_ref[0:8, c * L:(c + 1) * L]
        bcum = grow_ref[8:16, c * L:(c + 1) * L]
        cmax = grow_ref[16:24, c * L:(c + 1) * L]
        big_m = jnp.maximum(m_prev, cmax)
        m_last = jnp.broadcast_to(big_m[:, L - 1:L], (8, L))
        g_tot = jnp.broadcast_to(bcum[:, L - 1:L], (8, L))
        wq = jnp.exp(m_prev - big_m) * QK_SCALE
        log_em = -(bcum + big_m)
        wk = jnp.exp(ug - m_last)
        decay.append(jnp.exp(m_prev - m_last))
        us.append(ug + LOG_QK_SCALE)
        m_prev = g_tot + m_last
        stack = jnp.concatenate(_split3(big_m) + _split3(wq) + _split3(log_em) + _split3(wk)
                                + [pad_rows], axis=0)
        stack_t.append(stack.T.astype(BF16))
    mst[...] = m_prev
    if between[0] is not None:
        between[0]()

    reps = {}

    def rep(c, h, j):
        if (c, h) not in reps:
            c0 = h * N_ROWQ * LANES
            reps[(c, h)] = jnp.dot(stack_t[c], sel[:, c0:c0 + N_ROWQ * LANES],
                                   preferred_element_type=F32)
        return reps[(c, h)][:, j * LANES:(j + 1) * LANES]

    s_mat = {ch: lax.dot_general(cols(0, *ch), cols(1, *ch), (((1,), (1,)), ((), ())),
                                 preferred_element_type=F32) for ch in pairs}
    kw = {ch: (cols(1, *ch).astype(F32) * rep(*ch, 3)).astype(BF16) for ch in pairs}
    kw_t = {ch: lax.dot_general(eye, kw[ch], (((1,), (1,)), ((), ())),
                                preferred_element_type=F32).astype(BF16) for ch in pairs}
    if between[1] is not None:
        between[1]()
    lhs = {}
    for ch in pairs:
        c, h = ch
        big_m_rows = jnp.concatenate([rep(c, h, 0)] * (L // LANES), axis=1)
        p = jnp.where(causal, s_mat[ch] * jnp.exp(us[c][h:h + 1, :] - big_m_rows), 0.0)
        qw = cols(0, c, h).astype(F32) * rep(c, h, 1)
        lhs[ch] = jnp.concatenate([p.astype(BF16), qw.astype(BF16)], axis=1)
    vext = {ch: jnp.concatenate([cols(2, *ch), ones_ext], axis=1) for ch in pairs}
    upd = {ch: jnp.dot(kw_t[ch], vext[ch], preferred_element_type=F32) for ch in pairs}
    for c in range(n_chunks):
        tot = {}
        for h in range(HEADS):
            state = cst[h]
            rhs = jnp.concatenate([vext[(c, h)], state.astype(BF16)], axis=0)
            tot[h] = jnp.dot(lhs[(c, h)], rhs, preferred_element_type=F32)
            dec = jnp.concatenate([decay[c][h:h + 1, :LANES]] * 2, axis=1)
            cst[h] = dec * state + upd[(c, h)]
        for h in range(HEADS):
            c0 = h * D_HEAD
            num, den = tot[h][:, :D_HEAD], tot[h][:, D_HEAD:]
            hh = num / jnp.maximum(jnp.abs(den), jnp.exp(rep(c, h, 2)))
            mu = jnp.mean(hh, axis=1, keepdims=True)
            var = jnp.mean(jnp.square(hh - mu), axis=1, keepdims=True)
            hn = (hh - mu) * lax.rsqrt(var + LN_EPS) * nw_ref[:, c0:c0 + D_HEAD]
            og = jax.nn.sigmoid(cols(3, c, h).astype(F32))
            out_ref[c * L:(c + 1) * L, c0:c0 + D_HEAD] = (hn * og).astype(BF16)
        if c == 0 and between[2] is not None:
            between[2]()


def _layer_norm(r, g, b):
    mu = jnp.mean(r, axis=-1, keepdims=True)
    var = jnp.mean(jnp.square(r - mu), axis=-1, keepdims=True)
    return (r - mu) * lax.rsqrt(var + LN_EPS) * g + b


TOK_SUB = D_MODEL // LANES


def _store_token_major(ref, val, first_row=0):
    rows = val.shape[0]
    for k in range(TOK_SUB):
        ref[pl.ds(first_row * TOK_SUB + k, rows, stride=TOK_SUB), :] = val[:, k * LANES:(k + 1) * LANES]


def _load_token_major(ref, rows, first_row=0):
    return jnp.concatenate([ref[pl.ds(first_row * TOK_SUB + k, rows, stride=TOK_SUB), :]
                            for k in range(TOK_SUB)], axis=1)


def _outproj_parts(hn_ref, yc_ref, x_ref, gt_ref, sc_ref, sh_ref, wo_ref, lg_ref, lb_ref,
                   wr_ref, br_ref, x1_ref, u2_ref, route_ref, cnt_ref, base, triu, live):
    tm = x_ref.shape[0]
    held = {}

    def project():
        held["mix"] = (jnp.dot(hn_ref[...], wo_ref[0:M_WIDTH, :], preferred_element_type=F32)
                       + jnp.dot(yc_ref[...], wo_ref[M_WIDTH:, :], preferred_element_type=F32))

    def normalise():
        x1 = _layer_norm(ALPHA * x_ref[...] + (1.0 + gt_ref[0]) * held["mix"], lg_ref[...], lb_ref[...])
        x1_ref[...] = x1
        u2 = x1 * (1.0 + sc_ref[0]) + sh_ref[0]
        _store_token_major(u2_ref, u2)
        held["logits"] = lax.dot_general(wr_ref[...], u2.astype(BF16), (((1,), (1,)), ((), ())),
                                         preferred_element_type=F32) + br_ref[:, 0:1]

    def route():
        _route_block(held["logits"], route_ref, cnt_ref, base, triu, live, tm)

    return project, normalise, route


def _route_block(lt, route_ref, cnt_ref, base, triu, live, tm):
    lg = [lt[j:j + 1, :] for j in range(N_GROUPS + N_GROUPS * N_EXP)]
    best = lg[0]
    grp = jnp.zeros((1, tm), I32)
    for j in range(1, N_GROUPS):
        c = lg[j] > best
        grp = jnp.where(c, j, grp)
        best = jnp.where(c, lg[j], best)
    sel = []
    for e in range(N_EXP):
        val = lg[N_GROUPS + e]
        for gg in range(1, N_GROUPS):
            val = jnp.where(grp == gg, lg[N_GROUPS + gg * N_EXP + e], val)
        sel.append(val)
    v1 = sel[0]
    i1 = jnp.zeros((1, tm), I32)
    for e in range(1, N_EXP):
        c = sel[e] > v1
        i1 = jnp.where(c, e, i1)
        v1 = jnp.where(c, sel[e], v1)
    v2 = jnp.full((1, tm), -jnp.inf, F32)
    i2 = jnp.zeros((1, tm), I32)
    for e in range(N_EXP):
        cand = jnp.where(i1 == e, -jnp.inf, sel[e])
        c = cand > v2
        i2 = jnp.where(c, e, i2)
        v2 = jnp.where(c, cand, v2)
    ea = jnp.minimum(i1, i2)
    eb = jnp.maximum(i1, i2)
    lex = jnp.where(ea == 0, 0, jnp.where(ea == 1, 3, 5)) + eb - ea - 1
    pair = lex
    for i, p in enumerate(_PAIR_ORDER):
        if i != p:
            pair = jnp.where(lex == i, p, pair)
    bucket = grp * N_PAIRS + pair

    onehot = lax.broadcasted_iota(I32, (ROUTE_ROWS, tm), 0) == bucket
    cum = jnp.dot(onehot.astype(BF16), triu[...], preferred_element_type=F32)
    prev = base[:, 0:1]
    rank = jnp.sum(jnp.where(onehot, cum - 1.0 + prev, 0.0), axis=0, keepdims=True)
    new_base = prev + live * cum[:, tm - 1:tm]
    base[...] = jnp.broadcast_to(new_base, base.shape)
    cnt_ref[...] = jnp.broadcast_to(new_base, cnt_ref.shape)

    zrow = jnp.zeros((1, tm), F32)
    route_ref[...] = jnp.concatenate(
        [bucket.astype(F32), rank, zrow, zrow, zrow, zrow, zrow, zrow], axis=0)


def _mixer_kernel(qkvo_ref, grow_ref, nw_ref, yc_ref, x_ref, gt_ref, sc_ref, sh_ref, wo_ref, lg_ref,
                  lb_ref, wr_ref, br_ref, x1_ref, u2_ref, route_ref, cnt_ref,
                  cst, mst, sel, hn, base, triu, *, blocks_per_seq):
    i = pl.program_id(0)
    tm = x_ref.shape[0]

    @pl.when(i == 0)
    def _():
        _mlstm_selector(sel)
        hn[...] = jnp.zeros(hn.shape, BF16)
        base[...] = jnp.zeros(base.shape, F32)
        rid = lax.broadcasted_iota(I32, triu.shape, 0)
        cid = lax.broadcasted_iota(I32, triu.shape, 1)
        triu[...] = (rid <= cid).astype(BF16)

    @pl.when(i % blocks_per_seq == 0)
    def _():
        cst[...] = jnp.zeros(cst.shape, F32)
        mst[...] = jnp.zeros(mst.shape, F32)

    live = jnp.where(i > 0, 1.0, 0.0).astype(F32)
    parts = _outproj_parts(hn, yc_ref, x_ref, gt_ref, sc_ref, sh_ref, wo_ref, lg_ref, lb_ref,
                           wr_ref, br_ref, x1_ref, u2_ref, route_ref, cnt_ref, base, triu, live)
    _mlstm_block(qkvo_ref, grow_ref, nw_ref, hn, cst, mst, sel, between=parts)


def _mixer(qkvo, grow, nw, yc, x2, mod, layer, wo, lg, lb, wr, br, bsz, seq):
    tm = min(TOK_TILE, seq)
    nt = seq // tm
    tok = bsz * seq
    n = bsz * nt
    cur = lambda i: (jnp.minimum(i, n - 1), 0)
    prev = lambda i: (jnp.maximum(i - 1, 0), 0)
    const = lambda i: (0, 0)
    batch = lambda i: jnp.maximum(i - 1, 0) // nt
    return pl.pallas_call(
        functools.partial(_mixer_kernel, blocks_per_seq=nt),
        grid=(n + 1,),
        in_specs=[
            pl.BlockSpec((tm, 4 * M_WIDTH), cur),
            pl.BlockSpec((GATE_ROWS, tm), lambda i: (0, jnp.minimum(i, n - 1))),
            pl.BlockSpec((1, M_WIDTH), const),
            pl.BlockSpec((tm, C_WIDTH), prev),
            pl.BlockSpec((tm, D_MODEL), prev),
            _mod_spec(layer, MOD_GT1, bsz, batch),
            _mod_spec(layer, MOD_SC2, bsz, batch),
            _mod_spec(layer, MOD_SH2, bsz, batch),
            _layer_spec(wo, layer),
            pl.BlockSpec(lg.shape, const),
            pl.BlockSpec(lb.shape, const),
            pl.BlockSpec(wr.shape, const),
            pl.BlockSpec(br.shape, const),
        ],
        out_specs=[
            pl.BlockSpec((tm, D_MODEL), prev),
            pl.BlockSpec((tm * TOK_SUB, LANES), prev),
            pl.BlockSpec((8, tm), lambda i: (0, jnp.maximum(i - 1, 0))),
            pl.BlockSpec((ROUTE_ROWS, LANES), const),
        ],
        out_shape=[
            jax.ShapeDtypeStruct((tok, D_MODEL), F32),
            jax.ShapeDtypeStruct((tok * TOK_SUB, LANES), F32),
            jax.ShapeDtypeStruct((8, tok), F32),
            jax.ShapeDtypeStruct((ROUTE_ROWS, LANES), F32),
        ],
        scratch_shapes=[pltpu.VMEM((HEADS, D_HEAD, 2 * D_HEAD), F32),
                        pltpu.VMEM((8, CHUNK), F32),
                        pltpu.VMEM((LANES, HEADS * N_ROWQ * LANES), BF16),
                        pltpu.VMEM((tm, M_WIDTH), BF16),
                        pltpu.VMEM((ROUTE_ROWS, LANES), F32),
                        pltpu.VMEM((tm, tm), BF16)],
        compiler_params=_cparams(("arbitrary",)),
        name="mixer",
    )(qkvo, grow, nw, yc, x2, mod, mod, mod, wo, lg, lb, wr, br)


DISPATCH_TILE = 2048
DMA_UNROLL = 8


def _dispatch_kernel(pos_ref, pad_ref, u_ref, xs_ref, zeros, sem, zsem):
    ts = u_ref.shape[0] // TOK_SUB
    t0 = pl.program_id(0) * ts

    def zero_rows(first_row, n_rows_static):
        dst = pl.multiple_of(first_row * TOK_SUB, TOK_SUB)
        return pltpu.make_async_copy(zeros.at[pl.ds(0, n_rows_static * TOK_SUB), :],
                                     xs_ref.at[pl.ds(dst, n_rows_static * TOK_SUB), :], zsem)

    def fill_pads(wait):
        def bucket(b, carry):
            row = pad_ref[b]
            n = pad_ref[N_BUCKETS + b]
            size = MOE_STEP // 2
            while size >= 1:
                @pl.when((n & size) != 0)
                def _(row=row, size=size):
                    cp = zero_rows(row, size)
                    cp.wait() if wait else cp.start()
                row = row + (n & size)
                size //= 2
            return carry
        lax.fori_loop(0, N_BUCKETS, bucket, 0)

        def idle_tile(j, carry):
            cp = zero_rows(j * MOE_TILE, MOE_TILE)
            cp.wait() if wait else cp.start()
            return carry
        lax.fori_loop(pad_ref[2 * N_BUCKETS], xs_ref.shape[0] // (MOE_TILE * TOK_SUB), idle_tile, 0)

    @pl.when(pl.program_id(0) == 0)
    def _():
        zeros[...] = jnp.zeros(zeros.shape, F32)
        fill_pads(wait=False)
        fill_pads(wait=True)

    def issue(g, carry):
        for u in range(DMA_UNROLL):
            r = g * DMA_UNROLL + u
            dst = pl.multiple_of(pos_ref[t0 + r] * TOK_SUB, TOK_SUB)
            src = pl.multiple_of(r * TOK_SUB, TOK_SUB)
            pltpu.make_async_copy(u_ref.at[pl.ds(src, TOK_SUB), :],
                                  xs_ref.at[pl.ds(dst, TOK_SUB), :], sem).start(priority=u % 2)
        return carry

    lax.fori_loop(0, ts // DMA_UNROLL, issue, 0)
    pltpu.make_async_copy(u_ref, xs_ref.at[pl.ds(0, ts * TOK_SUB), :], sem).wait()


def _dispatch(pos, pad_start, u2, n_rows):
    tok = u2.shape[0] // TOK_SUB
    ts = min(DISPATCH_TILE, tok)
    return pl.pallas_call(
        _dispatch_kernel,
        grid_spec=pltpu.PrefetchScalarGridSpec(
            num_scalar_prefetch=2,
            grid=(tok // ts,),
            in_specs=[pl.BlockSpec((ts * TOK_SUB, LANES), lambda i, pos, pad: (i, 0))],
            out_specs=pl.BlockSpec(memory_space=pl.ANY),
            scratch_shapes=[pltpu.VMEM((MOE_TILE * TOK_SUB, LANES), F32),
                            pltpu.SemaphoreType.DMA(()), pltpu.SemaphoreType.DMA(())],
        ),
        out_shape=jax.ShapeDtypeStruct((n_rows * TOK_SUB, LANES), F32),
        compiler_params=_cparams(("arbitrary",)),
        name="dispatch",
    )(pos, pad_start, u2)


def _moe_kernel(s0_ref, s1_ref, tiles_ref, fresh0_ref, fresh1_ref, tb_ref, xb_ref, nxt0_ref, nxt1_ref,
                xs_ref, wsel_ref, bsel_ref, wg_hbm, wu_hbm, wd_hbm, y_ref,
                wgs, wus, wds, stg_g, stg_u, stg_d, wsem):
    del tb_ref, xb_ref
    j = pl.program_id(0)
    rows = MOE_TILE

    def fetch(slot, expert):
        return [pltpu.make_async_copy(src.at[expert], dst.at[slot], wsem.at[slot])
                for src, dst in ((wg_hbm, stg_g), (wu_hbm, stg_u), (wd_hbm, stg_d))]

    @pl.when(j == 0)
    def _():
        for cp in fetch(0, s0_ref[0]) + fetch(1, s1_ref[0]):
            cp.start()

    for slot, cur_ref, fresh_ref, nxt_ref in ((0, s0_ref, fresh0_ref, nxt0_ref),
                                              (1, s1_ref, fresh1_ref, nxt1_ref)):
        @pl.when(fresh_ref[j] == 1)
        def _(slot=slot, cur_ref=cur_ref, nxt_ref=nxt_ref):
            for cp in fetch(slot, cur_ref[j]):
                cp.wait()
            wgs[slot] = stg_g[slot].astype(BF16)
            wus[slot] = stg_u[slot].astype(BF16)
            wds[slot] = stg_d[slot].astype(BF16)

            @pl.when(nxt_ref[j] >= 0)
            def _():
                for cp in fetch(slot, nxt_ref[j]):
                    cp.start()

    def experts(part):
        x = _load_token_major(xs_ref, rows, part * rows).astype(BF16)
        lg = jnp.dot(x, wsel_ref[0], preferred_element_type=F32) + bsel_ref[0]
        grp = [lg[:, i:i + 1] for i in range(N_GROUPS)]
        gmax = jnp.maximum(jnp.maximum(grp[0], grp[1]), jnp.maximum(grp[2], grp[3]))
        gsum = sum(jnp.exp(v - gmax) for v in grp)
        la, lb = lg[:, N_GROUPS:N_GROUPS + 1], lg[:, N_GROUPS + 1:N_GROUPS + 2]
        emax = jnp.maximum(la, lb)
        ea, eb = jnp.exp(la - emax), jnp.exp(lb - emax)
        scale = 1.0 / (gsum * (ea + eb))
        wcols = (ea * scale, eb * scale)
        acc = None
        for slot in range(2):
            g = jnp.dot(x, wgs[slot], preferred_element_type=F32)
            u = jnp.dot(x, wus[slot], preferred_element_type=F32)
            hid = (g * jax.nn.sigmoid(g) * u * wcols[slot]).astype(BF16)
            y = jnp.dot(hid, wds[slot], preferred_element_type=F32)
            acc = y if acc is None else acc + y
        _store_token_major(y_ref, acc, part * rows)

    n_parts = MOE_STEP // MOE_TILE
    for live in range(n_parts + 1):
        @pl.when(tiles_ref[j] == live)
        def _(live=live):
            for part in range(live):
                experts(part)
            if live < n_parts:
                lo = live * rows * TOK_SUB
                y_ref[lo:, :] = jnp.zeros((y_ref.shape[0] - lo, LANES), F32)


def _moe(plan, xs, wsel, bsel, w_gate, w_up, w_down, n_tiles):
    by_bucket = lambda j, s0, s1, v, f0, f1, tb, xb, n0, n1: (tb[j], 0, 0)
    tile_rows = MOE_STEP * TOK_SUB
    hbm = pl.BlockSpec(memory_space=pl.ANY)
    return pl.pallas_call(
        _moe_kernel,
        grid_spec=pltpu.PrefetchScalarGridSpec(
            num_scalar_prefetch=9,
            grid=(n_tiles,),
            in_specs=[pl.BlockSpec((tile_rows, LANES), lambda j, s0, s1, v, f0, f1, tb, xb, n0, n1: (xb[j], 0)),
                      pl.BlockSpec((1, D_MODEL, LANES), by_bucket),
                      pl.BlockSpec((1, 1, LANES), by_bucket),
                      hbm, hbm, hbm],
            out_specs=pl.BlockSpec((tile_rows, LANES), lambda j, *_: (j, 0)),
            scratch_shapes=[pltpu.VMEM((2, D_MODEL, D_EXPERT), BF16),
                            pltpu.VMEM((2, D_MODEL, D_EXPERT), BF16),
                            pltpu.VMEM((2, D_EXPERT, D_MODEL), BF16),
                            pltpu.VMEM((2, D_MODEL, D_EXPERT), F32),
                            pltpu.VMEM((2, D_MODEL, D_EXPERT), F32),
                            pltpu.VMEM((2, D_EXPERT, D_MODEL), F32),
                            pltpu.SemaphoreType.DMA((2,))],
        ),
        out_shape=jax.ShapeDtypeStruct((n_tiles * tile_rows, LANES), F32),
        compiler_params=_cparams(("arbitrary",)),
        name="moe",
    )(*plan, xs, wsel, bsel, w_gate, w_up, w_down)


def _combine_kernel(pos_ref, ys_ref, x1_ref, gt_ref, lg_ref, lb_ref, out_ref, ybuf, sems):
    ts = x1_ref.shape[0]
    i = pl.program_id(0)
    n = pl.num_programs(0)
    slot = i % 2

    def gather(tile, to_slot):
        def issue(g, carry):
            for u in range(DMA_UNROLL):
                r = g * DMA_UNROLL + u
                src = pl.multiple_of(pos_ref[tile * ts + r] * TOK_SUB, TOK_SUB)
                dst = pl.multiple_of(r * TOK_SUB, TOK_SUB)
                pltpu.make_async_copy(ys_ref.at[pl.ds(src, TOK_SUB), :],
                                      ybuf.at[to_slot, pl.ds(dst, TOK_SUB), :],
                                      sems.at[to_slot]).start(priority=u % 2)
            return carry
        lax.fori_loop(0, ts // DMA_UNROLL, issue, 0)

    @pl.when(i == 0)
    def _():
        gather(0, 0)

    @pl.when(i + 1 < n)
    def _():
        gather(i + 1, 1 - slot)

    pltpu.make_async_copy(ys_ref.at[pl.ds(0, ts * TOK_SUB), :], ybuf.at[slot], sems.at[slot]).wait()
    y = _load_token_major(ybuf.at[slot], ts)
    out_ref[...] = _layer_norm(ALPHA * x1_ref[...] + (1.0 + gt_ref[0]) * y, lg_ref[...], lb_ref[...])


def _combine(pos, ys, x1, mod, layer, lg, lb, bsz, seq):
    ts = min(TOK_TILE, seq)
    nt = seq // ts
    tok = bsz * seq
    return pl.pallas_call(
        _combine_kernel,
        grid_spec=pltpu.PrefetchScalarGridSpec(
            num_scalar_prefetch=1,
            grid=(tok // ts,),
            in_specs=[pl.BlockSpec(memory_space=pl.ANY),
                      pl.BlockSpec((ts, D_MODEL), lambda i, pos: (i, 0)),
                      _mod_spec(layer, MOD_GT2, bsz, lambda i, pos: i // nt),
                      pl.BlockSpec((1, D_MODEL), lambda i, pos: (0, 0)),
                      pl.BlockSpec((1, D_MODEL), lambda i, pos: (0, 0))],
            out_specs=pl.BlockSpec((ts, D_MODEL), lambda i, pos: (i, 0)),
            scratch_shapes=[pltpu.VMEM((2, ts * TOK_SUB, LANES), F32), pltpu.SemaphoreType.DMA((2,))],
        ),
        out_shape=jax.ShapeDtypeStruct((tok, D_MODEL), F32),
        compiler_params=_cparams(("arbitrary",)),
        name="combine",
    )(pos, ys, x1, mod, lg, lb)


def _route_plan(route, cnt, n_tiles, layer):
    bucket = route[0].astype(I32)
    rank = route[1].astype(I32)
    counts = cnt[:N_BUCKETS, 0].astype(I32)
    tiles_b = (counts + MOE_STEP - 1) // MOE_STEP
    tile_end = jnp.cumsum(tiles_b)
    row_off = (tile_end - tiles_b) * MOE_STEP
    bucket_ids = jnp.arange(N_BUCKETS, dtype=I32)
    pos = jnp.sum(jnp.where(bucket[None, :] == bucket_ids[:, None], row_off[:, None], 0), axis=0) + rank
    total = tile_end[-1]
    j = jnp.arange(n_tiles, dtype=I32)
    tb = jnp.sum((tile_end[None, :] <= jnp.minimum(j, total - 1)[:, None]).astype(I32), axis=1)
    tb = jnp.clip(tb, 0, N_BUCKETS - 1)
    hit = tb[None, :] == bucket_ids[:, None]
    rows_left = jnp.sum(jnp.where(hit, (row_off + counts)[:, None], 0), axis=0) - j * MOE_STEP
    tiles_j = jnp.clip((rows_left + MOE_TILE - 1) // MOE_TILE, 0, MOE_STEP // MOE_TILE)
    tiles_j = jnp.where(j < total, tiles_j, 0)
    grp = tb // N_PAIRS
    pair = tb % N_PAIRS
    first = layer * (N_GROUPS * N_EXP) + grp * N_EXP
    s0 = first + sum((pair == i).astype(I32) * _SLOT0[i] for i in range(N_PAIRS))
    s1 = first + sum((pair == i).astype(I32) * _SLOT1[i] for i in range(N_PAIRS))
    one = jnp.ones((1,), I32)
    fresh0 = jnp.concatenate([one, (s0[1:] != s0[:-1]).astype(I32)])
    fresh1 = jnp.concatenate([one, (s1[1:] != s1[:-1]).astype(I32)])
    xblk = jnp.minimum(j, total - 1)
    pads = jnp.concatenate([row_off + counts, tiles_b * MOE_STEP - counts,
                            (total * (MOE_STEP // MOE_TILE))[None]])
    def next_expert(slot_expert, fresh):
        later = (j[None, :] > j[:, None]) & (fresh[None, :] == 1)
        nxt_step = jnp.min(jnp.where(later, j[None, :], n_tiles), axis=1)
        picked = jnp.sum(jnp.where(j[None, :] == nxt_step[:, None], slot_expert[None, :], 0), axis=1)
        return jnp.where(nxt_step < n_tiles, picked, -1)

    return pos, pads, (s0, s1, tiles_j, fresh0, fresh1, layer * N_BUCKETS + tb, xblk,
                       next_expert(s0, fresh0), next_expert(s1, fresh1))


def kernel(x, c, w_ada, b_ada, w_in, b_gates, mh_norm_w, w_conv, w_out, ln1_g, ln1_b,
           w_grp, b_grp, w_router, b_router, w_gate, w_up, w_down, ln2_g, ln2_b):
    bsz, seq, _ = x.shape
    tok = bsz * seq
    n_rows = tok + N_BUCKETS * MOE_STEP
    n_tiles = n_rows // MOE_STEP
    n_exp_total = N_GROUPS * N_EXP

    wg_all = w_gate.reshape(DEPTH * n_exp_total, D_MODEL, D_EXPERT)
    wu_all = w_up.reshape(DEPTH * n_exp_total, D_MODEL, D_EXPERT)
    wd_all = w_down.reshape(DEPTH * n_exp_total, D_EXPERT, D_MODEL)
    mod = _ada_mod(c, w_ada, b_ada).reshape(DEPTH * bsz * N_MOD, 1, D_MODEL)
    x2 = x.reshape(tok, D_MODEL)
    wq = w_in[:, :, :4 * M_WIDTH].astype(BF16)
    wg = jnp.pad(w_in[:, :, 4 * M_WIDTH:4 * M_WIDTH + 2 * HEADS],
                 ((0, 0), (0, 0), (0, LANES - 2 * HEADS))).astype(BF16)
    wc = w_in[:, :, 4 * M_WIDTH + 2 * HEADS:].astype(BF16)
    wo = w_out.astype(BF16)
    lb_shape = (DEPTH, N_BUCKETS)
    wsel = jnp.concatenate([jnp.broadcast_to(w_grp[:, None], lb_shape + (D_MODEL, N_GROUPS)),
                            jnp.swapaxes(w_router[:, :, _BUCKET_S0], 1, 2)[..., None],
                            jnp.swapaxes(w_router[:, :, _BUCKET_S1], 1, 2)[..., None]], axis=3)
    wsel = jnp.pad(wsel, ((0, 0), (0, 0), (0, 0), (0, LANES - N_GROUPS - 2))).astype(BF16)
    wsel = wsel.reshape(DEPTH * N_BUCKETS, D_MODEL, LANES)
    bsel = jnp.concatenate([jnp.broadcast_to(b_grp[:, None], lb_shape + (N_GROUPS,)),
                            b_router[:, _BUCKET_S0][..., None], b_router[:, _BUCKET_S1][..., None]], axis=2)
    bsel = jnp.pad(bsel, ((0, 0), (0, 0), (0, LANES - N_GROUPS - 2))).reshape(DEPTH * N_BUCKETS, 1, LANES)
    for l in range(DEPTH):
        bg = jnp.pad(b_gates[l], (0, LANES - 2 * HEADS)).reshape(1, LANES)
        qkvo, grow, yc = _inproj(x2, mod, l, wq, wg, wc, bg, w_conv[l], bsz, seq)
        n_logit = N_GROUPS + n_exp_total
        wr = jnp.pad(jnp.concatenate([w_grp[l], w_router[l]], axis=1).T,
                     ((0, ROUTE_ROWS - n_logit), (0, 0))).astype(BF16)
        br = jnp.pad(jnp.concatenate([b_grp[l], b_router[l]]), (0, ROUTE_ROWS - n_logit))
        br = jnp.broadcast_to(br[:, None], (ROUTE_ROWS, LANES))
        x1, u2, route, cnt = _mixer(qkvo, grow, mh_norm_w[l].reshape(1, M_WIDTH), yc, x2, mod, l,
                                    wo, ln1_g[l].reshape(1, D_MODEL),
                                    ln1_b[l].reshape(1, D_MODEL), wr, br, bsz, seq)
        pos, pads, plan = _route_plan(route, cnt, n_tiles, l)
        xs = _dispatch(pos, pads, u2, n_rows)
        ys = _moe(plan, xs, wsel, bsel, wg_all, wu_all, wd_all, n_tiles)
        x2 = _combine(pos, ys, x1, mod, l, ln2_g[l].reshape(1, D_MODEL),
                      ln2_b[l].reshape(1, D_MODEL), bsz, seq)
    return x2.reshape(bsz, seq, D_MODEL)
```

```python
import functools
import math

import jax
import jax.numpy as jnp
import numpy as np
from jax import lax
from jax.experimental import pallas as pl
from jax.experimental.pallas import tpu as pltpu

F32 = jnp.float32
BF16 = jnp.bfloat16
I32 = jnp.int32

D_MODEL = 1024
DEPTH = 2
HEADS = 4
D_HEAD = 128
M_WIDTH = HEADS * D_HEAD
C_WIDTH = D_MODEL - M_WIDTH
N_GROUPS = 4
N_EXP = 4
D_EXPERT = 512
N_PAIRS = 6
N_BUCKETS = N_GROUPS * N_PAIRS
N_MOD = 6
MOD_SH1, MOD_SC1, MOD_GT1, MOD_SH2, MOD_SC2, MOD_GT2 = range(N_MOD)
ALPHA = (2 * DEPTH) ** 0.25
LN_EPS = 1e-5
QK_SCALE = D_HEAD ** -0.5
LOG_QK_SCALE = math.log(QK_SCALE)
GATE_ROWS = 24

LANES = 128
ROUTE_ROWS = 32

TOK_TILE = 512
INPROJ_TILE = 1024
CHUNK = 256
MOE_TILE = 256
MOE_STEP = 2 * MOE_TILE
VMEM_LIMIT = 56 * 1024 * 1024

_PAIR_ORDER = (0, 2, 3, 1, 4, 5)
_SLOT0 = (0, 2, 2, 3, 3, 3)
_SLOT1 = (1, 1, 0, 0, 1, 2)
_BUCKET_S0 = np.array([g * N_EXP + _SLOT0[p] for g in range(N_GROUPS) for p in range(N_PAIRS)])
_BUCKET_S1 = np.array([g * N_EXP + _SLOT1[p] for g in range(N_GROUPS) for p in range(N_PAIRS)])


def _cparams(sem):
    return pltpu.CompilerParams(dimension_semantics=sem, vmem_limit_bytes=VMEM_LIMIT)


def _ada_kernel(c_ref, w_ref, b_ref, o_ref):
    c = c_ref[...]
    cond = (c * jax.nn.sigmoid(c)).astype(BF16)
    o_ref[0] = jnp.dot(cond, w_ref[0].astype(BF16), preferred_element_type=F32) + b_ref[0]


def _ada_mod(c, w_ada, b_ada):
    bsz = c.shape[0]
    nblk = w_ada.shape[2] // D_MODEL
    return pl.pallas_call(
        _ada_kernel,
        grid=(DEPTH, nblk),
        in_specs=[
            pl.BlockSpec((bsz, D_MODEL), lambda l, j: (0, 0)),
            pl.BlockSpec((1, D_MODEL, D_MODEL), lambda l, j: (l, 0, j)),
            pl.BlockSpec((1, 1, D_MODEL), lambda l, j: (l, 0, j)),
        ],
        out_specs=pl.BlockSpec((1, bsz, D_MODEL), lambda l, j: (l, 0, j)),
        out_shape=jax.ShapeDtypeStruct((DEPTH, bsz, nblk * D_MODEL), F32),
        compiler_params=_cparams(("arbitrary", "arbitrary")),
        name="ada_mod",
    )(c, w_ada, b_ada.reshape(DEPTH, 1, -1))


def _log_sigmoid(x):
    return jnp.minimum(x, 0.0) - jnp.log1p(jnp.exp(-jnp.abs(x)))


def _chunk_scan(x, op, identity):
    lane = lax.broadcasted_iota(I32, x.shape, 1) & (CHUNK - 1)
    sh = 1
    while sh < CHUNK:
        x = op(x, jnp.where(lane >= sh, pltpu.roll(x, sh, 1), identity))
        sh *= 2
    return x


def _inproj_kernel(x_ref, sc_ref, sh_ref, wq_ref, wg_ref, wc_ref, bg_ref, wconv_ref,
                   qkvo_ref, grow_ref, yc_ref, zbuf):
    tm = x_ref.shape[0]
    u = (x_ref[...] * (1.0 + sc_ref[0]) + sh_ref[0]).astype(BF16)
    qkvo_ref[...] = jnp.dot(u, wq_ref[...], preferred_element_type=F32).astype(BF16)
    g = jnp.dot(u, wg_ref[...], preferred_element_type=F32) + bg_ref[...]
    gt = g.T[:8]
    head_row = lax.broadcasted_iota(I32, gt.shape, 0) < HEADS
    ig = jnp.where(head_row, gt, 0.0)
    logf = jnp.where(head_row, _log_sigmoid(pltpu.roll(gt, HEADS, 0)), 0.0)
    bcum = _chunk_scan(logf, jnp.add, 0.0)
    ug = ig - bcum
    grow_ref[0:8, :] = ug
    grow_ref[8:16, :] = bcum
    grow_ref[16:24, :] = _chunk_scan(ug, jnp.maximum, -jnp.inf)
    pc = jnp.dot(u, wc_ref[...], preferred_element_type=F32)
    z = pc[:, C_WIDTH:2 * C_WIDTH] * pc[:, 2 * C_WIDTH:]

    @pl.when(pl.program_id(1) == 0)
    def _():
        zbuf[0:8, :] = jnp.zeros((8, C_WIDTH), F32)

    zbuf[8:8 + tm, :] = z
    zc = (wconv_ref[0:1, :] * zbuf[6:6 + tm, :] + wconv_ref[1:2, :] * zbuf[7:7 + tm, :]
          + wconv_ref[2:3, :] * z)
    yc_ref[...] = (pc[:, :C_WIDTH] * zc).astype(BF16)
    zbuf[0:8, :] = zbuf[tm:tm + 8, :]


def _mod_spec(layer, which, bsz, batch_of):
    return pl.BlockSpec((1, 1, D_MODEL),
                        lambda *idx: ((layer * bsz + batch_of(*idx)) * N_MOD + which, 0, 0))


def _layer_spec(stacked, layer):
    return pl.BlockSpec((None,) + stacked.shape[1:], lambda *idx: (layer, 0, 0))


def _inproj(x2, mod, layer, wq, wg, wc, bg, wconv, bsz, seq):
    tm = min(INPROJ_TILE, seq)
    nt = seq // tm
    tok = bsz * seq
    row = lambda b, s: (b * nt + s, 0)
    const = lambda b, s: (0, 0)
    batch = lambda b, s: b
    return pl.pallas_call(
        _inproj_kernel,
        grid=(bsz, nt),
        in_specs=[
            pl.BlockSpec((tm, D_MODEL), row),
            _mod_spec(layer, MOD_SC1, bsz, batch),
            _mod_spec(layer, MOD_SH1, bsz, batch),
            _layer_spec(wq, layer),
            _layer_spec(wg, layer),
            _layer_spec(wc, layer),
            pl.BlockSpec(bg.shape, const),
            pl.BlockSpec(wconv.shape, const),
        ],
        out_specs=[
            pl.BlockSpec((tm, 4 * M_WIDTH), row),
            pl.BlockSpec((GATE_ROWS, tm), lambda b, s: (0, b * nt + s)),
            pl.BlockSpec((tm, C_WIDTH), row),
        ],
        out_shape=[
            jax.ShapeDtypeStruct((tok, 4 * M_WIDTH), BF16),
            jax.ShapeDtypeStruct((GATE_ROWS, tok), F32),
            jax.ShapeDtypeStruct((tok, C_WIDTH), BF16),
        ],
        scratch_shapes=[pltpu.VMEM((tm + 8, C_WIDTH), F32)],
        compiler_params=_cparams(("arbitrary", "arbitrary")),
        name="inproj",
    )(x2, mod, mod, wq, wg, wc, bg, wconv)


N_ROWQ = 4
SPLIT = 3


def _split3(x):
    hi = x.astype(BF16).astype(F32)
    r1 = x - hi
    mid = r1.astype(BF16).astype(F32)
    return [hi, mid, r1 - mid]


def _mlstm_selector(sel):
    r = lax.broadcasted_iota(I32, sel.shape, 0)
    c = lax.broadcasted_iota(I32, sel.shape, 1) // LANES
    rq = r // 8
    quantity = ((rq >= SPLIT).astype(I32) + (rq >= 2 * SPLIT).astype(I32)
                + (rq >= 3 * SPLIT).astype(I32))
    hit = ((r % 8) == c // N_ROWQ) & (quantity == c % N_ROWQ) & (rq < N_ROWQ * SPLIT)
    sel[...] = hit.astype(BF16)


def _mlstm_block(qkvo_ref, grow_ref, nw_ref, out_ref, cst, mst, sel, between=()):
    between = tuple(between) + (None,) * 3
    sb = qkvo_ref.shape[0]
    L = CHUNK
    causal = lax.broadcasted_iota(I32, (L, L), 0) >= lax.broadcasted_iota(I32, (L, L), 1)
    ones_ext = jnp.ones((L, D_HEAD), BF16)
    pad_rows = jnp.zeros((LANES - 8 * N_ROWQ * SPLIT, L), F32)
    eye = (lax.broadcasted_iota(I32, (D_HEAD, D_HEAD), 0)
           == lax.broadcasted_iota(I32, (D_HEAD, D_HEAD), 1)).astype(BF16)
    n_chunks = sb // L
    pairs = [(c, h) for c in range(n_chunks) for h in range(HEADS)]

    def cols(part, c, h):
        c0 = part * M_WIDTH + h * D_HEAD
        return qkvo_ref[c * L:(c + 1) * L, c0:c0 + D_HEAD]

    m_prev = mst[...]
    stack_t, us, decay = [], [], []
    for c in range(n_chunks):
        ug = grow_ref[0:8, c * L:(c + 1) * L]
        bcum = grow_ref[8:16, c * L:(c + 1) * L]
        cmax = grow_ref[16:24, c * L:(c + 1) * L]
        big_m = jnp.maximum(m_prev, cmax)
        m_last = jnp.broadcast_to(big_m[:, L - 1:L], (8, L))
        g_tot = jnp.broadcast_to(bcum[:, L - 1:L], (8, L))
        wq = jnp.exp(m_prev - big_m) * QK_SCALE
        log_em = -(bcum + big_m)
        wk = jnp.exp(ug - m_last)
        decay.append(jnp.exp(m_prev - m_last))
        us.append(ug + LOG_QK_SCALE)
        m_prev = g_tot + m_last
        stack = jnp.concatenate(_split3(big_m) + _split3(wq) + _split3(log_em) + _split3(wk)
                                + [pad_rows], axis=0)
        stack_t.append(stack.T.astype(BF16))
    mst[...] = m_prev
    if between[0] is not None:
        between[0]()

    reps = {}

    def rep(c, h, j):
        if (c, h) not in reps:
            c0 = h * N_ROWQ * LANES
            reps[(c, h)] = jnp.dot(stack_t[c], sel[:, c0:c0 + N_ROWQ * LANES],
                                   preferred_element_type=F32)
        return reps[(c, h)][:, j * LANES:(j + 1) * LANES]

    s_mat = {ch: lax.dot_general(cols(0, *ch), cols(1, *ch), (((1,), (1,)), ((), ())),
                                 preferred_element_type=F32) for ch in pairs}
    kw = {ch: (cols(1, *ch).astype(F32) * rep(*ch, 3)).astype(BF16) for ch in pairs}
    kw_t = {ch: lax.dot_general(eye, kw[ch], (((1,), (1,)), ((), ())),
                                preferred_element_type=F32).astype(BF16) for ch in pairs}
    if between[1] is not None:
        between[1]()
    lhs = {}
    for ch in pairs:
        c, h = ch
        big_m_rows = jnp.concatenate([rep(c, h, 0)] * (L // LANES), axis=1)
        p = jnp.where(causal, s_mat[ch] * jnp.exp(us[c][h:h + 1, :] - big_m_rows), 0.0)
        qw = cols(0, c, h).astype(F32) * rep(c, h, 1)
        lhs[ch] = jnp.concatenate([p.astype(BF16), qw.astype(BF16)], axis=1)
    vext = {ch: jnp.concatenate([cols(2, *ch), ones_ext], axis=1) for ch in pairs}
    upd = {ch: jnp.dot(kw_t[ch], vext[ch], preferred_element_type=F32) for ch in pairs}
    for c in range(n_chunks):
        tot = {}
        for h in range(HEADS):
            state = cst[h]
            rhs = jnp.concatenate([vext[(c, h)], state.astype(BF16)], axis=0)
            tot[h] = jnp.dot(lhs[(c, h)], rhs, preferred_element_type=F32)
            dec = jnp.concatenate([decay[c][h:h + 1, :LANES]] * 2, axis=1)
            cst[h] = dec * state + upd[(c, h)]
        for h in range(HEADS):
            c0 = h * D_HEAD
            num, den = tot[h][:, :D_HEAD], tot[h][:, D_HEAD:]
            hh = num / jnp.maximum(jnp.abs(den), jnp.exp(rep(c, h, 2)))
            mu = jnp.mean(hh, axis=1, keepdims=True)
            var = jnp.mean(jnp.square(hh - mu), axis=1, keepdims=True)
            hn = (hh - mu) * lax.rsqrt(var + LN_EPS) * nw_ref[:, c0:c0 + D_HEAD]
            og = jax.nn.sigmoid(cols(3, c, h).astype(F32))
            out_ref[c * L:(c + 1) * L, c0:c0 + D_HEAD] = (hn * og).astype(BF16)
        if c == 0 and between[2] is not None:
            between[2]()


def _layer_norm(r, g, b):
    mu = jnp.mean(r, axis=-1, keepdims=True)
    var = jnp.mean(jnp.square(r - mu), axis=-1, keepdims=True)
    return (r - mu) * lax.rsqrt(var + LN_EPS) * g + b


TOK_SUB = D_MODEL // LANES


def _store_token_major(ref, val, first_row=0):
    rows = val.shape[0]
    for k in range(TOK_SUB):
        ref[pl.ds(first_row * TOK_SUB + k, rows, stride=TOK_SUB), :] = val[:, k * LANES:(k + 1) * LANES]


def _load_token_major(ref, rows, first_row=0):
    return jnp.concatenate([ref[pl.ds(first_row * TOK_SUB + k, rows, stride=TOK_SUB), :]
                            for k in range(TOK_SUB)], axis=1)


def _outproj_parts(hn_ref, yc_ref, x_ref, gt_ref, sc_ref, sh_ref, wo_ref, lg_ref, lb_ref,
                   wr_ref, br_ref, x1_ref, u2_ref, route_ref, cnt_ref, base, triu, live):
    tm = x_ref.shape[0]
    held = {}

    def project():
        held["mix"] = (jnp.dot(hn_ref[...], wo_ref[0:M_WIDTH, :], preferred_element_type=F32)
                       + jnp.dot(yc_ref[...], wo_ref[M_WIDTH:, :], preferred_element_type=F32))

    def normalise():
        x1 = _layer_norm(ALPHA * x_ref[...] + (1.0 + gt_ref[0]) * held["mix"], lg_ref[...], lb_ref[...])
        x1_ref[...] = x1
        u2 = x1 * (1.0 + sc_ref[0]) + sh_ref[0]
        _store_token_major(u2_ref, u2)
        held["logits"] = lax.dot_general(wr_ref[...], u2.astype(BF16), (((1,), (1,)), ((), ())),
                                         preferred_element_type=F32) + br_ref[:, 0:1]

    def route():
        _route_block(held["logits"], route_ref, cnt_ref, base, triu, live, tm)

    return project, normalise, route


def _route_block(lt, route_ref, cnt_ref, base, triu, live, tm):
    lg = [lt[j:j + 1, :] for j in range(N_GROUPS + N_GROUPS * N_EXP)]
    best = lg[0]
    grp = jnp.zeros((1, tm), I32)
    for j in range(1, N_GROUPS):
        c = lg[j] > best
        grp = jnp.where(c, j, grp)
        best = jnp.where(c, lg[j], best)
    sel = []
    for e in range(N_EXP):
        val = lg[N_GROUPS + e]
        for gg in range(1, N_GROUPS):
            val = jnp.where(grp == gg, lg[N_GROUPS + gg * N_EXP + e], val)
        sel.append(val)
    v1 = sel[0]
    i1 = jnp.zeros((1, tm), I32)
    for e in range(1, N_EXP):
        c = sel[e] > v1
        i1 = jnp.where(c, e, i1)
        v1 = jnp.where(c, sel[e], v1)
    v2 = jnp.full((1, tm), -jnp.inf, F32)
    i2 = jnp.zeros((1, tm), I32)
    for e in range(N_EXP):
        cand = jnp.where(i1 == e, -jnp.inf, sel[e])
        c = cand > v2
        i2 = jnp.where(c, e, i2)
        v2 = jnp.where(c, cand, v2)
    ea = jnp.minimum(i1, i2)
    eb = jnp.maximum(i1, i2)
    lex = jnp.where(ea == 0, 0, jnp.where(ea == 1, 3, 5)) + eb - ea - 1
    pair = lex
    for i, p in enumerate(_PAIR_ORDER):
        if i != p:
            pair = jnp.where(lex == i, p, pair)
    bucket = grp * N_PAIRS + pair

    onehot = lax.broadcasted_iota(I32, (ROUTE_ROWS, tm), 0) == bucket
    cum = jnp.dot(onehot.astype(BF16), triu[...], preferred_element_type=F32)
    prev = base[:, 0:1]
    rank = jnp.sum(jnp.where(onehot, cum - 1.0 + prev, 0.0), axis=0, keepdims=True)
    new_base = prev + live * cum[:, tm - 1:tm]
    base[...] = jnp.broadcast_to(new_base, base.shape)
    cnt_ref[...] = jnp.broadcast_to(new_base, cnt_ref.shape)

    zrow = jnp.zeros((1, tm), F32)
    route_ref[...] = jnp.concatenate(
        [bucket.astype(F32), rank, zrow, zrow, zrow, zrow, zrow, zrow], axis=0)


def _mixer_kernel(qkvo_ref, grow_ref, nw_ref, yc_ref, x_ref, gt_ref, sc_ref, sh_ref, wo_ref, lg_ref,
                  lb_ref, wr_ref, br_ref, x1_ref, u2_ref, route_ref, cnt_ref,
                  cst, mst, sel, hn, base, triu, *, blocks_per_seq):
    i = pl.program_id(0)
    tm = x_ref.shape[0]

    @pl.when(i == 0)
    def _():
        _mlstm_selector(sel)
        hn[...] = jnp.zeros(hn.shape, BF16)
        base[...] = jnp.zeros(base.shape, F32)
        rid = lax.broadcasted_iota(I32, triu.shape, 0)
        cid = lax.broadcasted_iota(I32, triu.shape, 1)
        triu[...] = (rid <= cid).astype(BF16)

    @pl.when(i % blocks_per_seq == 0)
    def _():
        cst[...] = jnp.zeros(cst.shape, F32)
        mst[...] = jnp.zeros(mst.shape, F32)

    live = jnp.where(i > 0, 1.0, 0.0).astype(F32)
    parts = _outproj_parts(hn, yc_ref, x_ref, gt_ref, sc_ref, sh_ref, wo_ref, lg_ref, lb_ref,
                           wr_ref, br_ref, x1_ref, u2_ref, route_ref, cnt_ref, base, triu, live)
    _mlstm_block(qkvo_ref, grow_ref, nw_ref, hn, cst, mst, sel, between=parts)


def _mixer(qkvo, grow, nw, yc, x2, mod, layer, wo, lg, lb, wr, br, bsz, seq):
    tm = min(TOK_TILE, seq)
    nt = seq // tm
    tok = bsz * seq
    n = bsz * nt
    cur = lambda i: (jnp.minimum(i, n - 1), 0)
    prev = lambda i: (jnp.maximum(i - 1, 0), 0)
    const = lambda i: (0, 0)
    batch = lambda i: jnp.maximum(i - 1, 0) // nt
    return pl.pallas_call(
        functools.partial(_mixer_kernel, blocks_per_seq=nt),
        grid=(n + 1,),
        in_specs=[
            pl.BlockSpec((tm, 4 * M_WIDTH), cur),
            pl.BlockSpec((GATE_ROWS, tm), lambda i: (0, jnp.minimum(i, n - 1))),
            pl.BlockSpec((1, M_WIDTH), const),
            pl.BlockSpec((tm, C_WIDTH), prev),
            pl.BlockSpec((tm, D_MODEL), prev),
            _mod_spec(layer, MOD_GT1, bsz, batch),
            _mod_spec(layer, MOD_SC2, bsz, batch),
            _mod_spec(layer, MOD_SH2, bsz, batch),
            _layer_spec(wo, layer),
            pl.BlockSpec(lg.shape, const),
            pl.BlockSpec(lb.shape, const),
            pl.BlockSpec(wr.shape, const),
            pl.BlockSpec(br.shape, const),
        ],
        out_specs=[
            pl.BlockSpec((tm, D_MODEL), prev),
            pl.BlockSpec((tm * TOK_SUB, LANES), prev),
            pl.BlockSpec((8, tm), lambda i: (0, jnp.maximum(i - 1, 0))),
            pl.BlockSpec((ROUTE_ROWS, LANES), const),
        ],
        out_shape=[
            jax.ShapeDtypeStruct((tok, D_MODEL), F32),
            jax.ShapeDtypeStruct((tok * TOK_SUB, LANES), F32),
            jax.ShapeDtypeStruct((8, tok), F32),
            jax.ShapeDtypeStruct((ROUTE_ROWS, LANES), F32),
        ],
        scratch_shapes=[pltpu.VMEM((HEADS, D_HEAD, 2 * D_HEAD), F32),
                        pltpu.VMEM((8, CHUNK), F32),
                        pltpu.VMEM((LANES, HEADS * N_ROWQ * LANES), BF16),
                        pltpu.VMEM((tm, M_WIDTH), BF16),
                        pltpu.VMEM((ROUTE_ROWS, LANES), F32),
                        pltpu.VMEM((tm, tm), BF16)],
        compiler_params=_cparams(("arbitrary",)),
        name="mixer",
    )(qkvo, grow, nw, yc, x2, mod, mod, mod, wo, lg, lb, wr, br)


DISPATCH_TILE = 2048
DMA_UNROLL = 8


def _dispatch_kernel(pos_ref, pad_ref, u_ref, xs_ref, zeros, sem, zsem):
    ts = u_ref.shape[0] // TOK_SUB
    t0 = pl.program_id(0) * ts

    def zero_rows(first_row, n_rows_static):
        dst = pl.multiple_of(first_row * TOK_SUB, TOK_SUB)
        return pltpu.make_async_copy(zeros.at[pl.ds(0, n_rows_static * TOK_SUB), :],
                                     xs_ref.at[pl.ds(dst, n_rows_static * TOK_SUB), :], zsem)

    def fill_pads(wait):
        def bucket(b, carry):
            row = pad_ref[b]
            n = pad_ref[N_BUCKETS + b]
            size = MOE_STEP // 2
            while size >= 1:
                @pl.when((n & size) != 0)
                def _(row=row, size=size):
                    cp = zero_rows(row, size)
                    cp.wait() if wait else cp.start()
                row = row + (n & size)
                size //= 2
            return carry
        lax.fori_loop(0, N_BUCKETS, bucket, 0)

        def idle_tile(j, carry):
            cp = zero_rows(j * MOE_TILE, MOE_TILE)
            cp.wait() if wait else cp.start()
            return carry
        lax.fori_loop(pad_ref[2 * N_BUCKETS], xs_ref.shape[0] // (MOE_TILE * TOK_SUB), idle_tile, 0)

    @pl.when(pl.program_id(0) == 0)
    def _():
        zeros[...] = jnp.zeros(zeros.shape, F32)
        fill_pads(wait=False)
        fill_pads(wait=True)

    def issue(g, carry):
        for u in range(DMA_UNROLL):
            r = g * DMA_UNROLL + u
            dst = pl.multiple_of(pos_ref[t0 + r] * TOK_SUB, TOK_SUB)
            src = pl.multiple_of(r * TOK_SUB, TOK_SUB)
            pltpu.make_async_copy(u_ref.at[pl.ds(src, TOK_SUB), :],
                                  xs_ref.at[pl.ds(dst, TOK_SUB), :], sem).start(priority=u % 2)
        return carry

    lax.fori_loop(0, ts // DMA_UNROLL, issue, 0)
    pltpu.make_async_copy(u_ref, xs_ref.at[pl.ds(0, ts * TOK_SUB), :], sem).wait()


def _dispatch(pos, pad_start, u2, n_rows):
    tok = u2.shape[0] // TOK_SUB
    ts = min(DISPATCH_TILE, tok)
    return pl.pallas_call(
        _dispatch_kernel,
        grid_spec=pltpu.PrefetchScalarGridSpec(
            num_scalar_prefetch=2,
            grid=(tok // ts,),
            in_specs=[pl.BlockSpec((ts * TOK_SUB, LANES), lambda i, pos, pad: (i, 0))],
            out_specs=pl.BlockSpec(memory_space=pl.ANY),
            scratch_shapes=[pltpu.VMEM((MOE_TILE * TOK_SUB, LANES), F32),
                            pltpu.SemaphoreType.DMA(()), pltpu.SemaphoreType.DMA(())],
        ),
        out_shape=jax.ShapeDtypeStruct((n_rows * TOK_SUB, LANES), F32),
        compiler_params=_cparams(("arbitrary",)),
        name="dispatch",
    )(pos, pad_start, u2)


def _moe_kernel(s0_ref, s1_ref, tiles_ref, fresh0_ref, fresh1_ref, tb_ref, xb_ref, nxt0_ref, nxt1_ref,
                xs_ref, wsel_ref, bsel_ref, wg_hbm, wu_hbm, wd_hbm, y_ref,
                wgs, wus, wds, stg_g, stg_u, stg_d, wsem):
    del tb_ref, xb_ref
    j = pl.program_id(0)
    rows = MOE_TILE

    def fetch(slot, expert):
        return [pltpu.make_async_copy(src.at[expert], dst.at[slot], wsem.at[slot])
                for src, dst in ((wg_hbm, stg_g), (wu_hbm, stg_u), (wd_hbm, stg_d))]

    @pl.when(j == 0)
    def _():
        for cp in fetch(0, s0_ref[0]) + fetch(1, s1_ref[0]):
            cp.start()

    for slot, cur_ref, fresh_ref, nxt_ref in ((0, s0_ref, fresh0_ref, nxt0_ref),
                                              (1, s1_ref, fresh1_ref, nxt1_ref)):
        @pl.when(fresh_ref[j] == 1)
        def _(slot=slot, cur_ref=cur_ref, nxt_ref=nxt_ref):
            for cp in fetch(slot, cur_ref[j]):
                cp.wait()
            wgs[slot] = stg_g[slot].astype(BF16)
            wus[slot] = stg_u[slot].astype(BF16)
            wds[slot] = stg_d[slot].astype(BF16)

            @pl.when(nxt_ref[j] >= 0)
            def _():
                for cp in fetch(slot, nxt_ref[j]):
                    cp.start(priority=1)

    def experts(part):
        x = _load_token_major(xs_ref, rows, part * rows).astype(BF16)
        lg = jnp.dot(x, wsel_ref[0], preferred_element_type=F32) + bsel_ref[0]
        grp = [lg[:, i:i + 1] for i in range(N_GROUPS)]
        gmax = jnp.maximum(jnp.maximum(grp[0], grp[1]), jnp.maximum(grp[2], grp[3]))
        gsum = sum(jnp.exp(v - gmax) for v in grp)
        la, lb = lg[:, N_GROUPS:N_GROUPS + 1], lg[:, N_GROUPS + 1:N_GROUPS + 2]
        emax = jnp.maximum(la, lb)
        ea, eb = jnp.exp(la - emax), jnp.exp(lb - emax)
        scale = 1.0 / (gsum * (ea + eb))
        wcols = (ea * scale, eb * scale)
        acc = None
        for slot in range(2):
            g = jnp.dot(x, wgs[slot], preferred_element_type=F32)
            u = jnp.dot(x, wus[slot], preferred_element_type=F32)
            hid = (g * jax.nn.sigmoid(g) * u * wcols[slot]).astype(BF16)
            y = jnp.dot(hid, wds[slot], preferred_element_type=F32)
            acc = y if acc is None else acc + y
        _store_token_major(y_ref, acc, part * rows)

    n_parts = MOE_STEP // MOE_TILE
    for live in range(n_parts + 1):
        @pl.when(tiles_ref[j] == live)
        def _(live=live):
            for part in range(live):
                experts(part)
            if live < n_parts:
                lo = live * rows * TOK_SUB
                y_ref[lo:, :] = jnp.zeros((y_ref.shape[0] - lo, LANES), F32)


def _moe(plan, xs, wsel, bsel, w_gate, w_up, w_down, n_tiles):
    by_bucket = lambda j, s0, s1, v, f0, f1, tb, xb, n0, n1: (tb[j], 0, 0)
    tile_rows = MOE_STEP * TOK_SUB
    hbm = pl.BlockSpec(memory_space=pl.ANY)
    return pl.pallas_call(
        _moe_kernel,
        grid_spec=pltpu.PrefetchScalarGridSpec(
            num_scalar_prefetch=9,
            grid=(n_tiles,),
            in_specs=[pl.BlockSpec((tile_rows, LANES), lambda j, s0, s1, v, f0, f1, tb, xb, n0, n1: (xb[j], 0)),
                      pl.BlockSpec((1, D_MODEL, LANES), by_bucket),
                      pl.BlockSpec((1, 1, LANES), by_bucket),
                      hbm, hbm, hbm],
            out_specs=pl.BlockSpec((tile_rows, LANES), lambda j, *_: (j, 0)),
            scratch_shapes=[pltpu.VMEM((2, D_MODEL, D_EXPERT), BF16),
                            pltpu.VMEM((2, D_MODEL, D_EXPERT), BF16),
                            pltpu.VMEM((2, D_EXPERT, D_MODEL), BF16),
                            pltpu.VMEM((2, D_MODEL, D_EXPERT), F32),
                            pltpu.VMEM((2, D_MODEL, D_EXPERT), F32),
                            pltpu.VMEM((2, D_EXPERT, D_MODEL), F32),
                            pltpu.SemaphoreType.DMA((2,))],
        ),
        out_shape=jax.ShapeDtypeStruct((n_tiles * tile_rows, LANES), F32),
        compiler_params=_cparams(("arbitrary",)),
        name="moe",
    )(*plan, xs, wsel, bsel, w_gate, w_up, w_down)


def _combine_kernel(pos_ref, ys_ref, x1_ref, gt_ref, lg_ref, lb_ref, out_ref, ybuf, sems):
    ts = x1_ref.shape[0]
    i = pl.program_id(0)
    n = pl.num_programs(0)
    slot = i % 2

    def gather(tile, to_slot):
        def issue(g, carry):
            for u in range(DMA_UNROLL):
                r = g * DMA_UNROLL + u
                src = pl.multiple_of(pos_ref[tile * ts + r] * TOK_SUB, TOK_SUB)
                dst = pl.multiple_of(r * TOK_SUB, TOK_SUB)
                pltpu.make_async_copy(ys_ref.at[pl.ds(src, TOK_SUB), :],
                                      ybuf.at[to_slot, pl.ds(dst, TOK_SUB), :],
                                      sems.at[to_slot]).start(priority=u % 2)
            return carry
        lax.fori_loop(0, ts // DMA_UNROLL, issue, 0)

    @pl.when(i == 0)
    def _():
        gather(0, 0)

    @pl.when(i + 1 < n)
    def _():
        gather(i + 1, 1 - slot)

    pltpu.make_async_copy(ys_ref.at[pl.ds(0, ts * TOK_SUB), :], ybuf.at[slot], sems.at[slot]).wait()
    y = _load_token_major(ybuf.at[slot], ts)
    out_ref[...] = _layer_norm(ALPHA * x1_ref[...] + (1.0 + gt_ref[0]) * y, lg_ref[...], lb_ref[...])


def _combine(pos, ys, x1, mod, layer, lg, lb, bsz, seq):
    ts = min(TOK_TILE, seq)
    nt = seq // ts
    tok = bsz * seq
    return pl.pallas_call(
        _combine_kernel,
        grid_spec=pltpu.PrefetchScalarGridSpec(
            num_scalar_prefetch=1,
            grid=(tok // ts,),
            in_specs=[pl.BlockSpec(memory_space=pl.ANY),
                      pl.BlockSpec((ts, D_MODEL), lambda i, pos: (i, 0)),
                      _mod_spec(layer, MOD_GT2, bsz, lambda i, pos: i // nt),
                      pl.BlockSpec((1, D_MODEL), lambda i, pos: (0, 0)),
                      pl.BlockSpec((1, D_MODEL), lambda i, pos: (0, 0))],
            out_specs=pl.BlockSpec((ts, D_MODEL), lambda i, pos: (i, 0)),
            scratch_shapes=[pltpu.VMEM((2, ts * TOK_SUB, LANES), F32), pltpu.SemaphoreType.DMA((2,))],
        ),
        out_shape=jax.ShapeDtypeStruct((tok, D_MODEL), F32),
        compiler_params=_cparams(("arbitrary",)),
        name="combine",
    )(pos, ys, x1, mod, lg, lb)


def _route_plan(route, cnt, n_tiles, layer):
    bucket = route[0].astype(I32)
    rank = route[1].astype(I32)
    counts = cnt[:N_BUCKETS, 0].astype(I32)
    tiles_b = (counts + MOE_STEP - 1) // MOE_STEP
    tile_end = jnp.cumsum(tiles_b)
    row_off = (tile_end - tiles_b) * MOE_STEP
    bucket_ids = jnp.arange(N_BUCKETS, dtype=I32)
    pos = jnp.sum(jnp.where(bucket[None, :] == bucket_ids[:, None], row_off[:, None], 0), axis=0) + rank
    total = tile_end[-1]
    j = jnp.arange(n_tiles, dtype=I32)
    tb = jnp.sum((tile_end[None, :] <= jnp.minimum(j, total - 1)[:, None]).astype(I32), axis=1)
    tb = jnp.clip(tb, 0, N_BUCKETS - 1)
    hit = tb[None, :] == bucket_ids[:, None]
    rows_left = jnp.sum(jnp.where(hit, (row_off + counts)[:, None], 0), axis=0) - j * MOE_STEP
    tiles_j = jnp.clip((rows_left + MOE_TILE - 1) // MOE_TILE, 0, MOE_STEP // MOE_TILE)
    tiles_j = jnp.where(j < total, tiles_j, 0)
    grp = tb // N_PAIRS
    pair = tb % N_PAIRS
    first = layer * (N_GROUPS * N_EXP) + grp * N_EXP
    s0 = first + sum((pair == i).astype(I32) * _SLOT0[i] for i in range(N_PAIRS))
    s1 = first + sum((pair == i).astype(I32) * _SLOT1[i] for i in range(N_PAIRS))
    one = jnp.ones((1,), I32)
    fresh0 = jnp.concatenate([one, (s0[1:] != s0[:-1]).astype(I32)])
    fresh1 = jnp.concatenate([one, (s1[1:] != s1[:-1]).astype(I32)])
    xblk = jnp.minimum(j, total - 1)
    pads = jnp.concatenate([row_off + counts, tiles_b * MOE_STEP - counts,
                            (total * (MOE_STEP // MOE_TILE))[None]])
    def next_expert(slot_expert, fresh):
        later = (j[None, :] > j[:, None]) & (fresh[None, :] == 1)
        nxt_step = jnp.min(jnp.where(later, j[None, :], n_tiles), axis=1)
        picked = jnp.sum(jnp.where(j[None, :] == nxt_step[:, None], slot_expert[None, :], 0), axis=1)
        return jnp.where(nxt_step < n_tiles, picked, -1)

    return pos, pads, (s0, s1, tiles_j, fresh0, fresh1, layer * N_BUCKETS + tb, xblk,
                       next_expert(s0, fresh0), next_expert(s1, fresh1))


def kernel(x, c, w_ada, b_ada, w_in, b_gates, mh_norm_w, w_conv, w_out, ln1_g, ln1_b,
           w_grp, b_grp, w_router, b_router, w_gate, w_up, w_down, ln2_g, ln2_b):
    bsz, seq, _ = x.shape
    tok = bsz * seq
    n_rows = tok + N_BUCKETS * MOE_STEP
    n_tiles = n_rows // MOE_STEP
    n_exp_total = N_GROUPS * N_EXP

    wg_all = w_gate.reshape(DEPTH * n_exp_total, D_MODEL, D_EXPERT)
    wu_all = w_up.reshape(DEPTH * n_exp_total, D_MODEL, D_EXPERT)
    wd_all = w_down.reshape(DEPTH * n_exp_total, D_EXPERT, D_MODEL)
    mod = _ada_mod(c, w_ada, b_ada).reshape(DEPTH * bsz * N_MOD, 1, D_MODEL)
    x2 = x.reshape(tok, D_MODEL)
    wq = w_in[:, :, :4 * M_WIDTH].astype(BF16)
    wg = jnp.pad(w_in[:, :, 4 * M_WIDTH:4 * M_WIDTH + 2 * HEADS],
                 ((0, 0), (0, 0), (0, LANES - 2 * HEADS))).astype(BF16)
    wc = w_in[:, :, 4 * M_WIDTH + 2 * HEADS:].astype(BF16)
    wo = w_out.astype(BF16)
    lb_shape = (DEPTH, N_BUCKETS)
    wsel = jnp.concatenate([jnp.broadcast_to(w_grp[:, None], lb_shape + (D_MODEL, N_GROUPS)),
                            jnp.swapaxes(w_router[:, :, _BUCKET_S0], 1, 2)[..., None],
                            jnp.swapaxes(w_router[:, :, _BUCKET_S1], 1, 2)[..., None]], axis=3)
    wsel = jnp.pad(wsel, ((0, 0), (0, 0), (0, 0), (0, LANES - N_GROUPS - 2))).astype(BF16)
    wsel = wsel.reshape(DEPTH * N_BUCKETS, D_MODEL, LANES)
    bsel = jnp.concatenate([jnp.broadcast_to(b_grp[:, None], lb_shape + (N_GROUPS,)),
                            b_router[:, _BUCKET_S0][..., None], b_router[:, _BUCKET_S1][..., None]], axis=2)
    bsel = jnp.pad(bsel, ((0, 0), (0, 0), (0, LANES - N_GROUPS - 2))).reshape(DEPTH * N_BUCKETS, 1, LANES)
    for l in range(DEPTH):
        bg = jnp.pad(b_gates[l], (0, LANES - 2 * HEADS)).reshape(1, LANES)
        qkvo, grow, yc = _inproj(x2, mod, l, wq, wg, wc, bg, w_conv[l], bsz, seq)
        n_logit = N_GROUPS + n_exp_total
        wr = jnp.pad(jnp.concatenate([w_grp[l], w_router[l]], axis=1).T,
                     ((0, ROUTE_ROWS - n_logit), (0, 0))).astype(BF16)
        br = jnp.pad(jnp.concatenate([b_grp[l], b_router[l]]), (0, ROUTE_ROWS - n_logit))
        br = jnp.broadcast_to(br[:, None], (ROUTE_ROWS, LANES))
        x1, u2, route, cnt = _mixer(qkvo, grow, mh_norm_w[l].reshape(1, M_WIDTH), yc, x2, mod, l,
                                    wo, ln1_g[l].reshape(1, D_MODEL),
                                    ln1_b[l].reshape(1, D_MODEL), wr, br, bsz, seq)
        pos, pads, plan = _route_plan(route, cnt, n_tiles, l)
        xs = _dispatch(pos, pads, u2, n_rows)
        ys = _moe(plan, xs, wsel, bsel, wg_all, wu_all, wd_all, n_tiles)
        x2 = _combine(pos, ys, x1, mod, l, ln2_g[l].reshape(1, D_MODEL),
                      ln2_b[l].reshape(1, D_MODEL), bsz, seq)
    return x2.reshape(bsz, seq, D_MODEL)
```
